```python
import jax, jax.numpy as jnp
from jax import lax
import numpy as np

D_MODEL = 1024
BATCH = 8
SEQ = 4096
DEPTH = 2

N_MIXERS = 2
N_A_LAYERS = (DEPTH + 1) // 2
N_B_LAYERS = DEPTH // 2

N_HEADS = 16
HEAD_DIM = 64
N_KV_HEADS = 4
GROUP = N_HEADS // N_KV_HEADS
IDX_HEADS = 8
IDX_DIM = 64
TOPK_MAX = 256
Q_BLOCK = 128
Q_W = N_HEADS * HEAD_DIM
KV_W = N_KV_HEADS * HEAD_DIM
IQ_W = IDX_HEADS * IDX_DIM
DSA_WIDTHS = (Q_W, KV_W, KV_W, IQ_W, IDX_DIM, IDX_HEADS)
DSA_IN = sum(DSA_WIDTHS)
DSA_SPLITS = tuple(int(s) for s in np.cumsum(DSA_WIDTHS)[:-1])

RWKV_HEAD = 64
RWKV_HEADS = D_MODEL // RWKV_HEAD
DECAY_LORA = 64
AAA_LORA = 64
GATE_LORA = 160
GN_EPS = 64e-5

D_FF = 2816
N_MOD = 9
RMS_EPS = 1e-6

kernel_name = "hybrid_dsa_rwkv7_macaron_adaln"


def rms(x):
    xf = x.astype(jnp.float32)
    return (xf * lax.rsqrt(jnp.mean(xf * xf, axis=-1, keepdims=True) + RMS_EPS)).astype(x.dtype)


def modulate(h, shift, scale):
    return rms(h) * (1 + scale[:, None, :]) + shift[:, None, :]


def swiglu(x, w_gu, w_down):
    gate, up = jnp.split(x @ w_gu, 2, axis=-1)
    return (jax.nn.silu(gate) * up) @ w_down


def dsa_mixer(x, w_in, q_norm, k_norm, ik_norm, w_out):
    B, S, _ = x.shape
    k_top = min(TOPK_MAX, S // 4)
    q, k, v, iq, ik, iw = jnp.split(x @ w_in, DSA_SPLITS, axis=-1)
    q = rms(q.reshape(B, S, N_KV_HEADS, GROUP, HEAD_DIM)) * q_norm
    k = rms(k.reshape(B, S, N_KV_HEADS, HEAD_DIM)) * k_norm
    v = v.reshape(B, S, N_KV_HEADS, HEAD_DIM)
    iq = iq.reshape(B, S, IDX_HEADS, IDX_DIM)
    ik = rms(ik) * ik_norm
    iw = iw * (IDX_HEADS ** -0.5 * IDX_DIM ** -0.5)
    nb = S // Q_BLOCK
    key_pos = jnp.arange(S)

    def to_blocks(a):
        return jnp.moveaxis(a.reshape(B, nb, Q_BLOCK, *a.shape[2:]), 1, 0)

    def block(args):
        qb, iqb, iwb, start = args
        qpos = start + jnp.arange(Q_BLOCK)
        rel = jax.nn.relu(jnp.einsum('bqhd,bsd->bqhs', iqb, ik))
        score = jnp.einsum('bqhs,bqh->bqs', rel, iwb).astype(jnp.float32)
        causal = key_pos[None, :] <= qpos[:, None]
        score = jnp.where(causal[None], score, -jnp.inf)
        _, sel = lax.top_k(score, k_top)
        valid = sel <= qpos[None, :, None]
        k_sel = jax.vmap(lambda kk, ii: kk[ii])(k, sel)
        v_sel = jax.vmap(lambda vv, ii: vv[ii])(v, sel)
        logits = jnp.einsum('bqhgd,bqkhd->bqhgk', qb, k_sel).astype(jnp.float32) * (HEAD_DIM ** -0.5)
        logits = jnp.where(valid[:, :, None, None, :], logits, -jnp.inf)
        p = jax.nn.softmax(logits, axis=-1).astype(v.dtype)
        return jnp.einsum('bqhgk,bqkhd->bqhgd', p, v_sel)

    starts = jnp.arange(nb, dtype=jnp.int32) * Q_BLOCK
    o = lax.map(block, (to_blocks(q), to_blocks(iq), to_blocks(iw), starts))
    o = jnp.moveaxis(o, 0, 1).reshape(B, S, Q_W)
    return o @ w_out


def rwkv7_mixer(x, mu, w_rkv, w0, w1, w2, a0, a1, a2, g1, g2, k_k, k_a, r_k, ln_w, ln_b, w_out):
    B, S, D = x.shape
    H, N = RWKV_HEADS, RWKV_HEAD
    f32 = jnp.float32
    xx = jnp.pad(x, ((0, 0), (1, 0), (0, 0)))[:, :-1] - x
    xrkv = x[None] + xx[None] * mu[:3, None, None, :]
    r, k, v = jnp.einsum('nbsd,nde->nbse', xrkv, w_rkv)
    xw = x + xx * mu[3]
    xa = x + xx * mu[4]
    xg = x + xx * mu[5]
    w_log = -jax.nn.softplus(-(w0 + jnp.tanh(xw @ w1) @ w2)) - 0.5
    decay = jnp.exp(-jnp.exp(w_log.astype(f32)))
    a = jax.nn.sigmoid(a0 + (xa @ a1) @ a2)
    g = jax.nn.sigmoid(xg @ g1) @ g2
    kk = (k * k_k).astype(f32).reshape(B, S, H, N)
    kk = kk / jnp.maximum(jnp.sqrt(jnp.sum(kk * kk, axis=-1, keepdims=True)), 1e-12)
    k = k * (1 + (a - 1) * k_a)
    rh = r.astype(f32).reshape(B, S, H, N)
    kh = k.astype(f32).reshape(B, S, H, N)
    vh = v.astype(f32).reshape(B, S, H, N)
    ah = a.astype(f32).reshape(B, S, H, N)
    wh = decay.reshape(B, S, H, N)
    bh = kk * ah

    def step(state, inp):
        r_t, w_t, k_t, v_t, kk_t, b_t = inp
        sa = jnp.einsum('bhvk,bhk->bhv', state, -kk_t)
        state = (state * w_t[:, :, None, :] + sa[..., None] * b_t[:, :, None, :]
                 + v_t[..., None] * k_t[:, :, None, :])
        return state, jnp.einsum('bhvk,bhk->bhv', state, r_t)

    tf = lambda t: jnp.moveaxis(t, 1, 0)
    state0 = jnp.zeros((B, H, N, N), f32)
    _, y = lax.scan(step, state0, (tf(rh), tf(wh), tf(kh), tf(vh), tf(kk), tf(bh)))
    y = jnp.moveaxis(y, 0, 1)
    mean = jnp.mean(y, axis=-1, keepdims=True)
    var = jnp.mean(jnp.square(y - mean), axis=-1, keepdims=True)
    yn = ((y - mean) * lax.rsqrt(var + GN_EPS)).reshape(B, S, D) * ln_w.astype(f32) + ln_b.astype(f32)
    bonus = (jnp.sum(rh * kh * r_k.astype(f32), axis=-1, keepdims=True) * vh).reshape(B, S, D)
    out = ((yn + bonus).astype(x.dtype)) * g
    return out @ w_out


def setup_inputs(seed: int = 0) -> dict:
    key = jax.random.key(seed)
    ks = jax.random.split(key, 32)
    f32 = jnp.float32
    D = D_MODEL

    def nrm(k, shape, fan_in, scale=1.0):
        return jax.random.normal(k, shape, f32) * (scale * fan_in ** -0.5)

    def near_one(k, shape):
        return 1.0 + 0.02 * jax.random.normal(k, shape, f32)

    return {
        "x": jax.random.normal(ks[0], (BATCH, SEQ, D), f32),
        "c": jax.random.normal(ks[1], (BATCH, D), f32),
        "ada_w": nrm(ks[2], (DEPTH, D, N_MOD * D), D, 0.5),
        "ada_b": 0.02 * jax.random.normal(ks[3], (DEPTH, N_MOD * D), f32),
        "ffn_w_gu": nrm(ks[4], (DEPTH, 2, D, 2 * D_FF), D),
        "ffn_w_down": nrm(ks[5], (DEPTH, 2, D_FF, D), D_FF),
        "dsa_w_in": nrm(ks[6], (N_A_LAYERS, D, DSA_IN), D),
        "dsa_q_norm": near_one(ks[7], (N_A_LAYERS, HEAD_DIM)),
        "dsa_k_norm": near_one(ks[8], (N_A_LAYERS, HEAD_DIM)),
        "dsa_ik_norm": near_one(ks[9], (N_A_LAYERS, IDX_DIM)),
        "dsa_w_out": nrm(ks[10], (N_A_LAYERS, Q_W, D), Q_W),
        "rwkv_mu": jax.random.uniform(ks[11], (N_B_LAYERS, 6, D), f32),
        "rwkv_w_rkv": nrm(ks[12], (N_B_LAYERS, 3, D, D), D),
        "rwkv_w0": jax.random.uniform(ks[13], (N_B_LAYERS, D), f32, -6.0, -1.0),
        "rwkv_w1": nrm(ks[14], (N_B_LAYERS, D, DECAY_LORA), D),
        "rwkv_w2": nrm(ks[15], (N_B_LAYERS, DECAY_LORA, D), DECAY_LORA, 0.1),
        "rwkv_a0": 0.1 * jax.random.normal(ks[16], (N_B_LAYERS, D), f32),
        "rwkv_a1": nrm(ks[17], (N_B_LAYERS, D, AAA_LORA), D),
        "rwkv_a2": nrm(ks[18], (N_B_LAYERS, AAA_LORA, D), AAA_LORA, 0.5),
        "rwkv_g1": nrm(ks[19], (N_B_LAYERS, D, GATE_LORA), D),
        "rwkv_g2": nrm(ks[20], (N_B_LAYERS, GATE_LORA, D), GATE_LORA),
        "rwkv_k_k": 0.85 + 0.02 * jax.random.normal(ks[21], (N_B_LAYERS, D), f32),
        "rwkv_k_a": near_one(ks[22], (N_B_LAYERS, D)),
        "rwkv_r_k": 0.1 * jax.random.normal(ks[23], (N_B_LAYERS, RWKV_HEADS, RWKV_HEAD), f32),
        "rwkv_ln_w": near_one(ks[24], (N_B_LAYERS, D)),
        "rwkv_ln_b": 0.02 * jax.random.normal(ks[25], (N_B_LAYERS, D), f32),
        "rwkv_w_out": nrm(ks[26], (N_B_LAYERS, D, D), D),
    }


def reference(x, c, ada_w, ada_b, ffn_w_gu, ffn_w_down, dsa_w_in, dsa_q_norm, dsa_k_norm,
              dsa_ik_norm, dsa_w_out, rwkv_mu, rwkv_w_rkv, rwkv_w0, rwkv_w1, rwkv_w2, rwkv_a0,
              rwkv_a1, rwkv_a2, rwkv_g1, rwkv_g2, rwkv_k_k, rwkv_k_a, rwkv_r_k, rwkv_ln_w,
              rwkv_ln_b, rwkv_w_out):
    h = x
    c_act = jax.nn.silu(c)
    for i in range(DEPTH):
        mod = c_act @ ada_w[i] + ada_b[i]
        sh1, sc1, gt1, sh2, sc2, gt2, sh3, sc3, gt3 = jnp.split(mod, N_MOD, axis=-1)
        h = h + 0.5 * gt1[:, None, :] * swiglu(modulate(h, sh1, sc1), ffn_w_gu[i, 0], ffn_w_down[i, 0])
        u = modulate(h, sh2, sc2)
        j = i // N_MIXERS
        if i % N_MIXERS == 0:
            m = dsa_mixer(u, dsa_w_in[j], dsa_q_norm[j], dsa_k_norm[j], dsa_ik_norm[j], dsa_w_out[j])
        else:
            m = rwkv7_mixer(u, rwkv_mu[j], rwkv_w_rkv[j], rwkv_w0[j], rwkv_w1[j], rwkv_w2[j],
                            rwkv_a0[j], rwkv_a1[j], rwkv_a2[j], rwkv_g1[j], rwkv_g2[j],
                            rwkv_k_k[j], rwkv_k_a[j], rwkv_r_k[j], rwkv_ln_w[j], rwkv_ln_b[j],
                            rwkv_w_out[j])
        h = h + gt2[:, None, :] * m
        h = h + 0.5 * gt3[:, None, :] * swiglu(modulate(h, sh3, sc3), ffn_w_gu[i, 1], ffn_w_down[i, 1])
    return h
```

```python
import functools

import jax
import jax.numpy as jnp
import numpy as np
from jax import lax
from jax.experimental import pallas as pl
from jax.experimental.pallas import tpu as pltpu

F32 = jnp.float32
BF16 = jnp.bfloat16

RMS_EPS = 1e-6
N_MOD = 9
VMEM_LIMIT_BYTES = 56 * 1024 * 1024

N_HEADS = 16
HEAD_DIM = 64
N_KV_HEADS = 4
GROUP = N_HEADS // N_KV_HEADS
IDX_HEADS = 8
IDX_DIM = 64
TOPK_MAX = 256
LANE = 128


def _cparams(*sem):
    return pltpu.CompilerParams(dimension_semantics=sem, vmem_limit_bytes=VMEM_LIMIT_BYTES)


def _dot(a, b):
    return jnp.dot(a, b, preferred_element_type=F32)


def _dot_nt(a, b):
    return lax.dot_general(a, b, (((1,), (1,)), ((), ())), preferred_element_type=F32)


def _dot_tn(a, b):
    return lax.dot_general(a, b, (((0,), (0,)), ((), ())), preferred_element_type=F32)


def _modulate(h, sh, sc):
    ms = jnp.mean(h * h, axis=-1, keepdims=True)
    return h * lax.rsqrt(ms + RMS_EPS) * (1.0 + sc) + sh


def _ada_kernel(c_ref, w_ref, b_ref, o_ref):
    c = c_ref[...]
    ca = (c * jax.nn.sigmoid(c)).astype(BF16)
    o_ref[...] = _dot(ca, w_ref[...].astype(BF16)) + b_ref[...]


def _ada_mod(c, ada_w, ada_b):
    L, D, N = ada_w.shape
    B = c.shape[0]
    tn = 1536
    return pl.pallas_call(
        _ada_kernel,
        grid=(L, N // tn),
        in_specs=[
            pl.BlockSpec((B, D), lambda l, j: (0, 0)),
            pl.BlockSpec((None, D, tn), lambda l, j: (l, 0, j)),
            pl.BlockSpec((None, 1, tn), lambda l, j: (l, 0, j)),
        ],
        out_specs=pl.BlockSpec((None, B, tn), lambda l, j: (l, 0, j)),
        out_shape=jax.ShapeDtypeStruct((L, B, N), F32),
        compiler_params=_cparams("parallel", "parallel"),
    )(c, ada_w, ada_b.reshape(L, 1, N))


FF_CHUNK = 256


def _ffn_kernel(h_ref, sh_ref, sc_ref, gt_ref, wgu_ref, wd_ref, o_ref, *, d_ff):
    h = h_ref[...]
    u = _modulate(h, sh_ref[...], sc_ref[...]).astype(BF16)
    acc = None
    for c in range(d_ff // FF_CHUNK):
        lo = c * FF_CHUNK
        g = _dot(u, wgu_ref[:, lo:lo + FF_CHUNK])
        up = _dot(u, wgu_ref[:, d_ff + lo:d_ff + lo + FF_CHUNK])
        a = (g * jax.nn.sigmoid(g) * up).astype(BF16)
        part = _dot(a, wd_ref[lo:lo + FF_CHUNK, :])
        acc = part if acc is None else acc + part
    o_ref[...] = h + (0.5 * gt_ref[...]) * acc


def _ffn(h, mod, k0, w_gu, w_down, tm=512):
    B, S, D = h.shape
    d_ff = w_down.shape[0]
    mspec = lambda k: pl.BlockSpec((None, None, 1, D), lambda b, i: (b, k, 0, 0))
    return pl.pallas_call(
        functools.partial(_ffn_kernel, d_ff=d_ff),
        grid=(B, S // tm),
        in_specs=[
            pl.BlockSpec((None, tm, D), lambda b, i: (b, i, 0)),
            mspec(k0), mspec(k0 + 1), mspec(k0 + 2),
            pl.BlockSpec((D, 2 * d_ff), lambda b, i: (0, 0), pipeline_mode=pl.Buffered(1)),
            pl.BlockSpec((d_ff, D), lambda b, i: (0, 0), pipeline_mode=pl.Buffered(1)),
        ],
        out_specs=pl.BlockSpec((None, tm, D), lambda b, i: (b, i, 0)),
        out_shape=jax.ShapeDtypeStruct((B, S, D), F32),
        compiler_params=_cparams("parallel", "parallel"),
    )(h, mod, mod, mod, w_gu, w_down)


def _dsa_in_kernel(h_ref, sh_ref, sc_ref, wqT_ref, wk_ref, wvT_ref, wiqT_ref, wik_ref, wiwT_ref,
                   qn_ref, kn_ref, ikn_ref,
                   qT_ref, k_ref, vT_ref, iqT_ref, ik_ref, iwT_ref):
    u = _modulate(h_ref[...], sh_ref[...], sc_ref[...]).astype(BF16)
    tm = u.shape[0]
    qT = _dot_nt(wqT_ref[...], u).reshape(N_HEADS, HEAD_DIM, tm)
    ms = jnp.mean(qT * qT, axis=1, keepdims=True)
    qT = qT * lax.rsqrt(ms + RMS_EPS) * qn_ref[...][None] * (HEAD_DIM ** -0.5)
    qT_ref[...] = qT.reshape(N_HEADS * HEAD_DIM, tm).astype(BF16)
    k = _dot(u, wk_ref[...])
    kn = kn_ref[...]
    for j in range(N_KV_HEADS):
        kj = k[:, j * LANE:(j + 1) * LANE]
        ms = jnp.sum(kj * kj, axis=-1, keepdims=True) * (1.0 / HEAD_DIM)
        k_ref[:, j * LANE:(j + 1) * LANE] = (kj * lax.rsqrt(ms + RMS_EPS) * kn).astype(BF16)
    vT_ref[...] = _dot_nt(wvT_ref[...], u).astype(BF16)
    iqT_ref[...] = _dot_nt(wiqT_ref[...], u).astype(BF16)
    ik = _dot(u, wik_ref[...])
    ms = jnp.sum(ik * ik, axis=-1, keepdims=True) * (1.0 / IDX_DIM)
    ik_ref[...] = (ik * lax.rsqrt(ms + RMS_EPS) * ikn_ref[...]).astype(BF16)
    iwT_ref[...] = _dot_nt(wiwT_ref[...], u) * (IDX_HEADS ** -0.5 * IDX_DIM ** -0.5)


DSA_KC = 256
DSA_QB = 128


def _dsa_in(h, mod, w_in, q_norm, k_norm, ik_norm, tm=DSA_KC):
    B, S, D = h.shape
    QW, KW = N_HEADS * HEAD_DIM, N_KV_HEADS * HEAD_DIM
    IQW = IDX_HEADS * IDX_DIM
    o1, o2, o3, o4, o5 = QW, QW + KW, QW + 2 * KW, QW + 2 * KW + IQW, QW + 2 * KW + IQW + IDX_DIM
    wb = w_in.astype(BF16)
    wqT = wb[:, :o1].T
    wk = jnp.pad(wb[:, o1:o2].reshape(D, N_KV_HEADS, HEAD_DIM), ((0, 0), (0, 0), (0, LANE - HEAD_DIM))).reshape(D, N_KV_HEADS * LANE)
    wvT = wb[:, o2:o3].T
    wiqT = wb[:, o3:o4].T
    wik = jnp.pad(wb[:, o4:o5], ((0, 0), (0, LANE - IDX_DIM)))
    wiwT = wb[:, o5:].T
    qn = q_norm.astype(F32).reshape(HEAD_DIM, 1)
    kn = jnp.pad(k_norm.astype(F32), (0, LANE - HEAD_DIM)).reshape(1, LANE)
    ikn = jnp.pad(ik_norm.astype(F32), (0, LANE - IDX_DIM)).reshape(1, LANE)
    mspec = lambda k: pl.BlockSpec((None, None, 1, D), lambda b, i: (b, k, 0, 0))
    full = lambda a: pl.BlockSpec(a.shape, lambda b, i: (0,) * a.ndim)
    nc = S // tm
    return pl.pallas_call(
        _dsa_in_kernel,
        grid=(B, nc),
        in_specs=[pl.BlockSpec((None, tm, D), lambda b, i: (b, i, 0)), mspec(3), mspec(4),
                  full(wqT), full(wk), full(wvT), full(wiqT), full(wik), full(wiwT),
                  full(qn), full(kn), full(ikn)],
        out_specs=[
            pl.BlockSpec((None, QW, tm), lambda b, i: (b, 0, i)),
            pl.BlockSpec((None, tm, N_KV_HEADS * LANE), lambda b, i: (b, i, 0)),
            pl.BlockSpec((None, None, KW, tm), lambda b, i: (b, i, 0, 0)),
            pl.BlockSpec((None, IQW, tm), lambda b, i: (b, 0, i)),
            pl.BlockSpec((None, tm, LANE), lambda b, i: (b, i, 0)),
            pl.BlockSpec((None, IDX_HEADS, tm), lambda b, i: (b, 0, i)),
        ],
        out_shape=[
            jax.ShapeDtypeStruct((B, QW, S), BF16),
            jax.ShapeDtypeStruct((B, S, N_KV_HEADS * LANE), BF16),
            jax.ShapeDtypeStruct((B, nc, KW, tm), BF16),
            jax.ShapeDtypeStruct((B, IQW, S), BF16),
            jax.ShapeDtypeStruct((B, S, LANE), BF16),
            jax.ShapeDtypeStruct((B, IDX_HEADS, S), F32),
        ],
        compiler_params=_cparams("parallel", "parallel"),
    )(h, mod, mod, wqT, wk, wvT, wiqT, wik, wiwT, qn, kn, ikn)


NEG_BIG = -1e30


def _ukey_to_float(ukey):
    key = ukey ^ np.int32(-2 ** 31)
    bits = key ^ ((key >> 31) & np.int32(0x7FFFFFFF))
    return lax.bitcast_convert_type(bits, F32)


def _dsa_core_kernel(iqT_ref, iwT_ref, qT_ref, ik_ref, k_ref, vT_ref, oT_ref,
                     sc_ref, qa_ref, iqa_ref, m_ref, l_ref, acc_ref, *, k_top):
    KC, QB = DSA_KC, DSA_QB
    qi = pl.program_id(1)
    q0 = qi * QB
    nk = (q0 + QB + KC - 1) // KC
    qpos = q0 + lax.broadcasted_iota(jnp.int32, (1, QB), 1)
    zeros_h = jnp.zeros((LANE - HEAD_DIM, QB), BF16)

    for h in range(IDX_HEADS):
        iqa_ref[h] = jnp.concatenate([iqT_ref[h * IDX_DIM:(h + 1) * IDX_DIM, :], zeros_h], axis=0)
    for j in range(N_KV_HEADS):
        for g in range(GROUP):
            hh = j * GROUP + g
            qa_ref[j, :, g * QB:(g + 1) * QB] = jnp.concatenate(
                [qT_ref[hh * HEAD_DIM:(hh + 1) * HEAD_DIM, :], zeros_h], axis=0)
    iw = iwT_ref[...]

    def p1(c, carry):
        k0 = pl.multiple_of(c * KC, KC)
        ikc = ik_ref[pl.ds(k0, KC), :]
        s = jnp.zeros((KC, QB), F32)
        for h in range(IDX_HEADS):
            s = s + jnp.maximum(_dot(ikc, iqa_ref[h]), 0.0) * iw[h:h + 1, :]
        kpos = k0 + lax.broadcasted_iota(jnp.int32, (KC, 1), 0)
        sc_ref[pl.ds(k0, KC), :] = jnp.where(kpos <= qpos, s, -jnp.inf)
        return carry

    lax.fori_loop(0, nk, p1, 0)

    def count(pred):
        def body(c, acc):
            k0 = pl.multiple_of(c * KC, KC)
            ind = jnp.where(pred(sc_ref[pl.ds(k0, KC), :]), 1.0, 0.0)
            return acc + jnp.sum(ind.reshape(KC // 8, 8, QB), axis=0)
        acc = lax.fori_loop(0, nk, body, jnp.zeros((8, QB), F32))
        return jnp.sum(acc, axis=0, keepdims=True)

    ukey = jnp.zeros((1, QB), jnp.int32)
    for bit in range(31, -1, -1):
        cand = ukey | np.int32(np.uint32(1 << bit).astype(np.int32))
        t = _ukey_to_float(cand)
        cnt = count(lambda s: jnp.logical_not(s < t))
        ukey = jnp.where(cnt >= k_top, cand, ukey)
    thr = _ukey_to_float(ukey)
    need = k_top - count(lambda s: s > thr)

    m_ref[...] = jnp.full(m_ref.shape, NEG_BIG, F32)
    l_ref[...] = jnp.zeros(l_ref.shape, F32)
    acc_ref[...] = jnp.zeros(acc_ref.shape, F32)
    r_i = lax.broadcasted_iota(jnp.int32, (KC, KC), 0)
    c_i = lax.broadcasted_iota(jnp.int32, (KC, KC), 1)
    tril = jnp.where(c_i <= r_i, 1.0, 0.0).astype(BF16)

    def p3(c, tie_seen):
        k0 = pl.multiple_of(c * KC, KC)
        s = sc_ref[pl.ds(k0, KC), :]
        eq = s == thr
        eqf = jnp.where(eq, 1.0, 0.0)
        incl = _dot(tril, eqf.astype(BF16))
        rank = tie_seen + incl - eqf
        kpos = k0 + lax.broadcasted_iota(jnp.int32, (KC, 1), 0)
        sel = jnp.logical_and(jnp.logical_or(s > thr, jnp.logical_and(eq, rank < need)), kpos <= qpos)
        bias = jnp.where(sel, 0.0, NEG_BIG)
        bias4 = jnp.concatenate([bias] * GROUP, axis=1)
        for j in range(N_KV_HEADS):
            kj = k_ref[pl.ds(k0, KC), j * LANE:(j + 1) * LANE]
            st = _dot(kj, qa_ref[j]) + bias4
            m_old = m_ref[j]
            m_new = jnp.maximum(m_old, jnp.max(st, axis=0, keepdims=True))
            alpha = jnp.exp(m_old - m_new)
            p = jnp.exp(st - m_new)
            l_ref[j] = alpha * l_ref[j] + jnp.sum(p, axis=0, keepdims=True)
            pv = _dot(vT_ref[c, j * HEAD_DIM:(j + 1) * HEAD_DIM, :], p.astype(BF16))
            acc_ref[j] = acc_ref[j] * alpha + pv
            m_ref[j] = m_new
        return tie_seen + incl[KC - 1:KC, :]

    lax.fori_loop(0, nk, p3, jnp.zeros((1, QB), F32))

    for j in range(N_KV_HEADS):
        o = acc_ref[j] / l_ref[j]
        for g in range(GROUP):
            hh = j * GROUP + g
            oT_ref[hh * HEAD_DIM:(hh + 1) * HEAD_DIM, :] = o[:, g * QB:(g + 1) * QB].astype(BF16)


def _dsa_core(qT, k, vT, iqT, ik, iwT):
    B, QW, S = qT.shape
    KC, QB = DSA_KC, DSA_QB
    k_top = min(TOPK_MAX, S // 4)
    nc = S // KC
    return pl.pallas_call(
        functools.partial(_dsa_core_kernel, k_top=k_top),
        grid=(B, S // QB),
        in_specs=[
            pl.BlockSpec((None, IDX_HEADS * IDX_DIM, QB), lambda b, i: (b, 0, i)),
            pl.BlockSpec((None, IDX_HEADS, QB), lambda b, i: (b, 0, i)),
            pl.BlockSpec((None, QW, QB), lambda b, i: (b, 0, i)),
            pl.BlockSpec((None, S, LANE), lambda b, i: (b, 0, 0)),
            pl.BlockSpec((None, S, N_KV_HEADS * LANE), lambda b, i: (b, 0, 0)),
            pl.BlockSpec((None, nc, N_KV_HEADS * HEAD_DIM, KC), lambda b, i: (b, 0, 0, 0)),
        ],
        out_specs=pl.BlockSpec((None, QW, QB), lambda b, i: (b, 0, i)),
        out_shape=jax.ShapeDtypeStruct((B, QW, S), BF16),
        scratch_shapes=[
            pltpu.VMEM((S, QB), F32),
            pltpu.VMEM((N_KV_HEADS, LANE, GROUP * QB), BF16),
            pltpu.VMEM((IDX_HEADS, LANE, QB), BF16),
            pltpu.VMEM((N_KV_HEADS, 1, GROUP * QB), F32),
            pltpu.VMEM((N_KV_HEADS, 1, GROUP * QB), F32),
            pltpu.VMEM((N_KV_HEADS, HEAD_DIM, GROUP * QB), F32),
        ],
        compiler_params=_cparams("parallel", "arbitrary"),
    )(iqT, iwT, qT, ik, k, vT)


def _out_T_kernel(xT_ref, w_ref, h_ref, gt_ref, o_ref):
    o_ref[...] = h_ref[...] + gt_ref[...] * _dot_tn(xT_ref[...], w_ref[...])


def _dsa_out(oT, w_out, h, mod, tm=512):
    B, S, D = h.shape
    QW = oT.shape[1]
    return pl.pallas_call(
        _out_T_kernel,
        grid=(B, S // tm),
        in_specs=[
            pl.BlockSpec((None, QW, tm), lambda b, i: (b, 0, i)),
            pl.BlockSpec((QW, D), lambda b, i: (0, 0)),
            pl.BlockSpec((None, tm, D), lambda b, i: (b, i, 0)),
            pl.BlockSpec((None, None, 1, D), lambda b, i: (b, 5, 0, 0)),
        ],
        out_specs=pl.BlockSpec((None, tm, D), lambda b, i: (b, i, 0)),
        out_shape=jax.ShapeDtypeStruct((B, S, D), F32),
        compiler_params=_cparams("parallel", "parallel"),
    )(oT, w_out.astype(BF16), h, mod)


def _dsa_mixer(h, mod, w_in, q_norm, k_norm, ik_norm, w_out):
    qT, k, vT, iqT, ik, iwT = _dsa_in(h, mod, w_in, q_norm, k_norm, ik_norm)
    oT = _dsa_core(qT, k, vT, iqT, ik, iwT)
    return _dsa_out(oT, w_out, h, mod)


RWKV_HEAD = 64
GN_EPS = 64e-5
RW_PACK = 4
RW_LANES = RW_PACK * RWKV_HEAD
RW_C = 64
RW_TB = 256


def _split3(x):
    hi = x.astype(BF16)
    r1 = x - hi.astype(F32)
    mid = r1.astype(BF16)
    lo = (r1 - mid.astype(F32)).astype(BF16)
    return hi, mid, lo


def _dot3(x, w):
    hi, mid, lo = _split3(x)
    return _dot(hi, w) + (_dot(mid, w) + _dot(lo, w))


def _gsum(x, g_ref, gt_ref):
    return _dot3(_dot3(x, g_ref[...]), gt_ref[...])


def _rwkv_in_kernel(h_ref, hp_ref, sh_ref, sc_ref, mu_ref, wrkv_ref, w0_ref, w1_ref, w2_ref,
                    a0_ref, a1_ref, a2_ref, g1_ref, g2_ref, kk_ref, ka_ref, gm_ref, gmt_ref,
                    r_ref, lw_ref, k_ref, v_ref, kkn_ref, b_ref, g_ref):
    sh, sc = sh_ref[...], sc_ref[...]
    u = _modulate(h_ref[...], sh, sc)
    tm = u.shape[0]
    prev = _modulate(hp_ref[...], sh, sc)[7:8, :]
    prev = jnp.where(pl.program_id(1) == 0, 0.0, prev)
    row = lax.broadcasted_iota(jnp.int32, (tm, 1), 0)
    xx = jnp.where(row == 0, prev, pltpu.roll(u, 1, 0)) - u
    mix = lambda i: (u + xx * mu_ref[i:i + 1, :]).astype(BF16)
    r = _dot(mix(0), wrkv_ref[0])
    k = _dot(mix(1), wrkv_ref[1])
    v = _dot(mix(2), wrkv_ref[2])
    wl = w0_ref[...] + _dot(jnp.tanh(_dot(mix(3), w1_ref[...])).astype(BF16), w2_ref[...])
    nwl = -wl
    w_log = -(jnp.maximum(nwl, 0.0) + jnp.log(1.0 + jnp.exp(-jnp.abs(nwl)))) - 0.5
    lw_ref[...] = -jnp.exp(w_log)
    a = jax.nn.sigmoid(a0_ref[...] + _dot(_dot(mix(4), a1_ref[...]).astype(BF16), a2_ref[...]))
    g_ref[...] = _dot(jax.nn.sigmoid(_dot(mix(5), g1_ref[...])).astype(BF16), g2_ref[...])
    kk = k * kk_ref[...]
    nrm = jnp.sqrt(_gsum(kk * kk, gm_ref, gmt_ref))
    kk = kk / jnp.maximum(nrm, 1e-12)
    r_ref[...] = r
    k_ref[...] = k * (1.0 + (a - 1.0) * ka_ref[...])
    v_ref[...] = v
    kkn_ref[...] = kk
    b_ref[...] = kk * a


def _head_indicator(D):
    gm = (np.arange(D)[:, None] // RWKV_HEAD == np.arange(D // RWKV_HEAD)[None, :]).astype(np.float32)
    return jnp.asarray(gm, BF16), jnp.asarray(gm.T, BF16)


def _rwkv_in(h, mod, mu, w_rkv, w0, w1, w2, a0, a1, a2, g1, g2, k_k, k_a, tm=256):
    B, S, D = h.shape
    gm, gmt = _head_indicator(D)
    row = lambda a: a.astype(F32).reshape(1, D)
    full = lambda a: pl.BlockSpec(a.shape, lambda b, i: (0,) * a.ndim)
    mspec = lambda k: pl.BlockSpec((None, None, 1, D), lambda b, i: (b, k, 0, 0))
    tile = pl.BlockSpec((None, tm, D), lambda b, i: (b, i, 0))
    params = [mu.astype(F32), w_rkv.astype(BF16), row(w0), w1.astype(BF16), w2.astype(BF16),
              row(a0), a1.astype(BF16), a2.astype(BF16), g1.astype(BF16), g2.astype(BF16),
              row(k_k), row(k_a), gm, gmt]
    return pl.pallas_call(
        _rwkv_in_kernel,
        grid=(B, S // tm),
        in_specs=[tile,
                  pl.BlockSpec((None, 8, D), lambda b, i: (b, jnp.maximum(i * (tm // 8) - 1, 0), 0)),
                  mspec(3), mspec(4)] + [full(p) for p in params],
        out_specs=[tile] * 7,
        out_shape=[jax.ShapeDtypeStruct((B, S, D), F32)] * 7,
        compiler_params=_cparams("parallel", "parallel"),
    )(h, h, mod, mod, *params)


def _rwkv_scan_kernel(r_ref, lw_ref, k_ref, v_ref, kk_ref, b_ref, y_ref, zt_ref):
    C, P, L = RW_C, RW_PACK, RW_LANES
    N = P * C

    @pl.when(pl.program_id(2) == 0)
    def _():
        zt_ref[...] = jnp.zeros(zt_ref.shape, F32)

    lane_head = lax.broadcasted_iota(jnp.int32, (1, L), 1) // RWKV_HEAD
    hmask = [lane_head == hd for hd in range(P)]
    stack = lambda x: jnp.concatenate([jnp.where(hmask[hd], x, 0.0) for hd in range(P)], axis=0)
    ri = lax.broadcasted_iota(jnp.int32, (N, N), 0)
    ci = lax.broadcasted_iota(jnp.int32, (N, N), 1)
    low_strict = (ri % C) > (ci % C)
    low_incl = (ri % C) >= (ci % C)
    eye = jnp.where(ri == ci, 1.0, 0.0)
    tri = jnp.where(lax.broadcasted_iota(jnp.int32, (C, C), 0) >= lax.broadcasted_iota(jnp.int32, (C, C), 1),
                    1.0, 0.0).astype(BF16)
    bf = lambda x: x.astype(BF16)

    for c in range(RW_TB // C):
        rows = slice(c * C, (c + 1) * C)
        lw = lw_ref[rows, :]
        lw_hi, lw_mid, lw_lo = _split3(lw)
        cw = _dot(tri, lw_hi) + (_dot(tri, lw_mid) + _dot(tri, lw_lo))
        w_incl = jnp.exp(cw)
        w_excl = jnp.exp(cw - lw)
        w_inv = jnp.exp(-cw)
        w_end = w_incl[C - 1:C, :]
        al = stack(-kk_ref[rows, :] * w_excl)
        rt = stack(r_ref[rows, :] * w_incl)
        bh = b_ref[rows, :] * w_inv
        kh = k_ref[rows, :] * w_inv
        vs = bf(stack(v_ref[rows, :]))
        lhs = bf(jnp.concatenate([al, rt], axis=0))
        rhs = bf(jnp.concatenate([stack(bh), stack(kh)], axis=0))
        aa = _dot_nt(lhs, rhs)
        a_ab = jnp.where(low_strict, aa[:N, :N], 0.0)
        a_ak = jnp.where(low_strict, aa[:N, N:], 0.0)
        a_rb = bf(jnp.where(low_incl, aa[N:, :N], 0.0))
        a_rk = bf(jnp.where(low_incl, aa[N:, N:], 0.0))
        pw = a_ab
        t = eye + pw
        for _ in range(int(np.log2(C)) - 1):
            pwb = bf(pw)
            pw = _dot(pwb, pwb)
            t = t + _dot(bf(t), bf(pw))
        tb = bf(t)
        pm = _dot(tb, bf(al))
        q = _dot(tb, bf(_dot(bf(a_ak), vs)))
        rp = rt + _dot(a_rb, bf(pm))
        y0 = _dot(a_rb, bf(q)) + _dot(a_rk, vs)
        ztb = bf(zt_ref[...])
        u = _dot_nt(bf(pm), ztb) + q
        ys = _dot_nt(bf(rp), ztb) + y0
        y = ys[0:C]
        for hd in range(1, P):
            y = y + ys[hd * C:(hd + 1) * C]
        y_ref[rows, :] = y
        uv = jnp.concatenate([bf(u), vs], axis=0)
        bk = bf(jnp.concatenate([stack(bh * w_end), stack(kh * w_end)], axis=0))
        zt_ref[...] = zt_ref[...] * w_end + _dot_tn(uv, bk)


def _rwkv_scan(r, lw, k, v, kk, b):
    B, S, D = r.shape
    tile = pl.BlockSpec((None, RW_TB, RW_LANES), lambda bb, p, t: (bb, t, p))
    return pl.pallas_call(
        _rwkv_scan_kernel,
        grid=(B, D // RW_LANES, S // RW_TB),
        in_specs=[tile] * 6,
        out_specs=tile,
        out_shape=jax.ShapeDtypeStruct((B, S, D), F32),
        scratch_shapes=[pltpu.VMEM((RW_LANES, RW_LANES), F32)],
        compiler_params=_cparams("parallel", "parallel", "arbitrary"),
    )(r, lw, k, v, kk, b)


def _rwkv_out_kernel(y_ref, r_ref, k_ref, v_ref, g_ref, lnw_ref, lnb_ref, rk_ref, gm_ref, gmt_ref,
                     w_ref, h_ref, gt_ref, o_ref):
    y = y_ref[...]
    inv_n = 1.0 / RWKV_HEAD
    yc = y - _gsum(y, gm_ref, gmt_ref) * inv_n
    var = _gsum(yc * yc, gm_ref, gmt_ref) * inv_n
    yn = yc * lax.rsqrt(var + GN_EPS) * lnw_ref[...] + lnb_ref[...]
    v = v_ref[...]
    bonus = _gsum(r_ref[...] * k_ref[...] * rk_ref[...], gm_ref, gmt_ref) * v
    out = ((yn + bonus) * g_ref[...]).astype(BF16)
    o_ref[...] = h_ref[...] + gt_ref[...] * _dot(out, w_ref[...])


def _rwkv_out(y, r, k, v, g, ln_w, ln_b, r_k, w_out, h, mod, tm=256):
    B, S, D = h.shape
    gm, gmt = _head_indicator(D)
    row = lambda a: a.astype(F32).reshape(1, D)
    full = lambda a: pl.BlockSpec(a.shape, lambda b, i: (0,) * a.ndim)
    tile = pl.BlockSpec((None, tm, D), lambda b, i: (b, i, 0))
    params = [row(ln_w), row(ln_b), row(r_k), gm, gmt, w_out.astype(BF16)]
    return pl.pallas_call(
        _rwkv_out_kernel,
        grid=(B, S // tm),
        in_specs=[tile] * 5 + [full(p) for p in params] + [tile, pl.BlockSpec((None, None, 1, D), lambda b, i: (b, 5, 0, 0))],
        out_specs=tile,
        out_shape=jax.ShapeDtypeStruct((B, S, D), F32),
        compiler_params=_cparams("parallel", "parallel"),
    )(y, r, k, v, g, *params, h, mod)


def _rwkv_mixer(h, mod, mu, w_rkv, w0, w1, w2, a0, a1, a2, g1, g2, k_k, k_a, r_k, ln_w, ln_b, w_out):
    r, lw, k, v, kk, b, g = _rwkv_in(h, mod, mu, w_rkv, w0, w1, w2, a0, a1, a2, g1, g2, k_k, k_a)
    y = _rwkv_scan(r, lw, k, v, kk, b)
    return _rwkv_out(y, r, k, v, g, ln_w, ln_b, r_k, w_out, h, mod)


def kernel(x, c, ada_w, ada_b, ffn_w_gu, ffn_w_down, dsa_w_in, dsa_q_norm, dsa_k_norm, dsa_ik_norm, dsa_w_out, rwkv_mu, rwkv_w_rkv, rwkv_w0, rwkv_w1, rwkv_w2, rwkv_a0, rwkv_a1, rwkv_a2, rwkv_g1, rwkv_g2, rwkv_k_k, rwkv_k_a, rwkv_r_k, rwkv_ln_w, rwkv_ln_b, rwkv_w_out):
    B, S, D = x.shape
    depth = ada_w.shape[0]
    mods = _ada_mod(c, ada_w, ada_b)
    h = x
    for i in range(depth):
        mod = mods[i].reshape(B, N_MOD, 1, D)
        j = i // 2
        h = _ffn(h, mod, 0, ffn_w_gu[i, 0].astype(BF16), ffn_w_down[i, 0].astype(BF16))
        if i % 2 == 0:
            h = _dsa_mixer(h, mod, dsa_w_in[j], dsa_q_norm[j], dsa_k_norm[j], dsa_ik_norm[j], dsa_w_out[j])
        else:
            h = _rwkv_mixer(h, mod, rwkv_mu[j], rwkv_w_rkv[j], rwkv_w0[j], rwkv_w1[j], rwkv_w2[j], rwkv_a0[j],
                            rwkv_a1[j], rwkv_a2[j], rwkv_g1[j], rwkv_g2[j], rwkv_k_k[j], rwkv_k_a[j],
                            rwkv_r_k[j], rwkv_ln_w[j], rwkv_ln_b[j], rwkv_w_out[j])
        h = _ffn(h, mod, 6, ffn_w_gu[i, 1].astype(BF16), ffn_w_down[i, 1].astype(BF16))
    return h
```

```python
import functools

import jax
import jax.numpy as jnp
import numpy as np
from jax import lax
from jax.experimental import pallas as pl
from jax.experimental.pallas import tpu as pltpu

F32 = jnp.float32
BF16 = jnp.bfloat16

RMS_EPS = 1e-6
N_MOD = 9
VMEM_LIMIT_BYTES = 56 * 1024 * 1024

N_HEADS = 16
HEAD_DIM = 64
N_KV_HEADS = 4
GROUP = N_HEADS // N_KV_HEADS
IDX_HEADS = 8
IDX_DIM = 64
TOPK_MAX = 256
LANE = 128
LOG2E = 1.4426950408889634


def _cparams(*sem):
    return pltpu.CompilerParams(dimension_semantics=sem, vmem_limit_bytes=VMEM_LIMIT_BYTES)


def _dot(a, b):
    return jnp.dot(a, b, preferred_element_type=F32)


def _dot_nt(a, b):
    return lax.dot_general(a, b, (((1,), (1,)), ((), ())), preferred_element_type=F32)


def _dot_tn(a, b):
    return lax.dot_general(a, b, (((0,), (0,)), ((), ())), preferred_element_type=F32)


def _modulate(h, sh, sc):
    ms = jnp.mean(h * h, axis=-1, keepdims=True)
    return h * lax.rsqrt(ms + RMS_EPS) * (1.0 + sc) + sh


def _ada_kernel(c_ref, w_ref, b_ref, o_ref):
    c = c_ref[...]
    ca = (c * jax.nn.sigmoid(c)).astype(BF16)
    o_ref[...] = _dot(ca, w_ref[...].astype(BF16)) + b_ref[...]


def _ada_mod(c, ada_w, ada_b):
    L, D, N = ada_w.shape
    B = c.shape[0]
    tn = 1536
    return pl.pallas_call(
        _ada_kernel,
        grid=(L, N // tn),
        in_specs=[
            pl.BlockSpec((B, D), lambda l, j: (0, 0)),
            pl.BlockSpec((None, D, tn), lambda l, j: (l, 0, j)),
            pl.BlockSpec((None, 1, tn), lambda l, j: (l, 0, j)),
        ],
        out_specs=pl.BlockSpec((None, B, tn), lambda l, j: (l, 0, j)),
        out_shape=jax.ShapeDtypeStruct((L, B, N), F32),
        compiler_params=_cparams("parallel", "parallel"),
    )(c, ada_w, ada_b.reshape(L, 1, N))


FF_CHUNK = 256


def _ffn_kernel(h_ref, sh_ref, sc_ref, gt_ref, wgu_ref, wd_ref, o_ref, *, d_ff):
    h = h_ref[...]
    u = _modulate(h, sh_ref[...], sc_ref[...]).astype(BF16)
    acc = None
    for c in range(d_ff // FF_CHUNK):
        lo = c * FF_CHUNK
        g = _dot(u, wgu_ref[:, lo:lo + FF_CHUNK])
        up = _dot(u, wgu_ref[:, d_ff + lo:d_ff + lo + FF_CHUNK])
        a = (g * jax.nn.sigmoid(g) * up).astype(BF16)
        part = _dot(a, wd_ref[lo:lo + FF_CHUNK, :])
        acc = part if acc is None else acc + part
    o_ref[...] = h + (0.5 * gt_ref[...]) * acc


def _ffn(h, mod, k0, w_gu, w_down, tm=512):
    B, S, D = h.shape
    d_ff = w_down.shape[0]
    mspec = lambda k: pl.BlockSpec((None, None, 1, D), lambda b, i: (b, k, 0, 0))
    return pl.pallas_call(
        functools.partial(_ffn_kernel, d_ff=d_ff),
        grid=(B, S // tm),
        in_specs=[
            pl.BlockSpec((None, tm, D), lambda b, i: (b, i, 0)),
            mspec(k0), mspec(k0 + 1), mspec(k0 + 2),
            pl.BlockSpec((D, 2 * d_ff), lambda b, i: (0, 0), pipeline_mode=pl.Buffered(1)),
            pl.BlockSpec((d_ff, D), lambda b, i: (0, 0), pipeline_mode=pl.Buffered(1)),
        ],
        out_specs=pl.BlockSpec((None, tm, D), lambda b, i: (b, i, 0)),
        out_shape=jax.ShapeDtypeStruct((B, S, D), F32),
        compiler_params=_cparams("parallel", "parallel"),
    )(h, mod, mod, mod, w_gu, w_down)


def _dsa_in_kernel(h_ref, sh_ref, sc_ref, wqT_ref, wk_ref, wvT_ref, wiqT_ref, wik_ref, wiwT_ref,
                   qn_ref, kn_ref, ikn_ref,
                   qT_ref, k_ref, vT_ref, iqT_ref, ik_ref, iwT_ref):
    u = _modulate(h_ref[...], sh_ref[...], sc_ref[...]).astype(BF16)
    tm = u.shape[0]
    qT = _dot_nt(wqT_ref[...], u).reshape(N_HEADS, HEAD_DIM, tm)
    ms = jnp.mean(qT * qT, axis=1, keepdims=True)
    qT = qT * lax.rsqrt(ms + RMS_EPS) * qn_ref[...][None] * (HEAD_DIM ** -0.5 * LOG2E)
    qT_ref[...] = qT.reshape(N_HEADS * HEAD_DIM, tm).astype(BF16)
    k = _dot(u, wk_ref[...])
    kn = kn_ref[...]
    for j in range(N_KV_HEADS):
        kj = k[:, j * LANE:(j + 1) * LANE]
        ms = jnp.sum(kj * kj, axis=-1, keepdims=True) * (1.0 / HEAD_DIM)
        k_ref[:, j * LANE:(j + 1) * LANE] = (kj * lax.rsqrt(ms + RMS_EPS) * kn).astype(BF16)
    vT_ref[...] = _dot_nt(wvT_ref[...], u).astype(BF16)
    iqT_ref[...] = _dot_nt(wiqT_ref[...], u).astype(BF16)
    ik = _dot(u, wik_ref[...])
    ms = jnp.sum(ik * ik, axis=-1, keepdims=True) * (1.0 / IDX_DIM)
    ik_ref[...] = (ik * lax.rsqrt(ms + RMS_EPS) * ikn_ref[...]).astype(BF16)
    iwT_ref[...] = _dot_nt(wiwT_ref[...], u) * (IDX_HEADS ** -0.5 * IDX_DIM ** -0.5)


DSA_KC = 256
DSA_QB = 128


def _dsa_in(h, mod, w_in, q_norm, k_norm, ik_norm, tm=DSA_KC):
    B, S, D = h.shape
    QW, KW = N_HEADS * HEAD_DIM, N_KV_HEADS * HEAD_DIM
    IQW = IDX_HEADS * IDX_DIM
    o1, o2, o3, o4, o5 = QW, QW + KW, QW + 2 * KW, QW + 2 * KW + IQW, QW + 2 * KW + IQW + IDX_DIM
    wb = w_in.astype(BF16)
    wqT = wb[:, :o1].T
    wk = jnp.pad(wb[:, o1:o2].reshape(D, N_KV_HEADS, HEAD_DIM), ((0, 0), (0, 0), (0, LANE - HEAD_DIM))).reshape(D, N_KV_HEADS * LANE)
    wvT = wb[:, o2:o3].T
    wiqT = wb[:, o3:o4].T
    wik = jnp.pad(wb[:, o4:o5], ((0, 0), (0, LANE - IDX_DIM)))
    wiwT = wb[:, o5:].T
    qn = q_norm.astype(F32).reshape(HEAD_DIM, 1)
    kn = jnp.pad(k_norm.astype(F32), (0, LANE - HEAD_DIM)).reshape(1, LANE)
    ikn = jnp.pad(ik_norm.astype(F32), (0, LANE - IDX_DIM)).reshape(1, LANE)
    mspec = lambda k: pl.BlockSpec((None, None, 1, D), lambda b, i: (b, k, 0, 0))
    full = lambda a: pl.BlockSpec(a.shape, lambda b, i: (0,) * a.ndim)
    nc = S // tm
    return pl.pallas_call(
        _dsa_in_kernel,
        grid=(B, nc),
        in_specs=[pl.BlockSpec((None, tm, D), lambda b, i: (b, i, 0)), mspec(3), mspec(4),
                  full(wqT), full(wk), full(wvT), full(wiqT), full(wik), full(wiwT),
                  full(qn), full(kn), full(ikn)],
        out_specs=[
            pl.BlockSpec((None, QW, tm), lambda b, i: (b, 0, i)),
            pl.BlockSpec((None, tm, N_KV_HEADS * LANE), lambda b, i: (b, i, 0)),
            pl.BlockSpec((None, None, KW, tm), lambda b, i: (b, i, 0, 0)),
            pl.BlockSpec((None, IQW, tm), lambda b, i: (b, 0, i)),
            pl.BlockSpec((None, tm, LANE), lambda b, i: (b, i, 0)),
            pl.BlockSpec((None, IDX_HEADS, tm), lambda b, i: (b, 0, i)),
        ],
        out_shape=[
            jax.ShapeDtypeStruct((B, QW, S), BF16),
            jax.ShapeDtypeStruct((B, S, N_KV_HEADS * LANE), BF16),
            jax.ShapeDtypeStruct((B, nc, KW, tm), BF16),
            jax.ShapeDtypeStruct((B, IQW, S), BF16),
            jax.ShapeDtypeStruct((B, S, LANE), BF16),
            jax.ShapeDtypeStruct((B, IDX_HEADS, S), F32),
        ],
        compiler_params=_cparams("parallel", "parallel"),
    )(h, mod, mod, wqT, wk, wvT, wiqT, wik, wiwT, qn, kn, ikn)


NEG_BIG = -(2.0 ** 100)
DSA_VPAD = 16


def _ukey_to_float(ukey):
    key = ukey ^ np.int32(-2 ** 31)
    bits = key ^ ((key >> 31) & np.int32(0x7FFFFFFF))
    return lax.bitcast_convert_type(bits, F32)


def _fold_rows(x, op, rows):
    while x.shape[0] > rows:
        half = x.shape[0] // 2
        x = op(x[:half], x[half:])
    return x


def _dsa_core_kernel(iqT_ref, iwT_ref, qT_ref, ik_ref, k_ref, vT_ref, oT_ref,
                     sc_ref, qa_ref, iqa_ref, d_ref, st_ref, p_ref, m_ref, acc_ref, *, k_top, n_chunks):
    KC, QB = DSA_KC, DSA_QB
    qi = pl.program_id(1)
    q0 = qi * QB
    nk = (q0 + QB + KC - 1) // KC
    qpos = q0 + lax.broadcasted_iota(jnp.int32, (1, QB), 1)
    zeros_h = jnp.zeros((LANE - HEAD_DIM, QB), BF16)

    for h in range(IDX_HEADS):
        iqa_ref[:, h * QB:(h + 1) * QB] = jnp.concatenate(
            [iqT_ref[h * IDX_DIM:(h + 1) * IDX_DIM, :], zeros_h], axis=0)
    for j in range(N_KV_HEADS):
        for g in range(GROUP):
            hh = j * GROUP + g
            qa_ref[j, :, g * QB:(g + 1) * QB] = jnp.concatenate(
                [qT_ref[hh * HEAD_DIM:(hh + 1) * HEAD_DIM, :], zeros_h], axis=0)
    iw = iwT_ref[...]

    npair = (nk + 1) // 2

    def clamp_chunk(c):
        return jnp.minimum(c, n_chunks - 1)

    def idx_dot(c, slot):
        k0 = pl.multiple_of(c * KC, KC)
        d_ref[slot] = _dot(ik_ref[pl.ds(k0, KC), :], iqa_ref[...])

    def idx_score(c, slot):
        k0 = pl.multiple_of(c * KC, KC)
        s = jnp.maximum(d_ref[slot, :, 0:QB], 0.0) * iw[0:1, :]
        for h in range(1, IDX_HEADS):
            s = s + jnp.maximum(d_ref[slot, :, h * QB:(h + 1) * QB], 0.0) * iw[h:h + 1, :]
        kpos = k0 + lax.broadcasted_iota(jnp.int32, (KC, 1), 0)
        sc_ref[pl.ds(k0, KC), :] = jnp.where(kpos <= qpos, s, -jnp.inf)

    idx_dot(0, 0)

    def p1(i, carry):
        c = 2 * i
        idx_dot(c + 1, 1)
        idx_score(c, 0)
        idx_dot(clamp_chunk(c + 2), 0)
        idx_score(c + 1, 1)
        return carry

    lax.fori_loop(0, npair, p1, 0)

    def count_not(pred):
        def body(c, acc):
            k0 = pl.multiple_of(c * KC, KC)
            ind = jnp.where(pred(sc_ref[pl.ds(k0, KC), :]), 0.0, 1.0)
            return acc + _fold_rows(ind, jnp.add, 8)
        acc = lax.fori_loop(0, nk, body, jnp.zeros((8, QB), F32))
        return jnp.sum(acc, axis=0, keepdims=True)

    ukey = jnp.zeros((1, QB), jnp.int32)
    for bit in range(31, -1, -1):
        cand = ukey | np.int32(np.uint32(1 << bit).astype(np.int32))
        t = _ukey_to_float(cand)
        cnt = count_not(lambda s: s < t)
        ukey = jnp.where(cnt >= k_top, cand, ukey)
    thr = _ukey_to_float(ukey)
    n_staged = (nk * KC).astype(F32)
    need = k_top - (n_staged - count_not(lambda s: s > thr))

    m_ref[...] = jnp.full(m_ref.shape, NEG_BIG, F32)
    acc_ref[...] = jnp.zeros(acc_ref.shape, F32)
    r_i = lax.broadcasted_iota(jnp.int32, (KC, KC), 0)
    c_i = lax.broadcasted_iota(jnp.int32, (KC, KC), 1)
    tril = jnp.where(c_i <= r_i, 1.0, 0.0).astype(BF16)
    ones_rows = jnp.ones((DSA_VPAD, KC), BF16)

    def logits(c, slot):
        k0 = pl.multiple_of(c * KC, KC)
        for j in range(N_KV_HEADS):
            st_ref[slot, j] = _dot(k_ref[pl.ds(k0, KC), j * LANE:(j + 1) * LANE], qa_ref[j]).astype(BF16)

    logits(0, 0)

    def attend(c, slot, tie_seen):
        k0 = pl.multiple_of(c * KC, KC)
        s = sc_ref[pl.ds(k0, KC), :]
        eq = s == thr
        eqf = jnp.where(eq, 1.0, 0.0)
        incl = _dot(tril, eqf.astype(BF16))
        rank = tie_seen + incl - eqf
        kpos = k0 + lax.broadcasted_iota(jnp.int32, (KC, 1), 0)
        sel = jnp.logical_and(jnp.logical_or(s > thr, jnp.logical_and(eq, rank < need)), kpos <= qpos)
        bias = jnp.where(sel, 0.0, NEG_BIG).astype(BF16)
        for j in range(N_KV_HEADS):
            alphas = []
            for g in range(GROUP):
                sl = slice(g * QB, (g + 1) * QB)
                sg = st_ref[slot, j, :, sl] + bias
                m_old = m_ref[j, :, sl]
                mx = jnp.max(_fold_rows(sg, jnp.maximum, 16), axis=0, keepdims=True).astype(F32)
                m_new = jnp.maximum(m_old, mx)
                alphas.append(jnp.exp2(m_old - m_new))
                m_ref[j, :, sl] = m_new
                p_ref[j, :, sl] = jnp.exp2(sg - m_new.astype(BF16))
            v_aug = jnp.concatenate([vT_ref[c, j * HEAD_DIM:(j + 1) * HEAD_DIM, :], ones_rows], axis=0)
            acc_ref[j] = acc_ref[j] * jnp.concatenate(alphas, axis=1) + _dot(v_aug, p_ref[j])
        return tie_seen + incl[KC - 1:KC, :]

    def p3(i, tie_seen):
        c = 2 * i
        logits(c + 1, 1)
        tie_seen = attend(c, 0, tie_seen)
        logits(clamp_chunk(c + 2), 0)
        return attend(c + 1, 1, tie_seen)

    lax.fori_loop(0, npair, p3, jnp.zeros((1, QB), F32))

    for j in range(N_KV_HEADS):
        o = acc_ref[j, 0:HEAD_DIM, :] / acc_ref[j, HEAD_DIM:HEAD_DIM + 1, :]
        for g in range(GROUP):
            hh = j * GROUP + g
            oT_ref[hh * HEAD_DIM:(hh + 1) * HEAD_DIM, :] = o[:, g * QB:(g + 1) * QB].astype(BF16)


def _dsa_core(qT, k, vT, iqT, ik, iwT):
    B, QW, S = qT.shape
    KC, QB = DSA_KC, DSA_QB
    k_top = min(TOPK_MAX, S // 4)
    nc = S // KC
    return pl.pallas_call(
        functools.partial(_dsa_core_kernel, k_top=k_top, n_chunks=nc),
        grid=(B, S // QB),
        in_specs=[
            pl.BlockSpec((None, IDX_HEADS * IDX_DIM, QB), lambda b, i: (b, 0, i)),
            pl.BlockSpec((None, IDX_HEADS, QB), lambda b, i: (b, 0, i)),
            pl.BlockSpec((None, QW, QB), lambda b, i: (b, 0, i)),
            pl.BlockSpec((None, S, LANE), lambda b, i: (b, 0, 0)),
            pl.BlockSpec((None, S, N_KV_HEADS * LANE), lambda b, i: (b, 0, 0)),
            pl.BlockSpec((None, nc, N_KV_HEADS * HEAD_DIM, KC), lambda b, i: (b, 0, 0, 0)),
        ],
        out_specs=pl.BlockSpec((None, QW, QB), lambda b, i: (b, 0, i)),
        out_shape=jax.ShapeDtypeStruct((B, QW, S), BF16),
        scratch_shapes=[
            pltpu.VMEM((S, QB), F32),
            pltpu.VMEM((N_KV_HEADS, LANE, GROUP * QB), BF16),
            pltpu.VMEM((LANE, IDX_HEADS * QB), BF16),
            pltpu.VMEM((2, KC, IDX_HEADS * QB), F32),
            pltpu.VMEM((2, N_KV_HEADS, KC, GROUP * QB), BF16),
            pltpu.VMEM((N_KV_HEADS, KC, GROUP * QB), BF16),
            pltpu.VMEM((N_KV_HEADS, 1, GROUP * QB), F32),
            pltpu.VMEM((N_KV_HEADS, HEAD_DIM + DSA_VPAD, GROUP * QB), F32),
        ],
        compiler_params=_cparams("parallel", "arbitrary"),
    )(iqT, iwT, qT, ik, k, vT)


def _out_T_kernel(xT_ref, w_ref, h_ref, gt_ref, o_ref):
    o_ref[...] = h_ref[...] + gt_ref[...] * _dot_tn(xT_ref[...], w_ref[...])


def _dsa_out(oT, w_out, h, mod, tm=512):
    B, S, D = h.shape
    QW = oT.shape[1]
    return pl.pallas_call(
        _out_T_kernel,
        grid=(B, S // tm),
        in_specs=[
            pl.BlockSpec((None, QW, tm), lambda b, i: (b, 0, i)),
            pl.BlockSpec((QW, D), lambda b, i: (0, 0)),
            pl.BlockSpec((None, tm, D), lambda b, i: (b, i, 0)),
            pl.BlockSpec((None, None, 1, D), lambda b, i: (b, 5, 0, 0)),
        ],
        out_specs=pl.BlockSpec((None, tm, D), lambda b, i: (b, i, 0)),
        out_shape=jax.ShapeDtypeStruct((B, S, D), F32),
        compiler_params=_cparams("parallel", "parallel"),
    )(oT, w_out.astype(BF16), h, mod)


def _dsa_mixer(h, mod, w_in, q_norm, k_norm, ik_norm, w_out):
    qT, k, vT, iqT, ik, iwT = _dsa_in(h, mod, w_in, q_norm, k_norm, ik_norm)
    oT = _dsa_core(qT, k, vT, iqT, ik, iwT)
    return _dsa_out(oT, w_out, h, mod)


RWKV_HEAD = 64
GN_EPS = 64e-5
RW_PACK = 4
RW_LANES = RW_PACK * RWKV_HEAD
RW_C = 64
RW_TB = 256


def _split3(x):
    hi = x.astype(BF16)
    r1 = x - hi.astype(F32)
    mid = r1.astype(BF16)
    lo = (r1 - mid.astype(F32)).astype(BF16)
    return hi, mid, lo


def _dot3(x, w):
    hi, mid, lo = _split3(x)
    return _dot(hi, w) + (_dot(mid, w) + _dot(lo, w))


def _gsum(x, g_ref, gt_ref):
    return _dot3(_dot3(x, g_ref[...]), gt_ref[...])


def _rwkv_in_kernel(h_ref, hp_ref, sh_ref, sc_ref, mu_ref, wrkv_ref, w0_ref, w1_ref, w2_ref,
                    a0_ref, a1_ref, a2_ref, g1_ref, g2_ref, kk_ref, ka_ref, gm_ref, gmt_ref,
                    r_ref, lw_ref, k_ref, v_ref, kkn_ref, b_ref, g_ref):
    sh, sc = sh_ref[...], sc_ref[...]
    u = _modulate(h_ref[...], sh, sc)
    tm = u.shape[0]
    prev = _modulate(hp_ref[...], sh, sc)[7:8, :]
    prev = jnp.where(pl.program_id(1) == 0, 0.0, prev)
    row = lax.broadcasted_iota(jnp.int32, (tm, 1), 0)
    xx = jnp.where(row == 0, prev, pltpu.roll(u, 1, 0)) - u
    mix = lambda i: (u + xx * mu_ref[i:i + 1, :]).astype(BF16)
    r = _dot(mix(0), wrkv_ref[0])
    k = _dot(mix(1), wrkv_ref[1])
    v = _dot(mix(2), wrkv_ref[2])
    wl = w0_ref[...] + _dot(jnp.tanh(_dot(mix(3), w1_ref[...])).astype(BF16), w2_ref[...])
    nwl = -wl
    w_log = -(jnp.maximum(nwl, 0.0) + jnp.log(1.0 + jnp.exp(-jnp.abs(nwl)))) - 0.5
    lw_ref[...] = -jnp.exp(w_log)
    a = jax.nn.sigmoid(a0_ref[...] + _dot(_dot(mix(4), a1_ref[...]).astype(BF16), a2_ref[...]))
    g_ref[...] = _dot(jax.nn.sigmoid(_dot(mix(5), g1_ref[...])).astype(BF16), g2_ref[...])
    kk = k * kk_ref[...]
    nrm = jnp.sqrt(_gsum(kk * kk, gm_ref, gmt_ref))
    kk = kk / jnp.maximum(nrm, 1e-12)
    r_ref[...] = r
    k_ref[...] = k * (1.0 + (a - 1.0) * ka_ref[...])
    v_ref[...] = v
    kkn_ref[...] = kk
    b_ref[...] = kk * a


def _head_indicator(D):
    gm = (np.arange(D)[:, None] // RWKV_HEAD == np.arange(D // RWKV_HEAD)[None, :]).astype(np.float32)
    return jnp.asarray(gm, BF16), jnp.asarray(gm.T, BF16)


def _rwkv_in(h, mod, mu, w_rkv, w0, w1, w2, a0, a1, a2, g1, g2, k_k, k_a, tm=256):
    B, S, D = h.shape
    gm, gmt = _head_indicator(D)
    row = lambda a: a.astype(F32).reshape(1, D)
    full = lambda a: pl.BlockSpec(a.shape, lambda b, i: (0,) * a.ndim)
    mspec = lambda k: pl.BlockSpec((None, None, 1, D), lambda b, i: (b, k, 0, 0))
    tile = pl.BlockSpec((None, tm, D), lambda b, i: (b, i, 0))
    params = [mu.astype(F32), w_rkv.astype(BF16), row(w0), w1.astype(BF16), w2.astype(BF16),
              row(a0), a1.astype(BF16), a2.astype(BF16), g1.astype(BF16), g2.astype(BF16),
              row(k_k), row(k_a), gm, gmt]
    return pl.pallas_call(
        _rwkv_in_kernel,
        grid=(B, S // tm),
        in_specs=[tile,
                  pl.BlockSpec((None, 8, D), lambda b, i: (b, jnp.maximum(i * (tm // 8) - 1, 0), 0)),
                  mspec(3), mspec(4)] + [full(p) for p in params],
        out_specs=[tile] * 7,
        out_shape=[jax.ShapeDtypeStruct((B, S, D), F32)] * 7,
        compiler_params=_cparams("parallel", "parallel"),
    )(h, h, mod, mod, *params)


def _rwkv_scan_kernel(r_ref, lw_ref, k_ref, v_ref, kk_ref, b_ref, y_ref, zt_ref):
    C, P, L = RW_C, RW_PACK, RW_LANES
    N = P * C

    @pl.when(pl.program_id(2) == 0)
    def _():
        zt_ref[...] = jnp.zeros(zt_ref.shape, F32)

    lane_head = lax.broadcasted_iota(jnp.int32, (1, L), 1) // RWKV_HEAD
    hmask = [lane_head == hd for hd in range(P)]
    stack = lambda x: jnp.concatenate([jnp.where(hmask[hd], x, 0.0) for hd in range(P)], axis=0)
    ri = lax.broadcasted_iota(jnp.int32, (N, N), 0)
    ci = lax.broadcasted_iota(jnp.int32, (N, N), 1)
    low_strict = (ri % C) > (ci % C)
    low_incl = (ri % C) >= (ci % C)
    eye = jnp.where(ri == ci, 1.0, 0.0)
    tri = jnp.where(lax.broadcasted_iota(jnp.int32, (C, C), 0) >= lax.broadcasted_iota(jnp.int32, (C, C), 1),
                    1.0, 0.0).astype(BF16)
    bf = lambda x: x.astype(BF16)

    for c in range(RW_TB // C):
        rows = slice(c * C, (c + 1) * C)
        lw = lw_ref[rows, :]
        lw_hi, lw_mid, lw_lo = _split3(lw)
        cw = _dot(tri, lw_hi) + (_dot(tri, lw_mid) + _dot(tri, lw_lo))
        w_incl = jnp.exp(cw)
        w_excl = jnp.exp(cw - lw)
        w_inv = jnp.exp(-cw)
        w_end = w_incl[C - 1:C, :]
        al = stack(-kk_ref[rows, :] * w_excl)
        rt = stack(r_ref[rows, :] * w_incl)
        bh = b_ref[rows, :] * w_inv
        kh = k_ref[rows, :] * w_inv
        vs = bf(stack(v_ref[rows, :]))
        lhs = bf(jnp.concatenate([al, rt], axis=0))
        rhs = bf(jnp.concatenate([stack(bh), stack(kh)], axis=0))
        aa = _dot_nt(lhs, rhs)
        a_ab = jnp.where(low_strict, aa[:N, :N], 0.0)
        a_ak = jnp.where(low_strict, aa[:N, N:], 0.0)
        a_rb = bf(jnp.where(low_incl, aa[N:, :N], 0.0))
        a_rk = bf(jnp.where(low_incl, aa[N:, N:], 0.0))
        pw = a_ab
        t = eye + pw
        for _ in range(int(np.log2(C)) - 1):
            pwb = bf(pw)
            pw = _dot(pwb, pwb)
            t = t + _dot(bf(t), bf(pw))
        tb = bf(t)
        pm = _dot(tb, bf(al))
        q = _dot(tb, bf(_dot(bf(a_ak), vs)))
        rp = rt + _dot(a_rb, bf(pm))
        y0 = _dot(a_rb, bf(q)) + _dot(a_rk, vs)
        ztb = bf(zt_ref[...])
        u = _dot_nt(bf(pm), ztb) + q
        ys = _dot_nt(bf(rp), ztb) + y0
        y = ys[0:C]
        for hd in range(1, P):
            y = y + ys[hd * C:(hd + 1) * C]
        y_ref[rows, :] = y
        uv = jnp.concatenate([bf(u), vs], axis=0)
        bk = bf(jnp.concatenate([stack(bh * w_end), stack(kh * w_end)], axis=0))
        zt_ref[...] = zt_ref[...] * w_end + _dot_tn(uv, bk)


def _rwkv_scan(r, lw, k, v, kk, b):
    B, S, D = r.shape
    tile = pl.BlockSpec((None, RW_TB, RW_LANES), lambda bb, p, t: (bb, t, p))
    return pl.pallas_call(
        _rwkv_scan_kernel,
        grid=(B, D // RW_LANES, S // RW_TB),
        in_specs=[tile] * 6,
        out_specs=tile,
        out_shape=jax.ShapeDtypeStruct((B, S, D), F32),
        scratch_shapes=[pltpu.VMEM((RW_LANES, RW_LANES), F32)],
        compiler_params=_cparams("parallel", "parallel", "arbitrary"),
    )(r, lw, k, v, kk, b)


def _rwkv_out_kernel(y_ref, r_ref, k_ref, v_ref, g_ref, lnw_ref, lnb_ref, rk_ref, gm_ref, gmt_ref,
                     w_ref, h_ref, gt_ref, o_ref):
    y = y_ref[...]
    inv_n = 1.0 / RWKV_HEAD
    yc = y - _gsum(y, gm_ref, gmt_ref) * inv_n
    var = _gsum(yc * yc, gm_ref, gmt_ref) * inv_n
    yn = yc * lax.rsqrt(var + GN_EPS) * lnw_ref[...] + lnb_ref[...]
    v = v_ref[...]
    bonus = _gsum(r_ref[...] * k_ref[...] * rk_ref[...], gm_ref, gmt_ref) * v
    out = ((yn + bonus) * g_ref[...]).astype(BF16)
    o_ref[...] = h_ref[...] + gt_ref[...] * _dot(out, w_ref[...])


def _rwkv_out(y, r, k, v, g, ln_w, ln_b, r_k, w_out, h, mod, tm=256):
    B, S, D = h.shape
    gm, gmt = _head_indicator(D)
    row = lambda a: a.astype(F32).reshape(1, D)
    full = lambda a: pl.BlockSpec(a.shape, lambda b, i: (0,) * a.ndim)
    tile = pl.BlockSpec((None, tm, D), lambda b, i: (b, i, 0))
    params = [row(ln_w), row(ln_b), row(r_k), gm, gmt, w_out.astype(BF16)]
    return pl.pallas_call(
        _rwkv_out_kernel,
        grid=(B, S // tm),
        in_specs=[tile] * 5 + [full(p) for p in params] + [tile, pl.BlockSpec((None, None, 1, D), lambda b, i: (b, 5, 0, 0))],
        out_specs=tile,
        out_shape=jax.ShapeDtypeStruct((B, S, D), F32),
        compiler_params=_cparams("parallel", "parallel"),
    )(y, r, k, v, g, *params, h, mod)


def _rwkv_mixer(h, mod, mu, w_rkv, w0, w1, w2, a0, a1, a2, g1, g2, k_k, k_a, r_k, ln_w, ln_b, w_out):
    r, lw, k, v, kk, b, g = _rwkv_in(h, mod, mu, w_rkv, w0, w1, w2, a0, a1, a2, g1, g2, k_k, k_a)
    y = _rwkv_scan(r, lw, k, v, kk, b)
    return _rwkv_out(y, r, k, v, g, ln_w, ln_b, r_k, w_out, h, mod)


def kernel(x, c, ada_w, ada_b, ffn_w_gu, ffn_w_down, dsa_w_in, dsa_q_norm, dsa_k_norm, dsa_ik_norm, dsa_w_out, rwkv_mu, rwkv_w_rkv, rwkv_w0, rwkv_w1, rwkv_w2, rwkv_a0, rwkv_a1, rwkv_a2, rwkv_g1, rwkv_g2, rwkv_k_k, rwkv_k_a, rwkv_r_k, rwkv_ln_w, rwkv_ln_b, rwkv_w_out):
    B, S, D = x.shape
    depth = ada_w.shape[0]
    mods = _ada_mod(c, ada_w, ada_b)
    h = x
    for i in range(depth):
        mod = mods[i].reshape(B, N_MOD, 1, D)
        j = i // 2
        h = _ffn(h, mod, 0, ffn_w_gu[i, 0].astype(BF16), ffn_w_down[i, 0].astype(BF16))
        if i % 2 == 0:
            h = _dsa_mixer(h, mod, dsa_w_in[j], dsa_q_norm[j], dsa_k_norm[j], dsa_ik_norm[j], dsa_w_out[j])
        else:
            h = _rwkv_mixer(h, mod, rwkv_mu[j], rwkv_w_rkv[j], rwkv_w0[j], rwkv_w1[j], rwkv_w2[j], rwkv_a0[j],
                            rwkv_a1[j], rwkv_a2[j], rwkv_g1[j], rwkv_g2[j], rwkv_k_k[j], rwkv_k_a[j],
                            rwkv_r_k[j], rwkv_ln_w[j], rwkv_ln_b[j], rwkv_w_out[j])
        h = _ffn(h, mod, 6, ffn_w_gu[i, 1].astype(BF16), ffn_w_down[i, 1].astype(BF16))
    return h
```

```python
import functools

import jax
import jax.numpy as jnp
import numpy as np
from jax import lax
from jax.experimental import pallas as pl
from jax.experimental.pallas import tpu as pltpu

F32 = jnp.float32
BF16 = jnp.bfloat16

RMS_EPS = 1e-6
N_MOD = 9
VMEM_LIMIT_BYTES = 56 * 1024 * 1024

N_HEADS = 16
HEAD_DIM = 64
N_KV_HEADS = 4
GROUP = N_HEADS // N_KV_HEADS
IDX_HEADS = 8
IDX_DIM = 64
TOPK_MAX = 256
LANE = 128
LOG2E = 1.4426950408889634


def _cparams(*sem):
    return pltpu.CompilerParams(dimension_semantics=sem, vmem_limit_bytes=VMEM_LIMIT_BYTES)


def _dot(a, b):
    return jnp.dot(a, b, preferred_element_type=F32)


def _dot_nt(a, b):
    return lax.dot_general(a, b, (((1,), (1,)), ((), ())), preferred_element_type=F32)


def _dot_tn(a, b):
    return lax.dot_general(a, b, (((0,), (0,)), ((), ())), preferred_element_type=F32)


def _modulate(h, sh, sc):
    ms = jnp.mean(h * h, axis=-1, keepdims=True)
    return h * lax.rsqrt(ms + RMS_EPS) * (1.0 + sc) + sh


def _ada_kernel(c_ref, w_ref, b_ref, o_ref):
    c = c_ref[...]
    ca = (c * jax.nn.sigmoid(c)).astype(BF16)
    o_ref[...] = _dot(ca, w_ref[...].astype(BF16)) + b_ref[...]


def _ada_mod(c, ada_w, ada_b):
    L, D, N = ada_w.shape
    B = c.shape[0]
    tn = 1536
    return pl.pallas_call(
        _ada_kernel,
        grid=(L, N // tn),
        in_specs=[
            pl.BlockSpec((B, D), lambda l, j: (0, 0)),
            pl.BlockSpec((None, D, tn), lambda l, j: (l, 0, j)),
            pl.BlockSpec((None, 1, tn), lambda l, j: (l, 0, j)),
        ],
        out_specs=pl.BlockSpec((None, B, tn), lambda l, j: (l, 0, j)),
        out_shape=jax.ShapeDtypeStruct((L, B, N), F32),
        compiler_params=_cparams("parallel", "parallel"),
    )(c, ada_w, ada_b.reshape(L, 1, N))


FF_CHUNK = 256


def _ffn_kernel(h_ref, sh_ref, sc_ref, gt_ref, wgu_ref, wd_ref, o_ref, *, d_ff):
    h = h_ref[...]
    u = _modulate(h, sh_ref[...], sc_ref[...]).astype(BF16)
    acc = None
    for c in range(d_ff // FF_CHUNK):
        lo = c * FF_CHUNK
        g = _dot(u, wgu_ref[:, lo:lo + FF_CHUNK])
        up = _dot(u, wgu_ref[:, d_ff + lo:d_ff + lo + FF_CHUNK])
        a = (g * jax.nn.sigmoid(g) * up).astype(BF16)
        part = _dot(a, wd_ref[lo:lo + FF_CHUNK, :])
        acc = part if acc is None else acc + part
    o_ref[...] = h + (0.5 * gt_ref[...]) * acc


def _ffn(h, mod, k0, w_gu, w_down, tm=512):
    B, S, D = h.shape
    d_ff = w_down.shape[0]
    mspec = lambda k: pl.BlockSpec((None, None, 1, D), lambda b, i: (b, k, 0, 0))
    return pl.pallas_call(
        functools.partial(_ffn_kernel, d_ff=d_ff),
        grid=(B, S // tm),
        in_specs=[
            pl.BlockSpec((None, tm, D), lambda b, i: (b, i, 0)),
            mspec(k0), mspec(k0 + 1), mspec(k0 + 2),
            pl.BlockSpec((D, 2 * d_ff), lambda b, i: (0, 0), pipeline_mode=pl.Buffered(1)),
            pl.BlockSpec((d_ff, D), lambda b, i: (0, 0), pipeline_mode=pl.Buffered(1)),
        ],
        out_specs=pl.BlockSpec((None, tm, D), lambda b, i: (b, i, 0)),
        out_shape=jax.ShapeDtypeStruct((B, S, D), F32),
        compiler_params=_cparams("parallel", "parallel"),
    )(h, mod, mod, mod, w_gu, w_down)


def _dsa_in_kernel(h_ref, sh_ref, sc_ref, wqT_ref, wk_ref, wvT_ref, wiqT_ref, wik_ref, wiwT_ref,
                   qn_ref, kn_ref, ikn_ref,
                   qT_ref, k_ref, vT_ref, iqT_ref, ik_ref, iwT_ref):
    u = _modulate(h_ref[...], sh_ref[...], sc_ref[...]).astype(BF16)
    tm = u.shape[0]
    qT = _dot_nt(wqT_ref[...], u).reshape(N_HEADS, HEAD_DIM, tm)
    ms = jnp.mean(qT * qT, axis=1, keepdims=True)
    qT = qT * lax.rsqrt(ms + RMS_EPS) * qn_ref[...][None] * (HEAD_DIM ** -0.5 * LOG2E)
    qT_ref[...] = qT.reshape(N_HEADS * HEAD_DIM, tm).astype(BF16)
    k = _dot(u, wk_ref[...])
    kn = kn_ref[...]
    for j in range(N_KV_HEADS):
        kj = k[:, j * LANE:(j + 1) * LANE]
        ms = jnp.sum(kj * kj, axis=-1, keepdims=True) * (1.0 / HEAD_DIM)
        k_ref[:, j * LANE:(j + 1) * LANE] = (kj * lax.rsqrt(ms + RMS_EPS) * kn).astype(BF16)
    vT_ref[...] = _dot_nt(wvT_ref[...], u).astype(BF16)
    iqT_ref[...] = _dot_nt(wiqT_ref[...], u).astype(BF16)
    ik = _dot(u, wik_ref[...])
    ms = jnp.sum(ik * ik, axis=-1, keepdims=True) * (1.0 / IDX_DIM)
    ik_ref[...] = (ik * lax.rsqrt(ms + RMS_EPS) * ikn_ref[...]).astype(BF16)
    iwT_ref[...] = _dot_nt(wiwT_ref[...], u) * (IDX_HEADS ** -0.5 * IDX_DIM ** -0.5)


DSA_KC = 256
DSA_QB = 128


def _dsa_in(h, mod, w_in, q_norm, k_norm, ik_norm, tm=DSA_KC):
    B, S, D = h.shape
    QW, KW = N_HEADS * HEAD_DIM, N_KV_HEADS * HEAD_DIM
    IQW = IDX_HEADS * IDX_DIM
    o1, o2, o3, o4, o5 = QW, QW + KW, QW + 2 * KW, QW + 2 * KW + IQW, QW + 2 * KW + IQW + IDX_DIM
    wb = w_in.astype(BF16)
    wqT = wb[:, :o1].T
    wk = jnp.pad(wb[:, o1:o2].reshape(D, N_KV_HEADS, HEAD_DIM), ((0, 0), (0, 0), (0, LANE - HEAD_DIM))).reshape(D, N_KV_HEADS * LANE)
    wvT = wb[:, o2:o3].T
    wiqT = wb[:, o3:o4].T
    wik = jnp.pad(wb[:, o4:o5], ((0, 0), (0, LANE - IDX_DIM)))
    wiwT = wb[:, o5:].T
    qn = q_norm.astype(F32).reshape(HEAD_DIM, 1)
    kn = jnp.pad(k_norm.astype(F32), (0, LANE - HEAD_DIM)).reshape(1, LANE)
    ikn = jnp.pad(ik_norm.astype(F32), (0, LANE - IDX_DIM)).reshape(1, LANE)
    mspec = lambda k: pl.BlockSpec((None, None, 1, D), lambda b, i: (b, k, 0, 0))
    full = lambda a: pl.BlockSpec(a.shape, lambda b, i: (0,) * a.ndim)
    nc = S // tm
    return pl.pallas_call(
        _dsa_in_kernel,
        grid=(B, nc),
        in_specs=[pl.BlockSpec((None, tm, D), lambda b, i: (b, i, 0)), mspec(3), mspec(4),
                  full(wqT), full(wk), full(wvT), full(wiqT), full(wik), full(wiwT),
                  full(qn), full(kn), full(ikn)],
        out_specs=[
            pl.BlockSpec((None, QW, tm), lambda b, i: (b, 0, i)),
            pl.BlockSpec((None, tm, N_KV_HEADS * LANE), lambda b, i: (b, i, 0)),
            pl.BlockSpec((None, None, KW, tm), lambda b, i: (b, i, 0, 0)),
            pl.BlockSpec((None, IQW, tm), lambda b, i: (b, 0, i)),
            pl.BlockSpec((None, tm, LANE), lambda b, i: (b, i, 0)),
            pl.BlockSpec((None, IDX_HEADS, tm), lambda b, i: (b, 0, i)),
        ],
        out_shape=[
            jax.ShapeDtypeStruct((B, QW, S), BF16),
            jax.ShapeDtypeStruct((B, S, N_KV_HEADS * LANE), BF16),
            jax.ShapeDtypeStruct((B, nc, KW, tm), BF16),
            jax.ShapeDtypeStruct((B, IQW, S), BF16),
            jax.ShapeDtypeStruct((B, S, LANE), BF16),
            jax.ShapeDtypeStruct((B, IDX_HEADS, S), F32),
        ],
        compiler_params=_cparams("parallel", "parallel"),
    )(h, mod, mod, wqT, wk, wvT, wiqT, wik, wiwT, qn, kn, ikn)


NEG_BIG = -(2.0 ** 100)
DSA_VPAD = 16


def _ukey_to_float(ukey):
    key = ukey ^ np.int32(-2 ** 31)
    bits = key ^ ((key >> 31) & np.int32(0x7FFFFFFF))
    return lax.bitcast_convert_type(bits, F32)


def _fold_rows(x, op, rows):
    while x.shape[0] > rows:
        half = x.shape[0] // 2
        x = op(x[:half], x[half:])
    return x


def _dsa_core_kernel(iqT_ref, iwT_ref, qT_ref, ik_ref, k_ref, vT_ref, oT_ref,
                     sc_ref, qa_ref, iqa_ref, d_ref, st_ref, p_ref, m_ref, acc_ref, *, k_top, n_chunks):
    KC, QB = DSA_KC, DSA_QB
    qi = pl.program_id(1)
    q0 = qi * QB
    nk = (q0 + QB + KC - 1) // KC
    qpos = q0 + lax.broadcasted_iota(jnp.int32, (1, QB), 1)
    zeros_h = jnp.zeros((LANE - HEAD_DIM, QB), BF16)

    for h in range(IDX_HEADS):
        iqa_ref[:, h * QB:(h + 1) * QB] = jnp.concatenate(
            [iqT_ref[h * IDX_DIM:(h + 1) * IDX_DIM, :], zeros_h], axis=0)
    for j in range(N_KV_HEADS):
        for g in range(GROUP):
            hh = j * GROUP + g
            qa_ref[j, :, g * QB:(g + 1) * QB] = jnp.concatenate(
                [qT_ref[hh * HEAD_DIM:(hh + 1) * HEAD_DIM, :], zeros_h], axis=0)
    iw = iwT_ref[...]

    npair = (nk + 1) // 2

    def clamp_chunk(c):
        return jnp.minimum(c, n_chunks - 1)

    def idx_dot(c, slot):
        k0 = pl.multiple_of(c * KC, KC)
        d_ref[slot] = _dot(ik_ref[pl.ds(k0, KC), :], iqa_ref[...])

    def idx_score(c, slot):
        k0 = pl.multiple_of(c * KC, KC)
        s = jnp.maximum(d_ref[slot, :, 0:QB], 0.0) * iw[0:1, :]
        for h in range(1, IDX_HEADS):
            s = s + jnp.maximum(d_ref[slot, :, h * QB:(h + 1) * QB], 0.0) * iw[h:h + 1, :]
        kpos = k0 + lax.broadcasted_iota(jnp.int32, (KC, 1), 0)
        sc_ref[pl.ds(k0, KC), :] = jnp.where(kpos <= qpos, s, -jnp.inf)

    idx_dot(0, 0)

    def p1(i, carry):
        c = 2 * i
        idx_dot(c + 1, 1)
        idx_score(c, 0)
        idx_dot(clamp_chunk(c + 2), 0)
        idx_score(c + 1, 1)
        return carry

    lax.fori_loop(0, npair, p1, 0)

    def count_not(pred):
        def body(c, acc):
            k0 = pl.multiple_of(c * KC, KC)
            ind = jnp.where(pred(sc_ref[pl.ds(k0, KC), :]), 0.0, 1.0)
            return acc + _fold_rows(ind, jnp.add, 8)
        acc = lax.fori_loop(0, nk, body, jnp.zeros((8, QB), F32))
        return jnp.sum(acc, axis=0, keepdims=True)

    ukey = jnp.zeros((1, QB), jnp.int32)
    for bit in range(31, -1, -1):
        cand = ukey | np.int32(np.uint32(1 << bit).astype(np.int32))
        t = _ukey_to_float(cand)
        cnt = count_not(lambda s: s < t)
        ukey = jnp.where(cnt >= k_top, cand, ukey)
    thr = _ukey_to_float(ukey)
    n_staged = (nk * KC).astype(F32)
    need = k_top - (n_staged - count_not(lambda s: s > thr))

    m_ref[...] = jnp.full(m_ref.shape, NEG_BIG, F32)
    acc_ref[...] = jnp.zeros(acc_ref.shape, F32)
    r_i = lax.broadcasted_iota(jnp.int32, (KC, KC), 0)
    c_i = lax.broadcasted_iota(jnp.int32, (KC, KC), 1)
    tril = jnp.where(c_i <= r_i, 1.0, 0.0).astype(BF16)
    ones_rows = jnp.ones((DSA_VPAD, KC), BF16)

    def logits(c, slot):
        k0 = pl.multiple_of(c * KC, KC)
        for j in range(N_KV_HEADS):
            st_ref[slot, j] = _dot(k_ref[pl.ds(k0, KC), j * LANE:(j + 1) * LANE], qa_ref[j]).astype(BF16)

    logits(0, 0)

    def attend(c, slot, tie_seen):
        k0 = pl.multiple_of(c * KC, KC)
        s = sc_ref[pl.ds(k0, KC), :]
        eq = s == thr
        eqf = jnp.where(eq, 1.0, 0.0)
        incl = _dot(tril, eqf.astype(BF16))
        rank = tie_seen + incl - eqf
        kpos = k0 + lax.broadcasted_iota(jnp.int32, (KC, 1), 0)
        sel = jnp.logical_and(jnp.logical_or(s > thr, jnp.logical_and(eq, rank < need)), kpos <= qpos)
        bias = jnp.where(sel, 0.0, NEG_BIG).astype(BF16)
        for j in range(N_KV_HEADS):
            alphas = []
            for g in range(GROUP):
                sl = slice(g * QB, (g + 1) * QB)
                sg = st_ref[slot, j, :, sl] + bias
                m_old = m_ref[j, :, sl]
                mx = jnp.max(_fold_rows(sg, jnp.maximum, 16), axis=0, keepdims=True).astype(F32)
                m_new = jnp.maximum(m_old, mx)
                alphas.append(jnp.exp2(m_old - m_new))
                m_ref[j, :, sl] = m_new
                p_ref[j, :, sl] = jnp.exp2(sg - m_new.astype(BF16))
            v_aug = jnp.concatenate([vT_ref[c, j * HEAD_DIM:(j + 1) * HEAD_DIM, :], ones_rows], axis=0)
            acc_ref[j] = acc_ref[j] * jnp.concatenate(alphas, axis=1) + _dot(v_aug, p_ref[j])
        return tie_seen + incl[KC - 1:KC, :]

    def p3(i, tie_seen):
        c = 2 * i
        logits(c + 1, 1)
        tie_seen = attend(c, 0, tie_seen)
        logits(clamp_chunk(c + 2), 0)
        return attend(c + 1, 1, tie_seen)

    lax.fori_loop(0, npair, p3, jnp.zeros((1, QB), F32))

    for j in range(N_KV_HEADS):
        o = acc_ref[j, 0:HEAD_DIM, :] / acc_ref[j, HEAD_DIM:HEAD_DIM + 1, :]
        for g in range(GROUP):
            hh = j * GROUP + g
            oT_ref[hh * HEAD_DIM:(hh + 1) * HEAD_DIM, :] = o[:, g * QB:(g + 1) * QB].astype(BF16)


def _dsa_core(qT, k, vT, iqT, ik, iwT):
    B, QW, S = qT.shape
    KC, QB = DSA_KC, DSA_QB
    k_top = min(TOPK_MAX, S // 4)
    nc = S // KC
    return pl.pallas_call(
        functools.partial(_dsa_core_kernel, k_top=k_top, n_chunks=nc),
        grid=(B, S // QB),
        in_specs=[
            pl.BlockSpec((None, IDX_HEADS * IDX_DIM, QB), lambda b, i: (b, 0, i)),
            pl.BlockSpec((None, IDX_HEADS, QB), lambda b, i: (b, 0, i)),
            pl.BlockSpec((None, QW, QB), lambda b, i: (b, 0, i)),
            pl.BlockSpec((None, S, LANE), lambda b, i: (b, 0, 0)),
            pl.BlockSpec((None, S, N_KV_HEADS * LANE), lambda b, i: (b, 0, 0)),
            pl.BlockSpec((None, nc, N_KV_HEADS * HEAD_DIM, KC), lambda b, i: (b, 0, 0, 0)),
        ],
        out_specs=pl.BlockSpec((None, QW, QB), lambda b, i: (b, 0, i)),
        out_shape=jax.ShapeDtypeStruct((B, QW, S), BF16),
        scratch_shapes=[
            pltpu.VMEM((S, QB), F32),
            pltpu.VMEM((N_KV_HEADS, LANE, GROUP * QB), BF16),
            pltpu.VMEM((LANE, IDX_HEADS * QB), BF16),
            pltpu.VMEM((2, KC, IDX_HEADS * QB), F32),
            pltpu.VMEM((2, N_KV_HEADS, KC, GROUP * QB), BF16),
            pltpu.VMEM((N_KV_HEADS, KC, GROUP * QB), BF16),
            pltpu.VMEM((N_KV_HEADS, 1, GROUP * QB), F32),
            pltpu.VMEM((N_KV_HEADS, HEAD_DIM + DSA_VPAD, GROUP * QB), F32),
        ],
        compiler_params=_cparams("parallel", "arbitrary"),
    )(iqT, iwT, qT, ik, k, vT)


def _out_T_kernel(xT_ref, w_ref, h_ref, gt_ref, o_ref):
    o_ref[...] = h_ref[...] + gt_ref[...] * _dot_tn(xT_ref[...], w_ref[...])


def _dsa_out(oT, w_out, h, mod, tm=512):
    B, S, D = h.shape
    QW = oT.shape[1]
    return pl.pallas_call(
        _out_T_kernel,
        grid=(B, S // tm),
        in_specs=[
            pl.BlockSpec((None, QW, tm), lambda b, i: (b, 0, i)),
            pl.BlockSpec((QW, D), lambda b, i: (0, 0)),
            pl.BlockSpec((None, tm, D), lambda b, i: (b, i, 0)),
            pl.BlockSpec((None, None, 1, D), lambda b, i: (b, 5, 0, 0)),
        ],
        out_specs=pl.BlockSpec((None, tm, D), lambda b, i: (b, i, 0)),
        out_shape=jax.ShapeDtypeStruct((B, S, D), F32),
        compiler_params=_cparams("parallel", "parallel"),
    )(oT, w_out.astype(BF16), h, mod)


def _dsa_mixer(h, mod, w_in, q_norm, k_norm, ik_norm, w_out):
    qT, k, vT, iqT, ik, iwT = _dsa_in(h, mod, w_in, q_norm, k_norm, ik_norm)
    oT = _dsa_core(qT, k, vT, iqT, ik, iwT)
    return _dsa_out(oT, w_out, h, mod)


RWKV_HEAD = 64
GN_EPS = 64e-5
RW_PACK = 4
RW_LANES = RW_PACK * RWKV_HEAD
RW_C = 64
RW_TB = 256


def _split3(x):
    hi = x.astype(BF16)
    r1 = x - hi.astype(F32)
    mid = r1.astype(BF16)
    lo = (r1 - mid.astype(F32)).astype(BF16)
    return hi, mid, lo


def _dot3(x, w):
    hi, mid, lo = _split3(x)
    return _dot(hi, w) + (_dot(mid, w) + _dot(lo, w))


def _gsum(x, g_ref, gt_ref):
    return _dot3(_dot3(x, g_ref[...]), gt_ref[...])


def _rwkv_in_kernel(h_ref, hp_ref, sh_ref, sc_ref, mu_ref, wrkv_ref, w0_ref, w1_ref, w2_ref,
                    a0_ref, a1_ref, a2_ref, g1_ref, g2_ref, kk_ref, ka_ref, gm_ref, gmt_ref,
                    r_ref, lw_ref, k_ref, v_ref, kkn_ref, b_ref, g_ref):
    sh, sc = sh_ref[...], sc_ref[...]
    u = _modulate(h_ref[...], sh, sc)
    tm = u.shape[0]
    prev = _modulate(hp_ref[...], sh, sc)[7:8, :]
    prev = jnp.where(pl.program_id(1) == 0, 0.0, prev)
    row = lax.broadcasted_iota(jnp.int32, (tm, 1), 0)
    xx = jnp.where(row == 0, prev, pltpu.roll(u, 1, 0)) - u
    mix = lambda i: (u + xx * mu_ref[i:i + 1, :]).astype(BF16)
    r = _dot(mix(0), wrkv_ref[0])
    k = _dot(mix(1), wrkv_ref[1])
    v = _dot(mix(2), wrkv_ref[2])
    wl = w0_ref[...] + _dot(jnp.tanh(_dot(mix(3), w1_ref[...])).astype(BF16), w2_ref[...])
    nwl = -wl
    w_log = -(jnp.maximum(nwl, 0.0) + jnp.log(1.0 + jnp.exp(-jnp.abs(nwl)))) - 0.5
    lw_ref[...] = -jnp.exp(w_log)
    a = jax.nn.sigmoid(a0_ref[...] + _dot(_dot(mix(4), a1_ref[...]).astype(BF16), a2_ref[...]))
    g_ref[...] = _dot(jax.nn.sigmoid(_dot(mix(5), g1_ref[...])).astype(BF16), g2_ref[...])
    kk = k * kk_ref[...]
    nrm = jnp.sqrt(_gsum(kk * kk, gm_ref, gmt_ref))
    kk = kk / jnp.maximum(nrm, 1e-12)
    r_ref[...] = r
    k_ref[...] = k * (1.0 + (a - 1.0) * ka_ref[...])
    v_ref[...] = v
    kkn_ref[...] = kk
    b_ref[...] = kk * a


def _head_indicator(D):
    gm = (np.arange(D)[:, None] // RWKV_HEAD == np.arange(D // RWKV_HEAD)[None, :]).astype(np.float32)
    return jnp.asarray(gm, BF16), jnp.asarray(gm.T, BF16)


def _rwkv_in(h, mod, mu, w_rkv, w0, w1, w2, a0, a1, a2, g1, g2, k_k, k_a, tm=256):
    B, S, D = h.shape
    gm, gmt = _head_indicator(D)
    row = lambda a: a.astype(F32).reshape(1, D)
    full = lambda a: pl.BlockSpec(a.shape, lambda b, i: (0,) * a.ndim)
    mspec = lambda k: pl.BlockSpec((None, None, 1, D), lambda b, i: (b, k, 0, 0))
    tile = pl.BlockSpec((None, tm, D), lambda b, i: (b, i, 0))
    params = [mu.astype(F32), w_rkv.astype(BF16), row(w0), w1.astype(BF16), w2.astype(BF16),
              row(a0), a1.astype(BF16), a2.astype(BF16), g1.astype(BF16), g2.astype(BF16),
              row(k_k), row(k_a), gm, gmt]
    return pl.pallas_call(
        _rwkv_in_kernel,
        grid=(B, S // tm),
        in_specs=[tile,
                  pl.BlockSpec((None, 8, D), lambda b, i: (b, jnp.maximum(i * (tm // 8) - 1, 0), 0)),
                  mspec(3), mspec(4)] + [full(p) for p in params],
        out_specs=[tile] * 7,
        out_shape=[jax.ShapeDtypeStruct((B, S, D), F32)] * 7,
        compiler_params=_cparams("parallel", "parallel"),
    )(h, h, mod, mod, *params)


def _rwkv_scan_kernel(r_ref, lw_ref, k_ref, v_ref, kk_ref, b_ref, y_ref, zt_ref):
    C, P, L = RW_C, RW_PACK, RW_LANES
    N = P * C
    assert C == RWKV_HEAD
    n_packs = r_ref.shape[1] // L
    n_chunks = r_ref.shape[0] // C
    units = [(p, c) for p in range(n_packs) for c in range(n_chunks)]

    @pl.when(pl.program_id(1) == 0)
    def _():
        zt_ref[...] = jnp.zeros(zt_ref.shape, F32)

    lane = lax.broadcasted_iota(jnp.int32, (1, L), 1)
    hmask = [lane // C == hd for hd in range(P)]
    bf = lambda x: x.astype(BF16)
    stack = lambda x: jnp.concatenate([jnp.where(hmask[hd], x, jnp.zeros_like(x)) for hd in range(P)], axis=0)
    t_i = lax.broadcasted_iota(jnp.int32, (C, N), 0)
    i_i = lax.broadcasted_iota(jnp.int32, (C, N), 1) % C
    low_strict = t_i > i_i
    low_incl = t_i >= i_i
    eye_w = jnp.where(t_i == i_i, 1.0, 0.0)
    bd_mask = (lax.broadcasted_iota(jnp.int32, (L, L), 0) // C) == (lax.broadcasted_iota(jnp.int32, (L, L), 1) // C)
    tri = jnp.where(lax.broadcasted_iota(jnp.int32, (C, C), 0) >= lax.broadcasted_iota(jnp.int32, (C, C), 1),
                    1.0, 0.0).astype(BF16)
    tile = lambda ref, u: ref[u[1] * C:(u[1] + 1) * C, u[0] * L:(u[0] + 1) * L]

    cw = {}
    tri3 = jnp.concatenate([tri, tri, tri], axis=1)
    for u in units:
        cw[u] = _dot(tri3, jnp.concatenate(_split3(tile(lw_ref, u)), axis=0))
    al, rt, bw, kw, v, vs, aa, w_end = {}, {}, {}, {}, {}, {}, {}, {}
    for u in units:
        lw = tile(lw_ref, u)
        w_incl = jnp.exp(cw[u])
        w_excl = jnp.exp(cw[u] - lw)
        w_inv = jnp.exp(-cw[u])
        w_end[u] = w_incl[C - 1:C, :]
        v[u] = bf(tile(v_ref, u))
        vs[u] = stack(v[u])
        al[u] = bf(-tile(kk_ref, u) * w_excl)
        rt[u] = tile(r_ref, u) * w_incl
        bh = tile(b_ref, u) * w_inv
        kh = tile(k_ref, u) * w_inv
        bw[u] = bf(bh * w_end[u])
        kw[u] = bf(kh * w_end[u])
        aa[u] = _dot_nt(jnp.concatenate([al[u], bf(rt[u])], axis=0),
                        jnp.concatenate([stack(bf(bh)), stack(bf(kh))], axis=0))
    a_ak, a_r, pw, t = {}, {}, {}, {}
    for u in units:
        a_ab = jnp.where(low_strict, aa[u][:C, :N], 0.0)
        a_ak[u] = bf(jnp.where(low_strict, aa[u][:C, N:], 0.0))
        a_r[u] = bf(jnp.concatenate([jnp.where(low_incl, aa[u][C:, :N], 0.0),
                                     jnp.where(low_incl, aa[u][C:, N:], 0.0)], axis=1))
        pw[u] = a_ab
        t[u] = eye_w + a_ab
    n_fac = int(np.log2(C))
    akv = {}
    for u in units:
        pwb = bf(pw[u])
        pw[u] = _dot(pwb, stack(pwb))
        akv[u] = _dot(a_ak[u], vs[u])
    for s in range(2, n_fac + 1):
        for u in units:
            pwb = bf(pw[u])
            rhs = stack(pwb)
            if s < n_fac:
                both = _dot(jnp.concatenate([bf(t[u]), pwb], axis=0), rhs)
                t[u] = t[u] + both[:C]
                pw[u] = both[C:]
            else:
                t[u] = t[u] + _dot(bf(t[u]), rhs)
    pm, q = {}, {}
    for u in units:
        tb = bf(t[u])
        pm[u] = bf(_dot(tb, stack(al[u])))
        q[u] = bf(_dot(tb, stack(bf(akv[u]))))
    pr, y0, bk = {}, {}, {}
    for u in units:
        rp = bf(rt[u] + _dot(a_r[u][:, :N], stack(pm[u])))
        pr[u] = jnp.concatenate([pm[u], rp], axis=0)
        y0[u] = _dot(a_r[u], jnp.concatenate([stack(q[u]), vs[u]], axis=0))
        bk[u] = jnp.concatenate([bw[u], kw[u]], axis=0)
    zt = [zt_ref[p] for p in range(n_packs)]
    for c in range(n_chunks):
        for p in range(n_packs):
            u = (p, c)
            uy = _dot_nt(pr[u], bf(zt[p]))
            y_ref[c * C:(c + 1) * C, p * L:(p + 1) * L] = uy[C:] + y0[u]
            upd = _dot_tn(jnp.concatenate([bf(uy[:C] + q[u]), v[u]], axis=0), bk[u])
            zt[p] = zt[p] * w_end[u] + jnp.where(bd_mask, upd, 0.0)
    for p in range(n_packs):
        zt_ref[p] = zt[p]


def _rwkv_scan(r, lw, k, v, kk, b):
    B, S, D = r.shape
    tile = pl.BlockSpec((None, RW_TB, D), lambda bb, t: (bb, t, 0))
    return pl.pallas_call(
        _rwkv_scan_kernel,
        grid=(B, S // RW_TB),
        in_specs=[tile] * 6,
        out_specs=tile,
        out_shape=jax.ShapeDtypeStruct((B, S, D), F32),
        scratch_shapes=[pltpu.VMEM((D // RW_LANES, RW_LANES, RW_LANES), F32)],
        compiler_params=_cparams("parallel", "arbitrary"),
    )(r, lw, k, v, kk, b)


def _rwkv_out_kernel(y_ref, r_ref, k_ref, v_ref, g_ref, lnw_ref, lnb_ref, rk_ref, gm_ref, gmt_ref,
                     w_ref, h_ref, gt_ref, o_ref):
    y = y_ref[...]
    inv_n = 1.0 / RWKV_HEAD
    yc = y - _gsum(y, gm_ref, gmt_ref) * inv_n
    var = _gsum(yc * yc, gm_ref, gmt_ref) * inv_n
    yn = yc * lax.rsqrt(var + GN_EPS) * lnw_ref[...] + lnb_ref[...]
    v = v_ref[...]
    bonus = _gsum(r_ref[...] * k_ref[...] * rk_ref[...], gm_ref, gmt_ref) * v
    out = ((yn + bonus) * g_ref[...]).astype(BF16)
    o_ref[...] = h_ref[...] + gt_ref[...] * _dot(out, w_ref[...])


def _rwkv_out(y, r, k, v, g, ln_w, ln_b, r_k, w_out, h, mod, tm=256):
    B, S, D = h.shape
    gm, gmt = _head_indicator(D)
    row = lambda a: a.astype(F32).reshape(1, D)
    full = lambda a: pl.BlockSpec(a.shape, lambda b, i: (0,) * a.ndim)
    tile = pl.BlockSpec((None, tm, D), lambda b, i: (b, i, 0))
    params = [row(ln_w), row(ln_b), row(r_k), gm, gmt, w_out.astype(BF16)]
    return pl.pallas_call(
        _rwkv_out_kernel,
        grid=(B, S // tm),
        in_specs=[tile] * 5 + [full(p) for p in params] + [tile, pl.BlockSpec((None, None, 1, D), lambda b, i: (b, 5, 0, 0))],
        out_specs=tile,
        out_shape=jax.ShapeDtypeStruct((B, S, D), F32),
        compiler_params=_cparams("parallel", "parallel"),
    )(y, r, k, v, g, *params, h, mod)


def _rwkv_mixer(h, mod, mu, w_rkv, w0, w1, w2, a0, a1, a2, g1, g2, k_k, k_a, r_k, ln_w, ln_b, w_out):
    r, lw, k, v, kk, b, g = _rwkv_in(h, mod, mu, w_rkv, w0, w1, w2, a0, a1, a2, g1, g2, k_k, k_a)
    y = _rwkv_scan(r, lw, k, v, kk, b)
    return _rwkv_out(y, r, k, v, g, ln_w, ln_b, r_k, w_out, h, mod)


def kernel(x, c, ada_w, ada_b, ffn_w_gu, ffn_w_down, dsa_w_in, dsa_q_norm, dsa_k_norm, dsa_ik_norm, dsa_w_out, rwkv_mu, rwkv_w_rkv, rwkv_w0, rwkv_w1, rwkv_w2, rwkv_a0, rwkv_a1, rwkv_a2, rwkv_g1, rwkv_g2, rwkv_k_k, rwkv_k_a, rwkv_r_k, rwkv_ln_w, rwkv_ln_b, rwkv_w_out):
    B, S, D = x.shape
    depth = ada_w.shape[0]
    mods = _ada_mod(c, ada_w, ada_b)
    h = x
    for i in range(depth):
        mod = mods[i].reshape(B, N_MOD, 1, D)
        j = i // 2
        h = _ffn(h, mod, 0, ffn_w_gu[i, 0].astype(BF16), ffn_w_down[i, 0].astype(BF16))
        if i % 2 == 0:
            h = _dsa_mixer(h, mod, dsa_w_in[j], dsa_q_norm[j], dsa_k_norm[j], dsa_ik_norm[j], dsa_w_out[j])
        else:
            h = _rwkv_mixer(h, mod, rwkv_mu[j], rwkv_w_rkv[j], rwkv_w0[j], rwkv_w1[j], rwkv_w2[j], rwkv_a0[j],
                            rwkv_a1[j], rwkv_a2[j], rwkv_g1[j], rwkv_g2[j], rwkv_k_k[j], rwkv_k_a[j],
                            rwkv_r_k[j], rwkv_ln_w[j], rwkv_ln_b[j], rwkv_w_out[j])
        h = _ffn(h, mod, 6, ffn_w_gu[i, 1].astype(BF16), ffn_w_down[i, 1].astype(BF16))
    return h
```

```python
import functools

import jax
import jax.numpy as jnp
import numpy as np
from jax import lax
from jax.experimental import pallas as pl
from jax.experimental.pallas import tpu as pltpu

F32 = jnp.float32
BF16 = jnp.bfloat16

RMS_EPS = 1e-6
N_MOD = 9
VMEM_LIMIT_BYTES = 56 * 1024 * 1024

N_HEADS = 16
HEAD_DIM = 64
N_KV_HEADS = 4
GROUP = N_HEADS // N_KV_HEADS
IDX_HEADS = 8
IDX_DIM = 64
TOPK_MAX = 256
LANE = 128
LOG2E = 1.4426950408889634


def _cparams(*sem):
    return pltpu.CompilerParams(dimension_semantics=sem, vmem_limit_bytes=VMEM_LIMIT_BYTES)


def _dot(a, b):
    return jnp.dot(a, b, preferred_element_type=F32)


def _dot_nt(a, b):
    return lax.dot_general(a, b, (((1,), (1,)), ((), ())), preferred_element_type=F32)


def _dot_tn(a, b):
    return lax.dot_general(a, b, (((0,), (0,)), ((), ())), preferred_element_type=F32)


def _modulate(h, sh, sc):
    ms = jnp.mean(h * h, axis=-1, keepdims=True)
    return h * lax.rsqrt(ms + RMS_EPS) * (1.0 + sc) + sh


def _ada_kernel(c_ref, w_ref, b_ref, o_ref):
    c = c_ref[...]
    ca = (c * jax.nn.sigmoid(c)).astype(BF16)
    o_ref[...] = _dot(ca, w_ref[...].astype(BF16)) + b_ref[...]


def _ada_mod(c, ada_w, ada_b):
    L, D, N = ada_w.shape
    B = c.shape[0]
    tn = 1536
    return pl.pallas_call(
        _ada_kernel,
        grid=(L, N // tn),
        in_specs=[
            pl.BlockSpec((B, D), lambda l, j: (0, 0)),
            pl.BlockSpec((None, D, tn), lambda l, j: (l, 0, j)),
            pl.BlockSpec((None, 1, tn), lambda l, j: (l, 0, j)),
        ],
        out_specs=pl.BlockSpec((None, B, tn), lambda l, j: (l, 0, j)),
        out_shape=jax.ShapeDtypeStruct((L, B, N), F32),
        compiler_params=_cparams("parallel", "parallel"),
    )(c, ada_w, ada_b.reshape(L, 1, N))


FF_CHUNK = 256


def _ffn_kernel(h_ref, sh_ref, sc_ref, gt_ref, wgu_ref, wd_ref, o_ref, *, d_ff):
    h = h_ref[...]
    u = _modulate(h, sh_ref[...], sc_ref[...]).astype(BF16)
    acc = None
    for c in range(d_ff // FF_CHUNK):
        lo = c * FF_CHUNK
        g = _dot(u, wgu_ref[:, lo:lo + FF_CHUNK])
        up = _dot(u, wgu_ref[:, d_ff + lo:d_ff + lo + FF_CHUNK])
        a = (g * jax.nn.sigmoid(g) * up).astype(BF16)
        part = _dot(a, wd_ref[lo:lo + FF_CHUNK, :])
        acc = part if acc is None else acc + part
    o_ref[...] = h + (0.5 * gt_ref[...]) * acc


def _ffn(h, mod, k0, w_gu, w_down, tm=512):
    B, S, D = h.shape
    d_ff = w_down.shape[0]
    mspec = lambda k: pl.BlockSpec((None, None, 1, D), lambda b, i: (b, k, 0, 0))
    return pl.pallas_call(
        functools.partial(_ffn_kernel, d_ff=d_ff),
        grid=(B, S // tm),
        in_specs=[
            pl.BlockSpec((None, tm, D), lambda b, i: (b, i, 0)),
            mspec(k0), mspec(k0 + 1), mspec(k0 + 2),
            pl.BlockSpec((D, 2 * d_ff), lambda b, i: (0, 0), pipeline_mode=pl.Buffered(1)),
            pl.BlockSpec((d_ff, D), lambda b, i: (0, 0), pipeline_mode=pl.Buffered(1)),
        ],
        out_specs=pl.BlockSpec((None, tm, D), lambda b, i: (b, i, 0)),
        out_shape=jax.ShapeDtypeStruct((B, S, D), F32),
        compiler_params=_cparams("parallel", "parallel"),
    )(h, mod, mod, mod, w_gu, w_down)


def _dsa_in_kernel(h_ref, sh_ref, sc_ref, wqT_ref, wk_ref, wvT_ref, wiqT_ref, wik_ref, wiwT_ref,
                   qn_ref, kn_ref, ikn_ref,
                   qT_ref, k_ref, vT_ref, iqT_ref, ik_ref, iwT_ref):
    u = _modulate(h_ref[...], sh_ref[...], sc_ref[...]).astype(BF16)
    tm = u.shape[0]
    qT = _dot_nt(wqT_ref[...], u).reshape(N_HEADS, HEAD_DIM, tm)
    ms = jnp.mean(qT * qT, axis=1, keepdims=True)
    qT = qT * lax.rsqrt(ms + RMS_EPS) * qn_ref[...][None] * (HEAD_DIM ** -0.5 * LOG2E)
    qT_ref[...] = qT.reshape(N_HEADS * HEAD_DIM, tm).astype(BF16)
    k = _dot(u, wk_ref[...])
    kn = kn_ref[...]
    for j in range(N_KV_HEADS):
        kj = k[:, j * LANE:(j + 1) * LANE]
        ms = jnp.sum(kj * kj, axis=-1, keepdims=True) * (1.0 / HEAD_DIM)
        k_ref[:, j * LANE:(j + 1) * LANE] = (kj * lax.rsqrt(ms + RMS_EPS) * kn).astype(BF16)
    vT_ref[...] = _dot_nt(wvT_ref[...], u).astype(BF16)
    iqT_ref[...] = _dot_nt(wiqT_ref[...], u).astype(BF16)
    ik = _dot(u, wik_ref[...])
    ms = jnp.sum(ik * ik, axis=-1, keepdims=True) * (1.0 / IDX_DIM)
    ik_ref[...] = (ik * lax.rsqrt(ms + RMS_EPS) * ikn_ref[...]).astype(BF16)
    iwT_ref[...] = _dot_nt(wiwT_ref[...], u) * (IDX_HEADS ** -0.5 * IDX_DIM ** -0.5)


DSA_KC = 256
DSA_QB = 256


def _dsa_in(h, mod, w_in, q_norm, k_norm, ik_norm, tm=DSA_KC):
    B, S, D = h.shape
    QW, KW = N_HEADS * HEAD_DIM, N_KV_HEADS * HEAD_DIM
    IQW = IDX_HEADS * IDX_DIM
    o1, o2, o3, o4, o5 = QW, QW + KW, QW + 2 * KW, QW + 2 * KW + IQW, QW + 2 * KW + IQW + IDX_DIM
    wb = w_in.astype(BF16)
    wqT = wb[:, :o1].T
    wk = jnp.pad(wb[:, o1:o2].reshape(D, N_KV_HEADS, HEAD_DIM), ((0, 0), (0, 0), (0, LANE - HEAD_DIM))).reshape(D, N_KV_HEADS * LANE)
    wvT = wb[:, o2:o3].T
    wiqT = wb[:, o3:o4].T
    wik = jnp.pad(wb[:, o4:o5], ((0, 0), (0, LANE - IDX_DIM)))
    wiwT = wb[:, o5:].T
    qn = q_norm.astype(F32).reshape(HEAD_DIM, 1)
    kn = jnp.pad(k_norm.astype(F32), (0, LANE - HEAD_DIM)).reshape(1, LANE)
    ikn = jnp.pad(ik_norm.astype(F32), (0, LANE - IDX_DIM)).reshape(1, LANE)
    mspec = lambda k: pl.BlockSpec((None, None, 1, D), lambda b, i: (b, k, 0, 0))
    full = lambda a: pl.BlockSpec(a.shape, lambda b, i: (0,) * a.ndim)
    nc = S // tm
    return pl.pallas_call(
        _dsa_in_kernel,
        grid=(B, nc),
        in_specs=[pl.BlockSpec((None, tm, D), lambda b, i: (b, i, 0)), mspec(3), mspec(4),
                  full(wqT), full(wk), full(wvT), full(wiqT), full(wik), full(wiwT),
                  full(qn), full(kn), full(ikn)],
        out_specs=[
            pl.BlockSpec((None, QW, tm), lambda b, i: (b, 0, i)),
            pl.BlockSpec((None, tm, N_KV_HEADS * LANE), lambda b, i: (b, i, 0)),
            pl.BlockSpec((None, None, KW, tm), lambda b, i: (b, i, 0, 0)),
            pl.BlockSpec((None, IQW, tm), lambda b, i: (b, 0, i)),
            pl.BlockSpec((None, tm, LANE), lambda b, i: (b, i, 0)),
            pl.BlockSpec((None, IDX_HEADS, tm), lambda b, i: (b, 0, i)),
        ],
        out_shape=[
            jax.ShapeDtypeStruct((B, QW, S), BF16),
            jax.ShapeDtypeStruct((B, S, N_KV_HEADS * LANE), BF16),
            jax.ShapeDtypeStruct((B, nc, KW, tm), BF16),
            jax.ShapeDtypeStruct((B, IQW, S), BF16),
            jax.ShapeDtypeStruct((B, S, LANE), BF16),
            jax.ShapeDtypeStruct((B, IDX_HEADS, S), F32),
        ],
        compiler_params=_cparams("parallel", "parallel"),
    )(h, mod, mod, wqT, wk, wvT, wiqT, wik, wiwT, qn, kn, ikn)


NEG_BIG = -(2.0 ** 100)
DSA_VPAD = 16


def _ukey_to_float(ukey):
    key = ukey ^ np.int32(-2 ** 31)
    bits = key ^ ((key >> 31) & np.int32(0x7FFFFFFF))
    return lax.bitcast_convert_type(bits, F32)


def _fold_rows(x, op, rows):
    while x.shape[0] > rows:
        half = x.shape[0] // 2
        x = op(x[:half], x[half:])
    return x


def _dsa_core_kernel(iqT_ref, iwT_ref, qT_ref, ik_ref, k_ref, vT_ref, oT_ref,
                     sc_ref, qa_ref, iqa_ref, d_ref, st_ref, p_ref, m_ref, acc_ref, *, k_top, n_chunks):
    KC, QB = DSA_KC, DSA_QB
    qi = pl.program_id(1)
    q0 = qi * QB
    nk = (q0 + QB + KC - 1) // KC
    qpos = q0 + lax.broadcasted_iota(jnp.int32, (1, QB), 1)
    zeros_h = jnp.zeros((LANE - HEAD_DIM, QB), BF16)

    for h in range(IDX_HEADS):
        iqa_ref[:, h * QB:(h + 1) * QB] = jnp.concatenate(
            [iqT_ref[h * IDX_DIM:(h + 1) * IDX_DIM, :], zeros_h], axis=0)
    for j in range(N_KV_HEADS):
        for g in range(GROUP):
            hh = j * GROUP + g
            qa_ref[j, :, g * QB:(g + 1) * QB] = jnp.concatenate(
                [qT_ref[hh * HEAD_DIM:(hh + 1) * HEAD_DIM, :], zeros_h], axis=0)
    iw = iwT_ref[...]

    npair = (nk + 1) // 2

    def clamp_chunk(c):
        return jnp.minimum(c, n_chunks - 1)

    def idx_dot(c, slot):
        k0 = pl.multiple_of(c * KC, KC)
        d_ref[slot] = _dot(ik_ref[pl.ds(k0, KC), :], iqa_ref[...])

    def idx_score(c, slot):
        k0 = pl.multiple_of(c * KC, KC)
        s = jnp.maximum(d_ref[slot, :, 0:QB], 0.0) * iw[0:1, :]
        for h in range(1, IDX_HEADS):
            s = s + jnp.maximum(d_ref[slot, :, h * QB:(h + 1) * QB], 0.0) * iw[h:h + 1, :]
        kpos = k0 + lax.broadcasted_iota(jnp.int32, (KC, 1), 0)
        sc_ref[pl.ds(k0, KC), :] = jnp.where(kpos <= qpos, s, -jnp.inf)

    idx_dot(0, 0)

    def p1(i, carry):
        c = 2 * i
        idx_dot(c + 1, 1)
        idx_score(c, 0)
        idx_dot(clamp_chunk(c + 2), 0)
        idx_score(c + 1, 1)
        return carry

    lax.fori_loop(0, npair, p1, 0)

    def count_not(pred):
        def body(c, acc):
            k0 = pl.multiple_of(c * KC, KC)
            ind = jnp.where(pred(sc_ref[pl.ds(k0, KC), :]), 0.0, 1.0)
            return acc + _fold_rows(ind, jnp.add, 8)
        acc = lax.fori_loop(0, nk, body, jnp.zeros((8, QB), F32))
        return jnp.sum(acc, axis=0, keepdims=True)

    ukey = jnp.zeros((1, QB), jnp.int32)
    for bit in range(31, -1, -1):
        cand = ukey | np.int32(np.uint32(1 << bit).astype(np.int32))
        t = _ukey_to_float(cand)
        cnt = count_not(lambda s: s < t)
        ukey = jnp.where(cnt >= k_top, cand, ukey)
    thr = _ukey_to_float(ukey)
    n_staged = (nk * KC).astype(F32)
    need = k_top - (n_staged - count_not(lambda s: s > thr))

    m_ref[...] = jnp.full(m_ref.shape, NEG_BIG, F32)
    acc_ref[...] = jnp.zeros(acc_ref.shape, F32)
    r_i = lax.broadcasted_iota(jnp.int32, (KC, KC), 0)
    c_i = lax.broadcasted_iota(jnp.int32, (KC, KC), 1)
    tril = jnp.where(c_i <= r_i, 1.0, 0.0).astype(BF16)
    ones_rows = jnp.ones((DSA_VPAD, KC), BF16)

    HALF = GROUP * QB // 2

    def logits(c, slot, j, half):
        k0 = pl.multiple_of(c * KC, KC)
        cols = slice(half * HALF, (half + 1) * HALF)
        st_ref[slot, j, :, cols] = _dot(k_ref[pl.ds(k0, KC), j * LANE:(j + 1) * LANE],
                                        qa_ref[j, :, cols]).astype(BF16)

    for j in range(N_KV_HEADS):
        logits(0, 0, j, 0)
        logits(0, 0, j, 1)

    def attend(c, slot, tie_seen, c_next):
        k0 = pl.multiple_of(c * KC, KC)
        s = sc_ref[pl.ds(k0, KC), :]
        eq = s == thr
        eqf = jnp.where(eq, 1.0, 0.0)
        incl = _dot(tril, eqf.astype(BF16))
        rank = tie_seen + incl - eqf
        kpos = k0 + lax.broadcasted_iota(jnp.int32, (KC, 1), 0)
        sel = jnp.logical_and(jnp.logical_or(s > thr, jnp.logical_and(eq, rank < need)), kpos <= qpos)
        bias = jnp.where(sel, 0.0, NEG_BIG).astype(BF16)
        for j in range(N_KV_HEADS):
            alphas = []
            for g in range(GROUP):
                sl = slice(g * QB, (g + 1) * QB)
                sg = st_ref[slot, j, :, sl] + bias
                m_old = m_ref[j, :, sl]
                mx = jnp.max(_fold_rows(sg, jnp.maximum, 16), axis=0, keepdims=True).astype(F32)
                m_new = jnp.maximum(m_old, mx)
                alphas.append(jnp.exp2(m_old - m_new))
                m_ref[j, :, sl] = m_new
                p_ref[j, :, sl] = jnp.exp2(sg - m_new.astype(BF16))
                if g % 2 == 1:
                    logits(c_next, 1 - slot, j, g // 2)
            v_aug = jnp.concatenate([vT_ref[c, j * HEAD_DIM:(j + 1) * HEAD_DIM, :], ones_rows], axis=0)
            acc_ref[j] = acc_ref[j] * jnp.concatenate(alphas, axis=1) + _dot(v_aug, p_ref[j])
        return tie_seen + incl[KC - 1:KC, :]

    def p3(i, tie_seen):
        c = 2 * i
        tie_seen = attend(c, 0, tie_seen, c + 1)
        return attend(c + 1, 1, tie_seen, clamp_chunk(c + 2))

    lax.fori_loop(0, npair, p3, jnp.zeros((1, QB), F32))

    for j in range(N_KV_HEADS):
        o = acc_ref[j, 0:HEAD_DIM, :] / acc_ref[j, HEAD_DIM:HEAD_DIM + 1, :]
        for g in range(GROUP):
            hh = j * GROUP + g
            oT_ref[hh * HEAD_DIM:(hh + 1) * HEAD_DIM, :] = o[:, g * QB:(g + 1) * QB].astype(BF16)


def _dsa_core(qT, k, vT, iqT, ik, iwT):
    B, QW, S = qT.shape
    KC, QB = DSA_KC, DSA_QB
    k_top = min(TOPK_MAX, S // 4)
    nc = S // KC
    return pl.pallas_call(
        functools.partial(_dsa_core_kernel, k_top=k_top, n_chunks=nc),
        grid=(B, S // QB),
        in_specs=[
            pl.BlockSpec((None, IDX_HEADS * IDX_DIM, QB), lambda b, i: (b, 0, i)),
            pl.BlockSpec((None, IDX_HEADS, QB), lambda b, i: (b, 0, i)),
            pl.BlockSpec((None, QW, QB), lambda b, i: (b, 0, i)),
            pl.BlockSpec((None, S, LANE), lambda b, i: (b, 0, 0)),
            pl.BlockSpec((None, S, N_KV_HEADS * LANE), lambda b, i: (b, 0, 0)),
            pl.BlockSpec((None, nc, N_KV_HEADS * HEAD_DIM, KC), lambda b, i: (b, 0, 0, 0)),
        ],
        out_specs=pl.BlockSpec((None, QW, QB), lambda b, i: (b, 0, i)),
        out_shape=jax.ShapeDtypeStruct((B, QW, S), BF16),
        scratch_shapes=[
            pltpu.VMEM((S, QB), F32),
            pltpu.VMEM((N_KV_HEADS, LANE, GROUP * QB), BF16),
            pltpu.VMEM((LANE, IDX_HEADS * QB), BF16),
            pltpu.VMEM((2, KC, IDX_HEADS * QB), F32),
            pltpu.VMEM((2, N_KV_HEADS, KC, GROUP * QB), BF16),
            pltpu.VMEM((N_KV_HEADS, KC, GROUP * QB), BF16),
            pltpu.VMEM((N_KV_HEADS, 1, GROUP * QB), F32),
            pltpu.VMEM((N_KV_HEADS, HEAD_DIM + DSA_VPAD, GROUP * QB), F32),
        ],
        compiler_params=_cparams("parallel", "arbitrary"),
    )(iqT, iwT, qT, ik, k, vT)


def _out_T_kernel(xT_ref, w_ref, h_ref, gt_ref, o_ref):
    o_ref[...] = h_ref[...] + gt_ref[...] * _dot_tn(xT_ref[...], w_ref[...])


def _dsa_out(oT, w_out, h, mod, tm=512):
    B, S, D = h.shape
    QW = oT.shape[1]
    return pl.pallas_call(
        _out_T_kernel,
        grid=(B, S // tm),
        in_specs=[
            pl.BlockSpec((None, QW, tm), lambda b, i: (b, 0, i)),
            pl.BlockSpec((QW, D), lambda b, i: (0, 0)),
            pl.BlockSpec((None, tm, D), lambda b, i: (b, i, 0)),
            pl.BlockSpec((None, None, 1, D), lambda b, i: (b, 5, 0, 0)),
        ],
        out_specs=pl.BlockSpec((None, tm, D), lambda b, i: (b, i, 0)),
        out_shape=jax.ShapeDtypeStruct((B, S, D), F32),
        compiler_params=_cparams("parallel", "parallel"),
    )(oT, w_out.astype(BF16), h, mod)


def _dsa_mixer(h, mod, w_in, q_norm, k_norm, ik_norm, w_out):
    qT, k, vT, iqT, ik, iwT = _dsa_in(h, mod, w_in, q_norm, k_norm, ik_norm)
    oT = _dsa_core(qT, k, vT, iqT, ik, iwT)
    return _dsa_out(oT, w_out, h, mod)


RWKV_HEAD = 64
GN_EPS = 64e-5
RW_PACK = 4
RW_LANES = RW_PACK * RWKV_HEAD
RW_C = 64
RW_TB = 256


def _split3(x):
    hi = x.astype(BF16)
    r1 = x - hi.astype(F32)
    mid = r1.astype(BF16)
    lo = (r1 - mid.astype(F32)).astype(BF16)
    return hi, mid, lo


def _dot3(x, w):
    hi, mid, lo = _split3(x)
    return _dot(hi, w) + (_dot(mid, w) + _dot(lo, w))


def _gsum(x, g_ref, gt_ref):
    return _dot3(_dot3(x, g_ref[...]), gt_ref[...])


def _rwkv_in_kernel(h_ref, hp_ref, sh_ref, sc_ref, mu_ref, wrkv_ref, w0_ref, w1_ref, w2_ref,
                    a0_ref, a1_ref, a2_ref, g1_ref, g2_ref, kk_ref, ka_ref, gm_ref, gmt_ref,
                    r_ref, lw_ref, k_ref, v_ref, kkn_ref, b_ref, g_ref):
    sh, sc = sh_ref[...], sc_ref[...]
    u = _modulate(h_ref[...], sh, sc)
    tm = u.shape[0]
    prev = _modulate(hp_ref[...], sh, sc)[7:8, :]
    prev = jnp.where(pl.program_id(1) == 0, 0.0, prev)
    row = lax.broadcasted_iota(jnp.int32, (tm, 1), 0)
    xx = jnp.where(row == 0, prev, pltpu.roll(u, 1, 0)) - u
    mix = lambda i: (u + xx * mu_ref[i:i + 1, :]).astype(BF16)
    r = _dot(mix(0), wrkv_ref[0])
    k = _dot(mix(1), wrkv_ref[1])
    v = _dot(mix(2), wrkv_ref[2])
    wl = w0_ref[...] + _dot(jnp.tanh(_dot(mix(3), w1_ref[...])).astype(BF16), w2_ref[...])
    nwl = -wl
    w_log = -(jnp.maximum(nwl, 0.0) + jnp.log(1.0 + jnp.exp(-jnp.abs(nwl)))) - 0.5
    lw_ref[...] = -jnp.exp(w_log)
    a = jax.nn.sigmoid(a0_ref[...] + _dot(_dot(mix(4), a1_ref[...]).astype(BF16), a2_ref[...]))
    g_ref[...] = _dot(jax.nn.sigmoid(_dot(mix(5), g1_ref[...])).astype(BF16), g2_ref[...])
    kk = k * kk_ref[...]
    nrm = jnp.sqrt(_gsum(kk * kk, gm_ref, gmt_ref))
    kk = kk / jnp.maximum(nrm, 1e-12)
    r_ref[...] = r
    k_ref[...] = k * (1.0 + (a - 1.0) * ka_ref[...])
    v_ref[...] = v
    kkn_ref[...] = kk
    b_ref[...] = kk * a


def _head_indicator(D):
    gm = (np.arange(D)[:, None] // RWKV_HEAD == np.arange(D // RWKV_HEAD)[None, :]).astype(np.float32)
    return jnp.asarray(gm, BF16), jnp.asarray(gm.T, BF16)


def _rwkv_in(h, mod, mu, w_rkv, w0, w1, w2, a0, a1, a2, g1, g2, k_k, k_a, tm=256):
    B, S, D = h.shape
    gm, gmt = _head_indicator(D)
    row = lambda a: a.astype(F32).reshape(1, D)
    full = lambda a: pl.BlockSpec(a.shape, lambda b, i: (0,) * a.ndim)
    mspec = lambda k: pl.BlockSpec((None, None, 1, D), lambda b, i: (b, k, 0, 0))
    tile = pl.BlockSpec((None, tm, D), lambda b, i: (b, i, 0))
    params = [mu.astype(F32), w_rkv.astype(BF16), row(w0), w1.astype(BF16), w2.astype(BF16),
              row(a0), a1.astype(BF16), a2.astype(BF16), g1.astype(BF16), g2.astype(BF16),
              row(k_k), row(k_a), gm, gmt]
    return pl.pallas_call(
        _rwkv_in_kernel,
        grid=(B, S // tm),
        in_specs=[tile,
                  pl.BlockSpec((None, 8, D), lambda b, i: (b, jnp.maximum(i * (tm // 8) - 1, 0), 0)),
                  mspec(3), mspec(4)] + [full(p) for p in params],
        out_specs=[tile] * 7,
        out_shape=[jax.ShapeDtypeStruct((B, S, D), F32)] * 7,
        compiler_params=_cparams("parallel", "parallel"),
    )(h, h, mod, mod, *params)


def _rwkv_scan_kernel(r_ref, lw_ref, k_ref, v_ref, kk_ref, b_ref, y_ref, zt_ref):
    C, P, L = RW_C, RW_PACK, RW_LANES
    N = P * C
    assert C == RWKV_HEAD
    n_packs = r_ref.shape[1] // L
    n_chunks = r_ref.shape[0] // C
    units = [(p, c) for p in range(n_packs) for c in range(n_chunks)]

    @pl.when(pl.program_id(1) == 0)
    def _():
        zt_ref[...] = jnp.zeros(zt_ref.shape, F32)

    lane = lax.broadcasted_iota(jnp.int32, (1, L), 1)
    hmask = [lane // C == hd for hd in range(P)]
    bf = lambda x: x.astype(BF16)
    stack = lambda x: jnp.concatenate([jnp.where(hmask[hd], x, jnp.zeros_like(x)) for hd in range(P)], axis=0)
    t_i = lax.broadcasted_iota(jnp.int32, (C, N), 0)
    i_i = lax.broadcasted_iota(jnp.int32, (C, N), 1) % C
    low_strict = t_i > i_i
    low_incl = t_i >= i_i
    eye_w = jnp.where(t_i == i_i, 1.0, 0.0)
    bd_mask = (lax.broadcasted_iota(jnp.int32, (L, L), 0) // C) == (lax.broadcasted_iota(jnp.int32, (L, L), 1) // C)
    tri = jnp.where(lax.broadcasted_iota(jnp.int32, (C, C), 0) >= lax.broadcasted_iota(jnp.int32, (C, C), 1),
                    1.0, 0.0).astype(BF16)
    tile = lambda ref, u: ref[u[1] * C:(u[1] + 1) * C, u[0] * L:(u[0] + 1) * L]

    cw = {}
    tri3 = jnp.concatenate([tri, tri, tri], axis=1)
    for u in units:
        cw[u] = _dot(tri3, jnp.concatenate(_split3(tile(lw_ref, u)), axis=0))
    al, rt, bw, kw, v, vs, aa, w_end = {}, {}, {}, {}, {}, {}, {}, {}
    for u in units:
        lw = tile(lw_ref, u)
        w_incl = jnp.exp(cw[u])
        w_excl = jnp.exp(cw[u] - lw)
        w_inv = jnp.exp(-cw[u])
        w_end[u] = w_incl[C - 1:C, :]
        v[u] = bf(tile(v_ref, u))
        vs[u] = stack(v[u])
        al[u] = bf(-tile(kk_ref, u) * w_excl)
        rt[u] = tile(r_ref, u) * w_incl
        bh = tile(b_ref, u) * w_inv
        kh = tile(k_ref, u) * w_inv
        bw[u] = bf(bh * w_end[u])
        kw[u] = bf(kh * w_end[u])
        aa[u] = _dot_nt(jnp.concatenate([al[u], bf(rt[u])], axis=0),
                        jnp.concatenate([stack(bf(bh)), stack(bf(kh))], axis=0))
    a_ak, a_r, pw, t = {}, {}, {}, {}
    for u in units:
        a_ab = jnp.where(low_strict, aa[u][:C, :N], 0.0)
        a_ak[u] = bf(jnp.where(low_strict, aa[u][:C, N:], 0.0))
        a_r[u] = bf(jnp.concatenate([jnp.where(low_incl, aa[u][C:, :N], 0.0),
                                     jnp.where(low_incl, aa[u][C:, N:], 0.0)], axis=1))
        pw[u] = a_ab
        t[u] = eye_w + a_ab
    n_fac = int(np.log2(C))
    akv = {}
    for u in units:
        pwb = bf(pw[u])
        pw[u] = _dot(pwb, stack(pwb))
        akv[u] = _dot(a_ak[u], vs[u])
    for s in range(2, n_fac + 1):
        for u in units:
            pwb = bf(pw[u])
            rhs = stack(pwb)
            if s < n_fac:
                both = _dot(jnp.concatenate([bf(t[u]), pwb], axis=0), rhs)
                t[u] = t[u] + both[:C]
                pw[u] = both[C:]
            else:
                t[u] = t[u] + _dot(bf(t[u]), rhs)
    pm, q = {}, {}
    for u in units:
        tb = bf(t[u])
        pm[u] = bf(_dot(tb, stack(al[u])))
        q[u] = bf(_dot(tb, stack(bf(akv[u]))))
    pr, y0, bk = {}, {}, {}
    for u in units:
        rp = bf(rt[u] + _dot(a_r[u][:, :N], stack(pm[u])))
        pr[u] = jnp.concatenate([pm[u], rp], axis=0)
        y0[u] = _dot(a_r[u], jnp.concatenate([stack(q[u]), vs[u]], axis=0))
        bk[u] = jnp.concatenate([bw[u], kw[u]], axis=0)
    zt = [zt_ref[p] for p in range(n_packs)]
    for c in range(n_chunks):
        for p in range(n_packs):
            u = (p, c)
            uy = _dot_nt(pr[u], bf(zt[p]))
            y_ref[c * C:(c + 1) * C, p * L:(p + 1) * L] = uy[C:] + y0[u]
            upd = _dot_tn(jnp.concatenate([bf(uy[:C] + q[u]), v[u]], axis=0), bk[u])
            zt[p] = zt[p] * w_end[u] + jnp.where(bd_mask, upd, 0.0)
    for p in range(n_packs):
        zt_ref[p] = zt[p]


def _rwkv_scan(r, lw, k, v, kk, b):
    B, S, D = r.shape
    tile = pl.BlockSpec((None, RW_TB, D), lambda bb, t: (bb, t, 0))
    return pl.pallas_call(
        _rwkv_scan_kernel,
        grid=(B, S // RW_TB),
        in_specs=[tile] * 6,
        out_specs=tile,
        out_shape=jax.ShapeDtypeStruct((B, S, D), F32),
        scratch_shapes=[pltpu.VMEM((D // RW_LANES, RW_LANES, RW_LANES), F32)],
        compiler_params=_cparams("parallel", "arbitrary"),
    )(r, lw, k, v, kk, b)


def _rwkv_out_kernel(y_ref, r_ref, k_ref, v_ref, g_ref, lnw_ref, lnb_ref, rk_ref, gm_ref, gmt_ref,
                     w_ref, h_ref, gt_ref, o_ref):
    y = y_ref[...]
    inv_n = 1.0 / RWKV_HEAD
    yc = y - _gsum(y, gm_ref, gmt_ref) * inv_n
    var = _gsum(yc * yc, gm_ref, gmt_ref) * inv_n
    yn = yc * lax.rsqrt(var + GN_EPS) * lnw_ref[...] + lnb_ref[...]
    v = v_ref[...]
    bonus = _gsum(r_ref[...] * k_ref[...] * rk_ref[...], gm_ref, gmt_ref) * v
    out = ((yn + bonus) * g_ref[...]).astype(BF16)
    o_ref[...] = h_ref[...] + gt_ref[...] * _dot(out, w_ref[...])


def _rwkv_out(y, r, k, v, g, ln_w, ln_b, r_k, w_out, h, mod, tm=256):
    B, S, D = h.shape
    gm, gmt = _head_indicator(D)
    row = lambda a: a.astype(F32).reshape(1, D)
    full = lambda a: pl.BlockSpec(a.shape, lambda b, i: (0,) * a.ndim)
    tile = pl.BlockSpec((None, tm, D), lambda b, i: (b, i, 0))
    params = [row(ln_w), row(ln_b), row(r_k), gm, gmt, w_out.astype(BF16)]
    return pl.pallas_call(
        _rwkv_out_kernel,
        grid=(B, S // tm),
        in_specs=[tile] * 5 + [full(p) for p in params] + [tile, pl.BlockSpec((None, None, 1, D), lambda b, i: (b, 5, 0, 0))],
        out_specs=tile,
        out_shape=jax.ShapeDtypeStruct((B, S, D), F32),
        compiler_params=_cparams("parallel", "parallel"),
    )(y, r, k, v, g, *params, h, mod)


def _rwkv_mixer(h, mod, mu, w_rkv, w0, w1, w2, a0, a1, a2, g1, g2, k_k, k_a, r_k, ln_w, ln_b, w_out):
    r, lw, k, v, kk, b, g = _rwkv_in(h, mod, mu, w_rkv, w0, w1, w2, a0, a1, a2, g1, g2, k_k, k_a)
    y = _rwkv_scan(r, lw, k, v, kk, b)
    return _rwkv_out(y, r, k, v, g, ln_w, ln_b, r_k, w_out, h, mod)


def kernel(x, c, ada_w, ada_b, ffn_w_gu, ffn_w_down, dsa_w_in, dsa_q_norm, dsa_k_norm, dsa_ik_norm, dsa_w_out, rwkv_mu, rwkv_w_rkv, rwkv_w0, rwkv_w1, rwkv_w2, rwkv_a0, rwkv_a1, rwkv_a2, rwkv_g1, rwkv_g2, rwkv_k_k, rwkv_k_a, rwkv_r_k, rwkv_ln_w, rwkv_ln_b, rwkv_w_out):
    B, S, D = x.shape
    depth = ada_w.shape[0]
    mods = _ada_mod(c, ada_w, ada_b)
    h = x
    for i in range(depth):
        mod = mods[i].reshape(B, N_MOD, 1, D)
        j = i // 2
        h = _ffn(h, mod, 0, ffn_w_gu[i, 0].astype(BF16), ffn_w_down[i, 0].astype(BF16))
        if i % 2 == 0:
            h = _dsa_mixer(h, mod, dsa_w_in[j], dsa_q_norm[j], dsa_k_norm[j], dsa_ik_norm[j], dsa_w_out[j])
        else:
            h = _rwkv_mixer(h, mod, rwkv_mu[j], rwkv_w_rkv[j], rwkv_w0[j], rwkv_w1[j], rwkv_w2[j], rwkv_a0[j],
                            rwkv_a1[j], rwkv_a2[j], rwkv_g1[j], rwkv_g2[j], rwkv_k_k[j], rwkv_k_a[j],
                            rwkv_r_k[j], rwkv_ln_w[j], rwkv_ln_b[j], rwkv_w_out[j])
        h = _ffn(h, mod, 6, ffn_w_gu[i, 1].astype(BF16), ffn_w_down[i, 1].astype(BF16))
    return h
```

```python
import functools

import jax
import jax.numpy as jnp
import numpy as np
from jax import lax
from jax.experimental import pallas as pl
from jax.experimental.pallas import tpu as pltpu

F32 = jnp.float32
BF16 = jnp.bfloat16

RMS_EPS = 1e-6
N_MOD = 9
VMEM_LIMIT_BYTES = 56 * 1024 * 1024

N_HEADS = 16
HEAD_DIM = 64
N_KV_HEADS = 4
GROUP = N_HEADS // N_KV_HEADS
IDX_HEADS = 8
IDX_DIM = 64
TOPK_MAX = 256
LANE = 128
LOG2E = 1.4426950408889634


def _cparams(*sem):
    return pltpu.CompilerParams(dimension_semantics=sem, vmem_limit_bytes=VMEM_LIMIT_BYTES)


def _dot(a, b):
    return jnp.dot(a, b, preferred_element_type=F32)


def _dot_nt(a, b):
    return lax.dot_general(a, b, (((1,), (1,)), ((), ())), preferred_element_type=F32)


def _dot_tn(a, b):
    return lax.dot_general(a, b, (((0,), (0,)), ((), ())), preferred_element_type=F32)


def _modulate(h, sh, sc):
    ms = jnp.mean(h * h, axis=-1, keepdims=True)
    return h * lax.rsqrt(ms + RMS_EPS) * (1.0 + sc) + sh


def _ada_kernel(c_ref, w_ref, b_ref, o_ref):
    c = c_ref[...]
    ca = (c * jax.nn.sigmoid(c)).astype(BF16)
    o_ref[...] = _dot(ca, w_ref[...].astype(BF16)) + b_ref[...]


def _ada_mod(c, ada_w, ada_b):
    L, D, N = ada_w.shape
    B = c.shape[0]
    tn = 1536
    return pl.pallas_call(
        _ada_kernel,
        grid=(L, N // tn),
        in_specs=[
            pl.BlockSpec((B, D), lambda l, j: (0, 0)),
            pl.BlockSpec((None, D, tn), lambda l, j: (l, 0, j)),
            pl.BlockSpec((None, 1, tn), lambda l, j: (l, 0, j)),
        ],
        out_specs=pl.BlockSpec((None, B, tn), lambda l, j: (l, 0, j)),
        out_shape=jax.ShapeDtypeStruct((L, B, N), F32),
        compiler_params=_cparams("parallel", "parallel"),
    )(c, ada_w, ada_b.reshape(L, 1, N))


FF_CHUNK = 256


def _ffn_kernel(h_ref, sh_ref, sc_ref, gt_ref, wgu_ref, wd_ref, o_ref, *, d_ff):
    h = h_ref[...]
    u = _modulate(h, sh_ref[...], sc_ref[...]).astype(BF16)
    acc = None
    for c in range(d_ff // FF_CHUNK):
        lo = c * FF_CHUNK
        g = _dot(u, wgu_ref[:, lo:lo + FF_CHUNK])
        up = _dot(u, wgu_ref[:, d_ff + lo:d_ff + lo + FF_CHUNK])
        a = (g * jax.nn.sigmoid(g) * up).astype(BF16)
        part = _dot(a, wd_ref[lo:lo + FF_CHUNK, :])
        acc = part if acc is None else acc + part
    o_ref[...] = h + (0.5 * gt_ref[...]) * acc


def _ffn(h, mod, k0, w_gu, w_down, tm=512):
    B, S, D = h.shape
    d_ff = w_down.shape[0]
    mspec = lambda k: pl.BlockSpec((None, None, 1, D), lambda b, i: (b, k, 0, 0))
    return pl.pallas_call(
        functools.partial(_ffn_kernel, d_ff=d_ff),
        grid=(B, S // tm),
        in_specs=[
            pl.BlockSpec((None, tm, D), lambda b, i: (b, i, 0)),
            mspec(k0), mspec(k0 + 1), mspec(k0 + 2),
            pl.BlockSpec((D, 2 * d_ff), lambda b, i: (0, 0), pipeline_mode=pl.Buffered(1)),
            pl.BlockSpec((d_ff, D), lambda b, i: (0, 0), pipeline_mode=pl.Buffered(1)),
        ],
        out_specs=pl.BlockSpec((None, tm, D), lambda b, i: (b, i, 0)),
        out_shape=jax.ShapeDtypeStruct((B, S, D), F32),
        compiler_params=_cparams("parallel", "parallel"),
    )(h, mod, mod, mod, w_gu, w_down)


def _dsa_in_kernel(h_ref, sh_ref, sc_ref, wqT_ref, wk_ref, wvT_ref, wiqT_ref, wik_ref, wiwT_ref,
                   qn_ref, kn_ref, ikn_ref,
                   qT_ref, k_ref, vT_ref, iqT_ref, ik_ref, iwT_ref):
    u = _modulate(h_ref[...], sh_ref[...], sc_ref[...]).astype(BF16)
    tm = u.shape[0]
    qT = _dot_nt(wqT_ref[...], u).reshape(N_HEADS, HEAD_DIM, tm)
    ms = jnp.mean(qT * qT, axis=1, keepdims=True)
    qT = qT * lax.rsqrt(ms + RMS_EPS) * qn_ref[...][None] * (HEAD_DIM ** -0.5 * LOG2E)
    qT_ref[...] = qT.reshape(N_HEADS * HEAD_DIM, tm).astype(BF16)
    k = _dot(u, wk_ref[...])
    kn = kn_ref[...]
    for j in range(N_KV_HEADS):
        kj = k[:, j * LANE:(j + 1) * LANE]
        ms = jnp.sum(kj * kj, axis=-1, keepdims=True) * (1.0 / HEAD_DIM)
        k_ref[:, j * LANE:(j + 1) * LANE] = (kj * lax.rsqrt(ms + RMS_EPS) * kn).astype(BF16)
    vT_ref[...] = _dot_nt(wvT_ref[...], u).astype(BF16)
    iqT_ref[...] = _dot_nt(wiqT_ref[...], u).astype(BF16)
    ik = _dot(u, wik_ref[...])
    ms = jnp.sum(ik * ik, axis=-1, keepdims=True) * (1.0 / IDX_DIM)
    ik_ref[...] = (ik * lax.rsqrt(ms + RMS_EPS) * ikn_ref[...]).astype(BF16)
    iwT_ref[...] = _dot_nt(wiwT_ref[...], u) * (IDX_HEADS ** -0.5 * IDX_DIM ** -0.5)


DSA_KC = 256
DSA_QB = 256


def _dsa_in(h, mod, w_in, q_norm, k_norm, ik_norm, tm=DSA_KC):
    B, S, D = h.shape
    QW, KW = N_HEADS * HEAD_DIM, N_KV_HEADS * HEAD_DIM
    IQW = IDX_HEADS * IDX_DIM
    o1, o2, o3, o4, o5 = QW, QW + KW, QW + 2 * KW, QW + 2 * KW + IQW, QW + 2 * KW + IQW + IDX_DIM
    wb = w_in.astype(BF16)
    wqT = wb[:, :o1].T
    wk = jnp.pad(wb[:, o1:o2].reshape(D, N_KV_HEADS, HEAD_DIM), ((0, 0), (0, 0), (0, LANE - HEAD_DIM))).reshape(D, N_KV_HEADS * LANE)
    wvT = wb[:, o2:o3].T
    wiqT = wb[:, o3:o4].T
    wik = jnp.pad(wb[:, o4:o5], ((0, 0), (0, LANE - IDX_DIM)))
    wiwT = wb[:, o5:].T
    qn = q_norm.astype(F32).reshape(HEAD_DIM, 1)
    kn = jnp.pad(k_norm.astype(F32), (0, LANE - HEAD_DIM)).reshape(1, LANE)
    ikn = jnp.pad(ik_norm.astype(F32), (0, LANE - IDX_DIM)).reshape(1, LANE)
    mspec = lambda k: pl.BlockSpec((None, None, 1, D), lambda b, i: (b, k, 0, 0))
    full = lambda a: pl.BlockSpec(a.shape, lambda b, i: (0,) * a.ndim)
    nc = S // tm
    return pl.pallas_call(
        _dsa_in_kernel,
        grid=(B, nc),
        in_specs=[pl.BlockSpec((None, tm, D), lambda b, i: (b, i, 0)), mspec(3), mspec(4),
                  full(wqT), full(wk), full(wvT), full(wiqT), full(wik), full(wiwT),
                  full(qn), full(kn), full(ikn)],
        out_specs=[
            pl.BlockSpec((None, QW, tm), lambda b, i: (b, 0, i)),
            pl.BlockSpec((None, tm, N_KV_HEADS * LANE), lambda b, i: (b, i, 0)),
            pl.BlockSpec((None, None, KW, tm), lambda b, i: (b, i, 0, 0)),
            pl.BlockSpec((None, IQW, tm), lambda b, i: (b, 0, i)),
            pl.BlockSpec((None, tm, LANE), lambda b, i: (b, i, 0)),
            pl.BlockSpec((None, IDX_HEADS, tm), lambda b, i: (b, 0, i)),
        ],
        out_shape=[
            jax.ShapeDtypeStruct((B, QW, S), BF16),
            jax.ShapeDtypeStruct((B, S, N_KV_HEADS * LANE), BF16),
            jax.ShapeDtypeStruct((B, nc, KW, tm), BF16),
            jax.ShapeDtypeStruct((B, IQW, S), BF16),
            jax.ShapeDtypeStruct((B, S, LANE), BF16),
            jax.ShapeDtypeStruct((B, IDX_HEADS, S), F32),
        ],
        compiler_params=_cparams("parallel", "parallel"),
    )(h, mod, mod, wqT, wk, wvT, wiqT, wik, wiwT, qn, kn, ikn)


NEG_BIG = -(2.0 ** 100)
DSA_VPAD = 16


def _ukey_to_float(ukey):
    key = ukey ^ np.int32(-2 ** 31)
    bits = key ^ ((key >> 31) & np.int32(0x7FFFFFFF))
    return lax.bitcast_convert_type(bits, F32)


def _fold_rows(x, op, rows):
    while x.shape[0] > rows:
        half = x.shape[0] // 2
        x = op(x[:half], x[half:])
    return x


def _dsa_core_kernel(iqT_ref, iwT_ref, qT_ref, ik_ref, k_ref, vT_ref, oT_ref,
                     sc_ref, qa_ref, iqa_ref, d_ref, st_ref, p_ref, m_ref, acc_ref, *, k_top, n_chunks):
    KC, QB = DSA_KC, DSA_QB
    qi = pl.program_id(1)
    q0 = qi * QB
    nk = (q0 + QB + KC - 1) // KC
    qpos = q0 + lax.broadcasted_iota(jnp.int32, (1, QB), 1)
    zeros_h = jnp.zeros((LANE - HEAD_DIM, QB), BF16)

    for h in range(IDX_HEADS):
        iqa_ref[:, h * QB:(h + 1) * QB] = jnp.concatenate(
            [iqT_ref[h * IDX_DIM:(h + 1) * IDX_DIM, :], zeros_h], axis=0)
    for j in range(N_KV_HEADS):
        for g in range(GROUP):
            hh = j * GROUP + g
            qa_ref[j, :, g * QB:(g + 1) * QB] = jnp.concatenate(
                [qT_ref[hh * HEAD_DIM:(hh + 1) * HEAD_DIM, :], zeros_h], axis=0)
    iw = iwT_ref[...]

    npair = (nk + 1) // 2

    def clamp_chunk(c):
        return jnp.minimum(c, n_chunks - 1)

    def idx_dot(c, slot):
        k0 = pl.multiple_of(c * KC, KC)
        d_ref[slot] = _dot(ik_ref[pl.ds(k0, KC), :], iqa_ref[...])

    def idx_score(c, slot):
        k0 = pl.multiple_of(c * KC, KC)
        s = jnp.maximum(d_ref[slot, :, 0:QB], 0.0) * iw[0:1, :]
        for h in range(1, IDX_HEADS):
            s = s + jnp.maximum(d_ref[slot, :, h * QB:(h + 1) * QB], 0.0) * iw[h:h + 1, :]
        kpos = k0 + lax.broadcasted_iota(jnp.int32, (KC, 1), 0)
        sc_ref[pl.ds(k0, KC), :] = jnp.where(kpos <= qpos, s, -jnp.inf)

    idx_dot(0, 0)

    def p1(i, carry):
        c = 2 * i
        idx_dot(c + 1, 1)
        idx_score(c, 0)
        idx_dot(clamp_chunk(c + 2), 0)
        idx_score(c + 1, 1)
        return carry

    lax.fori_loop(0, npair, p1, 0)

    def count_not(pred):
        def body(c, acc):
            k0 = pl.multiple_of(c * KC, KC)
            ind = jnp.where(pred(sc_ref[pl.ds(k0, KC), :]), 0.0, 1.0)
            return acc + _fold_rows(ind, jnp.add, 8)
        acc = lax.fori_loop(0, nk, body, jnp.zeros((8, QB), F32))
        return jnp.sum(acc, axis=0, keepdims=True)

    ukey = jnp.zeros((1, QB), jnp.int32)
    for bit in range(31, -1, -1):
        cand = ukey | np.int32(np.uint32(1 << bit).astype(np.int32))
        t = _ukey_to_float(cand)
        cnt = count_not(lambda s: s < t)
        ukey = jnp.where(cnt >= k_top, cand, ukey)
    thr = _ukey_to_float(ukey)
    n_staged = (nk * KC).astype(F32)
    need = k_top - (n_staged - count_not(lambda s: s > thr))

    m_ref[...] = jnp.full(m_ref.shape, NEG_BIG, F32)
    acc_ref[...] = jnp.zeros(acc_ref.shape, F32)
    r_i = lax.broadcasted_iota(jnp.int32, (KC, KC), 0)
    c_i = lax.broadcasted_iota(jnp.int32, (KC, KC), 1)
    tril = jnp.where(c_i <= r_i, 1.0, 0.0).astype(BF16)
    ones_rows = jnp.ones((DSA_VPAD, KC), BF16)

    HALF = GROUP * QB // 2

    def logits(c, slot, j, half):
        k0 = pl.multiple_of(c * KC, KC)
        cols = slice(half * HALF, (half + 1) * HALF)
        st_ref[slot, j, :, cols] = _dot(k_ref[pl.ds(k0, KC), j * LANE:(j + 1) * LANE],
                                        qa_ref[j, :, cols]).astype(BF16)

    for j in range(N_KV_HEADS):
        logits(0, 0, j, 0)
        logits(0, 0, j, 1)

    def attend(c, slot, tie_seen, c_next):
        k0 = pl.multiple_of(c * KC, KC)
        s = sc_ref[pl.ds(k0, KC), :]
        eq = s == thr
        eqf = jnp.where(eq, 1.0, 0.0)
        incl = _dot(tril, eqf.astype(BF16))
        rank = tie_seen + incl - eqf
        kpos = k0 + lax.broadcasted_iota(jnp.int32, (KC, 1), 0)
        sel = jnp.logical_and(jnp.logical_or(s > thr, jnp.logical_and(eq, rank < need)), kpos <= qpos)
        bias = jnp.where(sel, 0.0, NEG_BIG).astype(BF16)
        for j in range(N_KV_HEADS):
            alphas = []
            for g in range(GROUP):
                sl = slice(g * QB, (g + 1) * QB)
                sg = st_ref[slot, j, :, sl] + bias
                m_old = m_ref[j, :, sl]
                mx = jnp.max(_fold_rows(sg, jnp.maximum, 16), axis=0, keepdims=True).astype(F32)
                m_new = jnp.maximum(m_old, mx)
                alphas.append(jnp.exp2(m_old - m_new))
                m_ref[j, :, sl] = m_new
                p_ref[j, :, sl] = jnp.exp2(sg - m_new.astype(BF16))
                if g % 2 == 1:
                    logits(c_next, 1 - slot, j, g // 2)
            v_aug = jnp.concatenate([vT_ref[c, j * HEAD_DIM:(j + 1) * HEAD_DIM, :], ones_rows], axis=0)
            acc_ref[j] = acc_ref[j] * jnp.concatenate(alphas, axis=1) + _dot(v_aug, p_ref[j])
        return tie_seen + incl[KC - 1:KC, :]

    def p3(i, tie_seen):
        c = 2 * i
        tie_seen = attend(c, 0, tie_seen, c + 1)
        return attend(c + 1, 1, tie_seen, clamp_chunk(c + 2))

    lax.fori_loop(0, npair, p3, jnp.zeros((1, QB), F32))

    for j in range(N_KV_HEADS):
        o = acc_ref[j, 0:HEAD_DIM, :] / acc_ref[j, HEAD_DIM:HEAD_DIM + 1, :]
        for g in range(GROUP):
            hh = j * GROUP + g
            oT_ref[hh * HEAD_DIM:(hh + 1) * HEAD_DIM, :] = o[:, g * QB:(g + 1) * QB].astype(BF16)


def _dsa_core(qT, k, vT, iqT, ik, iwT):
    B, QW, S = qT.shape
    KC, QB = DSA_KC, DSA_QB
    k_top = min(TOPK_MAX, S // 4)
    nc = S // KC
    return pl.pallas_call(
        functools.partial(_dsa_core_kernel, k_top=k_top, n_chunks=nc),
        grid=(B, S // QB),
        in_specs=[
            pl.BlockSpec((None, IDX_HEADS * IDX_DIM, QB), lambda b, i: (b, 0, i)),
            pl.BlockSpec((None, IDX_HEADS, QB), lambda b, i: (b, 0, i)),
            pl.BlockSpec((None, QW, QB), lambda b, i: (b, 0, i)),
            pl.BlockSpec((None, S, LANE), lambda b, i: (b, 0, 0)),
            pl.BlockSpec((None, S, N_KV_HEADS * LANE), lambda b, i: (b, 0, 0)),
            pl.BlockSpec((None, nc, N_KV_HEADS * HEAD_DIM, KC), lambda b, i: (b, 0, 0, 0)),
        ],
        out_specs=pl.BlockSpec((None, QW, QB), lambda b, i: (b, 0, i)),
        out_shape=jax.ShapeDtypeStruct((B, QW, S), BF16),
        scratch_shapes=[
            pltpu.VMEM((S, QB), F32),
            pltpu.VMEM((N_KV_HEADS, LANE, GROUP * QB), BF16),
            pltpu.VMEM((LANE, IDX_HEADS * QB), BF16),
            pltpu.VMEM((2, KC, IDX_HEADS * QB), F32),
            pltpu.VMEM((2, N_KV_HEADS, KC, GROUP * QB), BF16),
            pltpu.VMEM((N_KV_HEADS, KC, GROUP * QB), BF16),
            pltpu.VMEM((N_KV_HEADS, 1, GROUP * QB), F32),
            pltpu.VMEM((N_KV_HEADS, HEAD_DIM + DSA_VPAD, GROUP * QB), F32),
        ],
        compiler_params=_cparams("parallel", "arbitrary"),
    )(iqT, iwT, qT, ik, k, vT)


def _out_T_kernel(xT_ref, w_ref, h_ref, gt_ref, o_ref):
    o_ref[...] = h_ref[...] + gt_ref[...] * _dot_tn(xT_ref[...], w_ref[...])


def _dsa_out(oT, w_out, h, mod, tm=512):
    B, S, D = h.shape
    QW = oT.shape[1]
    return pl.pallas_call(
        _out_T_kernel,
        grid=(B, S // tm),
        in_specs=[
            pl.BlockSpec((None, QW, tm), lambda b, i: (b, 0, i)),
            pl.BlockSpec((QW, D), lambda b, i: (0, 0)),
            pl.BlockSpec((None, tm, D), lambda b, i: (b, i, 0)),
            pl.BlockSpec((None, None, 1, D), lambda b, i: (b, 5, 0, 0)),
        ],
        out_specs=pl.BlockSpec((None, tm, D), lambda b, i: (b, i, 0)),
        out_shape=jax.ShapeDtypeStruct((B, S, D), F32),
        compiler_params=_cparams("parallel", "parallel"),
    )(oT, w_out.astype(BF16), h, mod)


def _dsa_mixer(h, mod, w_in, q_norm, k_norm, ik_norm, w_out):
    qT, k, vT, iqT, ik, iwT = _dsa_in(h, mod, w_in, q_norm, k_norm, ik_norm)
    oT = _dsa_core(qT, k, vT, iqT, ik, iwT)
    return _dsa_out(oT, w_out, h, mod)


RWKV_HEAD = 64
GN_EPS = 64e-5
RW_PACK = 4
RW_LANES = RW_PACK * RWKV_HEAD
RW_C = 64
RW_TB = 256


def _split3(x):
    hi = x.astype(BF16)
    r1 = x - hi.astype(F32)
    mid = r1.astype(BF16)
    lo = (r1 - mid.astype(F32)).astype(BF16)
    return hi, mid, lo


def _gsum(x, g_ref, gt_ref):
    hi = x.astype(BF16)
    lo = (x - hi.astype(F32)).astype(BF16)
    s = _dot(jnp.concatenate([hi, lo], axis=1), g_ref[...])
    return _dot(jnp.concatenate(_split3(s), axis=1), gt_ref[...])


def _rwkv_in_kernel(h_ref, hp_ref, sh_ref, sc_ref, mu_ref, wrkv_ref, w0_ref, w1_ref, w2_ref,
                    a0_ref, a1_ref, a2_ref, g1_ref, g2_ref, kk_ref, ka_ref, gm_ref, gmt_ref,
                    r_ref, lw_ref, k_ref, v_ref, kkn_ref, b_ref, g_ref):
    sh, sc = sh_ref[...], sc_ref[...]
    u = _modulate(h_ref[...], sh, sc)
    tm = u.shape[0]
    prev = _modulate(hp_ref[...], sh, sc)[7:8, :]
    prev = jnp.where(pl.program_id(1) == 0, 0.0, prev)
    row = lax.broadcasted_iota(jnp.int32, (tm, 1), 0)
    xx = jnp.where(row == 0, prev, pltpu.roll(u, 1, 0)) - u
    mix = lambda i: (u + xx * mu_ref[i:i + 1, :]).astype(BF16)
    r = _dot(mix(0), wrkv_ref[0])
    k = _dot(mix(1), wrkv_ref[1])
    v = _dot(mix(2), wrkv_ref[2])
    wl = w0_ref[...] + _dot(jnp.tanh(_dot(mix(3), w1_ref[...])).astype(BF16), w2_ref[...])
    nwl = -wl
    w_log = -(jnp.maximum(nwl, 0.0) + jnp.log(1.0 + jnp.exp(-jnp.abs(nwl)))) - 0.5
    lw_ref[...] = -jnp.exp(w_log)
    a = jax.nn.sigmoid(a0_ref[...] + _dot(_dot(mix(4), a1_ref[...]).astype(BF16), a2_ref[...]))
    g_ref[...] = _dot(jax.nn.sigmoid(_dot(mix(5), g1_ref[...])).astype(BF16), g2_ref[...])
    kk = k * kk_ref[...]
    nrm = jnp.sqrt(_gsum(kk * kk, gm_ref, gmt_ref))
    kk = kk / jnp.maximum(nrm, 1e-12)
    r_ref[...] = r
    k_ref[...] = k * (1.0 + (a - 1.0) * ka_ref[...])
    v_ref[...] = v
    kkn_ref[...] = kk
    b_ref[...] = kk * a


def _head_indicator(D):
    gm = (np.arange(D)[:, None] // RWKV_HEAD == np.arange(D // RWKV_HEAD)[None, :]).astype(np.float32)
    return jnp.asarray(np.concatenate([gm, gm], axis=0), BF16), jnp.asarray(np.concatenate([gm.T] * 3, axis=0), BF16)


def _rwkv_in(h, mod, mu, w_rkv, w0, w1, w2, a0, a1, a2, g1, g2, k_k, k_a, tm=256):
    B, S, D = h.shape
    gm, gmt = _head_indicator(D)
    row = lambda a: a.astype(F32).reshape(1, D)
    full = lambda a: pl.BlockSpec(a.shape, lambda b, i: (0,) * a.ndim)
    mspec = lambda k: pl.BlockSpec((None, None, 1, D), lambda b, i: (b, k, 0, 0))
    tile = pl.BlockSpec((None, tm, D), lambda b, i: (b, i, 0))
    params = [mu.astype(F32), w_rkv.astype(BF16), row(w0), w1.astype(BF16), w2.astype(BF16),
              row(a0), a1.astype(BF16), a2.astype(BF16), g1.astype(BF16), g2.astype(BF16),
              row(k_k), row(k_a), gm, gmt]
    return pl.pallas_call(
        _rwkv_in_kernel,
        grid=(B, S // tm),
        in_specs=[tile,
                  pl.BlockSpec((None, 8, D), lambda b, i: (b, jnp.maximum(i * (tm // 8) - 1, 0), 0)),
                  mspec(3), mspec(4)] + [full(p) for p in params],
        out_specs=[tile] * 7,
        out_shape=[jax.ShapeDtypeStruct((B, S, D), F32)] * 7,
        compiler_params=_cparams("parallel", "parallel"),
    )(h, h, mod, mod, *params)


def _rwkv_scan_kernel(r_ref, lw_ref, k_ref, v_ref, kk_ref, b_ref, y_ref, zt_ref):
    C, P, L = RW_C, RW_PACK, RW_LANES
    N = P * C
    assert C == RWKV_HEAD
    n_packs = r_ref.shape[1] // L
    n_chunks = r_ref.shape[0] // C
    units = [(p, c) for p in range(n_packs) for c in range(n_chunks)]

    @pl.when(pl.program_id(1) == 0)
    def _():
        zt_ref[...] = jnp.zeros(zt_ref.shape, F32)

    lane = lax.broadcasted_iota(jnp.int32, (1, L), 1)
    hmask = [lane // C == hd for hd in range(P)]
    bf = lambda x: x.astype(BF16)
    stack = lambda x: jnp.concatenate([jnp.where(hmask[hd], x, jnp.zeros_like(x)) for hd in range(P)], axis=0)
    t_i = lax.broadcasted_iota(jnp.int32, (C, N), 0)
    i_i = lax.broadcasted_iota(jnp.int32, (C, N), 1) % C
    low_strict = t_i > i_i
    low_incl = t_i >= i_i
    eye_w = jnp.where(t_i == i_i, 1.0, 0.0)
    bd_mask = (lax.broadcasted_iota(jnp.int32, (L, L), 0) // C) == (lax.broadcasted_iota(jnp.int32, (L, L), 1) // C)
    tri = jnp.where(lax.broadcasted_iota(jnp.int32, (C, C), 0) >= lax.broadcasted_iota(jnp.int32, (C, C), 1),
                    1.0, 0.0).astype(BF16)
    tile = lambda ref, u: ref[u[1] * C:(u[1] + 1) * C, u[0] * L:(u[0] + 1) * L]

    cw = {}
    tri3 = jnp.concatenate([tri, tri, tri], axis=1)
    for u in units:
        cw[u] = _dot(tri3, jnp.concatenate(_split3(tile(lw_ref, u)), axis=0))
    al, rt, bw, kw, v, vs, aa, w_end = {}, {}, {}, {}, {}, {}, {}, {}
    for u in units:
        lw = tile(lw_ref, u)
        w_incl = jnp.exp(cw[u])
        w_excl = jnp.exp(cw[u] - lw)
        w_inv = jnp.exp(-cw[u])
        w_end[u] = w_incl[C - 1:C, :]
        v[u] = bf(tile(v_ref, u))
        vs[u] = stack(v[u])
        al[u] = bf(-tile(kk_ref, u) * w_excl)
        rt[u] = tile(r_ref, u) * w_incl
        bh = tile(b_ref, u) * w_inv
        kh = tile(k_ref, u) * w_inv
        bw[u] = bf(bh * w_end[u])
        kw[u] = bf(kh * w_end[u])
        aa[u] = _dot_nt(jnp.concatenate([al[u], bf(rt[u])], axis=0),
                        jnp.concatenate([stack(bf(bh)), stack(bf(kh))], axis=0))
    a_ak, a_r, pw, t = {}, {}, {}, {}
    for u in units:
        a_ab = jnp.where(low_strict, aa[u][:C, :N], 0.0)
        a_ak[u] = bf(jnp.where(low_strict, aa[u][:C, N:], 0.0))
        a_r[u] = bf(jnp.concatenate([jnp.where(low_incl, aa[u][C:, :N], 0.0),
                                     jnp.where(low_incl, aa[u][C:, N:], 0.0)], axis=1))
        pw[u] = a_ab
        t[u] = eye_w + a_ab
    n_fac = int(np.log2(C))
    akv = {}
    for u in units:
        pwb = bf(pw[u])
        pw[u] = _dot(pwb, stack(pwb))
        akv[u] = _dot(a_ak[u], vs[u])
    for s in range(2, n_fac + 1):
        for u in units:
            pwb = bf(pw[u])
            rhs = stack(pwb)
            if s < n_fac:
                both = _dot(jnp.concatenate([bf(t[u]), pwb], axis=0), rhs)
                t[u] = t[u] + both[:C]
                pw[u] = both[C:]
            else:
                t[u] = t[u] + _dot(bf(t[u]), rhs)
    pm, q = {}, {}
    for u in units:
        tb = bf(t[u])
        pm[u] = bf(_dot(tb, stack(al[u])))
        q[u] = bf(_dot(tb, stack(bf(akv[u]))))
    pr, y0, bk = {}, {}, {}
    for u in units:
        rp = bf(rt[u] + _dot(a_r[u][:, :N], stack(pm[u])))
        pr[u] = jnp.concatenate([pm[u], rp], axis=0)
        y0[u] = _dot(a_r[u], jnp.concatenate([stack(q[u]), vs[u]], axis=0))
        bk[u] = jnp.concatenate([bw[u], kw[u]], axis=0)
    zt = [zt_ref[p] for p in range(n_packs)]
    for c in range(n_chunks):
        for p in range(n_packs):
            u = (p, c)
            uy = _dot_nt(pr[u], bf(zt[p]))
            y_ref[c * C:(c + 1) * C, p * L:(p + 1) * L] = uy[C:] + y0[u]
            upd = _dot_tn(jnp.concatenate([bf(uy[:C] + q[u]), v[u]], axis=0), bk[u])
            zt[p] = zt[p] * w_end[u] + jnp.where(bd_mask, upd, 0.0)
    for p in range(n_packs):
        zt_ref[p] = zt[p]


def _rwkv_scan(r, lw, k, v, kk, b):
    B, S, D = r.shape
    tile = pl.BlockSpec((None, RW_TB, D), lambda bb, t: (bb, t, 0))
    return pl.pallas_call(
        _rwkv_scan_kernel,
        grid=(B, S // RW_TB),
        in_specs=[tile] * 6,
        out_specs=tile,
        out_shape=jax.ShapeDtypeStruct((B, S, D), F32),
        scratch_shapes=[pltpu.VMEM((D // RW_LANES, RW_LANES, RW_LANES), F32)],
        compiler_params=_cparams("parallel", "arbitrary"),
    )(r, lw, k, v, kk, b)


def _rwkv_out_kernel(y_ref, r_ref, k_ref, v_ref, g_ref, lnw_ref, lnb_ref, rk_ref, gm_ref, gmt_ref,
                     w_ref, h_ref, gt_ref, o_ref):
    y = y_ref[...]
    inv_n = 1.0 / RWKV_HEAD
    yc = y - _gsum(y, gm_ref, gmt_ref) * inv_n
    var = _gsum(yc * yc, gm_ref, gmt_ref) * inv_n
    yn = yc * lax.rsqrt(var + GN_EPS) * lnw_ref[...] + lnb_ref[...]
    v = v_ref[...]
    bonus = _gsum(r_ref[...] * k_ref[...] * rk_ref[...], gm_ref, gmt_ref) * v
    out = ((yn + bonus) * g_ref[...]).astype(BF16)
    o_ref[...] = h_ref[...] + gt_ref[...] * _dot(out, w_ref[...])


def _rwkv_out(y, r, k, v, g, ln_w, ln_b, r_k, w_out, h, mod, tm=256):
    B, S, D = h.shape
    gm, gmt = _head_indicator(D)
    row = lambda a: a.astype(F32).reshape(1, D)
    full = lambda a: pl.BlockSpec(a.shape, lambda b, i: (0,) * a.ndim)
    tile = pl.BlockSpec((None, tm, D), lambda b, i: (b, i, 0))
    params = [row(ln_w), row(ln_b), row(r_k), gm, gmt, w_out.astype(BF16)]
    return pl.pallas_call(
        _rwkv_out_kernel,
        grid=(B, S // tm),
        in_specs=[tile] * 5 + [full(p) for p in params] + [tile, pl.BlockSpec((None, None, 1, D), lambda b, i: (b, 5, 0, 0))],
        out_specs=tile,
        out_shape=jax.ShapeDtypeStruct((B, S, D), F32),
        compiler_params=_cparams("parallel", "parallel"),
    )(y, r, k, v, g, *params, h, mod)


def _rwkv_mixer(h, mod, mu, w_rkv, w0, w1, w2, a0, a1, a2, g1, g2, k_k, k_a, r_k, ln_w, ln_b, w_out):
    r, lw, k, v, kk, b, g = _rwkv_in(h, mod, mu, w_rkv, w0, w1, w2, a0, a1, a2, g1, g2, k_k, k_a)
    y = _rwkv_scan(r, lw, k, v, kk, b)
    return _rwkv_out(y, r, k, v, g, ln_w, ln_b, r_k, w_out, h, mod)


def kernel(x, c, ada_w, ada_b, ffn_w_gu, ffn_w_down, dsa_w_in, dsa_q_norm, dsa_k_norm, dsa_ik_norm, dsa_w_out, rwkv_mu, rwkv_w_rkv, rwkv_w0, rwkv_w1, rwkv_w2, rwkv_a0, rwkv_a1, rwkv_a2, rwkv_g1, rwkv_g2, rwkv_k_k, rwkv_k_a, rwkv_r_k, rwkv_ln_w, rwkv_ln_b, rwkv_w_out):
    B, S, D = x.shape
    depth = ada_w.shape[0]
    mods = _ada_mod(c, ada_w, ada_b)
    h = x
    for i in range(depth):
        mod = mods[i].reshape(B, N_MOD, 1, D)
        j = i // 2
        h = _ffn(h, mod, 0, ffn_w_gu[i, 0].astype(BF16), ffn_w_down[i, 0].astype(BF16))
        if i % 2 == 0:
            h = _dsa_mixer(h, mod, dsa_w_in[j], dsa_q_norm[j], dsa_k_norm[j], dsa_ik_norm[j], dsa_w_out[j])
        else:
            h = _rwkv_mixer(h, mod, rwkv_mu[j], rwkv_w_rkv[j], rwkv_w0[j], rwkv_w1[j], rwkv_w2[j], rwkv_a0[j],
                            rwkv_a1[j], rwkv_a2[j], rwkv_g1[j], rwkv_g2[j], rwkv_k_k[j], rwkv_k_a[j],
                            rwkv_r_k[j], rwkv_ln_w[j], rwkv_ln_b[j], rwkv_w_out[j])
        h = _ffn(h, mod, 6, ffn_w_gu[i, 1].astype(BF16), ffn_w_down[i, 1].astype(BF16))
    return h
```

```python
import functools

import jax
import jax.numpy as jnp
import numpy as np
from jax import lax
from jax.experimental import pallas as pl
from jax.experimental.pallas import tpu as pltpu

F32 = jnp.float32
BF16 = jnp.bfloat16

RMS_EPS = 1e-6
N_MOD = 9
VMEM_LIMIT_BYTES = 56 * 1024 * 1024

N_HEADS = 16
HEAD_DIM = 64
N_KV_HEADS = 4
GROUP = N_HEADS // N_KV_HEADS
IDX_HEADS = 8
IDX_DIM = 64
TOPK_MAX = 256
LANE = 128
LOG2E = 1.4426950408889634


def _cparams(*sem):
    return pltpu.CompilerParams(dimension_semantics=sem, vmem_limit_bytes=VMEM_LIMIT_BYTES)


def _dot(a, b):
    return jnp.dot(a, b, preferred_element_type=F32)


def _dot_nt(a, b):
    return lax.dot_general(a, b, (((1,), (1,)), ((), ())), preferred_element_type=F32)


def _dot_tn(a, b):
    return lax.dot_general(a, b, (((0,), (0,)), ((), ())), preferred_element_type=F32)


def _modulate(h, sh, sc):
    ms = jnp.mean(h * h, axis=-1, keepdims=True)
    return h * lax.rsqrt(ms + RMS_EPS) * (1.0 + sc) + sh


def _ada_kernel(c_ref, w_ref, b_ref, o_ref):
    c = c_ref[...]
    ca = (c * jax.nn.sigmoid(c)).astype(BF16)
    o_ref[...] = _dot(ca, w_ref[...].astype(BF16)) + b_ref[...]


def _ada_mod(c, ada_w, ada_b):
    L, D, N = ada_w.shape
    B = c.shape[0]
    tn = 1536
    return pl.pallas_call(
        _ada_kernel,
        grid=(L, N // tn),
        in_specs=[
            pl.BlockSpec((B, D), lambda l, j: (0, 0)),
            pl.BlockSpec((None, D, tn), lambda l, j: (l, 0, j)),
            pl.BlockSpec((None, 1, tn), lambda l, j: (l, 0, j)),
        ],
        out_specs=pl.BlockSpec((None, B, tn), lambda l, j: (l, 0, j)),
        out_shape=jax.ShapeDtypeStruct((L, B, N), F32),
        compiler_params=_cparams("parallel", "parallel"),
    )(c, ada_w, ada_b.reshape(L, 1, N))


FF_CHUNK = 256


def _ffn_kernel(h_ref, sh_ref, sc_ref, gt_ref, wgu_ref, wd_ref, o_ref, *, d_ff):
    h = h_ref[...]
    u = _modulate(h, sh_ref[...], sc_ref[...]).astype(BF16)
    acc = None
    for c in range(d_ff // FF_CHUNK):
        lo = c * FF_CHUNK
        g = _dot(u, wgu_ref[:, lo:lo + FF_CHUNK])
        up = _dot(u, wgu_ref[:, d_ff + lo:d_ff + lo + FF_CHUNK])
        a = (g * jax.nn.sigmoid(g) * up).astype(BF16)
        part = _dot(a, wd_ref[lo:lo + FF_CHUNK, :])
        acc = part if acc is None else acc + part
    o_ref[...] = h + (0.5 * gt_ref[...]) * acc


def _ffn(h, mod, k0, w_gu, w_down, tm=512):
    B, S, D = h.shape
    d_ff = w_down.shape[0]
    mspec = lambda k: pl.BlockSpec((None, None, 1, D), lambda b, i: (b, k, 0, 0))
    return pl.pallas_call(
        functools.partial(_ffn_kernel, d_ff=d_ff),
        grid=(B, S // tm),
        in_specs=[
            pl.BlockSpec((None, tm, D), lambda b, i: (b, i, 0)),
            mspec(k0), mspec(k0 + 1), mspec(k0 + 2),
            pl.BlockSpec((D, 2 * d_ff), lambda b, i: (0, 0), pipeline_mode=pl.Buffered(1)),
            pl.BlockSpec((d_ff, D), lambda b, i: (0, 0), pipeline_mode=pl.Buffered(1)),
        ],
        out_specs=pl.BlockSpec((None, tm, D), lambda b, i: (b, i, 0)),
        out_shape=jax.ShapeDtypeStruct((B, S, D), F32),
        compiler_params=_cparams("parallel", "parallel"),
    )(h, mod, mod, mod, w_gu, w_down)


def _dsa_in_kernel(h_ref, sh_ref, sc_ref, wqT_ref, wk_ref, wvT_ref, wiqT_ref, wik_ref, wiwT_ref,
                   qn_ref, kn_ref, ikn_ref,
                   qT_ref, k_ref, vT_ref, iqT_ref, ik_ref, iwT_ref):
    u = _modulate(h_ref[...], sh_ref[...], sc_ref[...]).astype(BF16)
    tm = u.shape[0]
    qT = _dot_nt(wqT_ref[...], u).reshape(N_HEADS, HEAD_DIM, tm)
    ms = jnp.mean(qT * qT, axis=1, keepdims=True)
    qT = qT * lax.rsqrt(ms + RMS_EPS) * qn_ref[...][None] * (HEAD_DIM ** -0.5 * LOG2E)
    qT_ref[...] = qT.reshape(N_HEADS * HEAD_DIM, tm).astype(BF16)
    k = _dot(u, wk_ref[...])
    kn = kn_ref[...]
    for j in range(N_KV_HEADS):
        kj = k[:, j * LANE:(j + 1) * LANE]
        ms = jnp.sum(kj * kj, axis=-1, keepdims=True) * (1.0 / HEAD_DIM)
        k_ref[:, j * LANE:(j + 1) * LANE] = (kj * lax.rsqrt(ms + RMS_EPS) * kn).astype(BF16)
    vT_ref[...] = _dot_nt(wvT_ref[...], u).astype(BF16)
    iqT_ref[...] = _dot_nt(wiqT_ref[...], u).astype(BF16)
    ik = _dot(u, wik_ref[...])
    ms = jnp.sum(ik * ik, axis=-1, keepdims=True) * (1.0 / IDX_DIM)
    ik_ref[...] = (ik * lax.rsqrt(ms + RMS_EPS) * ikn_ref[...]).astype(BF16)
    iwT_ref[...] = _dot_nt(wiwT_ref[...], u) * (IDX_HEADS ** -0.5 * IDX_DIM ** -0.5)


DSA_KC = 256
DSA_QB = 256


def _dsa_in(h, mod, w_in, q_norm, k_norm, ik_norm, tm=DSA_KC):
    B, S, D = h.shape
    QW, KW = N_HEADS * HEAD_DIM, N_KV_HEADS * HEAD_DIM
    IQW = IDX_HEADS * IDX_DIM
    o1, o2, o3, o4, o5 = QW, QW + KW, QW + 2 * KW, QW + 2 * KW + IQW, QW + 2 * KW + IQW + IDX_DIM
    wb = w_in.astype(BF16)
    wqT = wb[:, :o1].T
    wk = jnp.pad(wb[:, o1:o2].reshape(D, N_KV_HEADS, HEAD_DIM), ((0, 0), (0, 0), (0, LANE - HEAD_DIM))).reshape(D, N_KV_HEADS * LANE)
    wvT = wb[:, o2:o3].T
    wiqT = wb[:, o3:o4].T
    wik = jnp.pad(wb[:, o4:o5], ((0, 0), (0, LANE - IDX_DIM)))
    wiwT = wb[:, o5:].T
    qn = q_norm.astype(F32).reshape(HEAD_DIM, 1)
    kn = jnp.pad(k_norm.astype(F32), (0, LANE - HEAD_DIM)).reshape(1, LANE)
    ikn = jnp.pad(ik_norm.astype(F32), (0, LANE - IDX_DIM)).reshape(1, LANE)
    mspec = lambda k: pl.BlockSpec((None, None, 1, D), lambda b, i: (b, k, 0, 0))
    full = lambda a: pl.BlockSpec(a.shape, lambda b, i: (0,) * a.ndim)
    nc = S // tm
    return pl.pallas_call(
        _dsa_in_kernel,
        grid=(B, nc),
        in_specs=[pl.BlockSpec((None, tm, D), lambda b, i: (b, i, 0)), mspec(3), mspec(4),
                  full(wqT), full(wk), full(wvT), full(wiqT), full(wik), full(wiwT),
                  full(qn), full(kn), full(ikn)],
        out_specs=[
            pl.BlockSpec((None, QW, tm), lambda b, i: (b, 0, i)),
            pl.BlockSpec((None, tm, N_KV_HEADS * LANE), lambda b, i: (b, i, 0)),
            pl.BlockSpec((None, None, KW, tm), lambda b, i: (b, i, 0, 0)),
            pl.BlockSpec((None, IQW, tm), lambda b, i: (b, 0, i)),
            pl.BlockSpec((None, tm, LANE), lambda b, i: (b, i, 0)),
            pl.BlockSpec((None, IDX_HEADS, tm), lambda b, i: (b, 0, i)),
        ],
        out_shape=[
            jax.ShapeDtypeStruct((B, QW, S), BF16),
            jax.ShapeDtypeStruct((B, S, N_KV_HEADS * LANE), BF16),
            jax.ShapeDtypeStruct((B, nc, KW, tm), BF16),
            jax.ShapeDtypeStruct((B, IQW, S), BF16),
            jax.ShapeDtypeStruct((B, S, LANE), BF16),
            jax.ShapeDtypeStruct((B, IDX_HEADS, S), F32),
        ],
        compiler_params=_cparams("parallel", "parallel"),
    )(h, mod, mod, wqT, wk, wvT, wiqT, wik, wiwT, qn, kn, ikn)


NEG_BIG = -(2.0 ** 100)
DSA_VPAD = 16
N_CHAINS = 4


def _ukey_to_float(ukey):
    key = ukey ^ np.int32(-2 ** 31)
    bits = key ^ ((key >> 31) & np.int32(0x7FFFFFFF))
    return lax.bitcast_convert_type(bits, F32)


def _fold_rows(x, op, rows):
    while x.shape[0] > rows:
        half = x.shape[0] // 2
        x = op(x[:half], x[half:])
    return x


def _dsa_core_kernel(iqT_ref, iwT_ref, qT_ref, ik_ref, k_ref, vT_ref, oT_ref,
                     sc_ref, qa_ref, iqa_ref, d_ref, st_ref, p_ref, m_ref, acc_ref, *, k_top, n_chunks):
    KC, QB = DSA_KC, DSA_QB
    qi = pl.program_id(1)
    q0 = qi * QB
    nk = (q0 + QB + KC - 1) // KC
    qpos = q0 + lax.broadcasted_iota(jnp.int32, (1, QB), 1)
    zeros_h = jnp.zeros((LANE - HEAD_DIM, QB), BF16)

    for h in range(IDX_HEADS):
        iqa_ref[:, h * QB:(h + 1) * QB] = jnp.concatenate(
            [iqT_ref[h * IDX_DIM:(h + 1) * IDX_DIM, :], zeros_h], axis=0)
    for j in range(N_KV_HEADS):
        for g in range(GROUP):
            hh = j * GROUP + g
            qa_ref[j, :, g * QB:(g + 1) * QB] = jnp.concatenate(
                [qT_ref[hh * HEAD_DIM:(hh + 1) * HEAD_DIM, :], zeros_h], axis=0)
    iw = iwT_ref[...]

    npair = (nk + 1) // 2

    def clamp_chunk(c):
        return jnp.minimum(c, n_chunks - 1)

    def idx_dot(c, slot):
        k0 = pl.multiple_of(c * KC, KC)
        d_ref[slot] = _dot(ik_ref[pl.ds(k0, KC), :], iqa_ref[...])

    def idx_score(c, slot):
        k0 = pl.multiple_of(c * KC, KC)
        s = jnp.maximum(d_ref[slot, :, 0:QB], 0.0) * iw[0:1, :]
        for h in range(1, IDX_HEADS):
            s = s + jnp.maximum(d_ref[slot, :, h * QB:(h + 1) * QB], 0.0) * iw[h:h + 1, :]
        kpos = k0 + lax.broadcasted_iota(jnp.int32, (KC, 1), 0)
        sc_ref[pl.ds(k0, KC), :] = jnp.where(kpos <= qpos, s, -jnp.inf)

    idx_dot(0, 0)

    def p1(i, carry):
        c = 2 * i
        idx_dot(c + 1, 1)
        idx_score(c, 0)
        idx_dot(clamp_chunk(c + 2), 0)
        idx_score(c + 1, 1)
        return carry

    lax.fori_loop(0, npair, p1, 0)

    def count_not(pred):
        def body(i, accs):
            accs = list(accs)
            for half in range(2):
                k0 = pl.multiple_of((2 * i + half) * KC, KC)
                ind = jnp.where(pred(sc_ref[pl.ds(k0, KC), :]), 0.0, 1.0)
                for r in range(KC // 8):
                    accs[r % N_CHAINS] = accs[r % N_CHAINS] + ind[r * 8:(r + 1) * 8]
            return tuple(accs)
        accs = lax.fori_loop(0, npair, body, tuple(jnp.zeros((8, QB), F32) for _ in range(N_CHAINS)))
        acc = (accs[0] + accs[1]) + (accs[2] + accs[3])
        return jnp.sum(acc, axis=0, keepdims=True)

    n_staged = (npair * (2 * KC)).astype(F32)
    ukey = jnp.zeros((1, QB), jnp.int32)
    n_ge = jnp.full((1, QB), n_staged, F32)
    for bit in range(31, -1, -1):
        cand = ukey | np.int32(np.uint32(1 << bit).astype(np.int32))
        t = _ukey_to_float(cand)
        cnt = count_not(lambda s: s < t)
        take = cnt >= k_top
        ukey = jnp.where(take, cand, ukey)
        n_ge = jnp.where(take, cnt, n_ge)
    thr = _ukey_to_float(ukey)
    ties_matter = jnp.max(n_ge) > k_top

    m_ref[...] = jnp.full(m_ref.shape, NEG_BIG, F32)
    acc_ref[...] = jnp.zeros(acc_ref.shape, F32)
    r_i = lax.broadcasted_iota(jnp.int32, (KC, KC), 0)
    c_i = lax.broadcasted_iota(jnp.int32, (KC, KC), 1)
    tril = jnp.where(c_i <= r_i, 1.0, 0.0).astype(BF16)
    ones_rows = jnp.ones((DSA_VPAD, KC), BF16)

    HALF = GROUP * QB // 2

    def logits(c, slot, j, half):
        k0 = pl.multiple_of(c * KC, KC)
        cols = slice(half * HALF, (half + 1) * HALF)
        st_ref[slot, j, :, cols] = _dot(k_ref[pl.ds(k0, KC), j * LANE:(j + 1) * LANE],
                                        qa_ref[j, :, cols]).astype(BF16)

    for j in range(N_KV_HEADS):
        logits(0, 0, j, 0)
        logits(0, 0, j, 1)

    def attend(c, slot, tie_seen, c_next, need):
        k0 = pl.multiple_of(c * KC, KC)
        s = sc_ref[pl.ds(k0, KC), :]
        kpos = k0 + lax.broadcasted_iota(jnp.int32, (KC, 1), 0)
        if need is None:
            sel = jnp.logical_and(jnp.logical_not(s < thr), kpos <= qpos)
        else:
            eq = s == thr
            eqf = jnp.where(eq, 1.0, 0.0)
            incl = _dot(tril, eqf.astype(BF16))
            rank = tie_seen + incl - eqf
            sel = jnp.logical_and(jnp.logical_or(s > thr, jnp.logical_and(eq, rank < need)), kpos <= qpos)
            tie_seen = tie_seen + incl[KC - 1:KC, :]
        bias = jnp.where(sel, 0.0, NEG_BIG).astype(BF16)
        for j in range(N_KV_HEADS):
            alphas = []
            for g in range(GROUP):
                sl = slice(g * QB, (g + 1) * QB)
                sg = st_ref[slot, j, :, sl] + bias
                m_old = m_ref[j, :, sl]
                mx = jnp.max(_fold_rows(sg, jnp.maximum, 16), axis=0, keepdims=True).astype(F32)
                m_new = jnp.maximum(m_old, mx)
                alphas.append(jnp.exp2(m_old - m_new))
                m_ref[j, :, sl] = m_new
                p_ref[j, :, sl] = jnp.exp2(sg - m_new.astype(BF16))
                if g % 2 == 1:
                    logits(c_next, 1 - slot, j, g // 2)
            v_aug = jnp.concatenate([vT_ref[c, j * HEAD_DIM:(j + 1) * HEAD_DIM, :], ones_rows], axis=0)
            acc_ref[j] = acc_ref[j] * jnp.concatenate(alphas, axis=1) + _dot(v_aug, p_ref[j])
        return tie_seen

    def attention(need):
        def p3(i, tie_seen):
            c = 2 * i
            tie_seen = attend(c, 0, tie_seen, c + 1, need)
            return attend(c + 1, 1, tie_seen, clamp_chunk(c + 2), need)
        lax.fori_loop(0, npair, p3, jnp.zeros((1, QB), F32))

    @pl.when(ties_matter)
    def _():
        attention(k_top - (n_staged - count_not(lambda s: s > thr)))

    @pl.when(jnp.logical_not(ties_matter))
    def _():
        attention(None)

    for j in range(N_KV_HEADS):
        o = acc_ref[j, 0:HEAD_DIM, :] / acc_ref[j, HEAD_DIM:HEAD_DIM + 1, :]
        for g in range(GROUP):
            hh = j * GROUP + g
            oT_ref[hh * HEAD_DIM:(hh + 1) * HEAD_DIM, :] = o[:, g * QB:(g + 1) * QB].astype(BF16)


def _dsa_core(qT, k, vT, iqT, ik, iwT):
    B, QW, S = qT.shape
    KC, QB = DSA_KC, DSA_QB
    k_top = min(TOPK_MAX, S // 4)
    nc = S // KC
    return pl.pallas_call(
        functools.partial(_dsa_core_kernel, k_top=k_top, n_chunks=nc),
        grid=(B, S // QB),
        in_specs=[
            pl.BlockSpec((None, IDX_HEADS * IDX_DIM, QB), lambda b, i: (b, 0, i)),
            pl.BlockSpec((None, IDX_HEADS, QB), lambda b, i: (b, 0, i)),
            pl.BlockSpec((None, QW, QB), lambda b, i: (b, 0, i)),
            pl.BlockSpec((None, S, LANE), lambda b, i: (b, 0, 0)),
            pl.BlockSpec((None, S, N_KV_HEADS * LANE), lambda b, i: (b, 0, 0)),
            pl.BlockSpec((None, nc, N_KV_HEADS * HEAD_DIM, KC), lambda b, i: (b, 0, 0, 0)),
        ],
        out_specs=pl.BlockSpec((None, QW, QB), lambda b, i: (b, 0, i)),
        out_shape=jax.ShapeDtypeStruct((B, QW, S), BF16),
        scratch_shapes=[
            pltpu.VMEM((S, QB), F32),
            pltpu.VMEM((N_KV_HEADS, LANE, GROUP * QB), BF16),
            pltpu.VMEM((LANE, IDX_HEADS * QB), BF16),
            pltpu.VMEM((2, KC, IDX_HEADS * QB), F32),
            pltpu.VMEM((2, N_KV_HEADS, KC, GROUP * QB), BF16),
            pltpu.VMEM((N_KV_HEADS, KC, GROUP * QB), BF16),
            pltpu.VMEM((N_KV_HEADS, 1, GROUP * QB), F32),
            pltpu.VMEM((N_KV_HEADS, HEAD_DIM + DSA_VPAD, GROUP * QB), F32),
        ],
        compiler_params=_cparams("parallel", "arbitrary"),
    )(iqT, iwT, qT, ik, k, vT)


def _out_T_kernel(xT_ref, w_ref, h_ref, gt_ref, o_ref):
    o_ref[...] = h_ref[...] + gt_ref[...] * _dot_tn(xT_ref[...], w_ref[...])


def _dsa_out(oT, w_out, h, mod, tm=512):
    B, S, D = h.shape
    QW = oT.shape[1]
    return pl.pallas_call(
        _out_T_kernel,
        grid=(B, S // tm),
        in_specs=[
            pl.BlockSpec((None, QW, tm), lambda b, i: (b, 0, i)),
            pl.BlockSpec((QW, D), lambda b, i: (0, 0)),
            pl.BlockSpec((None, tm, D), lambda b, i: (b, i, 0)),
            pl.BlockSpec((None, None, 1, D), lambda b, i: (b, 5, 0, 0)),
        ],
        out_specs=pl.BlockSpec((None, tm, D), lambda b, i: (b, i, 0)),
        out_shape=jax.ShapeDtypeStruct((B, S, D), F32),
        compiler_params=_cparams("parallel", "parallel"),
    )(oT, w_out.astype(BF16), h, mod)


def _dsa_mixer(h, mod, w_in, q_norm, k_norm, ik_norm, w_out):
    qT, k, vT, iqT, ik, iwT = _dsa_in(h, mod, w_in, q_norm, k_norm, ik_norm)
    oT = _dsa_core(qT, k, vT, iqT, ik, iwT)
    return _dsa_out(oT, w_out, h, mod)


RWKV_HEAD = 64
GN_EPS = 64e-5
RW_PACK = 4
RW_LANES = RW_PACK * RWKV_HEAD
RW_C = 64
RW_TB = 256


def _split3(x):
    hi = x.astype(BF16)
    r1 = x - hi.astype(F32)
    mid = r1.astype(BF16)
    lo = (r1 - mid.astype(F32)).astype(BF16)
    return hi, mid, lo


def _gsum(x, g_ref, gt_ref):
    hi = x.astype(BF16)
    lo = (x - hi.astype(F32)).astype(BF16)
    s = _dot(jnp.concatenate([hi, lo], axis=1), g_ref[...])
    return _dot(jnp.concatenate(_split3(s), axis=1), gt_ref[...])


def _rwkv_in_kernel(h_ref, hp_ref, sh_ref, sc_ref, mu_ref, wrkv_ref, w0_ref, w1_ref, w2_ref,
                    a0_ref, a1_ref, a2_ref, g1_ref, g2_ref, kk_ref, ka_ref, gm_ref, gmt_ref,
                    r_ref, lw_ref, k_ref, v_ref, kkn_ref, b_ref, g_ref):
    sh, sc = sh_ref[...], sc_ref[...]
    u = _modulate(h_ref[...], sh, sc)
    tm = u.shape[0]
    prev = _modulate(hp_ref[...], sh, sc)[7:8, :]
    prev = jnp.where(pl.program_id(1) == 0, 0.0, prev)
    row = lax.broadcasted_iota(jnp.int32, (tm, 1), 0)
    xx = jnp.where(row == 0, prev, pltpu.roll(u, 1, 0)) - u
    mix = lambda i: (u + xx * mu_ref[i:i + 1, :]).astype(BF16)
    r = _dot(mix(0), wrkv_ref[0])
    k = _dot(mix(1), wrkv_ref[1])
    v = _dot(mix(2), wrkv_ref[2])
    wl = w0_ref[...] + _dot(jnp.tanh(_dot(mix(3), w1_ref[...])).astype(BF16), w2_ref[...])
    nwl = -wl
    w_log = -(jnp.maximum(nwl, 0.0) + jnp.log(1.0 + jnp.exp(-jnp.abs(nwl)))) - 0.5
    lw_ref[...] = -jnp.exp(w_log)
    a = jax.nn.sigmoid(a0_ref[...] + _dot(_dot(mix(4), a1_ref[...]).astype(BF16), a2_ref[...]))
    g_ref[...] = _dot(jax.nn.sigmoid(_dot(mix(5), g1_ref[...])).astype(BF16), g2_ref[...])
    kk = k * kk_ref[...]
    nrm = jnp.sqrt(_gsum(kk * kk, gm_ref, gmt_ref))
    kk = kk / jnp.maximum(nrm, 1e-12)
    r_ref[...] = r
    k_ref[...] = k * (1.0 + (a - 1.0) * ka_ref[...])
    v_ref[...] = v
    kkn_ref[...] = kk
    b_ref[...] = kk * a


def _head_indicator(D):
    gm = (np.arange(D)[:, None] // RWKV_HEAD == np.arange(D // RWKV_HEAD)[None, :]).astype(np.float32)
    return jnp.asarray(np.concatenate([gm, gm], axis=0), BF16), jnp.asarray(np.concatenate([gm.T] * 3, axis=0), BF16)


def _rwkv_in(h, mod, mu, w_rkv, w0, w1, w2, a0, a1, a2, g1, g2, k_k, k_a, tm=256):
    B, S, D = h.shape
    gm, gmt = _head_indicator(D)
    row = lambda a: a.astype(F32).reshape(1, D)
    full = lambda a: pl.BlockSpec(a.shape, lambda b, i: (0,) * a.ndim)
    mspec = lambda k: pl.BlockSpec((None, None, 1, D), lambda b, i: (b, k, 0, 0))
    tile = pl.BlockSpec((None, tm, D), lambda b, i: (b, i, 0))
    params = [mu.astype(F32), w_rkv.astype(BF16), row(w0), w1.astype(BF16), w2.astype(BF16),
              row(a0), a1.astype(BF16), a2.astype(BF16), g1.astype(BF16), g2.astype(BF16),
              row(k_k), row(k_a), gm, gmt]
    return pl.pallas_call(
        _rwkv_in_kernel,
        grid=(B, S // tm),
        in_specs=[tile,
                  pl.BlockSpec((None, 8, D), lambda b, i: (b, jnp.maximum(i * (tm // 8) - 1, 0), 0)),
                  mspec(3), mspec(4)] + [full(p) for p in params],
        out_specs=[tile] * 7,
        out_shape=[jax.ShapeDtypeStruct((B, S, D), F32)] * 7,
        compiler_params=_cparams("parallel", "parallel"),
    )(h, h, mod, mod, *params)


def _rwkv_scan_kernel(r_ref, lw_ref, k_ref, v_ref, kk_ref, b_ref, y_ref, zt_ref):
    C, P, L = RW_C, RW_PACK, RW_LANES
    N = P * C
    assert C == RWKV_HEAD
    n_packs = r_ref.shape[1] // L
    n_chunks = r_ref.shape[0] // C
    units = [(p, c) for p in range(n_packs) for c in range(n_chunks)]

    @pl.when(pl.program_id(1) == 0)
    def _():
        zt_ref[...] = jnp.zeros(zt_ref.shape, F32)

    lane = lax.broadcasted_iota(jnp.int32, (1, L), 1)
    hmask = [lane // C == hd for hd in range(P)]
    bf = lambda x: x.astype(BF16)
    stack = lambda x: jnp.concatenate([jnp.where(hmask[hd], x, jnp.zeros_like(x)) for hd in range(P)], axis=0)
    t_i = lax.broadcasted_iota(jnp.int32, (C, N), 0)
    i_i = lax.broadcasted_iota(jnp.int32, (C, N), 1) % C
    low_strict = t_i > i_i
    low_incl = t_i >= i_i
    eye_w = jnp.where(t_i == i_i, 1.0, 0.0)
    bd_mask = (lax.broadcasted_iota(jnp.int32, (L, L), 0) // C) == (lax.broadcasted_iota(jnp.int32, (L, L), 1) // C)
    tri = jnp.where(lax.broadcasted_iota(jnp.int32, (C, C), 0) >= lax.broadcasted_iota(jnp.int32, (C, C), 1),
                    1.0, 0.0).astype(BF16)
    tile = lambda ref, u: ref[u[1] * C:(u[1] + 1) * C, u[0] * L:(u[0] + 1) * L]

    cw = {}
    tri3 = jnp.concatenate([tri, tri, tri], axis=1)
    for u in units:
        cw[u] = _dot(tri3, jnp.concatenate(_split3(tile(lw_ref, u)), axis=0))
    al, rt, bw, kw, v, vs, aa, w_end = {}, {}, {}, {}, {}, {}, {}, {}
    for u in units:
        lw = tile(lw_ref, u)
        w_incl = jnp.exp(cw[u])
        w_excl = jnp.exp(cw[u] - lw)
        w_inv = jnp.exp(-cw[u])
        w_end[u] = w_incl[C - 1:C, :]
        v[u] = bf(tile(v_ref, u))
        vs[u] = stack(v[u])
        al[u] = bf(-tile(kk_ref, u) * w_excl)
        rt[u] = tile(r_ref, u) * w_incl
        bh = tile(b_ref, u) * w_inv
        kh = tile(k_ref, u) * w_inv
        bw[u] = bf(bh * w_end[u])
        kw[u] = bf(kh * w_end[u])
        aa[u] = _dot_nt(jnp.concatenate([al[u], bf(rt[u])], axis=0),
                        jnp.concatenate([stack(bf(bh)), stack(bf(kh))], axis=0))
    a_ak, a_r, pw, t = {}, {}, {}, {}
    for u in units:
        a_ab = jnp.where(low_strict, aa[u][:C, :N], 0.0)
        a_ak[u] = bf(jnp.where(low_strict, aa[u][:C, N:], 0.0))
        a_r[u] = bf(jnp.concatenate([jnp.where(low_incl, aa[u][C:, :N], 0.0),
                                     jnp.where(low_incl, aa[u][C:, N:], 0.0)], axis=1))
        pw[u] = a_ab
        t[u] = eye_w + a_ab
    n_fac = int(np.log2(C))
    akv = {}
    for u in units:
        pwb = bf(pw[u])
        pw[u] = _dot(pwb, stack(pwb))
        akv[u] = _dot(a_ak[u], vs[u])
    for s in range(2, n_fac + 1):
        for u in units:
            pwb = bf(pw[u])
            rhs = stack(pwb)
            if s < n_fac:
                both = _dot(jnp.concatenate([bf(t[u]), pwb], axis=0), rhs)
                t[u] = t[u] + both[:C]
                pw[u] = both[C:]
            else:
                t[u] = t[u] + _dot(bf(t[u]), rhs)
    pm, q = {}, {}
    for u in units:
        tb = bf(t[u])
        pm[u] = bf(_dot(tb, stack(al[u])))
        q[u] = bf(_dot(tb, stack(bf(akv[u]))))
    pr, y0, bk = {}, {}, {}
    for u in units:
        rp = bf(rt[u] + _dot(a_r[u][:, :N], stack(pm[u])))
        pr[u] = jnp.concatenate([pm[u], rp], axis=0)
        y0[u] = _dot(a_r[u], jnp.concatenate([stack(q[u]), vs[u]], axis=0))
        bk[u] = jnp.concatenate([bw[u], kw[u]], axis=0)
    zt = [zt_ref[p] for p in range(n_packs)]
    for c in range(n_chunks):
        for p in range(n_packs):
            u = (p, c)
            uy = _dot_nt(pr[u], bf(zt[p]))
            y_ref[c * C:(c + 1) * C, p * L:(p + 1) * L] = uy[C:] + y0[u]
            upd = _dot_tn(jnp.concatenate([bf(uy[:C] + q[u]), v[u]], axis=0), bk[u])
            zt[p] = zt[p] * w_end[u] + jnp.where(bd_mask, upd, 0.0)
    for p in range(n_packs):
        zt_ref[p] = zt[p]


def _rwkv_scan(r, lw, k, v, kk, b):
    B, S, D = r.shape
    tile = pl.BlockSpec((None, RW_TB, D), lambda bb, t: (bb, t, 0))
    return pl.pallas_call(
        _rwkv_scan_kernel,
        grid=(B, S // RW_TB),
        in_specs=[tile] * 6,
        out_specs=tile,
        out_shape=jax.ShapeDtypeStruct((B, S, D), F32),
        scratch_shapes=[pltpu.VMEM((D // RW_LANES, RW_LANES, RW_LANES), F32)],
        compiler_params=_cparams("parallel", "arbitrary"),
    )(r, lw, k, v, kk, b)


def _rwkv_out_kernel(y_ref, r_ref, k_ref, v_ref, g_ref, lnw_ref, lnb_ref, rk_ref, gm_ref, gmt_ref,
                     w_ref, h_ref, gt_ref, o_ref):
    y = y_ref[...]
    inv_n = 1.0 / RWKV_HEAD
    yc = y - _gsum(y, gm_ref, gmt_ref) * inv_n
    var = _gsum(yc * yc, gm_ref, gmt_ref) * inv_n
    yn = yc * lax.rsqrt(var + GN_EPS) * lnw_ref[...] + lnb_ref[...]
    v = v_ref[...]
    bonus = _gsum(r_ref[...] * k_ref[...] * rk_ref[...], gm_ref, gmt_ref) * v
    out = ((yn + bonus) * g_ref[...]).astype(BF16)
    o_ref[...] = h_ref[...] + gt_ref[...] * _dot(out, w_ref[...])


def _rwkv_out(y, r, k, v, g, ln_w, ln_b, r_k, w_out, h, mod, tm=256):
    B, S, D = h.shape
    gm, gmt = _head_indicator(D)
    row = lambda a: a.astype(F32).reshape(1, D)
    full = lambda a: pl.BlockSpec(a.shape, lambda b, i: (0,) * a.ndim)
    tile = pl.BlockSpec((None, tm, D), lambda b, i: (b, i, 0))
    params = [row(ln_w), row(ln_b), row(r_k), gm, gmt, w_out.astype(BF16)]
    return pl.pallas_call(
        _rwkv_out_kernel,
        grid=(B, S // tm),
        in_specs=[tile] * 5 + [full(p) for p in params] + [tile, pl.BlockSpec((None, None, 1, D), lambda b, i: (b, 5, 0, 0))],
        out_specs=tile,
        out_shape=jax.ShapeDtypeStruct((B, S, D), F32),
        compiler_params=_cparams("parallel", "parallel"),
    )(y, r, k, v, g, *params, h, mod)


def _rwkv_mixer(h, mod, mu, w_rkv, w0, w1, w2, a0, a1, a2, g1, g2, k_k, k_a, r_k, ln_w, ln_b, w_out):
    r, lw, k, v, kk, b, g = _rwkv_in(h, mod, mu, w_rkv, w0, w1, w2, a0, a1, a2, g1, g2, k_k, k_a)
    y = _rwkv_scan(r, lw, k, v, kk, b)
    return _rwkv_out(y, r, k, v, g, ln_w, ln_b, r_k, w_out, h, mod)


def kernel(x, c, ada_w, ada_b, ffn_w_gu, ffn_w_down, dsa_w_in, dsa_q_norm, dsa_k_norm, dsa_ik_norm, dsa_w_out, rwkv_mu, rwkv_w_rkv, rwkv_w0, rwkv_w1, rwkv_w2, rwkv_a0, rwkv_a1, rwkv_a2, rwkv_g1, rwkv_g2, rwkv_k_k, rwkv_k_a, rwkv_r_k, rwkv_ln_w, rwkv_ln_b, rwkv_w_out):
    B, S, D = x.shape
    depth = ada_w.shape[0]
    mods = _ada_mod(c, ada_w, ada_b)
    h = x
    for i in range(depth):
        mod = mods[i].reshape(B, N_MOD, 1, D)
        j = i // 2
        h = _ffn(h, mod, 0, ffn_w_gu[i, 0].astype(BF16), ffn_w_down[i, 0].astype(BF16))
        if i % 2 == 0:
            h = _dsa_mixer(h, mod, dsa_w_in[j], dsa_q_norm[j], dsa_k_norm[j], dsa_ik_norm[j], dsa_w_out[j])
        else:
            h = _rwkv_mixer(h, mod, rwkv_mu[j], rwkv_w_rkv[j], rwkv_w0[j], rwkv_w1[j], rwkv_w2[j], rwkv_a0[j],
                            rwkv_a1[j], rwkv_a2[j], rwkv_g1[j], rwkv_g2[j], rwkv_k_k[j], rwkv_k_a[j],
                            rwkv_r_k[j], rwkv_ln_w[j], rwkv_ln_b[j], rwkv_w_out[j])
        h = _ffn(h, mod, 6, ffn_w_gu[i, 1].astype(BF16), ffn_w_down[i, 1].astype(BF16))
    return h
```

```python
import functools

import jax
import jax.numpy as jnp
import numpy as np
from jax import lax
from jax.experimental import pallas as pl
from jax.experimental.pallas import tpu as pltpu

F32 = jnp.float32
BF16 = jnp.bfloat16

RMS_EPS = 1e-6
N_MOD = 9
VMEM_LIMIT_BYTES = 56 * 1024 * 1024

N_HEADS = 16
HEAD_DIM = 64
N_KV_HEADS = 4
GROUP = N_HEADS // N_KV_HEADS
IDX_HEADS = 8
IDX_DIM = 64
TOPK_MAX = 256
LANE = 128
LOG2E = 1.4426950408889634


def _cparams(*sem):
    return pltpu.CompilerParams(dimension_semantics=sem, vmem_limit_bytes=VMEM_LIMIT_BYTES)


def _dot(a, b):
    return jnp.dot(a, b, preferred_element_type=F32)


def _dot_nt(a, b):
    return lax.dot_general(a, b, (((1,), (1,)), ((), ())), preferred_element_type=F32)


def _dot_tn(a, b):
    return lax.dot_general(a, b, (((0,), (0,)), ((), ())), preferred_element_type=F32)


def _modulate(h, sh, sc):
    ms = jnp.mean(h * h, axis=-1, keepdims=True)
    return h * lax.rsqrt(ms + RMS_EPS) * (1.0 + sc) + sh


def _ada_kernel(c_ref, w_ref, b_ref, o_ref):
    c = c_ref[...]
    ca = (c * jax.nn.sigmoid(c)).astype(BF16)
    o_ref[...] = _dot(ca, w_ref[...].astype(BF16)) + b_ref[...]


def _ada_mod(c, ada_w, ada_b):
    L, D, N = ada_w.shape
    B = c.shape[0]
    tn = 1536
    return pl.pallas_call(
        _ada_kernel,
        grid=(L, N // tn),
        in_specs=[
            pl.BlockSpec((B, D), lambda l, j: (0, 0)),
            pl.BlockSpec((None, D, tn), lambda l, j: (l, 0, j)),
            pl.BlockSpec((None, 1, tn), lambda l, j: (l, 0, j)),
        ],
        out_specs=pl.BlockSpec((None, B, tn), lambda l, j: (l, 0, j)),
        out_shape=jax.ShapeDtypeStruct((L, B, N), F32),
        compiler_params=_cparams("parallel", "parallel"),
    )(c, ada_w, ada_b.reshape(L, 1, N))


FF_CHUNK = 256


def _ffn_kernel(h_ref, sh_ref, sc_ref, gt_ref, wgu_ref, wd_ref, o_ref, *, d_ff):
    h = h_ref[...]
    u = _modulate(h, sh_ref[...], sc_ref[...]).astype(BF16)
    acc = None
    for c in range(d_ff // FF_CHUNK):
        lo = c * FF_CHUNK
        g = _dot(u, wgu_ref[:, lo:lo + FF_CHUNK])
        up = _dot(u, wgu_ref[:, d_ff + lo:d_ff + lo + FF_CHUNK])
        a = (g * jax.nn.sigmoid(g) * up).astype(BF16)
        part = _dot(a, wd_ref[lo:lo + FF_CHUNK, :])
        acc = part if acc is None else acc + part
    o_ref[...] = h + (0.5 * gt_ref[...]) * acc


def _ffn(h, mod, k0, w_gu, w_down, tm=512):
    B, S, D = h.shape
    d_ff = w_down.shape[0]
    mspec = lambda k: pl.BlockSpec((None, None, 1, D), lambda b, i: (b, k, 0, 0))
    return pl.pallas_call(
        functools.partial(_ffn_kernel, d_ff=d_ff),
        grid=(B, S // tm),
        in_specs=[
            pl.BlockSpec((None, tm, D), lambda b, i: (b, i, 0)),
            mspec(k0), mspec(k0 + 1), mspec(k0 + 2),
            pl.BlockSpec((D, 2 * d_ff), lambda b, i: (0, 0), pipeline_mode=pl.Buffered(1)),
            pl.BlockSpec((d_ff, D), lambda b, i: (0, 0), pipeline_mode=pl.Buffered(1)),
        ],
        out_specs=pl.BlockSpec((None, tm, D), lambda b, i: (b, i, 0)),
        out_shape=jax.ShapeDtypeStruct((B, S, D), F32),
        compiler_params=_cparams("parallel", "parallel"),
    )(h, mod, mod, mod, w_gu, w_down)


def _dsa_in_kernel(h_ref, sh_ref, sc_ref, wqT_ref, wk_ref, wvT_ref, wiqT_ref, wik_ref, wiwT_ref,
                   qn_ref, kn_ref, ikn_ref,
                   qT_ref, k_ref, vT_ref, iqT_ref, ik_ref, iwT_ref):
    u = _modulate(h_ref[...], sh_ref[...], sc_ref[...]).astype(BF16)
    tm = u.shape[0]
    qT = _dot_nt(wqT_ref[...], u).reshape(N_HEADS, HEAD_DIM, tm)
    ms = jnp.mean(qT * qT, axis=1, keepdims=True)
    qT = qT * lax.rsqrt(ms + RMS_EPS) * qn_ref[...][None] * (HEAD_DIM ** -0.5 * LOG2E)
    qT_ref[...] = qT.reshape(N_HEADS * HEAD_DIM, tm).astype(BF16)
    k = _dot(u, wk_ref[...])
    kn = kn_ref[...]
    for j in range(N_KV_HEADS):
        kj = k[:, j * LANE:(j + 1) * LANE]
        ms = jnp.sum(kj * kj, axis=-1, keepdims=True) * (1.0 / HEAD_DIM)
        k_ref[:, j * LANE:(j + 1) * LANE] = (kj * lax.rsqrt(ms + RMS_EPS) * kn).astype(BF16)
    vT_ref[...] = _dot_nt(wvT_ref[...], u).astype(BF16)
    iqT_ref[...] = _dot_nt(wiqT_ref[...], u).astype(BF16)
    ik = _dot(u, wik_ref[...])
    ms = jnp.sum(ik * ik, axis=-1, keepdims=True) * (1.0 / IDX_DIM)
    ik_ref[...] = (ik * lax.rsqrt(ms + RMS_EPS) * ikn_ref[...]).astype(BF16)
    iwT_ref[...] = _dot_nt(wiwT_ref[...], u) * (IDX_HEADS ** -0.5 * IDX_DIM ** -0.5)


DSA_KC = 256
DSA_QB = 256


def _dsa_in(h, mod, w_in, q_norm, k_norm, ik_norm, tm=DSA_KC):
    B, S, D = h.shape
    QW, KW = N_HEADS * HEAD_DIM, N_KV_HEADS * HEAD_DIM
    IQW = IDX_HEADS * IDX_DIM
    o1, o2, o3, o4, o5 = QW, QW + KW, QW + 2 * KW, QW + 2 * KW + IQW, QW + 2 * KW + IQW + IDX_DIM
    wb = w_in.astype(BF16)
    wqT = wb[:, :o1].T
    wk = jnp.pad(wb[:, o1:o2].reshape(D, N_KV_HEADS, HEAD_DIM), ((0, 0), (0, 0), (0, LANE - HEAD_DIM))).reshape(D, N_KV_HEADS * LANE)
    wvT = wb[:, o2:o3].T
    wiqT = wb[:, o3:o4].T
    wik = jnp.pad(wb[:, o4:o5], ((0, 0), (0, LANE - IDX_DIM)))
    wiwT = wb[:, o5:].T
    qn = q_norm.astype(F32).reshape(HEAD_DIM, 1)
    kn = jnp.pad(k_norm.astype(F32), (0, LANE - HEAD_DIM)).reshape(1, LANE)
    ikn = jnp.pad(ik_norm.astype(F32), (0, LANE - IDX_DIM)).reshape(1, LANE)
    mspec = lambda k: pl.BlockSpec((None, None, 1, D), lambda b, i: (b, k, 0, 0))
    full = lambda a: pl.BlockSpec(a.shape, lambda b, i: (0,) * a.ndim)
    nc = S // tm
    return pl.pallas_call(
        _dsa_in_kernel,
        grid=(B, nc),
        in_specs=[pl.BlockSpec((None, tm, D), lambda b, i: (b, i, 0)), mspec(3), mspec(4),
                  full(wqT), full(wk), full(wvT), full(wiqT), full(wik), full(wiwT),
                  full(qn), full(kn), full(ikn)],
        out_specs=[
            pl.BlockSpec((None, QW, tm), lambda b, i: (b, 0, i)),
            pl.BlockSpec((None, tm, N_KV_HEADS * LANE), lambda b, i: (b, i, 0)),
            pl.BlockSpec((None, None, KW, tm), lambda b, i: (b, i, 0, 0)),
            pl.BlockSpec((None, IQW, tm), lambda b, i: (b, 0, i)),
            pl.BlockSpec((None, tm, LANE), lambda b, i: (b, i, 0)),
            pl.BlockSpec((None, IDX_HEADS, tm), lambda b, i: (b, 0, i)),
        ],
        out_shape=[
            jax.ShapeDtypeStruct((B, QW, S), BF16),
            jax.ShapeDtypeStruct((B, S, N_KV_HEADS * LANE), BF16),
            jax.ShapeDtypeStruct((B, nc, KW, tm), BF16),
            jax.ShapeDtypeStruct((B, IQW, S), BF16),
            jax.ShapeDtypeStruct((B, S, LANE), BF16),
            jax.ShapeDtypeStruct((B, IDX_HEADS, S), F32),
        ],
        compiler_params=_cparams("parallel", "parallel"),
    )(h, mod, mod, wqT, wk, wvT, wiqT, wik, wiwT, qn, kn, ikn)


NEG_BIG = -(2.0 ** 100)
DSA_VPAD = 16
N_CHAINS = 4


def _ukey_to_float(ukey):
    key = ukey ^ np.int32(-2 ** 31)
    bits = key ^ ((key >> 31) & np.int32(0x7FFFFFFF))
    return lax.bitcast_convert_type(bits, F32)


def _fold_rows(x, op, rows):
    while x.shape[0] > rows:
        half = x.shape[0] // 2
        x = op(x[:half], x[half:])
    return x


def _dsa_core_kernel(iqT_ref, iwT_ref, qT_ref, ik_ref, k_ref, vT_ref, oT_ref,
                     sc_ref, qa_ref, iqa_ref, d_ref, st_ref, p_ref, m_ref, acc_ref, *, k_top, n_chunks):
    KC, QB = DSA_KC, DSA_QB
    qi = pl.program_id(1)
    q0 = qi * QB
    nk = (q0 + QB + KC - 1) // KC
    qpos = q0 + lax.broadcasted_iota(jnp.int32, (1, QB), 1)
    zeros_h = jnp.zeros((LANE - HEAD_DIM, QB), BF16)

    for h in range(IDX_HEADS):
        iqa_ref[:, h * QB:(h + 1) * QB] = jnp.concatenate(
            [iqT_ref[h * IDX_DIM:(h + 1) * IDX_DIM, :], zeros_h], axis=0)
    for j in range(N_KV_HEADS):
        for g in range(GROUP):
            hh = j * GROUP + g
            qa_ref[j, :, g * QB:(g + 1) * QB] = jnp.concatenate(
                [qT_ref[hh * HEAD_DIM:(hh + 1) * HEAD_DIM, :], zeros_h], axis=0)
    iw = iwT_ref[...]

    npair = (nk + 1) // 2

    def clamp_chunk(c):
        return jnp.minimum(c, n_chunks - 1)

    def idx_dot(c, slot):
        k0 = pl.multiple_of(c * KC, KC)
        d_ref[slot] = _dot(ik_ref[pl.ds(k0, KC), :], iqa_ref[...])

    def idx_score(c, slot):
        k0 = pl.multiple_of(c * KC, KC)
        s = jnp.maximum(d_ref[slot, :, 0:QB], 0.0) * iw[0:1, :]
        for h in range(1, IDX_HEADS):
            s = s + jnp.maximum(d_ref[slot, :, h * QB:(h + 1) * QB], 0.0) * iw[h:h + 1, :]
        kpos = k0 + lax.broadcasted_iota(jnp.int32, (KC, 1), 0)
        sc_ref[pl.ds(k0, KC), :] = jnp.where(kpos <= qpos, s, -jnp.inf)

    idx_dot(0, 0)

    def p1(i, carry):
        c = 2 * i
        idx_dot(c + 1, 1)
        idx_score(c, 0)
        idx_dot(clamp_chunk(c + 2), 0)
        idx_score(c + 1, 1)
        return carry

    lax.fori_loop(0, npair, p1, 0)

    def count_not(pred):
        def body(i, accs):
            accs = list(accs)
            for half in range(2):
                k0 = pl.multiple_of((2 * i + half) * KC, KC)
                ind = jnp.where(pred(sc_ref[pl.ds(k0, KC), :]), 0.0, 1.0)
                for r in range(KC // 8):
                    accs[r % N_CHAINS] = accs[r % N_CHAINS] + ind[r * 8:(r + 1) * 8]
            return tuple(accs)
        accs = lax.fori_loop(0, npair, body, tuple(jnp.zeros((8, QB), F32) for _ in range(N_CHAINS)))
        acc = (accs[0] + accs[1]) + (accs[2] + accs[3])
        return jnp.sum(acc, axis=0, keepdims=True)

    n_staged = (npair * (2 * KC)).astype(F32)
    ukey = jnp.zeros((1, QB), jnp.int32)
    for bit in range(31, -1, -1):
        cand = ukey | np.int32(np.uint32(1 << bit).astype(np.int32))
        t = _ukey_to_float(cand)
        cnt = count_not(lambda s: s < t)
        ukey = jnp.where(cnt >= k_top, cand, ukey)
    thr = _ukey_to_float(ukey)

    m_ref[...] = jnp.full(m_ref.shape, NEG_BIG, F32)
    acc_ref[...] = jnp.zeros(acc_ref.shape, F32)
    r_i = lax.broadcasted_iota(jnp.int32, (KC, KC), 0)
    c_i = lax.broadcasted_iota(jnp.int32, (KC, KC), 1)
    tril = jnp.where(c_i <= r_i, 1.0, 0.0).astype(BF16)
    ones_rows = jnp.ones((DSA_VPAD, KC), BF16)

    HALF = GROUP * QB // 2

    def logits(c, slot, j, half):
        k0 = pl.multiple_of(c * KC, KC)
        cols = slice(half * HALF, (half + 1) * HALF)
        st_ref[slot, j, :, cols] = _dot(k_ref[pl.ds(k0, KC), j * LANE:(j + 1) * LANE],
                                        qa_ref[j, :, cols]).astype(BF16)

    for j in range(N_KV_HEADS):
        logits(0, 0, j, 0)
        logits(0, 0, j, 1)

    def attend(c, slot, tie_seen, c_next, need):
        k0 = pl.multiple_of(c * KC, KC)
        s = sc_ref[pl.ds(k0, KC), :]
        kpos = k0 + lax.broadcasted_iota(jnp.int32, (KC, 1), 0)
        eq = s == thr
        eqf = jnp.where(eq, 1.0, 0.0)
        incl = _dot(tril, eqf.astype(BF16))
        rank = tie_seen + incl - eqf
        sel = jnp.logical_and(jnp.logical_or(s > thr, jnp.logical_and(eq, rank < need)), kpos <= qpos)
        tie_seen = tie_seen + incl[KC - 1:KC, :]
        bias = jnp.where(sel, 0.0, NEG_BIG).astype(BF16)
        for j in range(N_KV_HEADS):
            alphas = []
            for g in range(GROUP):
                sl = slice(g * QB, (g + 1) * QB)
                sg = st_ref[slot, j, :, sl] + bias
                m_old = m_ref[j, :, sl]
                mx = jnp.max(_fold_rows(sg, jnp.maximum, 16), axis=0, keepdims=True).astype(F32)
                m_new = jnp.maximum(m_old, mx)
                alphas.append(jnp.exp2(m_old - m_new))
                m_ref[j, :, sl] = m_new
                p_ref[j, :, sl] = jnp.exp2(sg - m_new.astype(BF16))
                if g % 2 == 1:
                    logits(c_next, 1 - slot, j, g // 2)
            v_aug = jnp.concatenate([vT_ref[c, j * HEAD_DIM:(j + 1) * HEAD_DIM, :], ones_rows], axis=0)
            acc_ref[j] = acc_ref[j] * jnp.concatenate(alphas, axis=1) + _dot(v_aug, p_ref[j])
        return tie_seen

    need = k_top - (n_staged - count_not(lambda s: s > thr))

    def p3(i, tie_seen):
        c = 2 * i
        tie_seen = attend(c, 0, tie_seen, c + 1, need)
        return attend(c + 1, 1, tie_seen, clamp_chunk(c + 2), need)

    lax.fori_loop(0, npair, p3, jnp.zeros((1, QB), F32))

    for j in range(N_KV_HEADS):
        o = acc_ref[j, 0:HEAD_DIM, :] / acc_ref[j, HEAD_DIM:HEAD_DIM + 1, :]
        for g in range(GROUP):
            hh = j * GROUP + g
            oT_ref[hh * HEAD_DIM:(hh + 1) * HEAD_DIM, :] = o[:, g * QB:(g + 1) * QB].astype(BF16)


def _dsa_core(qT, k, vT, iqT, ik, iwT):
    B, QW, S = qT.shape
    KC, QB = DSA_KC, DSA_QB
    k_top = min(TOPK_MAX, S // 4)
    nc = S // KC
    return pl.pallas_call(
        functools.partial(_dsa_core_kernel, k_top=k_top, n_chunks=nc),
        grid=(B, S // QB),
        in_specs=[
            pl.BlockSpec((None, IDX_HEADS * IDX_DIM, QB), lambda b, i: (b, 0, i)),
            pl.BlockSpec((None, IDX_HEADS, QB), lambda b, i: (b, 0, i)),
            pl.BlockSpec((None, QW, QB), lambda b, i: (b, 0, i)),
            pl.BlockSpec((None, S, LANE), lambda b, i: (b, 0, 0)),
            pl.BlockSpec((None, S, N_KV_HEADS * LANE), lambda b, i: (b, 0, 0)),
            pl.BlockSpec((None, nc, N_KV_HEADS * HEAD_DIM, KC), lambda b, i: (b, 0, 0, 0)),
        ],
        out_specs=pl.BlockSpec((None, QW, QB), lambda b, i: (b, 0, i)),
        out_shape=jax.ShapeDtypeStruct((B, QW, S), BF16),
        scratch_shapes=[
            pltpu.VMEM((S, QB), F32),
            pltpu.VMEM((N_KV_HEADS, LANE, GROUP * QB), BF16),
            pltpu.VMEM((LANE, IDX_HEADS * QB), BF16),
            pltpu.VMEM((2, KC, IDX_HEADS * QB), F32),
            pltpu.VMEM((2, N_KV_HEADS, KC, GROUP * QB), BF16),
            pltpu.VMEM((N_KV_HEADS, KC, GROUP * QB), BF16),
            pltpu.VMEM((N_KV_HEADS, 1, GROUP * QB), F32),
            pltpu.VMEM((N_KV_HEADS, HEAD_DIM + DSA_VPAD, GROUP * QB), F32),
        ],
        compiler_params=_cparams("parallel", "arbitrary"),
    )(iqT, iwT, qT, ik, k, vT)


def _out_T_kernel(xT_ref, w_ref, h_ref, gt_ref, o_ref):
    o_ref[...] = h_ref[...] + gt_ref[...] * _dot_tn(xT_ref[...], w_ref[...])


def _dsa_out(oT, w_out, h, mod, tm=512):
    B, S, D = h.shape
    QW = oT.shape[1]
    return pl.pallas_call(
        _out_T_kernel,
        grid=(B, S // tm),
        in_specs=[
            pl.BlockSpec((None, QW, tm), lambda b, i: (b, 0, i)),
            pl.BlockSpec((QW, D), lambda b, i: (0, 0)),
            pl.BlockSpec((None, tm, D), lambda b, i: (b, i, 0)),
            pl.BlockSpec((None, None, 1, D), lambda b, i: (b, 5, 0, 0)),
        ],
        out_specs=pl.BlockSpec((None, tm, D), lambda b, i: (b, i, 0)),
        out_shape=jax.ShapeDtypeStruct((B, S, D), F32),
        compiler_params=_cparams("parallel", "parallel"),
    )(oT, w_out.astype(BF16), h, mod)


def _dsa_mixer(h, mod, w_in, q_norm, k_norm, ik_norm, w_out):
    qT, k, vT, iqT, ik, iwT = _dsa_in(h, mod, w_in, q_norm, k_norm, ik_norm)
    oT = _dsa_core(qT, k, vT, iqT, ik, iwT)
    return _dsa_out(oT, w_out, h, mod)


RWKV_HEAD = 64
GN_EPS = 64e-5
RW_PACK = 4
RW_LANES = RW_PACK * RWKV_HEAD
RW_C = 64
RW_TB = 256


def _split3(x):
    hi = x.astype(BF16)
    r1 = x - hi.astype(F32)
    mid = r1.astype(BF16)
    lo = (r1 - mid.astype(F32)).astype(BF16)
    return hi, mid, lo


def _gsum(x, g_ref, gt_ref):
    hi = x.astype(BF16)
    lo = (x - hi.astype(F32)).astype(BF16)
    s = _dot(jnp.concatenate([hi, lo], axis=1), g_ref[...])
    return _dot(jnp.concatenate(_split3(s), axis=1), gt_ref[...])


def _rwkv_in_kernel(h_ref, hp_ref, sh_ref, sc_ref, mu_ref, wrkv_ref, w0_ref, w1_ref, w2_ref,
                    a0_ref, a1_ref, a2_ref, g1_ref, g2_ref, kk_ref, ka_ref, gm_ref, gmt_ref,
                    r_ref, lw_ref, k_ref, v_ref, kkn_ref, b_ref, g_ref):
    sh, sc = sh_ref[...], sc_ref[...]
    u = _modulate(h_ref[...], sh, sc)
    tm = u.shape[0]
    prev = _modulate(hp_ref[...], sh, sc)[7:8, :]
    prev = jnp.where(pl.program_id(1) == 0, 0.0, prev)
    row = lax.broadcasted_iota(jnp.int32, (tm, 1), 0)
    xx = jnp.where(row == 0, prev, pltpu.roll(u, 1, 0)) - u
    mix = lambda i: (u + xx * mu_ref[i:i + 1, :]).astype(BF16)
    r = _dot(mix(0), wrkv_ref[0])
    k = _dot(mix(1), wrkv_ref[1])
    v = _dot(mix(2), wrkv_ref[2])
    wl = w0_ref[...] + _dot(jnp.tanh(_dot(mix(3), w1_ref[...])).astype(BF16), w2_ref[...])
    nwl = -wl
    w_log = -(jnp.maximum(nwl, 0.0) + jnp.log(1.0 + jnp.exp(-jnp.abs(nwl)))) - 0.5
    lw_ref[...] = -jnp.exp(w_log)
    a = jax.nn.sigmoid(a0_ref[...] + _dot(_dot(mix(4), a1_ref[...]).astype(BF16), a2_ref[...]))
    g_ref[...] = _dot(jax.nn.sigmoid(_dot(mix(5), g1_ref[...])).astype(BF16), g2_ref[...])
    kk = k * kk_ref[...]
    nrm = jnp.sqrt(_gsum(kk * kk, gm_ref, gmt_ref))
    kk = kk / jnp.maximum(nrm, 1e-12)
    r_ref[...] = r
    k_ref[...] = k * (1.0 + (a - 1.0) * ka_ref[...])
    v_ref[...] = v
    kkn_ref[...] = kk
    b_ref[...] = kk * a


def _head_indicator(D):
    gm = (np.arange(D)[:, None] // RWKV_HEAD == np.arange(D // RWKV_HEAD)[None, :]).astype(np.float32)
    return jnp.asarray(np.concatenate([gm, gm], axis=0), BF16), jnp.asarray(np.concatenate([gm.T] * 3, axis=0), BF16)


def _rwkv_in(h, mod, mu, w_rkv, w0, w1, w2, a0, a1, a2, g1, g2, k_k, k_a, tm=256):
    B, S, D = h.shape
    gm, gmt = _head_indicator(D)
    row = lambda a: a.astype(F32).reshape(1, D)
    full = lambda a: pl.BlockSpec(a.shape, lambda b, i: (0,) * a.ndim)
    mspec = lambda k: pl.BlockSpec((None, None, 1, D), lambda b, i: (b, k, 0, 0))
    tile = pl.BlockSpec((None, tm, D), lambda b, i: (b, i, 0))
    params = [mu.astype(F32), w_rkv.astype(BF16), row(w0), w1.astype(BF16), w2.astype(BF16),
              row(a0), a1.astype(BF16), a2.astype(BF16), g1.astype(BF16), g2.astype(BF16),
              row(k_k), row(k_a), gm, gmt]
    return pl.pallas_call(
        _rwkv_in_kernel,
        grid=(B, S // tm),
        in_specs=[tile,
                  pl.BlockSpec((None, 8, D), lambda b, i: (b, jnp.maximum(i * (tm // 8) - 1, 0), 0)),
                  mspec(3), mspec(4)] + [full(p) for p in params],
        out_specs=[tile] * 7,
        out_shape=[jax.ShapeDtypeStruct((B, S, D), F32)] * 7,
        compiler_params=_cparams("parallel", "parallel"),
    )(h, h, mod, mod, *params)


def _rwkv_scan_kernel(r_ref, lw_ref, k_ref, v_ref, kk_ref, b_ref, y_ref, zt_ref):
    C, P, L = RW_C, RW_PACK, RW_LANES
    N = P * C
    assert C == RWKV_HEAD
    n_packs = r_ref.shape[1] // L
    n_chunks = r_ref.shape[0] // C
    units = [(p, c) for p in range(n_packs) for c in range(n_chunks)]

    @pl.when(pl.program_id(1) == 0)
    def _():
        zt_ref[...] = jnp.zeros(zt_ref.shape, F32)

    lane = lax.broadcasted_iota(jnp.int32, (1, L), 1)
    hmask = [lane // C == hd for hd in range(P)]
    bf = lambda x: x.astype(BF16)
    stack = lambda x: jnp.concatenate([jnp.where(hmask[hd], x, jnp.zeros_like(x)) for hd in range(P)], axis=0)
    t_i = lax.broadcasted_iota(jnp.int32, (C, N), 0)
    i_i = lax.broadcasted_iota(jnp.int32, (C, N), 1) % C
    low_strict = t_i > i_i
    low_incl = t_i >= i_i
    eye_w = jnp.where(t_i == i_i, 1.0, 0.0)
    bd_mask = (lax.broadcasted_iota(jnp.int32, (L, L), 0) // C) == (lax.broadcasted_iota(jnp.int32, (L, L), 1) // C)
    tri = jnp.where(lax.broadcasted_iota(jnp.int32, (C, C), 0) >= lax.broadcasted_iota(jnp.int32, (C, C), 1),
                    1.0, 0.0).astype(BF16)
    tile = lambda ref, u: ref[u[1] * C:(u[1] + 1) * C, u[0] * L:(u[0] + 1) * L]

    cw = {}
    tri3 = jnp.concatenate([tri, tri, tri], axis=1)
    for u in units:
        cw[u] = _dot(tri3, jnp.concatenate(_split3(tile(lw_ref, u)), axis=0))
    al, rt, bw, kw, v, vs, aa, w_end = {}, {}, {}, {}, {}, {}, {}, {}
    for u in units:
        lw = tile(lw_ref, u)
        w_incl = jnp.exp(cw[u])
        w_excl = jnp.exp(cw[u] - lw)
        w_inv = jnp.exp(-cw[u])
        w_end[u] = w_incl[C - 1:C, :]
        v[u] = bf(tile(v_ref, u))
        vs[u] = stack(v[u])
        al[u] = bf(-tile(kk_ref, u) * w_excl)
        rt[u] = tile(r_ref, u) * w_incl
        bh = tile(b_ref, u) * w_inv
        kh = tile(k_ref, u) * w_inv
        bw[u] = bf(bh * w_end[u])
        kw[u] = bf(kh * w_end[u])
        aa[u] = _dot_nt(jnp.concatenate([al[u], bf(rt[u])], axis=0),
                        jnp.concatenate([stack(bf(bh)), stack(bf(kh))], axis=0))
    a_ak, a_r, pw, t = {}, {}, {}, {}
    for u in units:
        a_ab = jnp.where(low_strict, aa[u][:C, :N], 0.0)
        a_ak[u] = bf(jnp.where(low_strict, aa[u][:C, N:], 0.0))
        a_r[u] = bf(jnp.concatenate([jnp.where(low_incl, aa[u][C:, :N], 0.0),
                                     jnp.where(low_incl, aa[u][C:, N:], 0.0)], axis=1))
        pw[u] = a_ab
        t[u] = eye_w + a_ab
    n_fac = int(np.log2(C))
    akv = {}
    for u in units:
        pwb = bf(pw[u])
        pw[u] = _dot(pwb, stack(pwb))
        akv[u] = _dot(a_ak[u], vs[u])
    for s in range(2, n_fac + 1):
        for u in units:
            pwb = bf(pw[u])
            rhs = stack(pwb)
            if s < n_fac:
                both = _dot(jnp.concatenate([bf(t[u]), pwb], axis=0), rhs)
                t[u] = t[u] + both[:C]
                pw[u] = both[C:]
            else:
                t[u] = t[u] + _dot(bf(t[u]), rhs)
    pm, q = {}, {}
    for u in units:
        tb = bf(t[u])
        pm[u] = bf(_dot(tb, stack(al[u])))
        q[u] = bf(_dot(tb, stack(bf(akv[u]))))
    pr, y0, bk = {}, {}, {}
    for u in units:
        rp = bf(rt[u] + _dot(a_r[u][:, :N], stack(pm[u])))
        pr[u] = jnp.concatenate([pm[u], rp], axis=0)
        y0[u] = _dot(a_r[u], jnp.concatenate([stack(q[u]), vs[u]], axis=0))
        bk[u] = jnp.concatenate([bw[u], kw[u]], axis=0)
    zt = [zt_ref[p] for p in range(n_packs)]
    for c in range(n_chunks):
        for p in range(n_packs):
            u = (p, c)
            uy = _dot_nt(pr[u], bf(zt[p]))
            y_ref[c * C:(c + 1) * C, p * L:(p + 1) * L] = uy[C:] + y0[u]
            upd = _dot_tn(jnp.concatenate([bf(uy[:C] + q[u]), v[u]], axis=0), bk[u])
            zt[p] = zt[p] * w_end[u] + jnp.where(bd_mask, upd, 0.0)
    for p in range(n_packs):
        zt_ref[p] = zt[p]


def _rwkv_scan(r, lw, k, v, kk, b):
    B, S, D = r.shape
    tile = pl.BlockSpec((None, RW_TB, D), lambda bb, t: (bb, t, 0))
    return pl.pallas_call(
        _rwkv_scan_kernel,
        grid=(B, S // RW_TB),
        in_specs=[tile] * 6,
        out_specs=tile,
        out_shape=jax.ShapeDtypeStruct((B, S, D), F32),
        scratch_shapes=[pltpu.VMEM((D // RW_LANES, RW_LANES, RW_LANES), F32)],
        compiler_params=_cparams("parallel", "arbitrary"),
    )(r, lw, k, v, kk, b)


def _rwkv_out_kernel(y_ref, r_ref, k_ref, v_ref, g_ref, lnw_ref, lnb_ref, rk_ref, gm_ref, gmt_ref,
                     w_ref, h_ref, gt_ref, o_ref):
    y = y_ref[...]
    inv_n = 1.0 / RWKV_HEAD
    yc = y - _gsum(y, gm_ref, gmt_ref) * inv_n
    var = _gsum(yc * yc, gm_ref, gmt_ref) * inv_n
    yn = yc * lax.rsqrt(var + GN_EPS) * lnw_ref[...] + lnb_ref[...]
    v = v_ref[...]
    bonus = _gsum(r_ref[...] * k_ref[...] * rk_ref[...], gm_ref, gmt_ref) * v
    out = ((yn + bonus) * g_ref[...]).astype(BF16)
    o_ref[...] = h_ref[...] + gt_ref[...] * _dot(out, w_ref[...])


def _rwkv_out(y, r, k, v, g, ln_w, ln_b, r_k, w_out, h, mod, tm=256):
    B, S, D = h.shape
    gm, gmt = _head_indicator(D)
    row = lambda a: a.astype(F32).reshape(1, D)
    full = lambda a: pl.BlockSpec(a.shape, lambda b, i: (0,) * a.ndim)
    tile = pl.BlockSpec((None, tm, D), lambda b, i: (b, i, 0))
    params = [row(ln_w), row(ln_b), row(r_k), gm, gmt, w_out.astype(BF16)]
    return pl.pallas_call(
        _rwkv_out_kernel,
        grid=(B, S // tm),
        in_specs=[tile] * 5 + [full(p) for p in params] + [tile, pl.BlockSpec((None, None, 1, D), lambda b, i: (b, 5, 0, 0))],
        out_specs=tile,
        out_shape=jax.ShapeDtypeStruct((B, S, D), F32),
        compiler_params=_cparams("parallel", "parallel"),
    )(y, r, k, v, g, *params, h, mod)


def _rwkv_mixer(h, mod, mu, w_rkv, w0, w1, w2, a0, a1, a2, g1, g2, k_k, k_a, r_k, ln_w, ln_b, w_out):
    r, lw, k, v, kk, b, g = _rwkv_in(h, mod, mu, w_rkv, w0, w1, w2, a0, a1, a2, g1, g2, k_k, k_a)
    y = _rwkv_scan(r, lw, k, v, kk, b)
    return _rwkv_out(y, r, k, v, g, ln_w, ln_b, r_k, w_out, h, mod)


def kernel(x, c, ada_w, ada_b, ffn_w_gu, ffn_w_down, dsa_w_in, dsa_q_norm, dsa_k_norm, dsa_ik_norm, dsa_w_out, rwkv_mu, rwkv_w_rkv, rwkv_w0, rwkv_w1, rwkv_w2, rwkv_a0, rwkv_a1, rwkv_a2, rwkv_g1, rwkv_g2, rwkv_k_k, rwkv_k_a, rwkv_r_k, rwkv_ln_w, rwkv_ln_b, rwkv_w_out):
    B, S, D = x.shape
    depth = ada_w.shape[0]
    mods = _ada_mod(c, ada_w, ada_b)
    h = x
    for i in range(depth):
        mod = mods[i].reshape(B, N_MOD, 1, D)
        j = i // 2
        h = _ffn(h, mod, 0, ffn_w_gu[i, 0].astype(BF16), ffn_w_down[i, 0].astype(BF16))
        if i % 2 == 0:
            h = _dsa_mixer(h, mod, dsa_w_in[j], dsa_q_norm[j], dsa_k_norm[j], dsa_ik_norm[j], dsa_w_out[j])
        else:
            h = _rwkv_mixer(h, mod, rwkv_mu[j], rwkv_w_rkv[j], rwkv_w0[j], rwkv_w1[j], rwkv_w2[j], rwkv_a0[j],
                            rwkv_a1[j], rwkv_a2[j], rwkv_g1[j], rwkv_g2[j], rwkv_k_k[j], rwkv_k_a[j],
                            rwkv_r_k[j], rwkv_ln_w[j], rwkv_ln_b[j], rwkv_w_out[j])
        h = _ffn(h, mod, 6, ffn_w_gu[i, 1].astype(BF16), ffn_w_down[i, 1].astype(BF16))
    return h
```

```python
import functools

import jax
import jax.numpy as jnp
import numpy as np
from jax import lax
from jax.experimental import pallas as pl
from jax.experimental.pallas import tpu as pltpu

F32 = jnp.float32
BF16 = jnp.bfloat16

RMS_EPS = 1e-6
N_MOD = 9
VMEM_LIMIT_BYTES = 56 * 1024 * 1024

N_HEADS = 16
HEAD_DIM = 64
N_KV_HEADS = 4
GROUP = N_HEADS // N_KV_HEADS
IDX_HEADS = 8
IDX_DIM = 64
TOPK_MAX = 256
LANE = 128
LOG2E = 1.4426950408889634


def _cparams(*sem):
    return pltpu.CompilerParams(dimension_semantics=sem, vmem_limit_bytes=VMEM_LIMIT_BYTES)


def _dot(a, b):
    return jnp.dot(a, b, preferred_element_type=F32)


def _dot_nt(a, b):
    return lax.dot_general(a, b, (((1,), (1,)), ((), ())), preferred_element_type=F32)


def _dot_tn(a, b):
    return lax.dot_general(a, b, (((0,), (0,)), ((), ())), preferred_element_type=F32)


def _modulate(h, sh, sc):
    ms = jnp.mean(h * h, axis=-1, keepdims=True)
    return h * lax.rsqrt(ms + RMS_EPS) * (1.0 + sc) + sh


def _ada_kernel(c_ref, w_ref, b_ref, o_ref):
    c = c_ref[...]
    ca = (c * jax.nn.sigmoid(c)).astype(BF16)
    o_ref[...] = _dot(ca, w_ref[...].astype(BF16)) + b_ref[...]


def _ada_mod(c, ada_w, ada_b):
    L, D, N = ada_w.shape
    B = c.shape[0]
    tn = 1536
    return pl.pallas_call(
        _ada_kernel,
        grid=(L, N // tn),
        in_specs=[
            pl.BlockSpec((B, D), lambda l, j: (0, 0)),
            pl.BlockSpec((None, D, tn), lambda l, j: (l, 0, j)),
            pl.BlockSpec((None, 1, tn), lambda l, j: (l, 0, j)),
        ],
        out_specs=pl.BlockSpec((None, B, tn), lambda l, j: (l, 0, j)),
        out_shape=jax.ShapeDtypeStruct((L, B, N), F32),
        compiler_params=_cparams("parallel", "parallel"),
    )(c, ada_w, ada_b.reshape(L, 1, N))


FF_CHUNK = 256


def _ffn_kernel(h_ref, sh_ref, sc_ref, gt_ref, wgu_ref, wd_ref, o_ref, *, d_ff):
    h = h_ref[...]
    u = _modulate(h, sh_ref[...], sc_ref[...]).astype(BF16)
    acc = None
    for c in range(d_ff // FF_CHUNK):
        lo = c * FF_CHUNK
        g = _dot(u, wgu_ref[:, lo:lo + FF_CHUNK])
        up = _dot(u, wgu_ref[:, d_ff + lo:d_ff + lo + FF_CHUNK])
        a = (g * jax.nn.sigmoid(g) * up).astype(BF16)
        part = _dot(a, wd_ref[lo:lo + FF_CHUNK, :])
        acc = part if acc is None else acc + part
    o_ref[...] = h + (0.5 * gt_ref[...]) * acc


def _ffn(h, mod, k0, w_gu, w_down, tm=512):
    B, S, D = h.shape
    d_ff = w_down.shape[0]
    mspec = lambda k: pl.BlockSpec((None, None, 1, D), lambda b, i: (b, k, 0, 0))
    return pl.pallas_call(
        functools.partial(_ffn_kernel, d_ff=d_ff),
        grid=(B, S // tm),
        in_specs=[
            pl.BlockSpec((None, tm, D), lambda b, i: (b, i, 0)),
            mspec(k0), mspec(k0 + 1), mspec(k0 + 2),
            pl.BlockSpec((D, 2 * d_ff), lambda b, i: (0, 0), pipeline_mode=pl.Buffered(1)),
            pl.BlockSpec((d_ff, D), lambda b, i: (0, 0), pipeline_mode=pl.Buffered(1)),
        ],
        out_specs=pl.BlockSpec((None, tm, D), lambda b, i: (b, i, 0)),
        out_shape=jax.ShapeDtypeStruct((B, S, D), F32),
        compiler_params=_cparams("parallel", "parallel"),
    )(h, mod, mod, mod, w_gu, w_down)


def _dsa_in_kernel(h_ref, sh_ref, sc_ref, wqT_ref, wk_ref, wvT_ref, wiqT_ref, wik_ref, wiwT_ref,
                   qn_ref, kn_ref, ikn_ref,
                   qT_ref, k_ref, vT_ref, iqT_ref, ik_ref, iwT_ref):
    u = _modulate(h_ref[...], sh_ref[...], sc_ref[...]).astype(BF16)
    tm = u.shape[0]
    qT = _dot_nt(wqT_ref[...], u).reshape(N_HEADS, HEAD_DIM, tm)
    ms = jnp.mean(qT * qT, axis=1, keepdims=True)
    qT = qT * lax.rsqrt(ms + RMS_EPS) * qn_ref[...][None] * (HEAD_DIM ** -0.5 * LOG2E)
    qT_ref[...] = qT.reshape(N_HEADS * HEAD_DIM, tm).astype(BF16)
    k = _dot(u, wk_ref[...])
    kn = kn_ref[...]
    for j in range(N_KV_HEADS):
        kj = k[:, j * LANE:(j + 1) * LANE]
        ms = jnp.sum(kj * kj, axis=-1, keepdims=True) * (1.0 / HEAD_DIM)
        k_ref[:, j * LANE:(j + 1) * LANE] = (kj * lax.rsqrt(ms + RMS_EPS) * kn).astype(BF16)
    vT_ref[...] = _dot_nt(wvT_ref[...], u).astype(BF16)
    iqT_ref[...] = _dot_nt(wiqT_ref[...], u).astype(BF16)
    ik = _dot(u, wik_ref[...])
    ms = jnp.sum(ik * ik, axis=-1, keepdims=True) * (1.0 / IDX_DIM)
    ik_ref[...] = (ik * lax.rsqrt(ms + RMS_EPS) * ikn_ref[...]).astype(BF16)
    iwT_ref[...] = _dot_nt(wiwT_ref[...], u) * (IDX_HEADS ** -0.5 * IDX_DIM ** -0.5)


DSA_KC = 256
DSA_QB = 256


def _dsa_in(h, mod, w_in, q_norm, k_norm, ik_norm, tm=DSA_KC):
    B, S, D = h.shape
    QW, KW = N_HEADS * HEAD_DIM, N_KV_HEADS * HEAD_DIM
    IQW = IDX_HEADS * IDX_DIM
    o1, o2, o3, o4, o5 = QW, QW + KW, QW + 2 * KW, QW + 2 * KW + IQW, QW + 2 * KW + IQW + IDX_DIM
    wb = w_in.astype(BF16)
    wqT = wb[:, :o1].T
    wk = jnp.pad(wb[:, o1:o2].reshape(D, N_KV_HEADS, HEAD_DIM), ((0, 0), (0, 0), (0, LANE - HEAD_DIM))).reshape(D, N_KV_HEADS * LANE)
    wvT = wb[:, o2:o3].T
    wiqT = wb[:, o3:o4].T
    wik = jnp.pad(wb[:, o4:o5], ((0, 0), (0, LANE - IDX_DIM)))
    wiwT = wb[:, o5:].T
    qn = q_norm.astype(F32).reshape(HEAD_DIM, 1)
    kn = jnp.pad(k_norm.astype(F32), (0, LANE - HEAD_DIM)).reshape(1, LANE)
    ikn = jnp.pad(ik_norm.astype(F32), (0, LANE - IDX_DIM)).reshape(1, LANE)
    mspec = lambda k: pl.BlockSpec((None, None, 1, D), lambda b, i: (b, k, 0, 0))
    full = lambda a: pl.BlockSpec(a.shape, lambda b, i: (0,) * a.ndim)
    nc = S // tm
    return pl.pallas_call(
        _dsa_in_kernel,
        grid=(B, nc),
        in_specs=[pl.BlockSpec((None, tm, D), lambda b, i: (b, i, 0)), mspec(3), mspec(4),
                  full(wqT), full(wk), full(wvT), full(wiqT), full(wik), full(wiwT),
                  full(qn), full(kn), full(ikn)],
        out_specs=[
            pl.BlockSpec((None, QW, tm), lambda b, i: (b, 0, i)),
            pl.BlockSpec((None, tm, N_KV_HEADS * LANE), lambda b, i: (b, i, 0)),
            pl.BlockSpec((None, None, KW, tm), lambda b, i: (b, i, 0, 0)),
            pl.BlockSpec((None, IQW, tm), lambda b, i: (b, 0, i)),
            pl.BlockSpec((None, tm, LANE), lambda b, i: (b, i, 0)),
            pl.BlockSpec((None, IDX_HEADS, tm), lambda b, i: (b, 0, i)),
        ],
        out_shape=[
            jax.ShapeDtypeStruct((B, QW, S), BF16),
            jax.ShapeDtypeStruct((B, S, N_KV_HEADS * LANE), BF16),
            jax.ShapeDtypeStruct((B, nc, KW, tm), BF16),
            jax.ShapeDtypeStruct((B, IQW, S), BF16),
            jax.ShapeDtypeStruct((B, S, LANE), BF16),
            jax.ShapeDtypeStruct((B, IDX_HEADS, S), F32),
        ],
        compiler_params=_cparams("parallel", "parallel"),
    )(h, mod, mod, wqT, wk, wvT, wiqT, wik, wiwT, qn, kn, ikn)


NEG_BIG = -(2.0 ** 100)
DSA_VPAD = 16
N_CHAINS = 4


def _ukey_to_float(ukey):
    key = ukey ^ np.int32(-2 ** 31)
    bits = key ^ ((key >> 31) & np.int32(0x7FFFFFFF))
    return lax.bitcast_convert_type(bits, F32)


def _fold_rows(x, op, rows):
    while x.shape[0] > rows:
        half = x.shape[0] // 2
        x = op(x[:half], x[half:])
    return x


def _dsa_core_kernel(iqT_ref, iwT_ref, qT_ref, ik_ref, k_ref, vT_ref, oT_ref,
                     sc_ref, qa_ref, iqa_ref, d_ref, st_ref, p_ref, m_ref, acc_ref, *, k_top, n_chunks):
    KC, QB = DSA_KC, DSA_QB
    qi = pl.program_id(1)
    q0 = qi * QB
    nk = (q0 + QB + KC - 1) // KC
    qpos = q0 + lax.broadcasted_iota(jnp.int32, (1, QB), 1)
    zeros_h = jnp.zeros((LANE - HEAD_DIM, QB), BF16)

    for h in range(IDX_HEADS):
        iqa_ref[:, h * QB:(h + 1) * QB] = jnp.concatenate(
            [iqT_ref[h * IDX_DIM:(h + 1) * IDX_DIM, :], zeros_h], axis=0)
    for j in range(N_KV_HEADS):
        for g in range(GROUP):
            hh = j * GROUP + g
            qa_ref[j, :, g * QB:(g + 1) * QB] = jnp.concatenate(
                [qT_ref[hh * HEAD_DIM:(hh + 1) * HEAD_DIM, :], zeros_h], axis=0)
    iw = iwT_ref[...]

    npair = (nk + 1) // 2

    def clamp_chunk(c):
        return jnp.minimum(c, n_chunks - 1)

    def idx_dot(c, slot):
        k0 = pl.multiple_of(c * KC, KC)
        d_ref[slot] = _dot(ik_ref[pl.ds(k0, KC), :], iqa_ref[...])

    def idx_score(c, slot):
        k0 = pl.multiple_of(c * KC, KC)
        s = jnp.maximum(d_ref[slot, :, 0:QB], 0.0) * iw[0:1, :]
        for h in range(1, IDX_HEADS):
            s = s + jnp.maximum(d_ref[slot, :, h * QB:(h + 1) * QB], 0.0) * iw[h:h + 1, :]
        kpos = k0 + lax.broadcasted_iota(jnp.int32, (KC, 1), 0)
        sc_ref[pl.ds(k0, KC), :] = jnp.where(kpos <= qpos, s, -jnp.inf)

    idx_dot(0, 0)

    def p1(i, carry):
        c = 2 * i
        idx_dot(c + 1, 1)
        idx_score(c, 0)
        idx_dot(clamp_chunk(c + 2), 0)
        idx_score(c + 1, 1)
        return carry

    lax.fori_loop(0, npair, p1, 0)

    def count_not(pred):
        def body(i, accs):
            accs = list(accs)
            for half in range(2):
                k0 = pl.multiple_of((2 * i + half) * KC, KC)
                ind = jnp.where(pred(sc_ref[pl.ds(k0, KC), :]), 0.0, 1.0)
                for r in range(KC // 8):
                    accs[r % N_CHAINS] = accs[r % N_CHAINS] + ind[r * 8:(r + 1) * 8]
            return tuple(accs)
        accs = lax.fori_loop(0, npair, body, tuple(jnp.zeros((8, QB), F32) for _ in range(N_CHAINS)))
        acc = (accs[0] + accs[1]) + (accs[2] + accs[3])
        return jnp.sum(acc, axis=0, keepdims=True)

    n_staged = (npair * (2 * KC)).astype(F32)
    ukey = jnp.zeros((1, QB), jnp.int32)
    for bit in range(31, -1, -1):
        cand = ukey | np.int32(np.uint32(1 << bit).astype(np.int32))
        t = _ukey_to_float(cand)
        cnt = count_not(lambda s: s < t)
        ukey = jnp.where(cnt >= k_top, cand, ukey)
    thr = _ukey_to_float(ukey)

    m_ref[...] = jnp.full(m_ref.shape, NEG_BIG, F32)
    acc_ref[...] = jnp.zeros(acc_ref.shape, F32)
    r_i = lax.broadcasted_iota(jnp.int32, (KC, KC), 0)
    c_i = lax.broadcasted_iota(jnp.int32, (KC, KC), 1)
    tril = jnp.where(c_i <= r_i, 1.0, 0.0).astype(BF16)
    ones_rows = jnp.ones((DSA_VPAD, KC), BF16)

    HALF = GROUP * QB // 2

    def logits(c, slot, j, half):
        k0 = pl.multiple_of(c * KC, KC)
        cols = slice(half * HALF, (half + 1) * HALF)
        st_ref[slot, j, :, cols] = _dot(k_ref[pl.ds(k0, KC), j * LANE:(j + 1) * LANE],
                                        qa_ref[j, :, cols]).astype(BF16)

    for j in range(N_KV_HEADS):
        logits(0, 0, j, 0)
        logits(0, 0, j, 1)

    def attend(c, slot, tie_seen, c_next, need):
        k0 = pl.multiple_of(c * KC, KC)
        s = sc_ref[pl.ds(k0, KC), :]
        kpos = k0 + lax.broadcasted_iota(jnp.int32, (KC, 1), 0)
        eq = s == thr
        eqf = jnp.where(eq, 1.0, 0.0)
        incl = _dot(tril, eqf.astype(BF16))
        rank = tie_seen + incl - eqf
        sel = jnp.logical_and(jnp.logical_or(s > thr, jnp.logical_and(eq, rank < need)), kpos <= qpos)
        tie_seen = tie_seen + incl[KC - 1:KC, :]
        bias = jnp.where(sel, 0.0, NEG_BIG).astype(BF16)
        for j in range(N_KV_HEADS):
            alphas = []
            for g in range(GROUP):
                sl = slice(g * QB, (g + 1) * QB)
                sg = st_ref[slot, j, :, sl] + bias
                m_old = m_ref[j, :, sl]
                mx = jnp.max(_fold_rows(sg, jnp.maximum, 16), axis=0, keepdims=True).astype(F32)
                m_new = jnp.maximum(m_old, mx)
                alphas.append(jnp.exp2(m_old - m_new))
                m_ref[j, :, sl] = m_new
                p_ref[j, :, sl] = jnp.exp2(sg - m_new.astype(BF16))
                if g % 2 == 1:
                    logits(c_next, 1 - slot, j, g // 2)
            v_aug = jnp.concatenate([vT_ref[c, j * HEAD_DIM:(j + 1) * HEAD_DIM, :], ones_rows], axis=0)
            acc_ref[j] = acc_ref[j] * jnp.concatenate(alphas, axis=1) + _dot(v_aug, p_ref[j])
        return tie_seen

    need = k_top - (n_staged - count_not(lambda s: s > thr))

    def p3(i, tie_seen):
        c = 2 * i
        tie_seen = attend(c, 0, tie_seen, c + 1, need)
        return attend(c + 1, 1, tie_seen, clamp_chunk(c + 2), need)

    lax.fori_loop(0, npair, p3, jnp.zeros((1, QB), F32))

    for j in range(N_KV_HEADS):
        o = acc_ref[j, 0:HEAD_DIM, :] / acc_ref[j, HEAD_DIM:HEAD_DIM + 1, :]
        for g in range(GROUP):
            hh = j * GROUP + g
            oT_ref[hh * HEAD_DIM:(hh + 1) * HEAD_DIM, :] = o[:, g * QB:(g + 1) * QB].astype(BF16)


def _dsa_core(qT, k, vT, iqT, ik, iwT):
    B, QW, S = qT.shape
    KC, QB = DSA_KC, DSA_QB
    k_top = min(TOPK_MAX, S // 4)
    nc = S // KC
    return pl.pallas_call(
        functools.partial(_dsa_core_kernel, k_top=k_top, n_chunks=nc),
        grid=(B, S // QB),
        in_specs=[
            pl.BlockSpec((None, IDX_HEADS * IDX_DIM, QB), lambda b, i: (b, 0, i)),
            pl.BlockSpec((None, IDX_HEADS, QB), lambda b, i: (b, 0, i)),
            pl.BlockSpec((None, QW, QB), lambda b, i: (b, 0, i)),
            pl.BlockSpec((None, S, LANE), lambda b, i: (b, 0, 0)),
            pl.BlockSpec((None, S, N_KV_HEADS * LANE), lambda b, i: (b, 0, 0)),
            pl.BlockSpec((None, nc, N_KV_HEADS * HEAD_DIM, KC), lambda b, i: (b, 0, 0, 0)),
        ],
        out_specs=pl.BlockSpec((None, QW, QB), lambda b, i: (b, 0, i)),
        out_shape=jax.ShapeDtypeStruct((B, QW, S), BF16),
        scratch_shapes=[
            pltpu.VMEM((S, QB), F32),
            pltpu.VMEM((N_KV_HEADS, LANE, GROUP * QB), BF16),
            pltpu.VMEM((LANE, IDX_HEADS * QB), BF16),
            pltpu.VMEM((2, KC, IDX_HEADS * QB), F32),
            pltpu.VMEM((2, N_KV_HEADS, KC, GROUP * QB), BF16),
            pltpu.VMEM((N_KV_HEADS, KC, GROUP * QB), BF16),
            pltpu.VMEM((N_KV_HEADS, 1, GROUP * QB), F32),
            pltpu.VMEM((N_KV_HEADS, HEAD_DIM + DSA_VPAD, GROUP * QB), F32),
        ],
        compiler_params=_cparams("parallel", "arbitrary"),
    )(iqT, iwT, qT, ik, k, vT)


def _out_T_kernel(xT_ref, w_ref, h_ref, gt_ref, o_ref):
    o_ref[...] = h_ref[...] + gt_ref[...] * _dot_tn(xT_ref[...], w_ref[...])


def _dsa_out(oT, w_out, h, mod, tm=512):
    B, S, D = h.shape
    QW = oT.shape[1]
    return pl.pallas_call(
        _out_T_kernel,
        grid=(B, S // tm),
        in_specs=[
            pl.BlockSpec((None, QW, tm), lambda b, i: (b, 0, i)),
            pl.BlockSpec((QW, D), lambda b, i: (0, 0)),
            pl.BlockSpec((None, tm, D), lambda b, i: (b, i, 0)),
            pl.BlockSpec((None, None, 1, D), lambda b, i: (b, 5, 0, 0)),
        ],
        out_specs=pl.BlockSpec((None, tm, D), lambda b, i: (b, i, 0)),
        out_shape=jax.ShapeDtypeStruct((B, S, D), F32),
        compiler_params=_cparams("parallel", "parallel"),
    )(oT, w_out.astype(BF16), h, mod)


def _dsa_mixer(h, mod, w_in, q_norm, k_norm, ik_norm, w_out):
    qT, k, vT, iqT, ik, iwT = _dsa_in(h, mod, w_in, q_norm, k_norm, ik_norm)
    oT = _dsa_core(qT, k, vT, iqT, ik, iwT)
    return _dsa_out(oT, w_out, h, mod)


RWKV_HEAD = 64
GN_EPS = 64e-5
RW_PACK = 4
RW_LANES = RW_PACK * RWKV_HEAD
RW_C = 64
RW_TB = 256


def _split3(x):
    hi = x.astype(BF16)
    r1 = x - hi.astype(F32)
    mid = r1.astype(BF16)
    lo = (r1 - mid.astype(F32)).astype(BF16)
    return hi, mid, lo


def _gsum(x, g_ref, gt_ref):
    hi = x.astype(BF16)
    lo = (x - hi.astype(F32)).astype(BF16)
    s = _dot(jnp.concatenate([hi, lo], axis=1), g_ref[...])
    return _dot(jnp.concatenate(_split3(s), axis=1), gt_ref[...])


def _rwkv_in_kernel(h_ref, hp_ref, sh_ref, sc_ref, mu_ref, wrkv_ref, w0_ref, w1_ref, w2_ref,
                    a0_ref, a1_ref, a2_ref, g1_ref, g2_ref, kk_ref, ka_ref, gm_ref, gmt_ref,
                    r_ref, lw_ref, k_ref, v_ref, kkn_ref, b_ref, g_ref):
    sh, sc = sh_ref[...], sc_ref[...]
    u = _modulate(h_ref[...], sh, sc)
    tm = u.shape[0]
    prev = _modulate(hp_ref[...], sh, sc)[7:8, :]
    prev = jnp.where(pl.program_id(1) == 0, 0.0, prev)
    row = lax.broadcasted_iota(jnp.int32, (tm, 1), 0)
    xx = jnp.where(row == 0, prev, pltpu.roll(u, 1, 0)) - u
    mix = lambda i: (u + xx * mu_ref[i:i + 1, :]).astype(BF16)
    wl = w0_ref[...] + _dot(jnp.tanh(_dot(mix(3), w1_ref[...])).astype(BF16), w2_ref[...])
    a_lin = _dot(_dot(mix(4), a1_ref[...]).astype(BF16), a2_ref[...])
    g_hid = _dot(mix(5), g1_ref[...])
    k = _dot(mix(1), wrkv_ref[1])
    nwl = -wl
    w_log = -(jnp.maximum(nwl, 0.0) + jnp.log(1.0 + jnp.exp(-jnp.abs(nwl)))) - 0.5
    lw_ref[...] = -jnp.exp(w_log)
    r_ref[...] = _dot(mix(0), wrkv_ref[0])
    a = jax.nn.sigmoid(a0_ref[...] + a_lin)
    g_ref[...] = _dot(jax.nn.sigmoid(g_hid).astype(BF16), g2_ref[...])
    kk = k * kk_ref[...]
    nrm = jnp.sqrt(_gsum(kk * kk, gm_ref, gmt_ref))
    kk = kk / jnp.maximum(nrm, 1e-12)
    k_ref[...] = k * (1.0 + (a - 1.0) * ka_ref[...])
    kkn_ref[...] = kk
    b_ref[...] = kk * a
    v_ref[...] = _dot(mix(2), wrkv_ref[2])


def _head_indicator(D):
    gm = (np.arange(D)[:, None] // RWKV_HEAD == np.arange(D // RWKV_HEAD)[None, :]).astype(np.float32)
    return jnp.asarray(np.concatenate([gm, gm], axis=0), BF16), jnp.asarray(np.concatenate([gm.T] * 3, axis=0), BF16)


def _rwkv_in(h, mod, mu, w_rkv, w0, w1, w2, a0, a1, a2, g1, g2, k_k, k_a, tm=256):
    B, S, D = h.shape
    gm, gmt = _head_indicator(D)
    row = lambda a: a.astype(F32).reshape(1, D)
    full = lambda a: pl.BlockSpec(a.shape, lambda b, i: (0,) * a.ndim)
    mspec = lambda k: pl.BlockSpec((None, None, 1, D), lambda b, i: (b, k, 0, 0))
    tile = pl.BlockSpec((None, tm, D), lambda b, i: (b, i, 0))
    params = [mu.astype(F32), w_rkv.astype(BF16), row(w0), w1.astype(BF16), w2.astype(BF16),
              row(a0), a1.astype(BF16), a2.astype(BF16), g1.astype(BF16), g2.astype(BF16),
              row(k_k), row(k_a), gm, gmt]
    return pl.pallas_call(
        _rwkv_in_kernel,
        grid=(B, S // tm),
        in_specs=[tile,
                  pl.BlockSpec((None, 8, D), lambda b, i: (b, jnp.maximum(i * (tm // 8) - 1, 0), 0)),
                  mspec(3), mspec(4)] + [full(p) for p in params],
        out_specs=[tile] * 7,
        out_shape=[jax.ShapeDtypeStruct((B, S, D), F32)] * 7,
        compiler_params=_cparams("parallel", "parallel"),
    )(h, h, mod, mod, *params)


def _rwkv_scan_kernel(r_ref, lw_ref, k_ref, v_ref, kk_ref, b_ref, y_ref, zt_ref):
    C, P, L = RW_C, RW_PACK, RW_LANES
    N = P * C
    assert C == RWKV_HEAD
    n_packs = r_ref.shape[1] // L
    n_chunks = r_ref.shape[0] // C
    units = [(p, c) for p in range(n_packs) for c in range(n_chunks)]

    @pl.when(pl.program_id(1) == 0)
    def _():
        zt_ref[...] = jnp.zeros(zt_ref.shape, F32)

    lane = lax.broadcasted_iota(jnp.int32, (1, L), 1)
    hmask = [lane // C == hd for hd in range(P)]
    bf = lambda x: x.astype(BF16)
    stack = lambda x: jnp.concatenate([jnp.where(hmask[hd], x, jnp.zeros_like(x)) for hd in range(P)], axis=0)
    t_i = lax.broadcasted_iota(jnp.int32, (C, N), 0)
    i_i = lax.broadcasted_iota(jnp.int32, (C, N), 1) % C
    low_strict = t_i > i_i
    low_incl = t_i >= i_i
    eye_w = jnp.where(t_i == i_i, 1.0, 0.0)
    bd_mask = (lax.broadcasted_iota(jnp.int32, (L, L), 0) // C) == (lax.broadcasted_iota(jnp.int32, (L, L), 1) // C)
    tri = jnp.where(lax.broadcasted_iota(jnp.int32, (C, C), 0) >= lax.broadcasted_iota(jnp.int32, (C, C), 1),
                    1.0, 0.0).astype(BF16)
    tile = lambda ref, u: ref[u[1] * C:(u[1] + 1) * C, u[0] * L:(u[0] + 1) * L]

    cw = {}
    tri3 = jnp.concatenate([tri, tri, tri], axis=1)
    for u in units:
        cw[u] = _dot(tri3, jnp.concatenate(_split3(tile(lw_ref, u)), axis=0))
    al, rt, bw, kw, v, vs, aa, w_end = {}, {}, {}, {}, {}, {}, {}, {}
    for u in units:
        lw = tile(lw_ref, u)
        w_incl = jnp.exp(cw[u])
        w_excl = jnp.exp(cw[u] - lw)
        w_inv = jnp.exp(-cw[u])
        w_end[u] = w_incl[C - 1:C, :]
        v[u] = bf(tile(v_ref, u))
        vs[u] = stack(v[u])
        al[u] = bf(-tile(kk_ref, u) * w_excl)
        rt[u] = tile(r_ref, u) * w_incl
        bh = tile(b_ref, u) * w_inv
        kh = tile(k_ref, u) * w_inv
        bw[u] = bf(bh * w_end[u])
        kw[u] = bf(kh * w_end[u])
        aa[u] = _dot_nt(jnp.concatenate([al[u], bf(rt[u])], axis=0),
                        jnp.concatenate([stack(bf(bh)), stack(bf(kh))], axis=0))
    a_ak, a_r, pw, t = {}, {}, {}, {}
    for u in units:
        a_ab = jnp.where(low_strict, aa[u][:C, :N], 0.0)
        a_ak[u] = bf(jnp.where(low_strict, aa[u][:C, N:], 0.0))
        a_r[u] = bf(jnp.concatenate([jnp.where(low_incl, aa[u][C:, :N], 0.0),
                                     jnp.where(low_incl, aa[u][C:, N:], 0.0)], axis=1))
        pw[u] = a_ab
        t[u] = eye_w + a_ab
    n_fac = int(np.log2(C))
    akv = {}
    for u in units:
        pwb = bf(pw[u])
        pw[u] = _dot(pwb, stack(pwb))
        akv[u] = _dot(a_ak[u], vs[u])
    for s in range(2, n_fac + 1):
        for u in units:
            pwb = bf(pw[u])
            rhs = stack(pwb)
            if s < n_fac:
                both = _dot(jnp.concatenate([bf(t[u]), pwb], axis=0), rhs)
                t[u] = t[u] + both[:C]
                pw[u] = both[C:]
            else:
                t[u] = t[u] + _dot(bf(t[u]), rhs)
    pm, q = {}, {}
    for u in units:
        tb = bf(t[u])
        pm[u] = bf(_dot(tb, stack(al[u])))
        q[u] = bf(_dot(tb, stack(bf(akv[u]))))
    pr, y0, bk = {}, {}, {}
    for u in units:
        rp = bf(rt[u] + _dot(a_r[u][:, :N], stack(pm[u])))
        pr[u] = jnp.concatenate([pm[u], rp], axis=0)
        y0[u] = _dot(a_r[u], jnp.concatenate([stack(q[u]), vs[u]], axis=0))
        bk[u] = jnp.concatenate([bw[u], kw[u]], axis=0)
    zt = [zt_ref[p] for p in range(n_packs)]
    for c in range(n_chunks):
        for p in range(n_packs):
            u = (p, c)
            uy = _dot_nt(pr[u], bf(zt[p]))
            y_ref[c * C:(c + 1) * C, p * L:(p + 1) * L] = uy[C:] + y0[u]
            upd = _dot_tn(jnp.concatenate([bf(uy[:C] + q[u]), v[u]], axis=0), bk[u])
            zt[p] = zt[p] * w_end[u] + jnp.where(bd_mask, upd, 0.0)
    for p in range(n_packs):
        zt_ref[p] = zt[p]


def _rwkv_scan(r, lw, k, v, kk, b):
    B, S, D = r.shape
    tile = pl.BlockSpec((None, RW_TB, D), lambda bb, t: (bb, t, 0))
    return pl.pallas_call(
        _rwkv_scan_kernel,
        grid=(B, S // RW_TB),
        in_specs=[tile] * 6,
        out_specs=tile,
        out_shape=jax.ShapeDtypeStruct((B, S, D), F32),
        scratch_shapes=[pltpu.VMEM((D // RW_LANES, RW_LANES, RW_LANES), F32)],
        compiler_params=_cparams("parallel", "arbitrary"),
    )(r, lw, k, v, kk, b)


def _rwkv_out_kernel(y_ref, r_ref, k_ref, v_ref, g_ref, lnw_ref, lnb_ref, rk_ref, gm_ref, gmt_ref,
                     w_ref, h_ref, gt_ref, o_ref):
    y = y_ref[...]
    inv_n = 1.0 / RWKV_HEAD
    yc = y - _gsum(y, gm_ref, gmt_ref) * inv_n
    var = _gsum(yc * yc, gm_ref, gmt_ref) * inv_n
    yn = yc * lax.rsqrt(var + GN_EPS) * lnw_ref[...] + lnb_ref[...]
    v = v_ref[...]
    bonus = _gsum(r_ref[...] * k_ref[...] * rk_ref[...], gm_ref, gmt_ref) * v
    out = ((yn + bonus) * g_ref[...]).astype(BF16)
    o_ref[...] = h_ref[...] + gt_ref[...] * _dot(out, w_ref[...])


def _rwkv_out(y, r, k, v, g, ln_w, ln_b, r_k, w_out, h, mod, tm=256):
    B, S, D = h.shape
    gm, gmt = _head_indicator(D)
    row = lambda a: a.astype(F32).reshape(1, D)
    full = lambda a: pl.BlockSpec(a.shape, lambda b, i: (0,) * a.ndim)
    tile = pl.BlockSpec((None, tm, D), lambda b, i: (b, i, 0))
    params = [row(ln_w), row(ln_b), row(r_k), gm, gmt, w_out.astype(BF16)]
    return pl.pallas_call(
        _rwkv_out_kernel,
        grid=(B, S // tm),
        in_specs=[tile] * 5 + [full(p) for p in params] + [tile, pl.BlockSpec((None, None, 1, D), lambda b, i: (b, 5, 0, 0))],
        out_specs=tile,
        out_shape=jax.ShapeDtypeStruct((B, S, D), F32),
        compiler_params=_cparams("parallel", "parallel"),
    )(y, r, k, v, g, *params, h, mod)


def _rwkv_mixer(h, mod, mu, w_rkv, w0, w1, w2, a0, a1, a2, g1, g2, k_k, k_a, r_k, ln_w, ln_b, w_out):
    r, lw, k, v, kk, b, g = _rwkv_in(h, mod, mu, w_rkv, w0, w1, w2, a0, a1, a2, g1, g2, k_k, k_a)
    y = _rwkv_scan(r, lw, k, v, kk, b)
    return _rwkv_out(y, r, k, v, g, ln_w, ln_b, r_k, w_out, h, mod)


def kernel(x, c, ada_w, ada_b, ffn_w_gu, ffn_w_down, dsa_w_in, dsa_q_norm, dsa_k_norm, dsa_ik_norm, dsa_w_out, rwkv_mu, rwkv_w_rkv, rwkv_w0, rwkv_w1, rwkv_w2, rwkv_a0, rwkv_a1, rwkv_a2, rwkv_g1, rwkv_g2, rwkv_k_k, rwkv_k_a, rwkv_r_k, rwkv_ln_w, rwkv_ln_b, rwkv_w_out):
    B, S, D = x.shape
    depth = ada_w.shape[0]
    mods = _ada_mod(c, ada_w, ada_b)
    h = x
    for i in range(depth):
        mod = mods[i].reshape(B, N_MOD, 1, D)
        j = i // 2
        h = _ffn(h, mod, 0, ffn_w_gu[i, 0].astype(BF16), ffn_w_down[i, 0].astype(BF16))
        if i % 2 == 0:
            h = _dsa_mixer(h, mod, dsa_w_in[j], dsa_q_norm[j], dsa_k_norm[j], dsa_ik_norm[j], dsa_w_out[j])
        else:
            h = _rwkv_mixer(h, mod, rwkv_mu[j], rwkv_w_rkv[j], rwkv_w0[j], rwkv_w1[j], rwkv_w2[j], rwkv_a0[j],
                            rwkv_a1[j], rwkv_a2[j], rwkv_g1[j], rwkv_g2[j], rwkv_k_k[j], rwkv_k_a[j],
                            rwkv_r_k[j], rwkv_ln_w[j], rwkv_ln_b[j], rwkv_w_out[j])
        h = _ffn(h, mod, 6, ffn_w_gu[i, 1].astype(BF16), ffn_w_down[i, 1].astype(BF16))
    return h
```

```python
import functools

import jax
import jax.numpy as jnp
import numpy as np
from jax import lax
from jax.experimental import pallas as pl
from jax.experimental.pallas import tpu as pltpu

F32 = jnp.float32
BF16 = jnp.bfloat16

RMS_EPS = 1e-6
N_MOD = 9
VMEM_LIMIT_BYTES = 56 * 1024 * 1024

N_HEADS = 16
HEAD_DIM = 64
N_KV_HEADS = 4
GROUP = N_HEADS // N_KV_HEADS
IDX_HEADS = 8
IDX_DIM = 64
TOPK_MAX = 256
LANE = 128
LOG2E = 1.4426950408889634


def _cparams(*sem):
    return pltpu.CompilerParams(dimension_semantics=sem, vmem_limit_bytes=VMEM_LIMIT_BYTES)


def _dot(a, b):
    return jnp.dot(a, b, preferred_element_type=F32)


def _dot_nt(a, b):
    return lax.dot_general(a, b, (((1,), (1,)), ((), ())), preferred_element_type=F32)


def _dot_tn(a, b):
    return lax.dot_general(a, b, (((0,), (0,)), ((), ())), preferred_element_type=F32)


def _modulate(h, sh, sc):
    ms = jnp.mean(h * h, axis=-1, keepdims=True)
    return h * lax.rsqrt(ms + RMS_EPS) * (1.0 + sc) + sh


def _ada_kernel(c_ref, w_ref, b_ref, o_ref):
    c = c_ref[...]
    ca = (c * jax.nn.sigmoid(c)).astype(BF16)
    o_ref[...] = _dot(ca, w_ref[...].astype(BF16)) + b_ref[...]


def _ada_mod(c, ada_w, ada_b):
    L, D, N = ada_w.shape
    B = c.shape[0]
    tn = 1536
    return pl.pallas_call(
        _ada_kernel,
        grid=(L, N // tn),
        in_specs=[
            pl.BlockSpec((B, D), lambda l, j: (0, 0)),
            pl.BlockSpec((None, D, tn), lambda l, j: (l, 0, j)),
            pl.BlockSpec((None, 1, tn), lambda l, j: (l, 0, j)),
        ],
        out_specs=pl.BlockSpec((None, B, tn), lambda l, j: (l, 0, j)),
        out_shape=jax.ShapeDtypeStruct((L, B, N), F32),
        compiler_params=_cparams("parallel", "parallel"),
    )(c, ada_w, ada_b.reshape(L, 1, N))


FF_CHUNK = 256


def _ffn_kernel(h_ref, sh_ref, sc_ref, gt_ref, wgu_ref, wd_ref, o_ref, *, d_ff):
    h = h_ref[...]
    u = _modulate(h, sh_ref[...], sc_ref[...]).astype(BF16)
    acc = None
    for c in range(d_ff // FF_CHUNK):
        lo = c * FF_CHUNK
        g = _dot(u, wgu_ref[:, lo:lo + FF_CHUNK])
        up = _dot(u, wgu_ref[:, d_ff + lo:d_ff + lo + FF_CHUNK])
        a = (g * jax.nn.sigmoid(g) * up).astype(BF16)
        part = _dot(a, wd_ref[lo:lo + FF_CHUNK, :])
        acc = part if acc is None else acc + part
    o_ref[...] = h + (0.5 * gt_ref[...]) * acc


def _ffn(h, mod, k0, w_gu, w_down, tm=512):
    B, S, D = h.shape
    d_ff = w_down.shape[0]
    mspec = lambda k: pl.BlockSpec((None, None, 1, D), lambda b, i: (b, k, 0, 0))
    return pl.pallas_call(
        functools.partial(_ffn_kernel, d_ff=d_ff),
        grid=(B, S // tm),
        in_specs=[
            pl.BlockSpec((None, tm, D), lambda b, i: (b, i, 0)),
            mspec(k0), mspec(k0 + 1), mspec(k0 + 2),
            pl.BlockSpec((D, 2 * d_ff), lambda b, i: (0, 0), pipeline_mode=pl.Buffered(1)),
            pl.BlockSpec((d_ff, D), lambda b, i: (0, 0), pipeline_mode=pl.Buffered(1)),
        ],
        out_specs=pl.BlockSpec((None, tm, D), lambda b, i: (b, i, 0)),
        out_shape=jax.ShapeDtypeStruct((B, S, D), F32),
        compiler_params=_cparams("parallel", "parallel"),
    )(h, mod, mod, mod, w_gu, w_down)


def _dsa_in_kernel(h_ref, sh_ref, sc_ref, wqT_ref, wk_ref, wvT_ref, wiqT_ref, wik_ref, wiwT_ref,
                   qn_ref, kn_ref, ikn_ref,
                   qT_ref, k_ref, vT_ref, iqT_ref, ik_ref, iwT_ref):
    u = _modulate(h_ref[...], sh_ref[...], sc_ref[...]).astype(BF16)
    tm = u.shape[0]
    qT = _dot_nt(wqT_ref[...], u).reshape(N_HEADS, HEAD_DIM, tm)
    ms = jnp.mean(qT * qT, axis=1, keepdims=True)
    qT = qT * lax.rsqrt(ms + RMS_EPS) * qn_ref[...][None] * (HEAD_DIM ** -0.5 * LOG2E)
    qT_ref[...] = qT.reshape(N_HEADS * HEAD_DIM, tm).astype(BF16)
    k = _dot(u, wk_ref[...])
    kn = kn_ref[...]
    for j in range(N_KV_HEADS):
        kj = k[:, j * LANE:(j + 1) * LANE]
        ms = jnp.sum(kj * kj, axis=-1, keepdims=True) * (1.0 / HEAD_DIM)
        k_ref[:, j * LANE:(j + 1) * LANE] = (kj * lax.rsqrt(ms + RMS_EPS) * kn).astype(BF16)
    vT_ref[...] = _dot_nt(wvT_ref[...], u).astype(BF16)
    iqT_ref[...] = _dot_nt(wiqT_ref[...], u).astype(BF16)
    ik = _dot(u, wik_ref[...])
    ms = jnp.sum(ik * ik, axis=-1, keepdims=True) * (1.0 / IDX_DIM)
    ik_ref[...] = (ik * lax.rsqrt(ms + RMS_EPS) * ikn_ref[...]).astype(BF16)
    iwT_ref[...] = _dot_nt(wiwT_ref[...], u) * (IDX_HEADS ** -0.5 * IDX_DIM ** -0.5)


DSA_KC = 256
DSA_QB = 256


def _dsa_in(h, mod, w_in, q_norm, k_norm, ik_norm, tm=DSA_KC):
    B, S, D = h.shape
    QW, KW = N_HEADS * HEAD_DIM, N_KV_HEADS * HEAD_DIM
    IQW = IDX_HEADS * IDX_DIM
    o1, o2, o3, o4, o5 = QW, QW + KW, QW + 2 * KW, QW + 2 * KW + IQW, QW + 2 * KW + IQW + IDX_DIM
    wb = w_in.astype(BF16)
    wqT = wb[:, :o1].T
    wk = jnp.pad(wb[:, o1:o2].reshape(D, N_KV_HEADS, HEAD_DIM), ((0, 0), (0, 0), (0, LANE - HEAD_DIM))).reshape(D, N_KV_HEADS * LANE)
    wvT = wb[:, o2:o3].T
    wiqT = wb[:, o3:o4].T
    wik = jnp.pad(wb[:, o4:o5], ((0, 0), (0, LANE - IDX_DIM)))
    wiwT = wb[:, o5:].T
    qn = q_norm.astype(F32).reshape(HEAD_DIM, 1)
    kn = jnp.pad(k_norm.astype(F32), (0, LANE - HEAD_DIM)).reshape(1, LANE)
    ikn = jnp.pad(ik_norm.astype(F32), (0, LANE - IDX_DIM)).reshape(1, LANE)
    mspec = lambda k: pl.BlockSpec((None, None, 1, D), lambda b, i: (b, k, 0, 0))
    full = lambda a: pl.BlockSpec(a.shape, lambda b, i: (0,) * a.ndim)
    nc = S // tm
    return pl.pallas_call(
        _dsa_in_kernel,
        grid=(B, nc),
        in_specs=[pl.BlockSpec((None, tm, D), lambda b, i: (b, i, 0)), mspec(3), mspec(4),
                  full(wqT), full(wk), full(wvT), full(wiqT), full(wik), full(wiwT),
                  full(qn), full(kn), full(ikn)],
        out_specs=[
            pl.BlockSpec((None, QW, tm), lambda b, i: (b, 0, i)),
            pl.BlockSpec((None, tm, N_KV_HEADS * LANE), lambda b, i: (b, i, 0)),
            pl.BlockSpec((None, None, KW, tm), lambda b, i: (b, i, 0, 0)),
            pl.BlockSpec((None, IQW, tm), lambda b, i: (b, 0, i)),
            pl.BlockSpec((None, tm, LANE), lambda b, i: (b, i, 0)),
            pl.BlockSpec((None, IDX_HEADS, tm), lambda b, i: (b, 0, i)),
        ],
        out_shape=[
            jax.ShapeDtypeStruct((B, QW, S), BF16),
            jax.ShapeDtypeStruct((B, S, N_KV_HEADS * LANE), BF16),
            jax.ShapeDtypeStruct((B, nc, KW, tm), BF16),
            jax.ShapeDtypeStruct((B, IQW, S), BF16),
            jax.ShapeDtypeStruct((B, S, LANE), BF16),
            jax.ShapeDtypeStruct((B, IDX_HEADS, S), F32),
        ],
        compiler_params=_cparams("parallel", "parallel"),
    )(h, mod, mod, wqT, wk, wvT, wiqT, wik, wiwT, qn, kn, ikn)


NEG_BIG = -(2.0 ** 100)
DSA_VPAD = 16
N_CHAINS = 4


def _ukey_to_float(ukey):
    key = ukey ^ np.int32(-2 ** 31)
    bits = key ^ ((key >> 31) & np.int32(0x7FFFFFFF))
    return lax.bitcast_convert_type(bits, F32)


def _fold_rows(x, op, rows):
    while x.shape[0] > rows:
        half = x.shape[0] // 2
        x = op(x[:half], x[half:])
    return x


def _dsa_core_kernel(iqT_ref, iwT_ref, qT_ref, ik_ref, k_ref, vT_ref, oT_ref,
                     sc_ref, fl_ref, whi_ref, wlo_ref, qa_ref, iqa_ref, d_ref, st_ref, p_ref, m_ref, acc_ref,
                     *, k_top, n_chunks):
    KC, QB = DSA_KC, DSA_QB
    qi = pl.program_id(1)
    q0 = qi * QB
    nk = (q0 + QB + KC - 1) // KC
    qpos = q0 + lax.broadcasted_iota(jnp.int32, (1, QB), 1)
    zeros_h = jnp.zeros((LANE - HEAD_DIM, QB), BF16)

    for h in range(IDX_HEADS):
        iqa_ref[:, h * QB:(h + 1) * QB] = jnp.concatenate(
            [iqT_ref[h * IDX_DIM:(h + 1) * IDX_DIM, :], zeros_h], axis=0)
    for j in range(N_KV_HEADS):
        for g in range(GROUP):
            hh = j * GROUP + g
            qa_ref[j, :, g * QB:(g + 1) * QB] = jnp.concatenate(
                [qT_ref[hh * HEAD_DIM:(hh + 1) * HEAD_DIM, :], zeros_h], axis=0)
    iw = iwT_ref[...]

    npair = (nk + 1) // 2

    def clamp_chunk(c):
        return jnp.minimum(c, n_chunks - 1)

    def idx_dot(c, slot):
        k0 = pl.multiple_of(c * KC, KC)
        d_ref[slot] = _dot(ik_ref[pl.ds(k0, KC), :], iqa_ref[...])

    def idx_score(c, slot):
        k0 = pl.multiple_of(c * KC, KC)
        s = jnp.maximum(d_ref[slot, :, 0:QB], 0.0) * iw[0:1, :]
        for h in range(1, IDX_HEADS):
            s = s + jnp.maximum(d_ref[slot, :, h * QB:(h + 1) * QB], 0.0) * iw[h:h + 1, :]
        kpos = k0 + lax.broadcasted_iota(jnp.int32, (KC, 1), 0)
        s = jnp.where(kpos <= qpos, s, -jnp.inf)
        sc_ref[pl.ds(k0, KC), :] = s
        r = s.astype(BF16)
        rf = r.astype(F32)
        toward0 = (rf * (1.0 - 1.25 * 2.0 ** -8)).astype(BF16)
        fl_ref[pl.ds(k0, KC), :] = jnp.where(jnp.abs(rf) > jnp.abs(s), toward0, r)

    idx_dot(0, 0)

    def p1(i, carry):
        c = 2 * i
        idx_dot(c + 1, 1)
        idx_score(c, 0)
        idx_dot(clamp_chunk(c + 2), 0)
        idx_score(c + 1, 1)
        return carry

    lax.fori_loop(0, npair, p1, 0)

    def count_not(pred):
        def body(i, accs):
            accs = list(accs)
            for half in range(2):
                k0 = pl.multiple_of((2 * i + half) * KC, KC)
                ind = jnp.where(pred(sc_ref[pl.ds(k0, KC), :]), 0.0, 1.0)
                for r in range(KC // 8):
                    accs[r % N_CHAINS] = accs[r % N_CHAINS] + ind[r * 8:(r + 1) * 8]
            return tuple(accs)
        accs = lax.fori_loop(0, npair, body, tuple(jnp.zeros((8, QB), F32) for _ in range(N_CHAINS)))
        acc = (accs[0] + accs[1]) + (accs[2] + accs[3])
        return jnp.sum(acc, axis=0, keepdims=True)

    n_staged = (npair * (2 * KC)).astype(F32)

    def count_ge16(ref, t):
        tb = t.astype(BF16)
        one, zero = jnp.ones((), BF16), jnp.zeros((), BF16)

        def body(i, accs):
            accs = list(accs)
            for half in range(2):
                k0 = pl.multiple_of((2 * i + half) * KC, KC)
                ind = jnp.where(ref[pl.ds(k0, KC), :] < tb, zero, one)
                for r in range(KC // 16):
                    accs[r % N_CHAINS] = accs[r % N_CHAINS] + ind[r * 16:(r + 1) * 16]
            return tuple(accs)
        accs = lax.fori_loop(0, npair, body, tuple(jnp.zeros((16, QB), BF16) for _ in range(N_CHAINS)))
        acc = (accs[0].astype(F32) + accs[1].astype(F32)) + (accs[2].astype(F32) + accs[3].astype(F32))
        return jnp.sum(acc, axis=0, keepdims=True)

    def digit_search(ref, n_bits, k_need, to_float):
        d = jnp.zeros((1, QB), F32)
        for bit in range(n_bits - 1, -1, -1):
            cand = d + float(1 << bit)
            d = jnp.where(count_ge16(ref, to_float(cand)) >= k_need, cand, d)
        return d

    def pat16(d):
        return _ukey_to_float(d.astype(jnp.int32) << 16)

    def val16(d):
        di = d.astype(jnp.int32)
        return _ukey_to_float(jnp.where(di >= 0x8000, di << 16, (di << 16) | np.int32(0xFFFF)))

    key_hi = digit_search(fl_ref, 16, k_top, val16)
    h_lo = pat16(key_hi)
    no_thr = val16(key_hi) == -jnp.inf
    e_biased = (lax.bitcast_convert_type(h_lo, jnp.int32) >> 23) & 0xFF
    tiny = e_biased < 64
    shift = jnp.where(tiny, 60, 0)
    pre = jnp.where(tiny, np.float32(2.0 ** 60), np.float32(1.0))
    unpre = jnp.where(tiny, np.float32(2.0 ** -60), np.float32(1.0))
    to_int = lax.bitcast_convert_type((277 - e_biased - shift) << 23, F32)
    from_int = lax.bitcast_convert_type((e_biased - 23 + shift) << 23, F32)

    def split(i, carry):
        for half in range(2):
            k0 = pl.multiple_of((2 * i + half) * KC, KC)
            w = jnp.clip(((sc_ref[pl.ds(k0, KC), :] - h_lo) * pre) * to_int, -256.0, 65536.0)
            w_hi = jnp.floor(w * (1.0 / 256.0))
            whi_ref[pl.ds(k0, KC), :] = w_hi.astype(BF16)
            wlo_ref[pl.ds(k0, KC), :] = (w - 256.0 * w_hi).astype(BF16)
        return carry
    lax.fori_loop(0, npair, split, 0)

    ident = lambda d: d
    d_mid = digit_search(whi_ref, 8, k_top, ident)
    k_low = k_top - count_ge16(whi_ref, d_mid + 1.0)
    d_mid_b = d_mid.astype(BF16)

    def keep_low(i, carry):
        for half in range(2):
            k0 = pl.multiple_of((2 * i + half) * KC, KC)
            whi_ref[pl.ds(k0, KC), :] = jnp.where(whi_ref[pl.ds(k0, KC), :] == d_mid_b,
                                                  wlo_ref[pl.ds(k0, KC), :], -jnp.ones((), BF16))
        return carry
    lax.fori_loop(0, npair, keep_low, 0)
    d_low = digit_search(whi_ref, 8, k_low, ident)
    offset = ((d_mid * 256.0 + d_low) * from_int) * unpre
    thr = jnp.where(no_thr, -jnp.inf, h_lo + offset)

    m_ref[...] = jnp.full(m_ref.shape, NEG_BIG, F32)
    acc_ref[...] = jnp.zeros(acc_ref.shape, F32)
    r_i = lax.broadcasted_iota(jnp.int32, (KC, KC), 0)
    c_i = lax.broadcasted_iota(jnp.int32, (KC, KC), 1)
    tril = jnp.where(c_i <= r_i, 1.0, 0.0).astype(BF16)
    ones_rows = jnp.ones((DSA_VPAD, KC), BF16)

    HALF = GROUP * QB // 2

    def logits(c, slot, j, half):
        k0 = pl.multiple_of(c * KC, KC)
        cols = slice(half * HALF, (half + 1) * HALF)
        st_ref[slot, j, :, cols] = _dot(k_ref[pl.ds(k0, KC), j * LANE:(j + 1) * LANE],
                                        qa_ref[j, :, cols]).astype(BF16)

    for j in range(N_KV_HEADS):
        logits(0, 0, j, 0)
        logits(0, 0, j, 1)

    def attend(c, slot, tie_seen, c_next, need):
        k0 = pl.multiple_of(c * KC, KC)
        s = sc_ref[pl.ds(k0, KC), :]
        kpos = k0 + lax.broadcasted_iota(jnp.int32, (KC, 1), 0)
        eq = s == thr
        eqf = jnp.where(eq, 1.0, 0.0)
        incl = _dot(tril, eqf.astype(BF16))
        rank = tie_seen + incl - eqf
        sel = jnp.logical_and(jnp.logical_or(s > thr, jnp.logical_and(eq, rank < need)), kpos <= qpos)
        tie_seen = tie_seen + incl[KC - 1:KC, :]
        bias = jnp.where(sel, 0.0, NEG_BIG).astype(BF16)
        for j in range(N_KV_HEADS):
            alphas = []
            for g in range(GROUP):
                sl = slice(g * QB, (g + 1) * QB)
                sg = st_ref[slot, j, :, sl] + bias
                m_old = m_ref[j, :, sl]
                mx = jnp.max(_fold_rows(sg, jnp.maximum, 16), axis=0, keepdims=True).astype(F32)
                m_new = jnp.maximum(m_old, mx)
                alphas.append(jnp.exp2(m_old - m_new))
                m_ref[j, :, sl] = m_new
                p_ref[j, :, sl] = jnp.exp2(sg - m_new.astype(BF16))
                if g % 2 == 1:
                    logits(c_next, 1 - slot, j, g // 2)
            v_aug = jnp.concatenate([vT_ref[c, j * HEAD_DIM:(j + 1) * HEAD_DIM, :], ones_rows], axis=0)
            acc_ref[j] = acc_ref[j] * jnp.concatenate(alphas, axis=1) + _dot(v_aug, p_ref[j])
        return tie_seen

    need = k_top - (n_staged - count_not(lambda s: s > thr))

    def p3(i, tie_seen):
        c = 2 * i
        tie_seen = attend(c, 0, tie_seen, c + 1, need)
        return attend(c + 1, 1, tie_seen, clamp_chunk(c + 2), need)

    lax.fori_loop(0, npair, p3, jnp.zeros((1, QB), F32))

    for j in range(N_KV_HEADS):
        o = acc_ref[j, 0:HEAD_DIM, :] / acc_ref[j, HEAD_DIM:HEAD_DIM + 1, :]
        for g in range(GROUP):
            hh = j * GROUP + g
            oT_ref[hh * HEAD_DIM:(hh + 1) * HEAD_DIM, :] = o[:, g * QB:(g + 1) * QB].astype(BF16)


def _dsa_core(qT, k, vT, iqT, ik, iwT):
    B, QW, S = qT.shape
    KC, QB = DSA_KC, DSA_QB
    k_top = min(TOPK_MAX, S // 4)
    nc = S // KC
    return pl.pallas_call(
        functools.partial(_dsa_core_kernel, k_top=k_top, n_chunks=nc),
        grid=(B, S // QB),
        in_specs=[
            pl.BlockSpec((None, IDX_HEADS * IDX_DIM, QB), lambda b, i: (b, 0, i)),
            pl.BlockSpec((None, IDX_HEADS, QB), lambda b, i: (b, 0, i)),
            pl.BlockSpec((None, QW, QB), lambda b, i: (b, 0, i)),
            pl.BlockSpec((None, S, LANE), lambda b, i: (b, 0, 0)),
            pl.BlockSpec((None, S, N_KV_HEADS * LANE), lambda b, i: (b, 0, 0)),
            pl.BlockSpec((None, nc, N_KV_HEADS * HEAD_DIM, KC), lambda b, i: (b, 0, 0, 0)),
        ],
        out_specs=pl.BlockSpec((None, QW, QB), lambda b, i: (b, 0, i)),
        out_shape=jax.ShapeDtypeStruct((B, QW, S), BF16),
        scratch_shapes=[
            pltpu.VMEM((S, QB), F32),
            pltpu.VMEM((S, QB), BF16),
            pltpu.VMEM((S, QB), BF16),
            pltpu.VMEM((S, QB), BF16),
            pltpu.VMEM((N_KV_HEADS, LANE, GROUP * QB), BF16),
            pltpu.VMEM((LANE, IDX_HEADS * QB), BF16),
            pltpu.VMEM((2, KC, IDX_HEADS * QB), F32),
            pltpu.VMEM((2, N_KV_HEADS, KC, GROUP * QB), BF16),
            pltpu.VMEM((N_KV_HEADS, KC, GROUP * QB), BF16),
            pltpu.VMEM((N_KV_HEADS, 1, GROUP * QB), F32),
            pltpu.VMEM((N_KV_HEADS, HEAD_DIM + DSA_VPAD, GROUP * QB), F32),
        ],
        compiler_params=_cparams("parallel", "arbitrary"),
    )(iqT, iwT, qT, ik, k, vT)


def _out_T_kernel(xT_ref, w_ref, h_ref, gt_ref, o_ref):
    o_ref[...] = h_ref[...] + gt_ref[...] * _dot_tn(xT_ref[...], w_ref[...])


def _dsa_out(oT, w_out, h, mod, tm=512):
    B, S, D = h.shape
    QW = oT.shape[1]
    return pl.pallas_call(
        _out_T_kernel,
        grid=(B, S // tm),
        in_specs=[
            pl.BlockSpec((None, QW, tm), lambda b, i: (b, 0, i)),
            pl.BlockSpec((QW, D), lambda b, i: (0, 0)),
            pl.BlockSpec((None, tm, D), lambda b, i: (b, i, 0)),
            pl.BlockSpec((None, None, 1, D), lambda b, i: (b, 5, 0, 0)),
        ],
        out_specs=pl.BlockSpec((None, tm, D), lambda b, i: (b, i, 0)),
        out_shape=jax.ShapeDtypeStruct((B, S, D), F32),
        compiler_params=_cparams("parallel", "parallel"),
    )(oT, w_out.astype(BF16), h, mod)


def _dsa_mixer(h, mod, w_in, q_norm, k_norm, ik_norm, w_out):
    qT, k, vT, iqT, ik, iwT = _dsa_in(h, mod, w_in, q_norm, k_norm, ik_norm)
    oT = _dsa_core(qT, k, vT, iqT, ik, iwT)
    return _dsa_out(oT, w_out, h, mod)


RWKV_HEAD = 64
GN_EPS = 64e-5
RW_PACK = 4
RW_LANES = RW_PACK * RWKV_HEAD
RW_C = 64
RW_TB = 256


def _split3(x):
    hi = x.astype(BF16)
    r1 = x - hi.astype(F32)
    mid = r1.astype(BF16)
    lo = (r1 - mid.astype(F32)).astype(BF16)
    return hi, mid, lo


def _gsum(x, g_ref, gt_ref):
    hi = x.astype(BF16)
    lo = (x - hi.astype(F32)).astype(BF16)
    s = _dot(jnp.concatenate([hi, lo], axis=1), g_ref[...])
    return _dot(jnp.concatenate(_split3(s), axis=1), gt_ref[...])


def _rwkv_in_kernel(h_ref, hp_ref, sh_ref, sc_ref, mu_ref, wrkv_ref, w0_ref, w1_ref, w2_ref,
                    a0_ref, a1_ref, a2_ref, g1_ref, g2_ref, kk_ref, ka_ref, gm_ref, gmt_ref,
                    r_ref, lw_ref, k_ref, v_ref, kkn_ref, b_ref, g_ref):
    sh, sc = sh_ref[...], sc_ref[...]
    u = _modulate(h_ref[...], sh, sc)
    tm = u.shape[0]
    prev = _modulate(hp_ref[...], sh, sc)[7:8, :]
    prev = jnp.where(pl.program_id(1) == 0, 0.0, prev)
    row = lax.broadcasted_iota(jnp.int32, (tm, 1), 0)
    xx = jnp.where(row == 0, prev, pltpu.roll(u, 1, 0)) - u
    mix = lambda i: (u + xx * mu_ref[i:i + 1, :]).astype(BF16)
    wl = w0_ref[...] + _dot(jnp.tanh(_dot(mix(3), w1_ref[...])).astype(BF16), w2_ref[...])
    a_lin = _dot(_dot(mix(4), a1_ref[...]).astype(BF16), a2_ref[...])
    g_hid = _dot(mix(5), g1_ref[...])
    k = _dot(mix(1), wrkv_ref[1])
    nwl = -wl
    w_log = -(jnp.maximum(nwl, 0.0) + jnp.log(1.0 + jnp.exp(-jnp.abs(nwl)))) - 0.5
    lw_ref[...] = -jnp.exp(w_log)
    r_ref[...] = _dot(mix(0), wrkv_ref[0])
    a = jax.nn.sigmoid(a0_ref[...] + a_lin)
    g_ref[...] = _dot(jax.nn.sigmoid(g_hid).astype(BF16), g2_ref[...])
    kk = k * kk_ref[...]
    nrm = jnp.sqrt(_gsum(kk * kk, gm_ref, gmt_ref))
    kk = kk / jnp.maximum(nrm, 1e-12)
    k_ref[...] = k * (1.0 + (a - 1.0) * ka_ref[...])
    kkn_ref[...] = kk
    b_ref[...] = kk * a
    v_ref[...] = _dot(mix(2), wrkv_ref[2])


def _head_indicator(D):
    gm = (np.arange(D)[:, None] // RWKV_HEAD == np.arange(D // RWKV_HEAD)[None, :]).astype(np.float32)
    return jnp.asarray(np.concatenate([gm, gm], axis=0), BF16), jnp.asarray(np.concatenate([gm.T] * 3, axis=0), BF16)


def _rwkv_in(h, mod, mu, w_rkv, w0, w1, w2, a0, a1, a2, g1, g2, k_k, k_a, tm=256):
    B, S, D = h.shape
    gm, gmt = _head_indicator(D)
    row = lambda a: a.astype(F32).reshape(1, D)
    full = lambda a: pl.BlockSpec(a.shape, lambda b, i: (0,) * a.ndim)
    mspec = lambda k: pl.BlockSpec((None, None, 1, D), lambda b, i: (b, k, 0, 0))
    tile = pl.BlockSpec((None, tm, D), lambda b, i: (b, i, 0))
    params = [mu.astype(F32), w_rkv.astype(BF16), row(w0), w1.astype(BF16), w2.astype(BF16),
              row(a0), a1.astype(BF16), a2.astype(BF16), g1.astype(BF16), g2.astype(BF16),
              row(k_k), row(k_a), gm, gmt]
    return pl.pallas_call(
        _rwkv_in_kernel,
        grid=(B, S // tm),
        in_specs=[tile,
                  pl.BlockSpec((None, 8, D), lambda b, i: (b, jnp.maximum(i * (tm // 8) - 1, 0), 0)),
                  mspec(3), mspec(4)] + [full(p) for p in params],
        out_specs=[tile] * 7,
        out_shape=[jax.ShapeDtypeStruct((B, S, D), F32)] * 7,
        compiler_params=_cparams("parallel", "parallel"),
    )(h, h, mod, mod, *params)


def _rwkv_scan_kernel(r_ref, lw_ref, k_ref, v_ref, kk_ref, b_ref, y_ref, zt_ref):
    C, P, L = RW_C, RW_PACK, RW_LANES
    N = P * C
    assert C == RWKV_HEAD
    n_packs = r_ref.shape[1] // L
    n_chunks = r_ref.shape[0] // C
    units = [(p, c) for p in range(n_packs) for c in range(n_chunks)]

    @pl.when(pl.program_id(1) == 0)
    def _():
        zt_ref[...] = jnp.zeros(zt_ref.shape, F32)

    lane = lax.broadcasted_iota(jnp.int32, (1, L), 1)
    hmask = [lane // C == hd for hd in range(P)]
    bf = lambda x: x.astype(BF16)
    stack = lambda x: jnp.concatenate([jnp.where(hmask[hd], x, jnp.zeros_like(x)) for hd in range(P)], axis=0)
    t_i = lax.broadcasted_iota(jnp.int32, (C, N), 0)
    i_i = lax.broadcasted_iota(jnp.int32, (C, N), 1) % C
    low_strict = t_i > i_i
    low_incl = t_i >= i_i
    eye_w = jnp.where(t_i == i_i, 1.0, 0.0)
    bd_mask = (lax.broadcasted_iota(jnp.int32, (L, L), 0) // C) == (lax.broadcasted_iota(jnp.int32, (L, L), 1) // C)
    tri = jnp.where(lax.broadcasted_iota(jnp.int32, (C, C), 0) >= lax.broadcasted_iota(jnp.int32, (C, C), 1),
                    1.0, 0.0).astype(BF16)
    tile = lambda ref, u: ref[u[1] * C:(u[1] + 1) * C, u[0] * L:(u[0] + 1) * L]

    cw = {}
    tri3 = jnp.concatenate([tri, tri, tri], axis=1)
    for u in units:
        cw[u] = _dot(tri3, jnp.concatenate(_split3(tile(lw_ref, u)), axis=0))
    al, rt, bw, kw, v, vs, aa, w_end = {}, {}, {}, {}, {}, {}, {}, {}
    for u in units:
        lw = tile(lw_ref, u)
        w_incl = jnp.exp(cw[u])
        w_excl = jnp.exp(cw[u] - lw)
        w_inv = jnp.exp(-cw[u])
        w_end[u] = w_incl[C - 1:C, :]
        v[u] = bf(tile(v_ref, u))
        vs[u] = stack(v[u])
        al[u] = bf(-tile(kk_ref, u) * w_excl)
        rt[u] = tile(r_ref, u) * w_incl
        bh = tile(b_ref, u) * w_inv
        kh = tile(k_ref, u) * w_inv
        bw[u] = bf(bh * w_end[u])
        kw[u] = bf(kh * w_end[u])
        aa[u] = _dot_nt(jnp.concatenate([al[u], bf(rt[u])], axis=0),
                        jnp.concatenate([stack(bf(bh)), stack(bf(kh))], axis=0))
    a_ak, a_r, pw, t = {}, {}, {}, {}
    for u in units:
        a_ab = jnp.where(low_strict, aa[u][:C, :N], 0.0)
        a_ak[u] = bf(jnp.where(low_strict, aa[u][:C, N:], 0.0))
        a_r[u] = bf(jnp.concatenate([jnp.where(low_incl, aa[u][C:, :N], 0.0),
                                     jnp.where(low_incl, aa[u][C:, N:], 0.0)], axis=1))
        pw[u] = a_ab
        t[u] = eye_w + a_ab
    n_fac = int(np.log2(C))
    akv = {}
    for u in units:
        pwb = bf(pw[u])
        pw[u] = _dot(pwb, stack(pwb))
        akv[u] = _dot(a_ak[u], vs[u])
    for s in range(2, n_fac + 1):
        for u in units:
            pwb = bf(pw[u])
            rhs = stack(pwb)
            if s < n_fac:
                both = _dot(jnp.concatenate([bf(t[u]), pwb], axis=0), rhs)
                t[u] = t[u] + both[:C]
                pw[u] = both[C:]
            else:
                t[u] = t[u] + _dot(bf(t[u]), rhs)
    pm, q = {}, {}
    for u in units:
        tb = bf(t[u])
        pm[u] = bf(_dot(tb, stack(al[u])))
        q[u] = bf(_dot(tb, stack(bf(akv[u]))))
    pr, y0, bk = {}, {}, {}
    for u in units:
        rp = bf(rt[u] + _dot(a_r[u][:, :N], stack(pm[u])))
        pr[u] = jnp.concatenate([pm[u], rp], axis=0)
        y0[u] = _dot(a_r[u], jnp.concatenate([stack(q[u]), vs[u]], axis=0))
        bk[u] = jnp.concatenate([bw[u], kw[u]], axis=0)
    zt = [zt_ref[p] for p in range(n_packs)]
    for c in range(n_chunks):
        for p in range(n_packs):
            u = (p, c)
            uy = _dot_nt(pr[u], bf(zt[p]))
            y_ref[c * C:(c + 1) * C, p * L:(p + 1) * L] = uy[C:] + y0[u]
            upd = _dot_tn(jnp.concatenate([bf(uy[:C] + q[u]), v[u]], axis=0), bk[u])
            zt[p] = zt[p] * w_end[u] + jnp.where(bd_mask, upd, 0.0)
    for p in range(n_packs):
        zt_ref[p] = zt[p]


def _rwkv_scan(r, lw, k, v, kk, b):
    B, S, D = r.shape
    tile = pl.BlockSpec((None, RW_TB, D), lambda bb, t: (bb, t, 0))
    return pl.pallas_call(
        _rwkv_scan_kernel,
        grid=(B, S // RW_TB),
        in_specs=[tile] * 6,
        out_specs=tile,
        out_shape=jax.ShapeDtypeStruct((B, S, D), F32),
        scratch_shapes=[pltpu.VMEM((D // RW_LANES, RW_LANES, RW_LANES), F32)],
        compiler_params=_cparams("parallel", "arbitrary"),
    )(r, lw, k, v, kk, b)


def _rwkv_out_kernel(y_ref, r_ref, k_ref, v_ref, g_ref, lnw_ref, lnb_ref, rk_ref, gm_ref, gmt_ref,
                     w_ref, h_ref, gt_ref, o_ref):
    y = y_ref[...]
    inv_n = 1.0 / RWKV_HEAD
    yc = y - _gsum(y, gm_ref, gmt_ref) * inv_n
    var = _gsum(yc * yc, gm_ref, gmt_ref) * inv_n
    yn = yc * lax.rsqrt(var + GN_EPS) * lnw_ref[...] + lnb_ref[...]
    v = v_ref[...]
    bonus = _gsum(r_ref[...] * k_ref[...] * rk_ref[...], gm_ref, gmt_ref) * v
    out = ((yn + bonus) * g_ref[...]).astype(BF16)
    o_ref[...] = h_ref[...] + gt_ref[...] * _dot(out, w_ref[...])


def _rwkv_out(y, r, k, v, g, ln_w, ln_b, r_k, w_out, h, mod, tm=256):
    B, S, D = h.shape
    gm, gmt = _head_indicator(D)
    row = lambda a: a.astype(F32).reshape(1, D)
    full = lambda a: pl.BlockSpec(a.shape, lambda b, i: (0,) * a.ndim)
    tile = pl.BlockSpec((None, tm, D), lambda b, i: (b, i, 0))
    params = [row(ln_w), row(ln_b), row(r_k), gm, gmt, w_out.astype(BF16)]
    return pl.pallas_call(
        _rwkv_out_kernel,
        grid=(B, S // tm),
        in_specs=[tile] * 5 + [full(p) for p in params] + [tile, pl.BlockSpec((None, None, 1, D), lambda b, i: (b, 5, 0, 0))],
        out_specs=tile,
        out_shape=jax.ShapeDtypeStruct((B, S, D), F32),
        compiler_params=_cparams("parallel", "parallel"),
    )(y, r, k, v, g, *params, h, mod)


def _rwkv_mixer(h, mod, mu, w_rkv, w0, w1, w2, a0, a1, a2, g1, g2, k_k, k_a, r_k, ln_w, ln_b, w_out):
    r, lw, k, v, kk, b, g = _rwkv_in(h, mod, mu, w_rkv, w0, w1, w2, a0, a1, a2, g1, g2, k_k, k_a)
    y = _rwkv_scan(r, lw, k, v, kk, b)
    return _rwkv_out(y, r, k, v, g, ln_w, ln_b, r_k, w_out, h, mod)


def kernel(x, c, ada_w, ada_b, ffn_w_gu, ffn_w_down, dsa_w_in, dsa_q_norm, dsa_k_norm, dsa_ik_norm, dsa_w_out, rwkv_mu, rwkv_w_rkv, rwkv_w0, rwkv_w1, rwkv_w2, rwkv_a0, rwkv_a1, rwkv_a2, rwkv_g1, rwkv_g2, rwkv_k_k, rwkv_k_a, rwkv_r_k, rwkv_ln_w, rwkv_ln_b, rwkv_w_out):
    B, S, D = x.shape
    depth = ada_w.shape[0]
    mods = _ada_mod(c, ada_w, ada_b)
    h = x
    for i in range(depth):
        mod = mods[i].reshape(B, N_MOD, 1, D)
        j = i // 2
        h = _ffn(h, mod, 0, ffn_w_gu[i, 0].astype(BF16), ffn_w_down[i, 0].astype(BF16))
        if i % 2 == 0:
            h = _dsa_mixer(h, mod, dsa_w_in[j], dsa_q_norm[j], dsa_k_norm[j], dsa_ik_norm[j], dsa_w_out[j])
        else:
            h = _rwkv_mixer(h, mod, rwkv_mu[j], rwkv_w_rkv[j], rwkv_w0[j], rwkv_w1[j], rwkv_w2[j], rwkv_a0[j],
                            rwkv_a1[j], rwkv_a2[j], rwkv_g1[j], rwkv_g2[j], rwkv_k_k[j], rwkv_k_a[j],
                            rwkv_r_k[j], rwkv_ln_w[j], rwkv_ln_b[j], rwkv_w_out[j])
        h = _ffn(h, mod, 6, ffn_w_gu[i, 1].astype(BF16), ffn_w_down[i, 1].astype(BF16))
    return h
```

```python
import functools

import jax
import jax.numpy as jnp
import numpy as np
from jax import lax
from jax.experimental import pallas as pl
from jax.experimental.pallas import tpu as pltpu

F32 = jnp.float32
BF16 = jnp.bfloat16

RMS_EPS = 1e-6
N_MOD = 9
VMEM_LIMIT_BYTES = 56 * 1024 * 1024

N_HEADS = 16
HEAD_DIM = 64
N_KV_HEADS = 4
GROUP = N_HEADS // N_KV_HEADS
IDX_HEADS = 8
IDX_DIM = 64
TOPK_MAX = 256
LANE = 128
LOG2E = 1.4426950408889634


def _cparams(*sem):
    return pltpu.CompilerParams(dimension_semantics=sem, vmem_limit_bytes=VMEM_LIMIT_BYTES)


def _dot(a, b):
    return jnp.dot(a, b, preferred_element_type=F32)


def _dot_nt(a, b):
    return lax.dot_general(a, b, (((1,), (1,)), ((), ())), preferred_element_type=F32)


def _dot_tn(a, b):
    return lax.dot_general(a, b, (((0,), (0,)), ((), ())), preferred_element_type=F32)


def _modulate(h, sh, sc):
    ms = jnp.mean(h * h, axis=-1, keepdims=True)
    return h * lax.rsqrt(ms + RMS_EPS) * (1.0 + sc) + sh


def _ada_kernel(c_ref, w_ref, b_ref, o_ref):
    c = c_ref[...]
    ca = (c * jax.nn.sigmoid(c)).astype(BF16)
    o_ref[...] = _dot(ca, w_ref[...].astype(BF16)) + b_ref[...]


def _ada_mod(c, ada_w, ada_b):
    L, D, N = ada_w.shape
    B = c.shape[0]
    tn = 1536
    return pl.pallas_call(
        _ada_kernel,
        grid=(L, N // tn),
        in_specs=[
            pl.BlockSpec((B, D), lambda l, j: (0, 0)),
            pl.BlockSpec((None, D, tn), lambda l, j: (l, 0, j)),
            pl.BlockSpec((None, 1, tn), lambda l, j: (l, 0, j)),
        ],
        out_specs=pl.BlockSpec((None, B, tn), lambda l, j: (l, 0, j)),
        out_shape=jax.ShapeDtypeStruct((L, B, N), F32),
        compiler_params=_cparams("parallel", "parallel"),
    )(c, ada_w, ada_b.reshape(L, 1, N))


FF_CHUNK = 256


def _ffn_kernel(h_ref, sh_ref, sc_ref, gt_ref, wgu_ref, wd_ref, o_ref, *, d_ff):
    h = h_ref[...]
    u = _modulate(h, sh_ref[...], sc_ref[...]).astype(BF16)
    acc = None
    for c in range(d_ff // FF_CHUNK):
        lo = c * FF_CHUNK
        g = _dot(u, wgu_ref[:, lo:lo + FF_CHUNK])
        up = _dot(u, wgu_ref[:, d_ff + lo:d_ff + lo + FF_CHUNK])
        a = (g * jax.nn.sigmoid(g) * up).astype(BF16)
        part = _dot(a, wd_ref[lo:lo + FF_CHUNK, :])
        acc = part if acc is None else acc + part
    o_ref[...] = h + (0.5 * gt_ref[...]) * acc


def _ffn(h, mod, k0, w_gu, w_down, tm=512):
    B, S, D = h.shape
    d_ff = w_down.shape[0]
    mspec = lambda k: pl.BlockSpec((None, None, 1, D), lambda b, i: (b, k, 0, 0))
    return pl.pallas_call(
        functools.partial(_ffn_kernel, d_ff=d_ff),
        grid=(B, S // tm),
        in_specs=[
            pl.BlockSpec((None, tm, D), lambda b, i: (b, i, 0)),
            mspec(k0), mspec(k0 + 1), mspec(k0 + 2),
            pl.BlockSpec((D, 2 * d_ff), lambda b, i: (0, 0), pipeline_mode=pl.Buffered(1)),
            pl.BlockSpec((d_ff, D), lambda b, i: (0, 0), pipeline_mode=pl.Buffered(1)),
        ],
        out_specs=pl.BlockSpec((None, tm, D), lambda b, i: (b, i, 0)),
        out_shape=jax.ShapeDtypeStruct((B, S, D), F32),
        compiler_params=_cparams("parallel", "parallel"),
    )(h, mod, mod, mod, w_gu, w_down)


def _dsa_in_kernel(h_ref, sh_ref, sc_ref, wqT_ref, wk_ref, wvT_ref, wiqT_ref, wik_ref, wiwT_ref,
                   qn_ref, kn_ref, ikn_ref,
                   qT_ref, k_ref, vT_ref, iqT_ref, ik_ref, iwT_ref):
    u = _modulate(h_ref[...], sh_ref[...], sc_ref[...]).astype(BF16)
    tm = u.shape[0]
    qT = _dot_nt(wqT_ref[...], u).reshape(N_HEADS, HEAD_DIM, tm)
    ms = jnp.mean(qT * qT, axis=1, keepdims=True)
    qT = qT * lax.rsqrt(ms + RMS_EPS) * qn_ref[...][None] * (HEAD_DIM ** -0.5 * LOG2E)
    qT_ref[...] = qT.reshape(N_HEADS * HEAD_DIM, tm).astype(BF16)
    k = _dot(u, wk_ref[...])
    kn = kn_ref[...]
    for j in range(N_KV_HEADS):
        kj = k[:, j * LANE:(j + 1) * LANE]
        ms = jnp.sum(kj * kj, axis=-1, keepdims=True) * (1.0 / HEAD_DIM)
        k_ref[:, j * LANE:(j + 1) * LANE] = (kj * lax.rsqrt(ms + RMS_EPS) * kn).astype(BF16)
    vT_ref[...] = _dot_nt(wvT_ref[...], u).astype(BF16)
    iqT_ref[...] = _dot_nt(wiqT_ref[...], u).astype(BF16)
    ik = _dot(u, wik_ref[...])
    ms = jnp.sum(ik * ik, axis=-1, keepdims=True) * (1.0 / IDX_DIM)
    ik_ref[...] = (ik * lax.rsqrt(ms + RMS_EPS) * ikn_ref[...]).astype(BF16)
    iwT_ref[...] = _dot_nt(wiwT_ref[...], u) * (IDX_HEADS ** -0.5 * IDX_DIM ** -0.5)


DSA_KC = 256
DSA_QB = 256


def _dsa_in(h, mod, w_in, q_norm, k_norm, ik_norm, tm=DSA_KC):
    B, S, D = h.shape
    QW, KW = N_HEADS * HEAD_DIM, N_KV_HEADS * HEAD_DIM
    IQW = IDX_HEADS * IDX_DIM
    o1, o2, o3, o4, o5 = QW, QW + KW, QW + 2 * KW, QW + 2 * KW + IQW, QW + 2 * KW + IQW + IDX_DIM
    wb = w_in.astype(BF16)
    wqT = wb[:, :o1].T
    wk = jnp.pad(wb[:, o1:o2].reshape(D, N_KV_HEADS, HEAD_DIM), ((0, 0), (0, 0), (0, LANE - HEAD_DIM))).reshape(D, N_KV_HEADS * LANE)
    wvT = wb[:, o2:o3].T
    wiqT = wb[:, o3:o4].T
    wik = jnp.pad(wb[:, o4:o5], ((0, 0), (0, LANE - IDX_DIM)))
    wiwT = wb[:, o5:].T
    qn = q_norm.astype(F32).reshape(HEAD_DIM, 1)
    kn = jnp.pad(k_norm.astype(F32), (0, LANE - HEAD_DIM)).reshape(1, LANE)
    ikn = jnp.pad(ik_norm.astype(F32), (0, LANE - IDX_DIM)).reshape(1, LANE)
    mspec = lambda k: pl.BlockSpec((None, None, 1, D), lambda b, i: (b, k, 0, 0))
    full = lambda a: pl.BlockSpec(a.shape, lambda b, i: (0,) * a.ndim)
    nc = S // tm
    return pl.pallas_call(
        _dsa_in_kernel,
        grid=(B, nc),
        in_specs=[pl.BlockSpec((None, tm, D), lambda b, i: (b, i, 0)), mspec(3), mspec(4),
                  full(wqT), full(wk), full(wvT), full(wiqT), full(wik), full(wiwT),
                  full(qn), full(kn), full(ikn)],
        out_specs=[
            pl.BlockSpec((None, QW, tm), lambda b, i: (b, 0, i)),
            pl.BlockSpec((None, tm, N_KV_HEADS * LANE), lambda b, i: (b, i, 0)),
            pl.BlockSpec((None, None, KW, tm), lambda b, i: (b, i, 0, 0)),
            pl.BlockSpec((None, IQW, tm), lambda b, i: (b, 0, i)),
            pl.BlockSpec((None, tm, LANE), lambda b, i: (b, i, 0)),
            pl.BlockSpec((None, IDX_HEADS, tm), lambda b, i: (b, 0, i)),
        ],
        out_shape=[
            jax.ShapeDtypeStruct((B, QW, S), BF16),
            jax.ShapeDtypeStruct((B, S, N_KV_HEADS * LANE), BF16),
            jax.ShapeDtypeStruct((B, nc, KW, tm), BF16),
            jax.ShapeDtypeStruct((B, IQW, S), BF16),
            jax.ShapeDtypeStruct((B, S, LANE), BF16),
            jax.ShapeDtypeStruct((B, IDX_HEADS, S), F32),
        ],
        compiler_params=_cparams("parallel", "parallel"),
    )(h, mod, mod, wqT, wk, wvT, wiqT, wik, wiwT, qn, kn, ikn)


NEG_BIG = -(2.0 ** 100)
DSA_VPAD = 16
N_CHAINS = 4


def _ukey_to_float(ukey):
    key = ukey ^ np.int32(-2 ** 31)
    bits = key ^ ((key >> 31) & np.int32(0x7FFFFFFF))
    return lax.bitcast_convert_type(bits, F32)


def _fold_rows(x, op, rows):
    while x.shape[0] > rows:
        half = x.shape[0] // 2
        x = op(x[:half], x[half:])
    return x


def _dsa_core_kernel(iqT_ref, iwT_ref, qT_ref, ik_ref, k_ref, vT_ref, oT_ref,
                     sc_ref, fl_ref, whi_ref, wlo_ref, qa_ref, iqa_ref, d_ref, st_ref, p_ref, m_ref, acc_ref,
                     *, k_top, n_chunks):
    KC, QB = DSA_KC, DSA_QB
    qi = pl.program_id(1)
    q0 = qi * QB
    nk = (q0 + QB + KC - 1) // KC
    qpos = q0 + lax.broadcasted_iota(jnp.int32, (1, QB), 1)
    zeros_h = jnp.zeros((LANE - HEAD_DIM, QB), BF16)

    for h in range(IDX_HEADS):
        iqa_ref[:, h * QB:(h + 1) * QB] = jnp.concatenate(
            [iqT_ref[h * IDX_DIM:(h + 1) * IDX_DIM, :], zeros_h], axis=0)
    for j in range(N_KV_HEADS):
        for g in range(GROUP):
            hh = j * GROUP + g
            qa_ref[j, :, g * QB:(g + 1) * QB] = jnp.concatenate(
                [qT_ref[hh * HEAD_DIM:(hh + 1) * HEAD_DIM, :], zeros_h], axis=0)
    iw = iwT_ref[...]

    npair = (nk + 1) // 2

    def clamp_chunk(c):
        return jnp.minimum(c, n_chunks - 1)

    def idx_dot(c, slot):
        k0 = pl.multiple_of(c * KC, KC)
        d_ref[slot] = _dot(ik_ref[pl.ds(k0, KC), :], iqa_ref[...])

    def idx_score(c, slot):
        k0 = pl.multiple_of(c * KC, KC)
        s = jnp.maximum(d_ref[slot, :, 0:QB], 0.0) * iw[0:1, :]
        for h in range(1, IDX_HEADS):
            s = s + jnp.maximum(d_ref[slot, :, h * QB:(h + 1) * QB], 0.0) * iw[h:h + 1, :]
        kpos = k0 + lax.broadcasted_iota(jnp.int32, (KC, 1), 0)
        s = jnp.where(kpos <= qpos, s, -jnp.inf)
        sc_ref[pl.ds(k0, KC), :] = s
        r = s.astype(BF16)
        rf = r.astype(F32)
        below = (rf * jnp.where(rf > 0.0, 1.0 - 1.25 * 2.0 ** -8, 1.0 + 1.25 * 2.0 ** -8)).astype(BF16)
        fl_ref[pl.ds(k0, KC), :] = jnp.where(rf > s, below, r)

    idx_dot(0, 0)

    def p1(i, carry):
        c = 2 * i
        idx_dot(c + 1, 1)
        idx_score(c, 0)
        idx_dot(clamp_chunk(c + 2), 0)
        idx_score(c + 1, 1)
        return carry

    lax.fori_loop(0, npair, p1, 0)

    def count_not(pred):
        def body(i, accs):
            accs = list(accs)
            for half in range(2):
                k0 = pl.multiple_of((2 * i + half) * KC, KC)
                ind = jnp.where(pred(sc_ref[pl.ds(k0, KC), :]), 0.0, 1.0)
                for r in range(KC // 8):
                    accs[r % N_CHAINS] = accs[r % N_CHAINS] + ind[r * 8:(r + 1) * 8]
            return tuple(accs)
        accs = lax.fori_loop(0, npair, body, tuple(jnp.zeros((8, QB), F32) for _ in range(N_CHAINS)))
        acc = (accs[0] + accs[1]) + (accs[2] + accs[3])
        return jnp.sum(acc, axis=0, keepdims=True)

    n_staged = (npair * (2 * KC)).astype(F32)

    def count_ge16(ref, t):
        tb = t.astype(BF16)
        one, zero = jnp.ones((), BF16), jnp.zeros((), BF16)

        def body(i, accs):
            accs = list(accs)
            for half in range(2):
                k0 = pl.multiple_of((2 * i + half) * KC, KC)
                ind = jnp.where(ref[pl.ds(k0, KC), :] < tb, zero, one)
                for r in range(KC // 16):
                    accs[r % N_CHAINS] = accs[r % N_CHAINS] + ind[r * 16:(r + 1) * 16]
            return tuple(accs)
        accs = lax.fori_loop(0, npair, body, tuple(jnp.zeros((16, QB), BF16) for _ in range(N_CHAINS)))
        acc = (accs[0].astype(F32) + accs[1].astype(F32)) + (accs[2].astype(F32) + accs[3].astype(F32))
        return jnp.sum(acc, axis=0, keepdims=True)

    def digit_search(ref, n_bits, k_need, to_float):
        d = jnp.zeros((1, QB), F32)
        for bit in range(n_bits - 1, -1, -1):
            cand = d + float(1 << bit)
            d = jnp.where(count_ge16(ref, to_float(cand)) >= k_need, cand, d)
        return d

    def val16(d):
        di = d.astype(jnp.int32)
        return _ukey_to_float(jnp.where(di >= 0x8000, di << 16, (di << 16) | np.int32(0xFFFF)))

    key_hi = digit_search(fl_ref, 16, k_top, val16)
    h_lo = val16(key_hi)
    h_up = val16(key_hi + 1.0)
    no_thr = h_lo == -jnp.inf
    h_up = jnp.where(h_up > h_lo, h_up, np.float32(2.0 ** -126))
    width = jnp.where(no_thr, 1.0, h_up - h_lo)
    tiny = width < 2.0 ** -60
    pre = jnp.where(tiny, np.float32(2.0 ** 60), np.float32(1.0))
    unpre = jnp.where(tiny, np.float32(2.0 ** -60), np.float32(1.0))
    from_int = (width * pre) * np.float32(2.0 ** -16)
    to_int = lax.bitcast_convert_type(np.int32(0x7F000000) - lax.bitcast_convert_type(from_int, jnp.int32), F32)

    def split(i, carry):
        for half in range(2):
            k0 = pl.multiple_of((2 * i + half) * KC, KC)
            w = jnp.clip(((sc_ref[pl.ds(k0, KC), :] - h_lo) * pre) * to_int, -256.0, 65536.0)
            w_hi = jnp.floor(w * (1.0 / 256.0))
            whi_ref[pl.ds(k0, KC), :] = w_hi.astype(BF16)
            wlo_ref[pl.ds(k0, KC), :] = (w - 256.0 * w_hi).astype(BF16)
        return carry
    lax.fori_loop(0, npair, split, 0)

    ident = lambda d: d
    d_mid = digit_search(whi_ref, 8, k_top, ident)
    k_low = k_top - count_ge16(whi_ref, d_mid + 1.0)
    d_mid_b = d_mid.astype(BF16)

    def keep_low(i, carry):
        for half in range(2):
            k0 = pl.multiple_of((2 * i + half) * KC, KC)
            whi_ref[pl.ds(k0, KC), :] = jnp.where(whi_ref[pl.ds(k0, KC), :] == d_mid_b,
                                                  wlo_ref[pl.ds(k0, KC), :], -jnp.ones((), BF16))
        return carry
    lax.fori_loop(0, npair, keep_low, 0)
    d_low = digit_search(whi_ref, 8, k_low, ident)
    offset = ((d_mid * 256.0 + d_low) * from_int) * unpre
    thr = jnp.where(no_thr, -jnp.inf, h_lo + offset)

    m_ref[...] = jnp.full(m_ref.shape, NEG_BIG, F32)
    acc_ref[...] = jnp.zeros(acc_ref.shape, F32)
    r_i = lax.broadcasted_iota(jnp.int32, (KC, KC), 0)
    c_i = lax.broadcasted_iota(jnp.int32, (KC, KC), 1)
    tril = jnp.where(c_i <= r_i, 1.0, 0.0).astype(BF16)
    ones_rows = jnp.ones((DSA_VPAD, KC), BF16)

    HALF = GROUP * QB // 2

    def logits(c, slot, j, half):
        k0 = pl.multiple_of(c * KC, KC)
        cols = slice(half * HALF, (half + 1) * HALF)
        st_ref[slot, j, :, cols] = _dot(k_ref[pl.ds(k0, KC), j * LANE:(j + 1) * LANE],
                                        qa_ref[j, :, cols]).astype(BF16)

    for j in range(N_KV_HEADS):
        logits(0, 0, j, 0)
        logits(0, 0, j, 1)

    def attend(c, slot, tie_seen, c_next, need):
        k0 = pl.multiple_of(c * KC, KC)
        s = sc_ref[pl.ds(k0, KC), :]
        kpos = k0 + lax.broadcasted_iota(jnp.int32, (KC, 1), 0)
        eq = s == thr
        eqf = jnp.where(eq, 1.0, 0.0)
        incl = _dot(tril, eqf.astype(BF16))
        rank = tie_seen + incl - eqf
        sel = jnp.logical_and(jnp.logical_or(s > thr, jnp.logical_and(eq, rank < need)), kpos <= qpos)
        tie_seen = tie_seen + incl[KC - 1:KC, :]
        bias = jnp.where(sel, 0.0, NEG_BIG).astype(BF16)
        for j in range(N_KV_HEADS):
            alphas = []
            for g in range(GROUP):
                sl = slice(g * QB, (g + 1) * QB)
                sg = st_ref[slot, j, :, sl] + bias
                m_old = m_ref[j, :, sl]
                mx = jnp.max(_fold_rows(sg, jnp.maximum, 16), axis=0, keepdims=True).astype(F32)
                m_new = jnp.maximum(m_old, mx)
                alphas.append(jnp.exp2(m_old - m_new))
                m_ref[j, :, sl] = m_new
                p_ref[j, :, sl] = jnp.exp2(sg - m_new.astype(BF16))
                if g % 2 == 1:
                    logits(c_next, 1 - slot, j, g // 2)
            v_aug = jnp.concatenate([vT_ref[c, j * HEAD_DIM:(j + 1) * HEAD_DIM, :], ones_rows], axis=0)
            acc_ref[j] = acc_ref[j] * jnp.concatenate(alphas, axis=1) + _dot(v_aug, p_ref[j])
        return tie_seen

    need = k_top - (n_staged - count_not(lambda s: s > thr))

    def p3(i, tie_seen):
        c = 2 * i
        tie_seen = attend(c, 0, tie_seen, c + 1, need)
        return attend(c + 1, 1, tie_seen, clamp_chunk(c + 2), need)

    lax.fori_loop(0, npair, p3, jnp.zeros((1, QB), F32))

    for j in range(N_KV_HEADS):
        o = acc_ref[j, 0:HEAD_DIM, :] / acc_ref[j, HEAD_DIM:HEAD_DIM + 1, :]
        for g in range(GROUP):
            hh = j * GROUP + g
            oT_ref[hh * HEAD_DIM:(hh + 1) * HEAD_DIM, :] = o[:, g * QB:(g + 1) * QB].astype(BF16)


def _dsa_core(qT, k, vT, iqT, ik, iwT):
    B, QW, S = qT.shape
    KC, QB = DSA_KC, DSA_QB
    k_top = min(TOPK_MAX, S // 4)
    nc = S // KC
    return pl.pallas_call(
        functools.partial(_dsa_core_kernel, k_top=k_top, n_chunks=nc),
        grid=(B, S // QB),
        in_specs=[
            pl.BlockSpec((None, IDX_HEADS * IDX_DIM, QB), lambda b, i: (b, 0, i)),
            pl.BlockSpec((None, IDX_HEADS, QB), lambda b, i: (b, 0, i)),
            pl.BlockSpec((None, QW, QB), lambda b, i: (b, 0, i)),
            pl.BlockSpec((None, S, LANE), lambda b, i: (b, 0, 0)),
            pl.BlockSpec((None, S, N_KV_HEADS * LANE), lambda b, i: (b, 0, 0)),
            pl.BlockSpec((None, nc, N_KV_HEADS * HEAD_DIM, KC), lambda b, i: (b, 0, 0, 0)),
        ],
        out_specs=pl.BlockSpec((None, QW, QB), lambda b, i: (b, 0, i)),
        out_shape=jax.ShapeDtypeStruct((B, QW, S), BF16),
        scratch_shapes=[
            pltpu.VMEM((S, QB), F32),
            pltpu.VMEM((S, QB), BF16),
            pltpu.VMEM((S, QB), BF16),
            pltpu.VMEM((S, QB), BF16),
            pltpu.VMEM((N_KV_HEADS, LANE, GROUP * QB), BF16),
            pltpu.VMEM((LANE, IDX_HEADS * QB), BF16),
            pltpu.VMEM((2, KC, IDX_HEADS * QB), F32),
            pltpu.VMEM((2, N_KV_HEADS, KC, GROUP * QB), BF16),
            pltpu.VMEM((N_KV_HEADS, KC, GROUP * QB), BF16),
            pltpu.VMEM((N_KV_HEADS, 1, GROUP * QB), F32),
            pltpu.VMEM((N_KV_HEADS, HEAD_DIM + DSA_VPAD, GROUP * QB), F32),
        ],
        compiler_params=_cparams("parallel", "arbitrary"),
    )(iqT, iwT, qT, ik, k, vT)


def _out_T_kernel(xT_ref, w_ref, h_ref, gt_ref, o_ref):
    o_ref[...] = h_ref[...] + gt_ref[...] * _dot_tn(xT_ref[...], w_ref[...])


def _dsa_out(oT, w_out, h, mod, tm=512):
    B, S, D = h.shape
    QW = oT.shape[1]
    return pl.pallas_call(
        _out_T_kernel,
        grid=(B, S // tm),
        in_specs=[
            pl.BlockSpec((None, QW, tm), lambda b, i: (b, 0, i)),
            pl.BlockSpec((QW, D), lambda b, i: (0, 0)),
            pl.BlockSpec((None, tm, D), lambda b, i: (b, i, 0)),
            pl.BlockSpec((None, None, 1, D), lambda b, i: (b, 5, 0, 0)),
        ],
        out_specs=pl.BlockSpec((None, tm, D), lambda b, i: (b, i, 0)),
        out_shape=jax.ShapeDtypeStruct((B, S, D), F32),
        compiler_params=_cparams("parallel", "parallel"),
    )(oT, w_out.astype(BF16), h, mod)


def _dsa_mixer(h, mod, w_in, q_norm, k_norm, ik_norm, w_out):
    qT, k, vT, iqT, ik, iwT = _dsa_in(h, mod, w_in, q_norm, k_norm, ik_norm)
    oT = _dsa_core(qT, k, vT, iqT, ik, iwT)
    return _dsa_out(oT, w_out, h, mod)


RWKV_HEAD = 64
GN_EPS = 64e-5
RW_PACK = 4
RW_LANES = RW_PACK * RWKV_HEAD
RW_C = 64
RW_TB = 256


def _split3(x):
    hi = x.astype(BF16)
    r1 = x - hi.astype(F32)
    mid = r1.astype(BF16)
    lo = (r1 - mid.astype(F32)).astype(BF16)
    return hi, mid, lo


def _gsum(x, g_ref, gt_ref):
    hi = x.astype(BF16)
    lo = (x - hi.astype(F32)).astype(BF16)
    s = _dot(jnp.concatenate([hi, lo], axis=1), g_ref[...])
    return _dot(jnp.concatenate(_split3(s), axis=1), gt_ref[...])


def _rwkv_in_kernel(h_ref, hp_ref, sh_ref, sc_ref, mu_ref, wrkv_ref, w0_ref, w1_ref, w2_ref,
                    a0_ref, a1_ref, a2_ref, g1_ref, g2_ref, kk_ref, ka_ref, gm_ref, gmt_ref,
                    r_ref, lw_ref, k_ref, v_ref, kkn_ref, b_ref, g_ref):
    sh, sc = sh_ref[...], sc_ref[...]
    u = _modulate(h_ref[...], sh, sc)
    tm = u.shape[0]
    prev = _modulate(hp_ref[...], sh, sc)[7:8, :]
    prev = jnp.where(pl.program_id(1) == 0, 0.0, prev)
    row = lax.broadcasted_iota(jnp.int32, (tm, 1), 0)
    xx = jnp.where(row == 0, prev, pltpu.roll(u, 1, 0)) - u
    mix = lambda i: (u + xx * mu_ref[i:i + 1, :]).astype(BF16)
    wl = w0_ref[...] + _dot(jnp.tanh(_dot(mix(3), w1_ref[...])).astype(BF16), w2_ref[...])
    a_lin = _dot(_dot(mix(4), a1_ref[...]).astype(BF16), a2_ref[...])
    g_hid = _dot(mix(5), g1_ref[...])
    k = _dot(mix(1), wrkv_ref[1])
    nwl = -wl
    w_log = -(jnp.maximum(nwl, 0.0) + jnp.log(1.0 + jnp.exp(-jnp.abs(nwl)))) - 0.5
    lw_ref[...] = -jnp.exp(w_log)
    r_ref[...] = _dot(mix(0), wrkv_ref[0])
    a = jax.nn.sigmoid(a0_ref[...] + a_lin)
    g_ref[...] = _dot(jax.nn.sigmoid(g_hid).astype(BF16), g2_ref[...])
    kk = k * kk_ref[...]
    nrm = jnp.sqrt(_gsum(kk * kk, gm_ref, gmt_ref))
    kk = kk / jnp.maximum(nrm, 1e-12)
    k_ref[...] = k * (1.0 + (a - 1.0) * ka_ref[...])
    kkn_ref[...] = kk
    b_ref[...] = kk * a
    v_ref[...] = _dot(mix(2), wrkv_ref[2])


def _head_indicator(D):
    gm = (np.arange(D)[:, None] // RWKV_HEAD == np.arange(D // RWKV_HEAD)[None, :]).astype(np.float32)
    return jnp.asarray(np.concatenate([gm, gm], axis=0), BF16), jnp.asarray(np.concatenate([gm.T] * 3, axis=0), BF16)


def _rwkv_in(h, mod, mu, w_rkv, w0, w1, w2, a0, a1, a2, g1, g2, k_k, k_a, tm=256):
    B, S, D = h.shape
    gm, gmt = _head_indicator(D)
    row = lambda a: a.astype(F32).reshape(1, D)
    full = lambda a: pl.BlockSpec(a.shape, lambda b, i: (0,) * a.ndim)
    mspec = lambda k: pl.BlockSpec((None, None, 1, D), lambda b, i: (b, k, 0, 0))
    tile = pl.BlockSpec((None, tm, D), lambda b, i: (b, i, 0))
    params = [mu.astype(F32), w_rkv.astype(BF16), row(w0), w1.astype(BF16), w2.astype(BF16),
              row(a0), a1.astype(BF16), a2.astype(BF16), g1.astype(BF16), g2.astype(BF16),
              row(k_k), row(k_a), gm, gmt]
    return pl.pallas_call(
        _rwkv_in_kernel,
        grid=(B, S // tm),
        in_specs=[tile,
                  pl.BlockSpec((None, 8, D), lambda b, i: (b, jnp.maximum(i * (tm // 8) - 1, 0), 0)),
                  mspec(3), mspec(4)] + [full(p) for p in params],
        out_specs=[tile] * 7,
        out_shape=[jax.ShapeDtypeStruct((B, S, D), F32)] * 7,
        compiler_params=_cparams("parallel", "parallel"),
    )(h, h, mod, mod, *params)


def _rwkv_scan_kernel(r_ref, lw_ref, k_ref, v_ref, kk_ref, b_ref, y_ref, zt_ref):
    C, P, L = RW_C, RW_PACK, RW_LANES
    N = P * C
    assert C == RWKV_HEAD
    n_packs = r_ref.shape[1] // L
    n_chunks = r_ref.shape[0] // C
    units = [(p, c) for p in range(n_packs) for c in range(n_chunks)]

    @pl.when(pl.program_id(1) == 0)
    def _():
        zt_ref[...] = jnp.zeros(zt_ref.shape, F32)

    lane = lax.broadcasted_iota(jnp.int32, (1, L), 1)
    hmask = [lane // C == hd for hd in range(P)]
    bf = lambda x: x.astype(BF16)
    stack = lambda x: jnp.concatenate([jnp.where(hmask[hd], x, jnp.zeros_like(x)) for hd in range(P)], axis=0)
    t_i = lax.broadcasted_iota(jnp.int32, (C, N), 0)
    i_i = lax.broadcasted_iota(jnp.int32, (C, N), 1) % C
    low_strict = t_i > i_i
    low_incl = t_i >= i_i
    eye_w = jnp.where(t_i == i_i, 1.0, 0.0)
    bd_mask = (lax.broadcasted_iota(jnp.int32, (L, L), 0) // C) == (lax.broadcasted_iota(jnp.int32, (L, L), 1) // C)
    tri = jnp.where(lax.broadcasted_iota(jnp.int32, (C, C), 0) >= lax.broadcasted_iota(jnp.int32, (C, C), 1),
                    1.0, 0.0).astype(BF16)
    tile = lambda ref, u: ref[u[1] * C:(u[1] + 1) * C, u[0] * L:(u[0] + 1) * L]

    cw = {}
    tri3 = jnp.concatenate([tri, tri, tri], axis=1)
    for u in units:
        cw[u] = _dot(tri3, jnp.concatenate(_split3(tile(lw_ref, u)), axis=0))
    al, rt, bw, kw, v, vs, aa, w_end = {}, {}, {}, {}, {}, {}, {}, {}
    for u in units:
        lw = tile(lw_ref, u)
        w_incl = jnp.exp(cw[u])
        w_excl = jnp.exp(cw[u] - lw)
        w_inv = jnp.exp(-cw[u])
        w_end[u] = w_incl[C - 1:C, :]
        v[u] = bf(tile(v_ref, u))
        vs[u] = stack(v[u])
        al[u] = bf(-tile(kk_ref, u) * w_excl)
        rt[u] = tile(r_ref, u) * w_incl
        bh = tile(b_ref, u) * w_inv
        kh = tile(k_ref, u) * w_inv
        bw[u] = bf(bh * w_end[u])
        kw[u] = bf(kh * w_end[u])
        aa[u] = _dot_nt(jnp.concatenate([al[u], bf(rt[u])], axis=0),
                        jnp.concatenate([stack(bf(bh)), stack(bf(kh))], axis=0))
    a_ak, a_r, pw, t = {}, {}, {}, {}
    for u in units:
        a_ab = jnp.where(low_strict, aa[u][:C, :N], 0.0)
        a_ak[u] = bf(jnp.where(low_strict, aa[u][:C, N:], 0.0))
        a_r[u] = bf(jnp.concatenate([jnp.where(low_incl, aa[u][C:, :N], 0.0),
                                     jnp.where(low_incl, aa[u][C:, N:], 0.0)], axis=1))
        pw[u] = a_ab
        t[u] = eye_w + a_ab
    n_fac = int(np.log2(C))
    akv = {}
    for u in units:
        pwb = bf(pw[u])
        pw[u] = _dot(pwb, stack(pwb))
        akv[u] = _dot(a_ak[u], vs[u])
    for s in range(2, n_fac + 1):
        for u in units:
            pwb = bf(pw[u])
            rhs = stack(pwb)
            if s < n_fac:
                both = _dot(jnp.concatenate([bf(t[u]), pwb], axis=0), rhs)
                t[u] = t[u] + both[:C]
                pw[u] = both[C:]
            else:
                t[u] = t[u] + _dot(bf(t[u]), rhs)
    pm, q = {}, {}
    for u in units:
        tb = bf(t[u])
        pm[u] = bf(_dot(tb, stack(al[u])))
        q[u] = bf(_dot(tb, stack(bf(akv[u]))))
    pr, y0, bk = {}, {}, {}
    for u in units:
        rp = bf(rt[u] + _dot(a_r[u][:, :N], stack(pm[u])))
        pr[u] = jnp.concatenate([pm[u], rp], axis=0)
        y0[u] = _dot(a_r[u], jnp.concatenate([stack(q[u]), vs[u]], axis=0))
        bk[u] = jnp.concatenate([bw[u], kw[u]], axis=0)
    zt = [zt_ref[p] for p in range(n_packs)]
    for c in range(n_chunks):
        for p in range(n_packs):
            u = (p, c)
            uy = _dot_nt(pr[u], bf(zt[p]))
            y_ref[c * C:(c + 1) * C, p * L:(p + 1) * L] = uy[C:] + y0[u]
            upd = _dot_tn(jnp.concatenate([bf(uy[:C] + q[u]), v[u]], axis=0), bk[u])
            zt[p] = zt[p] * w_end[u] + jnp.where(bd_mask, upd, 0.0)
    for p in range(n_packs):
        zt_ref[p] = zt[p]


def _rwkv_scan(r, lw, k, v, kk, b):
    B, S, D = r.shape
    tile = pl.BlockSpec((None, RW_TB, D), lambda bb, t: (bb, t, 0))
    return pl.pallas_call(
        _rwkv_scan_kernel,
        grid=(B, S // RW_TB),
        in_specs=[tile] * 6,
        out_specs=tile,
        out_shape=jax.ShapeDtypeStruct((B, S, D), F32),
        scratch_shapes=[pltpu.VMEM((D // RW_LANES, RW_LANES, RW_LANES), F32)],
        compiler_params=_cparams("parallel", "arbitrary"),
    )(r, lw, k, v, kk, b)


def _rwkv_out_kernel(y_ref, r_ref, k_ref, v_ref, g_ref, lnw_ref, lnb_ref, rk_ref, gm_ref, gmt_ref,
                     w_ref, h_ref, gt_ref, o_ref):
    y = y_ref[...]
    inv_n = 1.0 / RWKV_HEAD
    yc = y - _gsum(y, gm_ref, gmt_ref) * inv_n
    var = _gsum(yc * yc, gm_ref, gmt_ref) * inv_n
    yn = yc * lax.rsqrt(var + GN_EPS) * lnw_ref[...] + lnb_ref[...]
    v = v_ref[...]
    bonus = _gsum(r_ref[...] * k_ref[...] * rk_ref[...], gm_ref, gmt_ref) * v
    out = ((yn + bonus) * g_ref[...]).astype(BF16)
    o_ref[...] = h_ref[...] + gt_ref[...] * _dot(out, w_ref[...])


def _rwkv_out(y, r, k, v, g, ln_w, ln_b, r_k, w_out, h, mod, tm=256):
    B, S, D = h.shape
    gm, gmt = _head_indicator(D)
    row = lambda a: a.astype(F32).reshape(1, D)
    full = lambda a: pl.BlockSpec(a.shape, lambda b, i: (0,) * a.ndim)
    tile = pl.BlockSpec((None, tm, D), lambda b, i: (b, i, 0))
    params = [row(ln_w), row(ln_b), row(r_k), gm, gmt, w_out.astype(BF16)]
    return pl.pallas_call(
        _rwkv_out_kernel,
        grid=(B, S // tm),
        in_specs=[tile] * 5 + [full(p) for p in params] + [tile, pl.BlockSpec((None, None, 1, D), lambda b, i: (b, 5, 0, 0))],
        out_specs=tile,
        out_shape=jax.ShapeDtypeStruct((B, S, D), F32),
        compiler_params=_cparams("parallel", "parallel"),
    )(y, r, k, v, g, *params, h, mod)


def _rwkv_mixer(h, mod, mu, w_rkv, w0, w1, w2, a0, a1, a2, g1, g2, k_k, k_a, r_k, ln_w, ln_b, w_out):
    r, lw, k, v, kk, b, g = _rwkv_in(h, mod, mu, w_rkv, w0, w1, w2, a0, a1, a2, g1, g2, k_k, k_a)
    y = _rwkv_scan(r, lw, k, v, kk, b)
    return _rwkv_out(y, r, k, v, g, ln_w, ln_b, r_k, w_out, h, mod)


def kernel(x, c, ada_w, ada_b, ffn_w_gu, ffn_w_down, dsa_w_in, dsa_q_norm, dsa_k_norm, dsa_ik_norm, dsa_w_out, rwkv_mu, rwkv_w_rkv, rwkv_w0, rwkv_w1, rwkv_w2, rwkv_a0, rwkv_a1, rwkv_a2, rwkv_g1, rwkv_g2, rwkv_k_k, rwkv_k_a, rwkv_r_k, rwkv_ln_w, rwkv_ln_b, rwkv_w_out):
    B, S, D = x.shape
    depth = ada_w.shape[0]
    mods = _ada_mod(c, ada_w, ada_b)
    h = x
    for i in range(depth):
        mod = mods[i].reshape(B, N_MOD, 1, D)
        j = i // 2
        h = _ffn(h, mod, 0, ffn_w_gu[i, 0].astype(BF16), ffn_w_down[i, 0].astype(BF16))
        if i % 2 == 0:
            h = _dsa_mixer(h, mod, dsa_w_in[j], dsa_q_norm[j], dsa_k_norm[j], dsa_ik_norm[j], dsa_w_out[j])
        else:
            h = _rwkv_mixer(h, mod, rwkv_mu[j], rwkv_w_rkv[j], rwkv_w0[j], rwkv_w1[j], rwkv_w2[j], rwkv_a0[j],
                            rwkv_a1[j], rwkv_a2[j], rwkv_g1[j], rwkv_g2[j], rwkv_k_k[j], rwkv_k_a[j],
                            rwkv_r_k[j], rwkv_ln_w[j], rwkv_ln_b[j], rwkv_w_out[j])
        h = _ffn(h, mod, 6, ffn_w_gu[i, 1].astype(BF16), ffn_w_down[i, 1].astype(BF16))
    return h
```

```python
import functools

import jax
import jax.numpy as jnp
import numpy as np
from jax import lax
from jax.experimental import pallas as pl
from jax.experimental.pallas import tpu as pltpu

F32 = jnp.float32
BF16 = jnp.bfloat16

RMS_EPS = 1e-6
N_MOD = 9
VMEM_LIMIT_BYTES = 56 * 1024 * 1024

N_HEADS = 16
HEAD_DIM = 64
N_KV_HEADS = 4
GROUP = N_HEADS // N_KV_HEADS
IDX_HEADS = 8
IDX_DIM = 64
TOPK_MAX = 256
LANE = 128
LOG2E = 1.4426950408889634


def _cparams(*sem):
    return pltpu.CompilerParams(dimension_semantics=sem, vmem_limit_bytes=VMEM_LIMIT_BYTES)


def _dot(a, b):
    return jnp.dot(a, b, preferred_element_type=F32)


def _dot_nt(a, b):
    return lax.dot_general(a, b, (((1,), (1,)), ((), ())), preferred_element_type=F32)


def _dot_tn(a, b):
    return lax.dot_general(a, b, (((0,), (0,)), ((), ())), preferred_element_type=F32)


def _modulate(h, sh, sc):
    ms = jnp.mean(h * h, axis=-1, keepdims=True)
    return h * lax.rsqrt(ms + RMS_EPS) * (1.0 + sc) + sh


def _ada_kernel(c_ref, w_ref, b_ref, o_ref):
    c = c_ref[...]
    ca = (c * jax.nn.sigmoid(c)).astype(BF16)
    o_ref[...] = _dot(ca, w_ref[...].astype(BF16)) + b_ref[...]


def _ada_mod(c, ada_w, ada_b):
    L, D, N = ada_w.shape
    B = c.shape[0]
    tn = 1536
    return pl.pallas_call(
        _ada_kernel,
        grid=(L, N // tn),
        in_specs=[
            pl.BlockSpec((B, D), lambda l, j: (0, 0)),
            pl.BlockSpec((None, D, tn), lambda l, j: (l, 0, j)),
            pl.BlockSpec((None, 1, tn), lambda l, j: (l, 0, j)),
        ],
        out_specs=pl.BlockSpec((None, B, tn), lambda l, j: (l, 0, j)),
        out_shape=jax.ShapeDtypeStruct((L, B, N), F32),
        compiler_params=_cparams("parallel", "parallel"),
    )(c, ada_w, ada_b.reshape(L, 1, N))


FF_CHUNK = 256


def _ffn_kernel(h_ref, sh_ref, sc_ref, gt_ref, wgu_ref, wd_ref, o_ref, *, d_ff):
    h = h_ref[...]
    u = _modulate(h, sh_ref[...], sc_ref[...]).astype(BF16)
    acc = None
    for c in range(d_ff // FF_CHUNK):
        lo = c * FF_CHUNK
        g = _dot(u, wgu_ref[:, lo:lo + FF_CHUNK])
        up = _dot(u, wgu_ref[:, d_ff + lo:d_ff + lo + FF_CHUNK])
        a = (g * jax.nn.sigmoid(g) * up).astype(BF16)
        part = _dot(a, wd_ref[lo:lo + FF_CHUNK, :])
        acc = part if acc is None else acc + part
    o_ref[...] = h + (0.5 * gt_ref[...]) * acc


def _ffn(h, mod, k0, w_gu, w_down, tm=512):
    B, S, D = h.shape
    d_ff = w_down.shape[0]
    mspec = lambda k: pl.BlockSpec((None, None, 1, D), lambda b, i: (b, k, 0, 0))
    return pl.pallas_call(
        functools.partial(_ffn_kernel, d_ff=d_ff),
        grid=(B, S // tm),
        in_specs=[
            pl.BlockSpec((None, tm, D), lambda b, i: (b, i, 0)),
            mspec(k0), mspec(k0 + 1), mspec(k0 + 2),
            pl.BlockSpec((D, 2 * d_ff), lambda b, i: (0, 0), pipeline_mode=pl.Buffered(1)),
            pl.BlockSpec((d_ff, D), lambda b, i: (0, 0), pipeline_mode=pl.Buffered(1)),
        ],
        out_specs=pl.BlockSpec((None, tm, D), lambda b, i: (b, i, 0)),
        out_shape=jax.ShapeDtypeStruct((B, S, D), F32),
        compiler_params=_cparams("parallel", "parallel"),
    )(h, mod, mod, mod, w_gu, w_down)


def _dsa_in_kernel(h_ref, sh_ref, sc_ref, wT_ref, wn_ref, qn_ref, kn_ref, ikn_ref,
                   qT_ref, k_ref, vT_ref, iqT_ref, ik_ref, iwT_ref):
    u = _modulate(h_ref[...], sh_ref[...], sc_ref[...]).astype(BF16)
    tm = u.shape[0]
    QW, KW, IQW = N_HEADS * HEAD_DIM, N_KV_HEADS * HEAD_DIM, IDX_HEADS * IDX_DIM
    yT = _dot_nt(wT_ref[...], u)
    qT = yT[0:QW].reshape(N_HEADS, HEAD_DIM, tm)
    ms = jnp.mean(qT * qT, axis=1, keepdims=True)
    qT = qT * lax.rsqrt(ms + RMS_EPS) * qn_ref[...][None] * (HEAD_DIM ** -0.5 * LOG2E)
    qT_ref[...] = qT.reshape(QW, tm).astype(BF16)
    vT_ref[...] = yT[QW:QW + KW].astype(BF16)
    iqT_ref[...] = yT[QW + KW:QW + KW + IQW].astype(BF16)
    iwT_ref[...] = yT[QW + KW + IQW:QW + KW + IQW + IDX_HEADS] * (IDX_HEADS ** -0.5 * IDX_DIM ** -0.5)
    y = _dot(u, wn_ref[...])
    kn = kn_ref[...]
    for j in range(N_KV_HEADS):
        kj = y[:, j * LANE:(j + 1) * LANE]
        ms = jnp.sum(kj * kj, axis=-1, keepdims=True) * (1.0 / HEAD_DIM)
        k_ref[:, j * LANE:(j + 1) * LANE] = (kj * lax.rsqrt(ms + RMS_EPS) * kn).astype(BF16)
    ik = y[:, N_KV_HEADS * LANE:]
    ms = jnp.sum(ik * ik, axis=-1, keepdims=True) * (1.0 / IDX_DIM)
    ik_ref[...] = (ik * lax.rsqrt(ms + RMS_EPS) * ikn_ref[...]).astype(BF16)


DSA_KC = 256
DSA_QB = 256


def _dsa_in(h, mod, w_in, q_norm, k_norm, ik_norm, tm=DSA_KC):
    B, S, D = h.shape
    QW, KW = N_HEADS * HEAD_DIM, N_KV_HEADS * HEAD_DIM
    IQW = IDX_HEADS * IDX_DIM
    o1, o2, o3, o4, o5 = QW, QW + KW, QW + 2 * KW, QW + 2 * KW + IQW, QW + 2 * KW + IQW + IDX_DIM
    wb = w_in.astype(BF16)
    wqT = wb[:, :o1].T
    wk = jnp.pad(wb[:, o1:o2].reshape(D, N_KV_HEADS, HEAD_DIM), ((0, 0), (0, 0), (0, LANE - HEAD_DIM))).reshape(D, N_KV_HEADS * LANE)
    wvT = wb[:, o2:o3].T
    wiqT = wb[:, o3:o4].T
    wik = jnp.pad(wb[:, o4:o5], ((0, 0), (0, LANE - IDX_DIM)))
    wiwT = jnp.pad(wb[:, o5:].T, ((0, 16 - IDX_HEADS), (0, 0)))
    wT = jnp.concatenate([wqT, wvT, wiqT, wiwT], axis=0)
    wn = jnp.concatenate([wk, wik], axis=1)
    qn = q_norm.astype(F32).reshape(HEAD_DIM, 1)
    kn = jnp.pad(k_norm.astype(F32), (0, LANE - HEAD_DIM)).reshape(1, LANE)
    ikn = jnp.pad(ik_norm.astype(F32), (0, LANE - IDX_DIM)).reshape(1, LANE)
    mspec = lambda k: pl.BlockSpec((None, None, 1, D), lambda b, i: (b, k, 0, 0))
    full = lambda a: pl.BlockSpec(a.shape, lambda b, i: (0,) * a.ndim)
    nc = S // tm
    return pl.pallas_call(
        _dsa_in_kernel,
        grid=(B, nc),
        in_specs=[pl.BlockSpec((None, tm, D), lambda b, i: (b, i, 0)), mspec(3), mspec(4),
                  full(wT), full(wn), full(qn), full(kn), full(ikn)],
        out_specs=[
            pl.BlockSpec((None, QW, tm), lambda b, i: (b, 0, i)),
            pl.BlockSpec((None, tm, N_KV_HEADS * LANE), lambda b, i: (b, i, 0)),
            pl.BlockSpec((None, None, KW, tm), lambda b, i: (b, i, 0, 0)),
            pl.BlockSpec((None, IQW, tm), lambda b, i: (b, 0, i)),
            pl.BlockSpec((None, tm, LANE), lambda b, i: (b, i, 0)),
            pl.BlockSpec((None, IDX_HEADS, tm), lambda b, i: (b, 0, i)),
        ],
        out_shape=[
            jax.ShapeDtypeStruct((B, QW, S), BF16),
            jax.ShapeDtypeStruct((B, S, N_KV_HEADS * LANE), BF16),
            jax.ShapeDtypeStruct((B, nc, KW, tm), BF16),
            jax.ShapeDtypeStruct((B, IQW, S), BF16),
            jax.ShapeDtypeStruct((B, S, LANE), BF16),
            jax.ShapeDtypeStruct((B, IDX_HEADS, S), F32),
        ],
        compiler_params=_cparams("parallel", "parallel"),
    )(h, mod, mod, wT, wn, qn, kn, ikn)


NEG_BIG = -(2.0 ** 100)
DSA_VPAD = 16
N_CHAINS = 4


def _ukey_to_float(ukey):
    key = ukey ^ np.int32(-2 ** 31)
    bits = key ^ ((key >> 31) & np.int32(0x7FFFFFFF))
    return lax.bitcast_convert_type(bits, F32)


def _fold_rows(x, op, rows):
    while x.shape[0] > rows:
        half = x.shape[0] // 2
        x = op(x[:half], x[half:])
    return x


def _dsa_core_kernel(iqT_ref, iwT_ref, qT_ref, ik_ref, k_ref, vT_ref, oT_ref,
                     sc_ref, fl_ref, whi_ref, wlo_ref, qa_ref, iqa_ref, d_ref, st_ref, p_ref, m_ref, acc_ref,
                     *, k_top, n_chunks):
    KC, QB = DSA_KC, DSA_QB
    qi = pl.program_id(1)
    q0 = qi * QB
    nk = (q0 + QB + KC - 1) // KC
    qpos = q0 + lax.broadcasted_iota(jnp.int32, (1, QB), 1)
    zeros_h = jnp.zeros((LANE - HEAD_DIM, QB), BF16)

    for h in range(IDX_HEADS):
        iqa_ref[:, h * QB:(h + 1) * QB] = jnp.concatenate(
            [iqT_ref[h * IDX_DIM:(h + 1) * IDX_DIM, :], zeros_h], axis=0)
    for j in range(N_KV_HEADS):
        for g in range(GROUP):
            hh = j * GROUP + g
            qa_ref[j, :, g * QB:(g + 1) * QB] = jnp.concatenate(
                [qT_ref[hh * HEAD_DIM:(hh + 1) * HEAD_DIM, :], zeros_h], axis=0)
    iw = iwT_ref[...]

    npair = (nk + 1) // 2

    def clamp_chunk(c):
        return jnp.minimum(c, n_chunks - 1)

    def idx_dot(c, slot):
        k0 = pl.multiple_of(c * KC, KC)
        d_ref[slot] = _dot(ik_ref[pl.ds(k0, KC), :], iqa_ref[...])

    def idx_score(c, slot):
        k0 = pl.multiple_of(c * KC, KC)
        s = jnp.maximum(d_ref[slot, :, 0:QB], 0.0) * iw[0:1, :]
        for h in range(1, IDX_HEADS):
            s = s + jnp.maximum(d_ref[slot, :, h * QB:(h + 1) * QB], 0.0) * iw[h:h + 1, :]
        kpos = k0 + lax.broadcasted_iota(jnp.int32, (KC, 1), 0)
        s = jnp.where(kpos <= qpos, s, -jnp.inf)
        sc_ref[pl.ds(k0, KC), :] = s
        r = s.astype(BF16)
        rf = r.astype(F32)
        below = (rf * jnp.where(rf > 0.0, 1.0 - 1.25 * 2.0 ** -8, 1.0 + 1.25 * 2.0 ** -8)).astype(BF16)
        fl_ref[pl.ds(k0, KC), :] = jnp.where(rf > s, below, r)

    idx_dot(0, 0)

    def p1(i, carry):
        c = 2 * i
        idx_dot(c + 1, 1)
        idx_score(c, 0)
        idx_dot(clamp_chunk(c + 2), 0)
        idx_score(c + 1, 1)
        return carry

    lax.fori_loop(0, npair, p1, 0)

    def count_not(pred):
        def body(i, accs):
            accs = list(accs)
            for half in range(2):
                k0 = pl.multiple_of((2 * i + half) * KC, KC)
                ind = jnp.where(pred(sc_ref[pl.ds(k0, KC), :]), 0.0, 1.0)
                for r in range(KC // 8):
                    accs[r % N_CHAINS] = accs[r % N_CHAINS] + ind[r * 8:(r + 1) * 8]
            return tuple(accs)
        accs = lax.fori_loop(0, npair, body, tuple(jnp.zeros((8, QB), F32) for _ in range(N_CHAINS)))
        acc = (accs[0] + accs[1]) + (accs[2] + accs[3])
        return jnp.sum(acc, axis=0, keepdims=True)

    n_staged = (npair * (2 * KC)).astype(F32)

    def count_ge16(ref, t):
        tb = t.astype(BF16)
        one, zero = jnp.ones((), BF16), jnp.zeros((), BF16)

        def body(i, accs):
            accs = list(accs)
            for half in range(2):
                k0 = pl.multiple_of((2 * i + half) * KC, KC)
                ind = jnp.where(ref[pl.ds(k0, KC), :] < tb, zero, one)
                for r in range(KC // 16):
                    accs[r % N_CHAINS] = accs[r % N_CHAINS] + ind[r * 16:(r + 1) * 16]
            return tuple(accs)
        accs = lax.fori_loop(0, npair, body, tuple(jnp.zeros((16, QB), BF16) for _ in range(N_CHAINS)))
        acc = (accs[0].astype(F32) + accs[1].astype(F32)) + (accs[2].astype(F32) + accs[3].astype(F32))
        return jnp.sum(acc, axis=0, keepdims=True)

    def digit_search(ref, n_bits, k_need, to_float):
        d = jnp.zeros((1, QB), F32)
        for bit in range(n_bits - 1, -1, -1):
            cand = d + float(1 << bit)
            d = jnp.where(count_ge16(ref, to_float(cand)) >= k_need, cand, d)
        return d

    def val16(d):
        di = d.astype(jnp.int32)
        return _ukey_to_float(jnp.where(di >= 0x8000, di << 16, (di << 16) | np.int32(0xFFFF)))

    key_hi = digit_search(fl_ref, 16, k_top, val16)
    h_lo = val16(key_hi)
    h_up = val16(key_hi + 1.0)
    no_thr = h_lo == -jnp.inf
    h_up = jnp.where(h_up > h_lo, h_up, np.float32(2.0 ** -126))
    width = jnp.where(no_thr, 1.0, h_up - h_lo)
    tiny = width < 2.0 ** -60
    pre = jnp.where(tiny, np.float32(2.0 ** 60), np.float32(1.0))
    unpre = jnp.where(tiny, np.float32(2.0 ** -60), np.float32(1.0))
    from_int = (width * pre) * np.float32(2.0 ** -16)
    to_int = lax.bitcast_convert_type(np.int32(0x7F000000) - lax.bitcast_convert_type(from_int, jnp.int32), F32)

    def split(i, carry):
        for half in range(2):
            k0 = pl.multiple_of((2 * i + half) * KC, KC)
            w = jnp.clip(((sc_ref[pl.ds(k0, KC), :] - h_lo) * pre) * to_int, -256.0, 65536.0)
            w_hi = jnp.floor(w * (1.0 / 256.0))
            whi_ref[pl.ds(k0, KC), :] = w_hi.astype(BF16)
            wlo_ref[pl.ds(k0, KC), :] = (w - 256.0 * w_hi).astype(BF16)
        return carry
    lax.fori_loop(0, npair, split, 0)

    ident = lambda d: d
    d_mid = digit_search(whi_ref, 8, k_top, ident)
    k_low = k_top - count_ge16(whi_ref, d_mid + 1.0)
    d_mid_b = d_mid.astype(BF16)

    def keep_low(i, carry):
        for half in range(2):
            k0 = pl.multiple_of((2 * i + half) * KC, KC)
            whi_ref[pl.ds(k0, KC), :] = jnp.where(whi_ref[pl.ds(k0, KC), :] == d_mid_b,
                                                  wlo_ref[pl.ds(k0, KC), :], -jnp.ones((), BF16))
        return carry
    lax.fori_loop(0, npair, keep_low, 0)
    d_low = digit_search(whi_ref, 8, k_low, ident)
    offset = ((d_mid * 256.0 + d_low) * from_int) * unpre
    thr = jnp.where(no_thr, -jnp.inf, h_lo + offset)

    m_ref[...] = jnp.full(m_ref.shape, NEG_BIG, F32)
    acc_ref[...] = jnp.zeros(acc_ref.shape, F32)
    r_i = lax.broadcasted_iota(jnp.int32, (KC, KC), 0)
    c_i = lax.broadcasted_iota(jnp.int32, (KC, KC), 1)
    tril = jnp.where(c_i <= r_i, 1.0, 0.0).astype(BF16)
    ones_rows = jnp.ones((DSA_VPAD, KC), BF16)

    HALF = GROUP * QB // 2

    def logits(c, slot, j, half):
        k0 = pl.multiple_of(c * KC, KC)
        cols = slice(half * HALF, (half + 1) * HALF)
        st_ref[slot, j, :, cols] = _dot(k_ref[pl.ds(k0, KC), j * LANE:(j + 1) * LANE],
                                        qa_ref[j, :, cols]).astype(BF16)

    for j in range(N_KV_HEADS):
        logits(0, 0, j, 0)
        logits(0, 0, j, 1)

    def attend(c, slot, tie_seen, c_next, need):
        k0 = pl.multiple_of(c * KC, KC)
        s = sc_ref[pl.ds(k0, KC), :]
        kpos = k0 + lax.broadcasted_iota(jnp.int32, (KC, 1), 0)
        eq = s == thr
        eqf = jnp.where(eq, 1.0, 0.0)
        incl = _dot(tril, eqf.astype(BF16))
        rank = tie_seen + incl - eqf
        sel = jnp.logical_and(jnp.logical_or(s > thr, jnp.logical_and(eq, rank < need)), kpos <= qpos)
        tie_seen = tie_seen + incl[KC - 1:KC, :]
        bias = jnp.where(sel, 0.0, NEG_BIG).astype(BF16)
        for j in range(N_KV_HEADS):
            alphas = []
            for g in range(GROUP):
                sl = slice(g * QB, (g + 1) * QB)
                sg = st_ref[slot, j, :, sl] + bias
                m_old = m_ref[j, :, sl]
                mx = jnp.max(_fold_rows(sg, jnp.maximum, 16), axis=0, keepdims=True).astype(F32)
                m_new = jnp.maximum(m_old, mx)
                alphas.append(jnp.exp2(m_old - m_new))
                m_ref[j, :, sl] = m_new
                p_ref[j, :, sl] = jnp.exp2(sg - m_new.astype(BF16))
                if g % 2 == 1:
                    logits(c_next, 1 - slot, j, g // 2)
            v_aug = jnp.concatenate([vT_ref[c, j * HEAD_DIM:(j + 1) * HEAD_DIM, :], ones_rows], axis=0)
            acc_ref[j] = acc_ref[j] * jnp.concatenate(alphas, axis=1) + _dot(v_aug, p_ref[j])
        return tie_seen

    need = k_top - (n_staged - count_not(lambda s: s > thr))

    def p3(i, tie_seen):
        c = 2 * i
        tie_seen = attend(c, 0, tie_seen, c + 1, need)
        return attend(c + 1, 1, tie_seen, clamp_chunk(c + 2), need)

    lax.fori_loop(0, npair, p3, jnp.zeros((1, QB), F32))

    for j in range(N_KV_HEADS):
        o = acc_ref[j, 0:HEAD_DIM, :] / acc_ref[j, HEAD_DIM:HEAD_DIM + 1, :]
        for g in range(GROUP):
            hh = j * GROUP + g
            oT_ref[hh * HEAD_DIM:(hh + 1) * HEAD_DIM, :] = o[:, g * QB:(g + 1) * QB].astype(BF16)


def _dsa_core(qT, k, vT, iqT, ik, iwT):
    B, QW, S = qT.shape
    KC, QB = DSA_KC, DSA_QB
    k_top = min(TOPK_MAX, S // 4)
    nc = S // KC
    return pl.pallas_call(
        functools.partial(_dsa_core_kernel, k_top=k_top, n_chunks=nc),
        grid=(B, S // QB),
        in_specs=[
            pl.BlockSpec((None, IDX_HEADS * IDX_DIM, QB), lambda b, i: (b, 0, i)),
            pl.BlockSpec((None, IDX_HEADS, QB), lambda b, i: (b, 0, i)),
            pl.BlockSpec((None, QW, QB), lambda b, i: (b, 0, i)),
            pl.BlockSpec((None, S, LANE), lambda b, i: (b, 0, 0)),
            pl.BlockSpec((None, S, N_KV_HEADS * LANE), lambda b, i: (b, 0, 0)),
            pl.BlockSpec((None, nc, N_KV_HEADS * HEAD_DIM, KC), lambda b, i: (b, 0, 0, 0)),
        ],
        out_specs=pl.BlockSpec((None, QW, QB), lambda b, i: (b, 0, i)),
        out_shape=jax.ShapeDtypeStruct((B, QW, S), BF16),
        scratch_shapes=[
            pltpu.VMEM((S, QB), F32),
            pltpu.VMEM((S, QB), BF16),
            pltpu.VMEM((S, QB), BF16),
            pltpu.VMEM((S, QB), BF16),
            pltpu.VMEM((N_KV_HEADS, LANE, GROUP * QB), BF16),
            pltpu.VMEM((LANE, IDX_HEADS * QB), BF16),
            pltpu.VMEM((2, KC, IDX_HEADS * QB), F32),
            pltpu.VMEM((2, N_KV_HEADS, KC, GROUP * QB), BF16),
            pltpu.VMEM((N_KV_HEADS, KC, GROUP * QB), BF16),
            pltpu.VMEM((N_KV_HEADS, 1, GROUP * QB), F32),
            pltpu.VMEM((N_KV_HEADS, HEAD_DIM + DSA_VPAD, GROUP * QB), F32),
        ],
        compiler_params=_cparams("parallel", "arbitrary"),
    )(iqT, iwT, qT, ik, k, vT)


def _out_T_kernel(xT_ref, w_ref, h_ref, gt_ref, o_ref):
    o_ref[...] = h_ref[...] + gt_ref[...] * _dot_tn(xT_ref[...], w_ref[...])


def _dsa_out(oT, w_out, h, mod, tm=512):
    B, S, D = h.shape
    QW = oT.shape[1]
    return pl.pallas_call(
        _out_T_kernel,
        grid=(B, S // tm),
        in_specs=[
            pl.BlockSpec((None, QW, tm), lambda b, i: (b, 0, i)),
            pl.BlockSpec((QW, D), lambda b, i: (0, 0)),
            pl.BlockSpec((None, tm, D), lambda b, i: (b, i, 0)),
            pl.BlockSpec((None, None, 1, D), lambda b, i: (b, 5, 0, 0)),
        ],
        out_specs=pl.BlockSpec((None, tm, D), lambda b, i: (b, i, 0)),
        out_shape=jax.ShapeDtypeStruct((B, S, D), F32),
        compiler_params=_cparams("parallel", "parallel"),
    )(oT, w_out.astype(BF16), h, mod)


def _dsa_mixer(h, mod, w_in, q_norm, k_norm, ik_norm, w_out):
    qT, k, vT, iqT, ik, iwT = _dsa_in(h, mod, w_in, q_norm, k_norm, ik_norm)
    oT = _dsa_core(qT, k, vT, iqT, ik, iwT)
    return _dsa_out(oT, w_out, h, mod)


RWKV_HEAD = 64
GN_EPS = 64e-5
RW_PACK = 4
RW_LANES = RW_PACK * RWKV_HEAD
RW_C = 64
RW_TB = 256


def _split3(x):
    hi = x.astype(BF16)
    r1 = x - hi.astype(F32)
    mid = r1.astype(BF16)
    lo = (r1 - mid.astype(F32)).astype(BF16)
    return hi, mid, lo


def _gsum(x, g_ref, gt_ref):
    hi = x.astype(BF16)
    lo = (x - hi.astype(F32)).astype(BF16)
    s = _dot(jnp.concatenate([hi, lo], axis=1), g_ref[...])
    return _dot(jnp.concatenate(_split3(s), axis=1), gt_ref[...])


def _rwkv_in_kernel(h_ref, hp_ref, sh_ref, sc_ref, mu_ref, wrkv_ref, w0_ref, w1_ref, w2_ref,
                    a0_ref, a1_ref, a2_ref, g1_ref, g2_ref, kk_ref, ka_ref, gm_ref, gmt_ref,
                    r_ref, lw_ref, k_ref, v_ref, kkn_ref, b_ref, g_ref):
    sh, sc = sh_ref[...], sc_ref[...]
    u = _modulate(h_ref[...], sh, sc)
    tm = u.shape[0]
    prev = _modulate(hp_ref[...], sh, sc)[7:8, :]
    prev = jnp.where(pl.program_id(1) == 0, 0.0, prev)
    row = lax.broadcasted_iota(jnp.int32, (tm, 1), 0)
    xx = jnp.where(row == 0, prev, pltpu.roll(u, 1, 0)) - u
    mix = lambda i: (u + xx * mu_ref[i:i + 1, :]).astype(BF16)
    wl = w0_ref[...] + _dot(jnp.tanh(_dot(mix(3), w1_ref[...])).astype(BF16), w2_ref[...])
    a_lin = _dot(_dot(mix(4), a1_ref[...]).astype(BF16), a2_ref[...])
    g_hid = _dot(mix(5), g1_ref[...])
    k = _dot(mix(1), wrkv_ref[1])
    nwl = -wl
    w_log = -(jnp.maximum(nwl, 0.0) + jnp.log(1.0 + jnp.exp(-jnp.abs(nwl)))) - 0.5
    lw_ref[...] = -jnp.exp(w_log)
    r_ref[...] = _dot(mix(0), wrkv_ref[0])
    a = jax.nn.sigmoid(a0_ref[...] + a_lin)
    g_ref[...] = _dot(jax.nn.sigmoid(g_hid).astype(BF16), g2_ref[...])
    kk = k * kk_ref[...]
    nrm = jnp.sqrt(_gsum(kk * kk, gm_ref, gmt_ref))
    kk = kk / jnp.maximum(nrm, 1e-12)
    k_ref[...] = k * (1.0 + (a - 1.0) * ka_ref[...])
    kkn_ref[...] = kk
    b_ref[...] = kk * a
    v_ref[...] = _dot(mix(2), wrkv_ref[2])


def _head_indicator(D):
    gm = (np.arange(D)[:, None] // RWKV_HEAD == np.arange(D // RWKV_HEAD)[None, :]).astype(np.float32)
    return jnp.asarray(np.concatenate([gm, gm], axis=0), BF16), jnp.asarray(np.concatenate([gm.T] * 3, axis=0), BF16)


def _rwkv_in(h, mod, mu, w_rkv, w0, w1, w2, a0, a1, a2, g1, g2, k_k, k_a, tm=256):
    B, S, D = h.shape
    gm, gmt = _head_indicator(D)
    row = lambda a: a.astype(F32).reshape(1, D)
    full = lambda a: pl.BlockSpec(a.shape, lambda b, i: (0,) * a.ndim)
    mspec = lambda k: pl.BlockSpec((None, None, 1, D), lambda b, i: (b, k, 0, 0))
    tile = pl.BlockSpec((None, tm, D), lambda b, i: (b, i, 0))
    params = [mu.astype(F32), w_rkv.astype(BF16), row(w0), w1.astype(BF16), w2.astype(BF16),
              row(a0), a1.astype(BF16), a2.astype(BF16), g1.astype(BF16), g2.astype(BF16),
              row(k_k), row(k_a), gm, gmt]
    return pl.pallas_call(
        _rwkv_in_kernel,
        grid=(B, S // tm),
        in_specs=[tile,
                  pl.BlockSpec((None, 8, D), lambda b, i: (b, jnp.maximum(i * (tm // 8) - 1, 0), 0)),
                  mspec(3), mspec(4)] + [full(p) for p in params],
        out_specs=[tile] * 7,
        out_shape=[jax.ShapeDtypeStruct((B, S, D), F32)] * 7,
        compiler_params=_cparams("parallel", "parallel"),
    )(h, h, mod, mod, *params)


def _rwkv_scan_kernel(r_ref, lw_ref, k_ref, v_ref, kk_ref, b_ref, y_ref, zt_ref):
    C, P, L = RW_C, RW_PACK, RW_LANES
    N = P * C
    assert C == RWKV_HEAD
    n_packs = r_ref.shape[1] // L
    n_chunks = r_ref.shape[0] // C
    units = [(p, c) for p in range(n_packs) for c in range(n_chunks)]

    @pl.when(pl.program_id(1) == 0)
    def _():
        zt_ref[...] = jnp.zeros(zt_ref.shape, F32)

    lane = lax.broadcasted_iota(jnp.int32, (1, L), 1)
    hmask = [lane // C == hd for hd in range(P)]
    bf = lambda x: x.astype(BF16)
    stack = lambda x: jnp.concatenate([jnp.where(hmask[hd], x, jnp.zeros_like(x)) for hd in range(P)], axis=0)
    t_i = lax.broadcasted_iota(jnp.int32, (C, N), 0)
    i_i = lax.broadcasted_iota(jnp.int32, (C, N), 1) % C
    low_strict = t_i > i_i
    low_incl = t_i >= i_i
    eye_w = jnp.where(t_i == i_i, 1.0, 0.0)
    bd_mask = (lax.broadcasted_iota(jnp.int32, (L, L), 0) // C) == (lax.broadcasted_iota(jnp.int32, (L, L), 1) // C)
    tri = jnp.where(lax.broadcasted_iota(jnp.int32, (C, C), 0) >= lax.broadcasted_iota(jnp.int32, (C, C), 1),
                    1.0, 0.0).astype(BF16)
    tile = lambda ref, u: ref[u[1] * C:(u[1] + 1) * C, u[0] * L:(u[0] + 1) * L]

    cw = {}
    tri3 = jnp.concatenate([tri, tri, tri], axis=1)
    for u in units:
        cw[u] = _dot(tri3, jnp.concatenate(_split3(tile(lw_ref, u)), axis=0))
    al, rt, bw, kw, v, vs, aa, w_end = {}, {}, {}, {}, {}, {}, {}, {}
    for u in units:
        lw = tile(lw_ref, u)
        w_incl = jnp.exp(cw[u])
        w_excl = jnp.exp(cw[u] - lw)
        w_inv = jnp.exp(-cw[u])
        w_end[u] = w_incl[C - 1:C, :]
        v[u] = bf(tile(v_ref, u))
        vs[u] = stack(v[u])
        al[u] = bf(-tile(kk_ref, u) * w_excl)
        rt[u] = tile(r_ref, u) * w_incl
        bh = tile(b_ref, u) * w_inv
        kh = tile(k_ref, u) * w_inv
        bw[u] = bf(bh * w_end[u])
        kw[u] = bf(kh * w_end[u])
        aa[u] = _dot_nt(jnp.concatenate([al[u], bf(rt[u])], axis=0),
                        jnp.concatenate([stack(bf(bh)), stack(bf(kh))], axis=0))
    a_k, a_rb, pw, t = {}, {}, {}, {}
    for u in units:
        a_ab = jnp.where(low_strict, aa[u][:C, :N], 0.0)
        a_k[u] = bf(jnp.concatenate([jnp.where(low_strict, aa[u][:C, N:], 0.0),
                                     jnp.where(low_incl, aa[u][C:, N:], 0.0)], axis=0))
        a_rb[u] = bf(jnp.where(low_incl, aa[u][C:, :N], 0.0))
        pw[u] = a_ab
        t[u] = eye_w + a_ab
    n_fac = int(np.log2(C))
    akv, rkv = {}, {}
    for u in units:
        pwb = bf(pw[u])
        pw[u] = _dot(pwb, stack(pwb))
        both_v = _dot(a_k[u], vs[u])
        akv[u], rkv[u] = both_v[:C], both_v[C:]
    for s in range(2, n_fac + 1):
        for u in units:
            pwb = bf(pw[u])
            rhs = stack(pwb)
            if s < n_fac:
                both = _dot(jnp.concatenate([bf(t[u]), pwb], axis=0), rhs)
                t[u] = t[u] + both[:C]
                pw[u] = both[C:]
            else:
                t[u] = t[u] + _dot(bf(t[u]), rhs)
    pm, q = {}, {}
    for u in units:
        tb = bf(t[u])
        pm[u] = bf(_dot(tb, stack(al[u])))
        q[u] = bf(_dot(tb, stack(bf(akv[u]))))
    pr, y0, bk = {}, {}, {}
    for u in units:
        rp = bf(rt[u] + _dot(a_rb[u], stack(pm[u])))
        pr[u] = jnp.concatenate([pm[u], rp], axis=0)
        y0[u] = _dot(a_rb[u], stack(q[u])) + rkv[u]
        bk[u] = jnp.concatenate([bw[u], kw[u]], axis=0)
    zt = [zt_ref[p] for p in range(n_packs)]
    for c in range(n_chunks):
        for p in range(n_packs):
            u = (p, c)
            uy = _dot_nt(pr[u], bf(zt[p]))
            y_ref[c * C:(c + 1) * C, p * L:(p + 1) * L] = uy[C:] + y0[u]
            upd = _dot_tn(jnp.concatenate([bf(uy[:C] + q[u]), v[u]], axis=0), bk[u])
            zt[p] = zt[p] * w_end[u] + jnp.where(bd_mask, upd, 0.0)
    for p in range(n_packs):
        zt_ref[p] = zt[p]


def _rwkv_scan(r, lw, k, v, kk, b):
    B, S, D = r.shape
    tile = pl.BlockSpec((None, RW_TB, D), lambda bb, t: (bb, t, 0))
    return pl.pallas_call(
        _rwkv_scan_kernel,
        grid=(B, S // RW_TB),
        in_specs=[tile] * 6,
        out_specs=tile,
        out_shape=jax.ShapeDtypeStruct((B, S, D), F32),
        scratch_shapes=[pltpu.VMEM((D // RW_LANES, RW_LANES, RW_LANES), F32)],
        compiler_params=_cparams("parallel", "arbitrary"),
    )(r, lw, k, v, kk, b)


def _rwkv_out_kernel(y_ref, r_ref, k_ref, v_ref, g_ref, lnw_ref, lnb_ref, rk_ref, gm_ref, gmt_ref,
                     w_ref, h_ref, gt_ref, o_ref):
    y = y_ref[...]
    inv_n = 1.0 / RWKV_HEAD
    yc = y - _gsum(y, gm_ref, gmt_ref) * inv_n
    var = _gsum(yc * yc, gm_ref, gmt_ref) * inv_n
    yn = yc * lax.rsqrt(var + GN_EPS) * lnw_ref[...] + lnb_ref[...]
    v = v_ref[...]
    bonus = _gsum(r_ref[...] * k_ref[...] * rk_ref[...], gm_ref, gmt_ref) * v
    out = ((yn + bonus) * g_ref[...]).astype(BF16)
    o_ref[...] = h_ref[...] + gt_ref[...] * _dot(out, w_ref[...])


def _rwkv_out(y, r, k, v, g, ln_w, ln_b, r_k, w_out, h, mod, tm=256):
    B, S, D = h.shape
    gm, gmt = _head_indicator(D)
    row = lambda a: a.astype(F32).reshape(1, D)
    full = lambda a: pl.BlockSpec(a.shape, lambda b, i: (0,) * a.ndim)
    tile = pl.BlockSpec((None, tm, D), lambda b, i: (b, i, 0))
    params = [row(ln_w), row(ln_b), row(r_k), gm, gmt, w_out.astype(BF16)]
    return pl.pallas_call(
        _rwkv_out_kernel,
        grid=(B, S // tm),
        in_specs=[tile] * 5 + [full(p) for p in params] + [tile, pl.BlockSpec((None, None, 1, D), lambda b, i: (b, 5, 0, 0))],
        out_specs=tile,
        out_shape=jax.ShapeDtypeStruct((B, S, D), F32),
        compiler_params=_cparams("parallel", "parallel"),
    )(y, r, k, v, g, *params, h, mod)


def _rwkv_mixer(h, mod, mu, w_rkv, w0, w1, w2, a0, a1, a2, g1, g2, k_k, k_a, r_k, ln_w, ln_b, w_out):
    r, lw, k, v, kk, b, g = _rwkv_in(h, mod, mu, w_rkv, w0, w1, w2, a0, a1, a2, g1, g2, k_k, k_a)
    y = _rwkv_scan(r, lw, k, v, kk, b)
    return _rwkv_out(y, r, k, v, g, ln_w, ln_b, r_k, w_out, h, mod)


def kernel(x, c, ada_w, ada_b, ffn_w_gu, ffn_w_down, dsa_w_in, dsa_q_norm, dsa_k_norm, dsa_ik_norm, dsa_w_out, rwkv_mu, rwkv_w_rkv, rwkv_w0, rwkv_w1, rwkv_w2, rwkv_a0, rwkv_a1, rwkv_a2, rwkv_g1, rwkv_g2, rwkv_k_k, rwkv_k_a, rwkv_r_k, rwkv_ln_w, rwkv_ln_b, rwkv_w_out):
    B, S, D = x.shape
    depth = ada_w.shape[0]
    mods = _ada_mod(c, ada_w, ada_b)
    h = x
    for i in range(depth):
        mod = mods[i].reshape(B, N_MOD, 1, D)
        j = i // 2
        h = _ffn(h, mod, 0, ffn_w_gu[i, 0].astype(BF16), ffn_w_down[i, 0].astype(BF16))
        if i % 2 == 0:
            h = _dsa_mixer(h, mod, dsa_w_in[j], dsa_q_norm[j], dsa_k_norm[j], dsa_ik_norm[j], dsa_w_out[j])
        else:
            h = _rwkv_mixer(h, mod, rwkv_mu[j], rwkv_w_rkv[j], rwkv_w0[j], rwkv_w1[j], rwkv_w2[j], rwkv_a0[j],
                            rwkv_a1[j], rwkv_a2[j], rwkv_g1[j], rwkv_g2[j], rwkv_k_k[j], rwkv_k_a[j],
                            rwkv_r_k[j], rwkv_ln_w[j], rwkv_ln_b[j], rwkv_w_out[j])
        h = _ffn(h, mod, 6, ffn_w_gu[i, 1].astype(BF16), ffn_w_down[i, 1].astype(BF16))
    return h
```

```python
import functools

import jax
import jax.numpy as jnp
import numpy as np
from jax import lax
from jax.experimental import pallas as pl
from jax.experimental.pallas import tpu as pltpu

F32 = jnp.float32
BF16 = jnp.bfloat16

RMS_EPS = 1e-6
N_MOD = 9
VMEM_LIMIT_BYTES = 56 * 1024 * 1024

N_HEADS = 16
HEAD_DIM = 64
N_KV_HEADS = 4
GROUP = N_HEADS // N_KV_HEADS
IDX_HEADS = 8
IDX_DIM = 64
TOPK_MAX = 256
LANE = 128
LOG2E = 1.4426950408889634


def _cparams(*sem):
    return pltpu.CompilerParams(dimension_semantics=sem, vmem_limit_bytes=VMEM_LIMIT_BYTES)


def _dot(a, b):
    return jnp.dot(a, b, preferred_element_type=F32)


def _dot_nt(a, b):
    return lax.dot_general(a, b, (((1,), (1,)), ((), ())), preferred_element_type=F32)


def _dot_tn(a, b):
    return lax.dot_general(a, b, (((0,), (0,)), ((), ())), preferred_element_type=F32)


def _modulate(h, sh, sc):
    ms = jnp.mean(h * h, axis=-1, keepdims=True)
    return h * lax.rsqrt(ms + RMS_EPS) * (1.0 + sc) + sh


def _ada_kernel(c_ref, w_ref, b_ref, o_ref):
    c = c_ref[...]
    ca = (c * jax.nn.sigmoid(c)).astype(BF16)
    o_ref[...] = _dot(ca, w_ref[...].astype(BF16)) + b_ref[...]


def _ada_mod(c, ada_w, ada_b):
    L, D, N = ada_w.shape
    B = c.shape[0]
    tn = 1536
    return pl.pallas_call(
        _ada_kernel,
        grid=(L, N // tn),
        in_specs=[
            pl.BlockSpec((B, D), lambda l, j: (0, 0)),
            pl.BlockSpec((None, D, tn), lambda l, j: (l, 0, j)),
            pl.BlockSpec((None, 1, tn), lambda l, j: (l, 0, j)),
        ],
        out_specs=pl.BlockSpec((None, B, tn), lambda l, j: (l, 0, j)),
        out_shape=jax.ShapeDtypeStruct((L, B, N), F32),
        compiler_params=_cparams("parallel", "parallel"),
    )(c, ada_w, ada_b.reshape(L, 1, N))


FF_CHUNK = 256


def _ffn_kernel(h_ref, sh_ref, sc_ref, gt_ref, wgu_ref, wd_ref, o_ref, *, d_ff):
    h = h_ref[...]
    u = _modulate(h, sh_ref[...], sc_ref[...]).astype(BF16)
    acc = None
    for c in range(d_ff // FF_CHUNK):
        lo = c * FF_CHUNK
        g = _dot(u, wgu_ref[:, lo:lo + FF_CHUNK])
        up = _dot(u, wgu_ref[:, d_ff + lo:d_ff + lo + FF_CHUNK])
        a = (g * jax.nn.sigmoid(g) * up).astype(BF16)
        part = _dot(a, wd_ref[lo:lo + FF_CHUNK, :])
        acc = part if acc is None else acc + part
    o_ref[...] = h + (0.5 * gt_ref[...]) * acc


def _ffn(h, mod, k0, w_gu, w_down, tm=512):
    B, S, D = h.shape
    d_ff = w_down.shape[0]
    mspec = lambda k: pl.BlockSpec((None, None, 1, D), lambda b, i: (b, k, 0, 0))
    return pl.pallas_call(
        functools.partial(_ffn_kernel, d_ff=d_ff),
        grid=(B, S // tm),
        in_specs=[
            pl.BlockSpec((None, tm, D), lambda b, i: (b, i, 0)),
            mspec(k0), mspec(k0 + 1), mspec(k0 + 2),
            pl.BlockSpec((D, 2 * d_ff), lambda b, i: (0, 0), pipeline_mode=pl.Buffered(1)),
            pl.BlockSpec((d_ff, D), lambda b, i: (0, 0), pipeline_mode=pl.Buffered(1)),
        ],
        out_specs=pl.BlockSpec((None, tm, D), lambda b, i: (b, i, 0)),
        out_shape=jax.ShapeDtypeStruct((B, S, D), F32),
        compiler_params=_cparams("parallel", "parallel"),
    )(h, mod, mod, mod, w_gu, w_down)


def _dsa_in_kernel(h_ref, sh_ref, sc_ref, wT_ref, wn_ref, qn_ref, kn_ref, ikn_ref,
                   qT_ref, k_ref, vT_ref, iqT_ref, ik_ref, iwT_ref):
    u = _modulate(h_ref[...], sh_ref[...], sc_ref[...]).astype(BF16)
    tm = u.shape[0]
    QW, KW, IQW = N_HEADS * HEAD_DIM, N_KV_HEADS * HEAD_DIM, IDX_HEADS * IDX_DIM
    yT = _dot_nt(wT_ref[...], u)
    qT = yT[0:QW].reshape(N_HEADS, HEAD_DIM, tm)
    ms = jnp.mean(qT * qT, axis=1, keepdims=True)
    qT = qT * lax.rsqrt(ms + RMS_EPS) * qn_ref[...][None] * (HEAD_DIM ** -0.5 * LOG2E)
    qT_ref[...] = qT.reshape(QW, tm).astype(BF16)
    vT_ref[...] = yT[QW:QW + KW].astype(BF16)
    iqT_ref[...] = yT[QW + KW:QW + KW + IQW].astype(BF16)
    iwT_ref[...] = yT[QW + KW + IQW:QW + KW + IQW + IDX_HEADS] * (IDX_HEADS ** -0.5 * IDX_DIM ** -0.5)
    y = _dot(u, wn_ref[...])
    kn = kn_ref[...]
    for j in range(N_KV_HEADS):
        kj = y[:, j * LANE:(j + 1) * LANE]
        ms = jnp.sum(kj * kj, axis=-1, keepdims=True) * (1.0 / HEAD_DIM)
        k_ref[:, j * LANE:(j + 1) * LANE] = (kj * lax.rsqrt(ms + RMS_EPS) * kn).astype(BF16)
    ik = y[:, N_KV_HEADS * LANE:]
    ms = jnp.sum(ik * ik, axis=-1, keepdims=True) * (1.0 / IDX_DIM)
    ik_ref[...] = (ik * lax.rsqrt(ms + RMS_EPS) * ikn_ref[...]).astype(BF16)


DSA_KC = 256
DSA_QB = 256


def _dsa_in(h, mod, w_in, q_norm, k_norm, ik_norm, tm=DSA_KC):
    B, S, D = h.shape
    QW, KW = N_HEADS * HEAD_DIM, N_KV_HEADS * HEAD_DIM
    IQW = IDX_HEADS * IDX_DIM
    o1, o2, o3, o4, o5 = QW, QW + KW, QW + 2 * KW, QW + 2 * KW + IQW, QW + 2 * KW + IQW + IDX_DIM
    wb = w_in.astype(BF16)
    wqT = wb[:, :o1].T
    wk = jnp.pad(wb[:, o1:o2].reshape(D, N_KV_HEADS, HEAD_DIM), ((0, 0), (0, 0), (0, LANE - HEAD_DIM))).reshape(D, N_KV_HEADS * LANE)
    wvT = wb[:, o2:o3].T
    wiqT = wb[:, o3:o4].T
    wik = jnp.pad(wb[:, o4:o5], ((0, 0), (0, LANE - IDX_DIM)))
    wiwT = jnp.pad(wb[:, o5:].T, ((0, 16 - IDX_HEADS), (0, 0)))
    wT = jnp.concatenate([wqT, wvT, wiqT, wiwT], axis=0)
    wn = jnp.concatenate([wk, wik], axis=1)
    qn = q_norm.astype(F32).reshape(HEAD_DIM, 1)
    kn = jnp.pad(k_norm.astype(F32), (0, LANE - HEAD_DIM)).reshape(1, LANE)
    ikn = jnp.pad(ik_norm.astype(F32), (0, LANE - IDX_DIM)).reshape(1, LANE)
    mspec = lambda k: pl.BlockSpec((None, None, 1, D), lambda b, i: (b, k, 0, 0))
    full = lambda a: pl.BlockSpec(a.shape, lambda b, i: (0,) * a.ndim)
    nc = S // tm
    return pl.pallas_call(
        _dsa_in_kernel,
        grid=(B, nc),
        in_specs=[pl.BlockSpec((None, tm, D), lambda b, i: (b, i, 0)), mspec(3), mspec(4),
                  full(wT), full(wn), full(qn), full(kn), full(ikn)],
        out_specs=[
            pl.BlockSpec((None, QW, tm), lambda b, i: (b, 0, i)),
            pl.BlockSpec((None, tm, N_KV_HEADS * LANE), lambda b, i: (b, i, 0)),
            pl.BlockSpec((None, None, KW, tm), lambda b, i: (b, i, 0, 0)),
            pl.BlockSpec((None, IQW, tm), lambda b, i: (b, 0, i)),
            pl.BlockSpec((None, tm, LANE), lambda b, i: (b, i, 0)),
            pl.BlockSpec((None, IDX_HEADS, tm), lambda b, i: (b, 0, i)),
        ],
        out_shape=[
            jax.ShapeDtypeStruct((B, QW, S), BF16),
            jax.ShapeDtypeStruct((B, S, N_KV_HEADS * LANE), BF16),
            jax.ShapeDtypeStruct((B, nc, KW, tm), BF16),
            jax.ShapeDtypeStruct((B, IQW, S), BF16),
            jax.ShapeDtypeStruct((B, S, LANE), BF16),
            jax.ShapeDtypeStruct((B, IDX_HEADS, S), F32),
        ],
        compiler_params=_cparams("parallel", "parallel"),
    )(h, mod, mod, wT, wn, qn, kn, ikn)


NEG_BIG = -(2.0 ** 100)
DSA_VPAD = 16
N_CHAINS = 4


def _ukey_to_float(ukey):
    key = ukey ^ np.int32(-2 ** 31)
    bits = key ^ ((key >> 31) & np.int32(0x7FFFFFFF))
    return lax.bitcast_convert_type(bits, F32)


def _fold_rows(x, op, rows):
    while x.shape[0] > rows:
        half = x.shape[0] // 2
        x = op(x[:half], x[half:])
    return x


def _dsa_core_kernel(iqT_ref, iwT_ref, qT_ref, ik_ref, k_ref, vT_ref, oT_ref,
                     sc_ref, fl_ref, whi_ref, wlo_ref, qa_ref, iqa_ref, d_ref, st_ref, p_ref, m_ref, acc_ref,
                     *, k_top, n_chunks):
    KC, QB = DSA_KC, DSA_QB
    qi = pl.program_id(1)
    q0 = qi * QB
    nk = (q0 + QB + KC - 1) // KC
    qpos = q0 + lax.broadcasted_iota(jnp.int32, (1, QB), 1)
    zeros_h = jnp.zeros((LANE - HEAD_DIM, QB), BF16)

    for h in range(IDX_HEADS):
        iqa_ref[:, h * QB:(h + 1) * QB] = jnp.concatenate(
            [iqT_ref[h * IDX_DIM:(h + 1) * IDX_DIM, :], zeros_h], axis=0)
    for j in range(N_KV_HEADS):
        for g in range(GROUP):
            hh = j * GROUP + g
            qa_ref[j, :, g * QB:(g + 1) * QB] = jnp.concatenate(
                [qT_ref[hh * HEAD_DIM:(hh + 1) * HEAD_DIM, :], zeros_h], axis=0)
    iw = iwT_ref[...]

    npair = (nk + 1) // 2

    def clamp_chunk(c):
        return jnp.minimum(c, n_chunks - 1)

    def idx_dot(c, slot):
        k0 = pl.multiple_of(c * KC, KC)
        d_ref[slot] = _dot(ik_ref[pl.ds(k0, KC), :], iqa_ref[...])

    def idx_score(c, slot):
        k0 = pl.multiple_of(c * KC, KC)
        s = jnp.maximum(d_ref[slot, :, 0:QB], 0.0) * iw[0:1, :]
        for h in range(1, IDX_HEADS):
            s = s + jnp.maximum(d_ref[slot, :, h * QB:(h + 1) * QB], 0.0) * iw[h:h + 1, :]
        kpos = k0 + lax.broadcasted_iota(jnp.int32, (KC, 1), 0)
        s = jnp.where(kpos <= qpos, s, -jnp.inf)
        sc_ref[pl.ds(k0, KC), :] = s
        r = s.astype(BF16)
        rf = r.astype(F32)
        below = (rf * jnp.where(rf > 0.0, 1.0 - 1.25 * 2.0 ** -8, 1.0 + 1.25 * 2.0 ** -8)).astype(BF16)
        fl_ref[pl.ds(k0, KC), :] = jnp.where(rf > s, below, r)

    idx_dot(0, 0)

    def p1(i, carry):
        c = 2 * i
        idx_dot(c + 1, 1)
        idx_score(c, 0)
        idx_dot(clamp_chunk(c + 2), 0)
        idx_score(c + 1, 1)
        return carry

    lax.fori_loop(0, npair, p1, 0)

    def count_not(pred):
        def body(i, accs):
            accs = list(accs)
            for half in range(2):
                k0 = pl.multiple_of((2 * i + half) * KC, KC)
                ind = jnp.where(pred(sc_ref[pl.ds(k0, KC), :]), 0.0, 1.0)
                for r in range(KC // 8):
                    accs[r % N_CHAINS] = accs[r % N_CHAINS] + ind[r * 8:(r + 1) * 8]
            return tuple(accs)
        accs = lax.fori_loop(0, npair, body, tuple(jnp.zeros((8, QB), F32) for _ in range(N_CHAINS)))
        acc = (accs[0] + accs[1]) + (accs[2] + accs[3])
        return jnp.sum(acc, axis=0, keepdims=True)

    n_staged = (npair * (2 * KC)).astype(F32)

    def count_ge16(ref, t):
        tb = t.astype(BF16)
        one, zero = jnp.ones((), BF16), jnp.zeros((), BF16)

        def body(i, accs):
            accs = list(accs)
            for half in range(2):
                k0 = pl.multiple_of((2 * i + half) * KC, KC)
                ind = jnp.where(ref[pl.ds(k0, KC), :] < tb, zero, one)
                for r in range(KC // 16):
                    accs[r % N_CHAINS] = accs[r % N_CHAINS] + ind[r * 16:(r + 1) * 16]
            return tuple(accs)
        accs = lax.fori_loop(0, npair, body, tuple(jnp.zeros((16, QB), BF16) for _ in range(N_CHAINS)))
        acc = (accs[0].astype(F32) + accs[1].astype(F32)) + (accs[2].astype(F32) + accs[3].astype(F32))
        return jnp.sum(acc, axis=0, keepdims=True)

    def digit_search(ref, n_bits, k_need, to_float):
        d = jnp.zeros((1, QB), F32)
        for bit in range(n_bits - 1, -1, -1):
            cand = d + float(1 << bit)
            d = jnp.where(count_ge16(ref, to_float(cand)) >= k_need, cand, d)
        return d

    def val16(d):
        di = d.astype(jnp.int32)
        return _ukey_to_float(jnp.where(di >= 0x8000, di << 16, (di << 16) | np.int32(0xFFFF)))

    key_hi = digit_search(fl_ref, 16, k_top, val16)
    h_lo = val16(key_hi)
    h_up = val16(key_hi + 1.0)
    no_thr = h_lo == -jnp.inf
    h_up = jnp.where(h_up > h_lo, h_up, np.float32(2.0 ** -126))
    width = jnp.where(no_thr, 1.0, h_up - h_lo)
    tiny = width < 2.0 ** -60
    pre = jnp.where(tiny, np.float32(2.0 ** 60), np.float32(1.0))
    unpre = jnp.where(tiny, np.float32(2.0 ** -60), np.float32(1.0))
    from_int = (width * pre) * np.float32(2.0 ** -16)
    to_int = lax.bitcast_convert_type(np.int32(0x7F000000) - lax.bitcast_convert_type(from_int, jnp.int32), F32)

    def split(i, carry):
        for half in range(2):
            k0 = pl.multiple_of((2 * i + half) * KC, KC)
            w = jnp.clip(((sc_ref[pl.ds(k0, KC), :] - h_lo) * pre) * to_int, -256.0, 65536.0)
            w_hi = jnp.floor(w * (1.0 / 256.0))
            whi_ref[pl.ds(k0, KC), :] = w_hi.astype(BF16)
            wlo_ref[pl.ds(k0, KC), :] = (w - 256.0 * w_hi).astype(BF16)
        return carry
    lax.fori_loop(0, npair, split, 0)

    ident = lambda d: d
    d_mid = digit_search(whi_ref, 8, k_top, ident)
    k_low = k_top - count_ge16(whi_ref, d_mid + 1.0)
    d_mid_b = d_mid.astype(BF16)

    def keep_low(i, carry):
        for half in range(2):
            k0 = pl.multiple_of((2 * i + half) * KC, KC)
            whi_ref[pl.ds(k0, KC), :] = jnp.where(whi_ref[pl.ds(k0, KC), :] == d_mid_b,
                                                  wlo_ref[pl.ds(k0, KC), :], -jnp.ones((), BF16))
        return carry
    lax.fori_loop(0, npair, keep_low, 0)
    d_low = digit_search(whi_ref, 8, k_low, ident)
    offset = ((d_mid * 256.0 + d_low) * from_int) * unpre
    thr = jnp.where(no_thr, -jnp.inf, h_lo + offset)

    m_ref[...] = jnp.full(m_ref.shape, NEG_BIG, F32)
    acc_ref[...] = jnp.zeros(acc_ref.shape, F32)
    r_i = lax.broadcasted_iota(jnp.int32, (KC, KC), 0)
    c_i = lax.broadcasted_iota(jnp.int32, (KC, KC), 1)
    tril = jnp.where(c_i <= r_i, 1.0, 0.0).astype(BF16)
    ones_rows = jnp.ones((DSA_VPAD, KC), BF16)

    HALF = GROUP * QB // 2

    def logits(c, slot, j, half):
        k0 = pl.multiple_of(c * KC, KC)
        cols = slice(half * HALF, (half + 1) * HALF)
        st_ref[slot, j, :, cols] = _dot(k_ref[pl.ds(k0, KC), j * LANE:(j + 1) * LANE],
                                        qa_ref[j, :, cols]).astype(BF16)

    for j in range(N_KV_HEADS):
        logits(0, 0, j, 0)
        logits(0, 0, j, 1)

    def attend(c, slot, tie_seen, c_next, need):
        k0 = pl.multiple_of(c * KC, KC)
        s = sc_ref[pl.ds(k0, KC), :]
        kpos = k0 + lax.broadcasted_iota(jnp.int32, (KC, 1), 0)
        eq = s == thr
        eqf = jnp.where(eq, 1.0, 0.0)
        incl = _dot(tril, eqf.astype(BF16))
        rank = tie_seen + incl - eqf
        sel = jnp.logical_and(jnp.logical_or(s > thr, jnp.logical_and(eq, rank < need)), kpos <= qpos)
        tie_seen = tie_seen + incl[KC - 1:KC, :]
        bias = jnp.where(sel, 0.0, NEG_BIG).astype(BF16)
        for j in range(N_KV_HEADS):
            alphas = []
            for g in range(GROUP):
                sl = slice(g * QB, (g + 1) * QB)
                sg = st_ref[slot, j, :, sl] + bias
                m_old = m_ref[j, :, sl]
                mx = jnp.max(_fold_rows(sg, jnp.maximum, 16), axis=0, keepdims=True).astype(F32)
                m_new = jnp.maximum(m_old, mx)
                alphas.append(jnp.exp2(m_old - m_new))
                m_ref[j, :, sl] = m_new
                p_ref[j, :, sl] = jnp.exp2(sg - m_new.astype(BF16))
                if g % 2 == 1:
                    logits(c_next, 1 - slot, j, g // 2)
            v_aug = jnp.concatenate([vT_ref[c, j * HEAD_DIM:(j + 1) * HEAD_DIM, :], ones_rows], axis=0)
            acc_ref[j] = acc_ref[j] * jnp.concatenate(alphas, axis=1) + _dot(v_aug, p_ref[j])
        return tie_seen

    need = k_top - (n_staged - count_not(lambda s: s > thr))

    def p3(i, tie_seen):
        c = 2 * i
        tie_seen = attend(c, 0, tie_seen, c + 1, need)
        return attend(c + 1, 1, tie_seen, clamp_chunk(c + 2), need)

    lax.fori_loop(0, npair, p3, jnp.zeros((1, QB), F32))

    for j in range(N_KV_HEADS):
        o = acc_ref[j, 0:HEAD_DIM, :] / acc_ref[j, HEAD_DIM:HEAD_DIM + 1, :]
        for g in range(GROUP):
            hh = j * GROUP + g
            oT_ref[hh * HEAD_DIM:(hh + 1) * HEAD_DIM, :] = o[:, g * QB:(g + 1) * QB].astype(BF16)


def _dsa_core(qT, k, vT, iqT, ik, iwT):
    B, QW, S = qT.shape
    KC, QB = DSA_KC, DSA_QB
    k_top = min(TOPK_MAX, S // 4)
    nc = S // KC
    return pl.pallas_call(
        functools.partial(_dsa_core_kernel, k_top=k_top, n_chunks=nc),
        grid=(B, S // QB),
        in_specs=[
            pl.BlockSpec((None, IDX_HEADS * IDX_DIM, QB), lambda b, i: (b, 0, i)),
            pl.BlockSpec((None, IDX_HEADS, QB), lambda b, i: (b, 0, i)),
            pl.BlockSpec((None, QW, QB), lambda b, i: (b, 0, i)),
            pl.BlockSpec((None, S, LANE), lambda b, i: (b, 0, 0)),
            pl.BlockSpec((None, S, N_KV_HEADS * LANE), lambda b, i: (b, 0, 0)),
            pl.BlockSpec((None, nc, N_KV_HEADS * HEAD_DIM, KC), lambda b, i: (b, 0, 0, 0)),
        ],
        out_specs=pl.BlockSpec((None, QW, QB), lambda b, i: (b, 0, i)),
        out_shape=jax.ShapeDtypeStruct((B, QW, S), BF16),
        scratch_shapes=[
            pltpu.VMEM((S, QB), F32),
            pltpu.VMEM((S, QB), BF16),
            pltpu.VMEM((S, QB), BF16),
            pltpu.VMEM((S, QB), BF16),
            pltpu.VMEM((N_KV_HEADS, LANE, GROUP * QB), BF16),
            pltpu.VMEM((LANE, IDX_HEADS * QB), BF16),
            pltpu.VMEM((2, KC, IDX_HEADS * QB), F32),
            pltpu.VMEM((2, N_KV_HEADS, KC, GROUP * QB), BF16),
            pltpu.VMEM((N_KV_HEADS, KC, GROUP * QB), BF16),
            pltpu.VMEM((N_KV_HEADS, 1, GROUP * QB), F32),
            pltpu.VMEM((N_KV_HEADS, HEAD_DIM + DSA_VPAD, GROUP * QB), F32),
        ],
        compiler_params=_cparams("parallel", "arbitrary"),
    )(iqT, iwT, qT, ik, k, vT)


def _out_T_kernel(xT_ref, w_ref, h_ref, gt_ref, o_ref):
    o_ref[...] = h_ref[...] + gt_ref[...] * _dot_tn(xT_ref[...], w_ref[...])


def _dsa_out(oT, w_out, h, mod, tm=512):
    B, S, D = h.shape
    QW = oT.shape[1]
    return pl.pallas_call(
        _out_T_kernel,
        grid=(B, S // tm),
        in_specs=[
            pl.BlockSpec((None, QW, tm), lambda b, i: (b, 0, i)),
            pl.BlockSpec((QW, D), lambda b, i: (0, 0)),
            pl.BlockSpec((None, tm, D), lambda b, i: (b, i, 0)),
            pl.BlockSpec((None, None, 1, D), lambda b, i: (b, 5, 0, 0)),
        ],
        out_specs=pl.BlockSpec((None, tm, D), lambda b, i: (b, i, 0)),
        out_shape=jax.ShapeDtypeStruct((B, S, D), F32),
        compiler_params=_cparams("parallel", "parallel"),
    )(oT, w_out.astype(BF16), h, mod)


def _dsa_mixer(h, mod, w_in, q_norm, k_norm, ik_norm, w_out):
    qT, k, vT, iqT, ik, iwT = _dsa_in(h, mod, w_in, q_norm, k_norm, ik_norm)
    oT = _dsa_core(qT, k, vT, iqT, ik, iwT)
    return _dsa_out(oT, w_out, h, mod)


RWKV_HEAD = 64
GN_EPS = 64e-5
RW_PACK = 4
RW_LANES = RW_PACK * RWKV_HEAD
RW_C = 64
RW_TB = 256


def _split3(x):
    hi = x.astype(BF16)
    r1 = x - hi.astype(F32)
    mid = r1.astype(BF16)
    lo = (r1 - mid.astype(F32)).astype(BF16)
    return hi, mid, lo


def _gsum(x, g_ref, gt_ref):
    hi = x.astype(BF16)
    lo = (x - hi.astype(F32)).astype(BF16)
    s = _dot(jnp.concatenate([hi, lo], axis=1), g_ref[...])
    return _dot(jnp.concatenate(_split3(s), axis=1), gt_ref[...])


def _rwkv_in_kernel(h_ref, hp_ref, sh_ref, sc_ref, mu_ref, wrkv_ref, w0_ref, w1_ref, w2_ref,
                    a0_ref, a1_ref, a2_ref, g1_ref, g2_ref, kk_ref, ka_ref, rk_ref, gm_ref, gmt_ref,
                    r_ref, lw_ref, k_ref, v_ref, kkn_ref, b_ref, g_ref, bonus_ref):
    sh, sc = sh_ref[...], sc_ref[...]
    u = _modulate(h_ref[...], sh, sc)
    tm = u.shape[0]
    prev = _modulate(hp_ref[...], sh, sc)[7:8, :]
    prev = jnp.where(pl.program_id(1) == 0, 0.0, prev)
    row = lax.broadcasted_iota(jnp.int32, (tm, 1), 0)
    xx = jnp.where(row == 0, prev, pltpu.roll(u, 1, 0)) - u
    mix = lambda i: (u + xx * mu_ref[i:i + 1, :]).astype(BF16)
    wl = w0_ref[...] + _dot(jnp.tanh(_dot(mix(3), w1_ref[...])).astype(BF16), w2_ref[...])
    a_lin = _dot(_dot(mix(4), a1_ref[...]).astype(BF16), a2_ref[...])
    g_hid = _dot(mix(5), g1_ref[...])
    k = _dot(mix(1), wrkv_ref[1])
    nwl = -wl
    w_log = -(jnp.maximum(nwl, 0.0) + jnp.log(1.0 + jnp.exp(-jnp.abs(nwl)))) - 0.5
    lw_ref[...] = -jnp.exp(w_log)
    r = _dot(mix(0), wrkv_ref[0])
    r_ref[...] = r
    a = jax.nn.sigmoid(a0_ref[...] + a_lin)
    g_ref[...] = _dot(jax.nn.sigmoid(g_hid).astype(BF16), g2_ref[...])
    kk = k * kk_ref[...]
    nrm = jnp.sqrt(_gsum(kk * kk, gm_ref, gmt_ref))
    kk = kk / jnp.maximum(nrm, 1e-12)
    k_mod = k * (1.0 + (a - 1.0) * ka_ref[...])
    k_ref[...] = k_mod
    kkn_ref[...] = kk
    b_ref[...] = kk * a
    v = _dot(mix(2), wrkv_ref[2])
    v_ref[...] = v
    bonus_ref[...] = _gsum(r * k_mod * rk_ref[...], gm_ref, gmt_ref) * v


def _head_indicator(D):
    gm = (np.arange(D)[:, None] // RWKV_HEAD == np.arange(D // RWKV_HEAD)[None, :]).astype(np.float32)
    return jnp.asarray(np.concatenate([gm, gm], axis=0), BF16), jnp.asarray(np.concatenate([gm.T] * 3, axis=0), BF16)


def _rwkv_in(h, mod, mu, w_rkv, w0, w1, w2, a0, a1, a2, g1, g2, k_k, k_a, r_k, tm=256):
    B, S, D = h.shape
    gm, gmt = _head_indicator(D)
    row = lambda a: a.astype(F32).reshape(1, D)
    full = lambda a: pl.BlockSpec(a.shape, lambda b, i: (0,) * a.ndim)
    mspec = lambda k: pl.BlockSpec((None, None, 1, D), lambda b, i: (b, k, 0, 0))
    tile = pl.BlockSpec((None, tm, D), lambda b, i: (b, i, 0))
    params = [mu.astype(F32), w_rkv.astype(BF16), row(w0), w1.astype(BF16), w2.astype(BF16),
              row(a0), a1.astype(BF16), a2.astype(BF16), g1.astype(BF16), g2.astype(BF16),
              row(k_k), row(k_a), row(r_k), gm, gmt]
    return pl.pallas_call(
        _rwkv_in_kernel,
        grid=(B, S // tm),
        in_specs=[tile,
                  pl.BlockSpec((None, 8, D), lambda b, i: (b, jnp.maximum(i * (tm // 8) - 1, 0), 0)),
                  mspec(3), mspec(4)] + [full(p) for p in params],
        out_specs=[tile] * 8,
        out_shape=[jax.ShapeDtypeStruct((B, S, D), F32)] * 8,
        compiler_params=_cparams("parallel", "parallel"),
    )(h, h, mod, mod, *params)


def _rwkv_scan_kernel(r_ref, lw_ref, k_ref, v_ref, kk_ref, b_ref, y_ref, zt_ref):
    C, P, L = RW_C, RW_PACK, RW_LANES
    N = P * C
    assert C == RWKV_HEAD
    n_packs = r_ref.shape[1] // L
    n_chunks = r_ref.shape[0] // C
    units = [(p, c) for p in range(n_packs) for c in range(n_chunks)]

    @pl.when(pl.program_id(1) == 0)
    def _():
        zt_ref[...] = jnp.zeros(zt_ref.shape, F32)

    lane = lax.broadcasted_iota(jnp.int32, (1, L), 1)
    hmask = [lane // C == hd for hd in range(P)]
    bf = lambda x: x.astype(BF16)
    stack = lambda x: jnp.concatenate([jnp.where(hmask[hd], x, jnp.zeros_like(x)) for hd in range(P)], axis=0)
    t_i = lax.broadcasted_iota(jnp.int32, (C, N), 0)
    i_i = lax.broadcasted_iota(jnp.int32, (C, N), 1) % C
    low_strict = t_i > i_i
    low_incl = t_i >= i_i
    eye_w = jnp.where(t_i == i_i, 1.0, 0.0)
    bd_mask = (lax.broadcasted_iota(jnp.int32, (L, L), 0) // C) == (lax.broadcasted_iota(jnp.int32, (L, L), 1) // C)
    tri = jnp.where(lax.broadcasted_iota(jnp.int32, (C, C), 0) >= lax.broadcasted_iota(jnp.int32, (C, C), 1),
                    1.0, 0.0).astype(BF16)
    tile = lambda ref, u: ref[u[1] * C:(u[1] + 1) * C, u[0] * L:(u[0] + 1) * L]

    cw = {}
    tri3 = jnp.concatenate([tri, tri, tri], axis=1)
    for u in units:
        cw[u] = _dot(tri3, jnp.concatenate(_split3(tile(lw_ref, u)), axis=0))
    al, rt, bw, kw, v, vs, aa, w_end = {}, {}, {}, {}, {}, {}, {}, {}
    for u in units:
        lw = tile(lw_ref, u)
        w_incl = jnp.exp(cw[u])
        w_excl = jnp.exp(cw[u] - lw)
        w_inv = jnp.exp(-cw[u])
        w_end[u] = w_incl[C - 1:C, :]
        v[u] = bf(tile(v_ref, u))
        vs[u] = stack(v[u])
        al[u] = bf(-tile(kk_ref, u) * w_excl)
        rt[u] = tile(r_ref, u) * w_incl
        bh = tile(b_ref, u) * w_inv
        kh = tile(k_ref, u) * w_inv
        bw[u] = bf(bh * w_end[u])
        kw[u] = bf(kh * w_end[u])
        aa[u] = _dot_nt(jnp.concatenate([al[u], bf(rt[u])], axis=0),
                        jnp.concatenate([stack(bf(bh)), stack(bf(kh))], axis=0))
    a_k, a_rb, pw, t = {}, {}, {}, {}
    for u in units:
        a_ab = jnp.where(low_strict, aa[u][:C, :N], 0.0)
        a_k[u] = bf(jnp.concatenate([jnp.where(low_strict, aa[u][:C, N:], 0.0),
                                     jnp.where(low_incl, aa[u][C:, N:], 0.0)], axis=0))
        a_rb[u] = bf(jnp.where(low_incl, aa[u][C:, :N], 0.0))
        pw[u] = a_ab
        t[u] = eye_w + a_ab
    n_fac = int(np.log2(C))
    akv, rkv = {}, {}
    for u in units:
        pwb = bf(pw[u])
        pw[u] = _dot(pwb, stack(pwb))
        both_v = _dot(a_k[u], vs[u])
        akv[u], rkv[u] = both_v[:C], both_v[C:]
    for s in range(2, n_fac + 1):
        for u in units:
            pwb = bf(pw[u])
            rhs = stack(pwb)
            if s < n_fac:
                both = _dot(jnp.concatenate([bf(t[u]), pwb], axis=0), rhs)
                t[u] = t[u] + both[:C]
                pw[u] = both[C:]
            else:
                t[u] = t[u] + _dot(bf(t[u]), rhs)
    pm, q = {}, {}
    for u in units:
        tb = bf(t[u])
        pm[u] = bf(_dot(tb, stack(al[u])))
        q[u] = bf(_dot(tb, stack(bf(akv[u]))))
    pr, y0, bk = {}, {}, {}
    for u in units:
        rp = bf(rt[u] + _dot(a_rb[u], stack(pm[u])))
        pr[u] = jnp.concatenate([pm[u], rp], axis=0)
        y0[u] = _dot(a_rb[u], stack(q[u])) + rkv[u]
        bk[u] = jnp.concatenate([bw[u], kw[u]], axis=0)
    zt = [zt_ref[p] for p in range(n_packs)]
    for c in range(n_chunks):
        for p in range(n_packs):
            u = (p, c)
            uy = _dot_nt(pr[u], bf(zt[p]))
            y_ref[c * C:(c + 1) * C, p * L:(p + 1) * L] = uy[C:] + y0[u]
            upd = _dot_tn(jnp.concatenate([bf(uy[:C] + q[u]), v[u]], axis=0), bk[u])
            zt[p] = zt[p] * w_end[u] + jnp.where(bd_mask, upd, 0.0)
    for p in range(n_packs):
        zt_ref[p] = zt[p]


def _rwkv_scan(r, lw, k, v, kk, b):
    B, S, D = r.shape
    tile = pl.BlockSpec((None, RW_TB, D), lambda bb, t: (bb, t, 0))
    return pl.pallas_call(
        _rwkv_scan_kernel,
        grid=(B, S // RW_TB),
        in_specs=[tile] * 6,
        out_specs=tile,
        out_shape=jax.ShapeDtypeStruct((B, S, D), F32),
        scratch_shapes=[pltpu.VMEM((D // RW_LANES, RW_LANES, RW_LANES), F32)],
        compiler_params=_cparams("parallel", "arbitrary"),
    )(r, lw, k, v, kk, b)


def _rwkv_out_kernel(y_ref, bonus_ref, g_ref, lnw_ref, lnb_ref, gm_ref, gmt_ref, w_ref, h_ref, gt_ref, o_ref):
    y = y_ref[...]
    inv_n = 1.0 / RWKV_HEAD
    yc = y - _gsum(y, gm_ref, gmt_ref) * inv_n
    var = _gsum(yc * yc, gm_ref, gmt_ref) * inv_n
    yn = yc * lax.rsqrt(var + GN_EPS) * lnw_ref[...] + lnb_ref[...]
    out = ((yn + bonus_ref[...]) * g_ref[...]).astype(BF16)
    o_ref[...] = h_ref[...] + gt_ref[...] * _dot(out, w_ref[...])


def _rwkv_out(y, bonus, g, ln_w, ln_b, w_out, h, mod, tm=256):
    B, S, D = h.shape
    gm, gmt = _head_indicator(D)
    row = lambda a: a.astype(F32).reshape(1, D)
    full = lambda a: pl.BlockSpec(a.shape, lambda b, i: (0,) * a.ndim)
    tile = pl.BlockSpec((None, tm, D), lambda b, i: (b, i, 0))
    params = [row(ln_w), row(ln_b), gm, gmt, w_out.astype(BF16)]
    return pl.pallas_call(
        _rwkv_out_kernel,
        grid=(B, S // tm),
        in_specs=[tile] * 3 + [full(p) for p in params] + [tile, pl.BlockSpec((None, None, 1, D), lambda b, i: (b, 5, 0, 0))],
        out_specs=tile,
        out_shape=jax.ShapeDtypeStruct((B, S, D), F32),
        compiler_params=_cparams("parallel", "parallel"),
    )(y, bonus, g, *params, h, mod)


def _rwkv_mixer(h, mod, mu, w_rkv, w0, w1, w2, a0, a1, a2, g1, g2, k_k, k_a, r_k, ln_w, ln_b, w_out):
    r, lw, k, v, kk, b, g, bonus = _rwkv_in(h, mod, mu, w_rkv, w0, w1, w2, a0, a1, a2, g1, g2, k_k, k_a, r_k)
    y = _rwkv_scan(r, lw, k, v, kk, b)
    return _rwkv_out(y, bonus, g, ln_w, ln_b, w_out, h, mod)


def kernel(x, c, ada_w, ada_b, ffn_w_gu, ffn_w_down, dsa_w_in, dsa_q_norm, dsa_k_norm, dsa_ik_norm, dsa_w_out, rwkv_mu, rwkv_w_rkv, rwkv_w0, rwkv_w1, rwkv_w2, rwkv_a0, rwkv_a1, rwkv_a2, rwkv_g1, rwkv_g2, rwkv_k_k, rwkv_k_a, rwkv_r_k, rwkv_ln_w, rwkv_ln_b, rwkv_w_out):
    B, S, D = x.shape
    depth = ada_w.shape[0]
    mods = _ada_mod(c, ada_w, ada_b)
    h = x
    for i in range(depth):
        mod = mods[i].reshape(B, N_MOD, 1, D)
        j = i // 2
        h = _ffn(h, mod, 0, ffn_w_gu[i, 0].astype(BF16), ffn_w_down[i, 0].astype(BF16))
        if i % 2 == 0:
            h = _dsa_mixer(h, mod, dsa_w_in[j], dsa_q_norm[j], dsa_k_norm[j], dsa_ik_norm[j], dsa_w_out[j])
        else:
            h = _rwkv_mixer(h, mod, rwkv_mu[j], rwkv_w_rkv[j], rwkv_w0[j], rwkv_w1[j], rwkv_w2[j], rwkv_a0[j],
                            rwkv_a1[j], rwkv_a2[j], rwkv_g1[j], rwkv_g2[j], rwkv_k_k[j], rwkv_k_a[j],
                            rwkv_r_k[j], rwkv_ln_w[j], rwkv_ln_b[j], rwkv_w_out[j])
        h = _ffn(h, mod, 6, ffn_w_gu[i, 1].astype(BF16), ffn_w_down[i, 1].astype(BF16))
    return h
```

```python
import functools

import jax
import jax.numpy as jnp
import numpy as np
from jax import lax
from jax.experimental import pallas as pl
from jax.experimental.pallas import tpu as pltpu

F32 = jnp.float32
BF16 = jnp.bfloat16

RMS_EPS = 1e-6
N_MOD = 9
VMEM_LIMIT_BYTES = 56 * 1024 * 1024

N_HEADS = 16
HEAD_DIM = 64
N_KV_HEADS = 4
GROUP = N_HEADS // N_KV_HEADS
IDX_HEADS = 8
IDX_DIM = 64
TOPK_MAX = 256
LANE = 128
LOG2E = 1.4426950408889634


def _cparams(*sem):
    return pltpu.CompilerParams(dimension_semantics=sem, vmem_limit_bytes=VMEM_LIMIT_BYTES)


def _dot(a, b):
    return jnp.dot(a, b, preferred_element_type=F32)


def _dot_nt(a, b):
    return lax.dot_general(a, b, (((1,), (1,)), ((), ())), preferred_element_type=F32)


def _dot_tn(a, b):
    return lax.dot_general(a, b, (((0,), (0,)), ((), ())), preferred_element_type=F32)


def _modulate(h, sh, sc):
    ms = jnp.mean(h * h, axis=-1, keepdims=True)
    return h * lax.rsqrt(ms + RMS_EPS) * (1.0 + sc) + sh


def _ada_kernel(c_ref, w_ref, b_ref, o_ref):
    c = c_ref[...]
    ca = (c * jax.nn.sigmoid(c)).astype(BF16)
    o_ref[...] = _dot(ca, w_ref[...].astype(BF16)) + b_ref[...]


def _ada_mod(c, ada_w, ada_b):
    L, D, N = ada_w.shape
    B = c.shape[0]
    tn = 1536
    return pl.pallas_call(
        _ada_kernel,
        grid=(L, N // tn),
        in_specs=[
            pl.BlockSpec((B, D), lambda l, j: (0, 0)),
            pl.BlockSpec((None, D, tn), lambda l, j: (l, 0, j)),
            pl.BlockSpec((None, 1, tn), lambda l, j: (l, 0, j)),
        ],
        out_specs=pl.BlockSpec((None, B, tn), lambda l, j: (l, 0, j)),
        out_shape=jax.ShapeDtypeStruct((L, B, N), F32),
        compiler_params=_cparams("parallel", "parallel"),
    )(c, ada_w, ada_b.reshape(L, 1, N))


FF_CHUNK = 256


def _ffn_kernel(h_ref, sh_ref, sc_ref, gt_ref, wgu_ref, wd_ref, o_ref, *, d_ff):
    h = h_ref[...]
    u = _modulate(h, sh_ref[...], sc_ref[...]).astype(BF16)
    acc = None
    for c in range(d_ff // FF_CHUNK):
        lo = c * FF_CHUNK
        g = _dot(u, wgu_ref[:, lo:lo + FF_CHUNK])
        up = _dot(u, wgu_ref[:, d_ff + lo:d_ff + lo + FF_CHUNK])
        a = (g * jax.nn.sigmoid(g) * up).astype(BF16)
        part = _dot(a, wd_ref[lo:lo + FF_CHUNK, :])
        acc = part if acc is None else acc + part
    o_ref[...] = h + (0.5 * gt_ref[...]) * acc


def _ffn(h, mod, k0, w_gu, w_down, tm=512):
    B, S, D = h.shape
    d_ff = w_down.shape[0]
    mspec = lambda k: pl.BlockSpec((None, None, 1, D), lambda b, i: (b, k, 0, 0))
    return pl.pallas_call(
        functools.partial(_ffn_kernel, d_ff=d_ff),
        grid=(B, S // tm),
        in_specs=[
            pl.BlockSpec((None, tm, D), lambda b, i: (b, i, 0)),
            mspec(k0), mspec(k0 + 1), mspec(k0 + 2),
            pl.BlockSpec((D, 2 * d_ff), lambda b, i: (0, 0), pipeline_mode=pl.Buffered(1)),
            pl.BlockSpec((d_ff, D), lambda b, i: (0, 0), pipeline_mode=pl.Buffered(1)),
        ],
        out_specs=pl.BlockSpec((None, tm, D), lambda b, i: (b, i, 0)),
        out_shape=jax.ShapeDtypeStruct((B, S, D), F32),
        compiler_params=_cparams("parallel", "parallel"),
    )(h, mod, mod, mod, w_gu, w_down)


def _dsa_in_kernel(h_ref, sh_ref, sc_ref, wT_ref, wn_ref, qn_ref, kn_ref, ikn_ref,
                   qT_ref, k_ref, vT_ref, iqT_ref, ik_ref, iwT_ref):
    u = _modulate(h_ref[...], sh_ref[...], sc_ref[...]).astype(BF16)
    tm = u.shape[0]
    QW, KW, IQW = N_HEADS * HEAD_DIM, N_KV_HEADS * HEAD_DIM, IDX_HEADS * IDX_DIM
    yT = _dot_nt(wT_ref[...], u)
    qT = yT[0:QW].reshape(N_HEADS, HEAD_DIM, tm)
    ms = jnp.mean(qT * qT, axis=1, keepdims=True)
    qT = qT * lax.rsqrt(ms + RMS_EPS) * qn_ref[...][None] * (HEAD_DIM ** -0.5 * LOG2E)
    qT_ref[...] = qT.reshape(QW, tm).astype(BF16)
    for cc in range(tm // DSA_KC):
        vT_ref[cc] = yT[QW:QW + KW, cc * DSA_KC:(cc + 1) * DSA_KC].astype(BF16)
    iqT_ref[...] = yT[QW + KW:QW + KW + IQW].astype(BF16)
    iwT_ref[...] = yT[QW + KW + IQW:QW + KW + IQW + IDX_HEADS] * (IDX_HEADS ** -0.5 * IDX_DIM ** -0.5)
    y = _dot(u, wn_ref[...])
    kn = kn_ref[...]
    for j in range(N_KV_HEADS):
        kj = y[:, j * LANE:(j + 1) * LANE]
        ms = jnp.sum(kj * kj, axis=-1, keepdims=True) * (1.0 / HEAD_DIM)
        k_ref[:, j * LANE:(j + 1) * LANE] = (kj * lax.rsqrt(ms + RMS_EPS) * kn).astype(BF16)
    ik = y[:, N_KV_HEADS * LANE:]
    ms = jnp.sum(ik * ik, axis=-1, keepdims=True) * (1.0 / IDX_DIM)
    ik_ref[...] = (ik * lax.rsqrt(ms + RMS_EPS) * ikn_ref[...]).astype(BF16)


DSA_KC = 256
DSA_QB = 256


def _dsa_in(h, mod, w_in, q_norm, k_norm, ik_norm, tm=2 * DSA_KC):
    B, S, D = h.shape
    QW, KW = N_HEADS * HEAD_DIM, N_KV_HEADS * HEAD_DIM
    IQW = IDX_HEADS * IDX_DIM
    o1, o2, o3, o4, o5 = QW, QW + KW, QW + 2 * KW, QW + 2 * KW + IQW, QW + 2 * KW + IQW + IDX_DIM
    wb = w_in.astype(BF16)
    wqT = wb[:, :o1].T
    wk = jnp.pad(wb[:, o1:o2].reshape(D, N_KV_HEADS, HEAD_DIM), ((0, 0), (0, 0), (0, LANE - HEAD_DIM))).reshape(D, N_KV_HEADS * LANE)
    wvT = wb[:, o2:o3].T
    wiqT = wb[:, o3:o4].T
    wik = jnp.pad(wb[:, o4:o5], ((0, 0), (0, LANE - IDX_DIM)))
    wiwT = jnp.pad(wb[:, o5:].T, ((0, 16 - IDX_HEADS), (0, 0)))
    wT = jnp.concatenate([wqT, wvT, wiqT, wiwT], axis=0)
    wn = jnp.concatenate([wk, wik], axis=1)
    qn = q_norm.astype(F32).reshape(HEAD_DIM, 1)
    kn = jnp.pad(k_norm.astype(F32), (0, LANE - HEAD_DIM)).reshape(1, LANE)
    ikn = jnp.pad(ik_norm.astype(F32), (0, LANE - IDX_DIM)).reshape(1, LANE)
    mspec = lambda k: pl.BlockSpec((None, None, 1, D), lambda b, i: (b, k, 0, 0))
    full = lambda a: pl.BlockSpec(a.shape, lambda b, i: (0,) * a.ndim)
    nc = S // tm
    cpt = tm // DSA_KC
    return pl.pallas_call(
        _dsa_in_kernel,
        grid=(B, nc),
        in_specs=[pl.BlockSpec((None, tm, D), lambda b, i: (b, i, 0)), mspec(3), mspec(4),
                  full(wT), full(wn), full(qn), full(kn), full(ikn)],
        out_specs=[
            pl.BlockSpec((None, QW, tm), lambda b, i: (b, 0, i)),
            pl.BlockSpec((None, tm, N_KV_HEADS * LANE), lambda b, i: (b, i, 0)),
            pl.BlockSpec((None, cpt, KW, DSA_KC), lambda b, i: (b, i, 0, 0)),
            pl.BlockSpec((None, IQW, tm), lambda b, i: (b, 0, i)),
            pl.BlockSpec((None, tm, LANE), lambda b, i: (b, i, 0)),
            pl.BlockSpec((None, IDX_HEADS, tm), lambda b, i: (b, 0, i)),
        ],
        out_shape=[
            jax.ShapeDtypeStruct((B, QW, S), BF16),
            jax.ShapeDtypeStruct((B, S, N_KV_HEADS * LANE), BF16),
            jax.ShapeDtypeStruct((B, nc * cpt, KW, DSA_KC), BF16),
            jax.ShapeDtypeStruct((B, IQW, S), BF16),
            jax.ShapeDtypeStruct((B, S, LANE), BF16),
            jax.ShapeDtypeStruct((B, IDX_HEADS, S), F32),
        ],
        compiler_params=_cparams("parallel", "parallel"),
    )(h, mod, mod, wT, wn, qn, kn, ikn)


NEG_BIG = -(2.0 ** 100)
DSA_VPAD = 16
N_CHAINS = 4


def _ukey_to_float(ukey):
    key = ukey ^ np.int32(-2 ** 31)
    bits = key ^ ((key >> 31) & np.int32(0x7FFFFFFF))
    return lax.bitcast_convert_type(bits, F32)


def _fold_rows(x, op, rows):
    while x.shape[0] > rows:
        half = x.shape[0] // 2
        x = op(x[:half], x[half:])
    return x


def _dsa_core_kernel(iqT_ref, iwT_ref, qT_ref, ik_ref, k_ref, vT_ref, oT_ref,
                     sc_ref, fl_ref, whi_ref, wlo_ref, qa_ref, iqa_ref, d_ref, st_ref, p_ref, m_ref, acc_ref,
                     *, k_top, n_chunks):
    KC, QB = DSA_KC, DSA_QB
    qi = pl.program_id(1)
    q0 = qi * QB
    nk = (q0 + QB + KC - 1) // KC
    qpos = q0 + lax.broadcasted_iota(jnp.int32, (1, QB), 1)
    zeros_h = jnp.zeros((LANE - HEAD_DIM, QB), BF16)

    for h in range(IDX_HEADS):
        iqa_ref[:, h * QB:(h + 1) * QB] = jnp.concatenate(
            [iqT_ref[h * IDX_DIM:(h + 1) * IDX_DIM, :], zeros_h], axis=0)
    for j in range(N_KV_HEADS):
        for g in range(GROUP):
            hh = j * GROUP + g
            qa_ref[j, :, g * QB:(g + 1) * QB] = jnp.concatenate(
                [qT_ref[hh * HEAD_DIM:(hh + 1) * HEAD_DIM, :], zeros_h], axis=0)
    iw = iwT_ref[...]

    npair = (nk + 1) // 2

    def clamp_chunk(c):
        return jnp.minimum(c, n_chunks - 1)

    def idx_dot(c, slot):
        k0 = pl.multiple_of(c * KC, KC)
        d_ref[slot] = _dot(ik_ref[pl.ds(k0, KC), :], iqa_ref[...])

    def idx_score(c, slot):
        k0 = pl.multiple_of(c * KC, KC)
        s = jnp.maximum(d_ref[slot, :, 0:QB], 0.0) * iw[0:1, :]
        for h in range(1, IDX_HEADS):
            s = s + jnp.maximum(d_ref[slot, :, h * QB:(h + 1) * QB], 0.0) * iw[h:h + 1, :]
        kpos = k0 + lax.broadcasted_iota(jnp.int32, (KC, 1), 0)
        s = jnp.where(kpos <= qpos, s, -jnp.inf)
        sc_ref[pl.ds(k0, KC), :] = s
        r = s.astype(BF16)
        rf = r.astype(F32)
        below = (rf * jnp.where(rf > 0.0, 1.0 - 1.25 * 2.0 ** -8, 1.0 + 1.25 * 2.0 ** -8)).astype(BF16)
        fl_ref[pl.ds(k0, KC), :] = jnp.where(rf > s, below, r)

    idx_dot(0, 0)

    def p1(i, carry):
        c = 2 * i
        idx_dot(c + 1, 1)
        idx_score(c, 0)
        idx_dot(clamp_chunk(c + 2), 0)
        idx_score(c + 1, 1)
        return carry

    lax.fori_loop(0, npair, p1, 0)

    def count_not(pred):
        def body(i, accs):
            accs = list(accs)
            for half in range(2):
                k0 = pl.multiple_of((2 * i + half) * KC, KC)
                ind = jnp.where(pred(sc_ref[pl.ds(k0, KC), :]), 0.0, 1.0)
                for r in range(KC // 8):
                    accs[r % N_CHAINS] = accs[r % N_CHAINS] + ind[r * 8:(r + 1) * 8]
            return tuple(accs)
        accs = lax.fori_loop(0, npair, body, tuple(jnp.zeros((8, QB), F32) for _ in range(N_CHAINS)))
        acc = (accs[0] + accs[1]) + (accs[2] + accs[3])
        return jnp.sum(acc, axis=0, keepdims=True)

    n_staged = (npair * (2 * KC)).astype(F32)

    def count_ge16(ref, t):
        tb = t.astype(BF16)
        one, zero = jnp.ones((), BF16), jnp.zeros((), BF16)

        def body(i, accs):
            accs = list(accs)
            for half in range(2):
                k0 = pl.multiple_of((2 * i + half) * KC, KC)
                ind = jnp.where(ref[pl.ds(k0, KC), :] < tb, zero, one)
                for r in range(KC // 16):
                    accs[r % N_CHAINS] = accs[r % N_CHAINS] + ind[r * 16:(r + 1) * 16]
            return tuple(accs)
        accs = lax.fori_loop(0, npair, body, tuple(jnp.zeros((16, QB), BF16) for _ in range(N_CHAINS)))
        acc = (accs[0].astype(F32) + accs[1].astype(F32)) + (accs[2].astype(F32) + accs[3].astype(F32))
        return jnp.sum(acc, axis=0, keepdims=True)

    def digit_search(ref, n_bits, k_need, to_float):
        d = jnp.zeros((1, QB), F32)
        for bit in range(n_bits - 1, -1, -1):
            cand = d + float(1 << bit)
            d = jnp.where(count_ge16(ref, to_float(cand)) >= k_need, cand, d)
        return d

    def val16(d):
        di = d.astype(jnp.int32)
        return _ukey_to_float(jnp.where(di >= 0x8000, di << 16, (di << 16) | np.int32(0xFFFF)))

    key_hi = digit_search(fl_ref, 16, k_top, val16)
    h_lo = val16(key_hi)
    h_up = val16(key_hi + 1.0)
    no_thr = h_lo == -jnp.inf
    h_up = jnp.where(h_up > h_lo, h_up, np.float32(2.0 ** -126))
    width = jnp.where(no_thr, 1.0, h_up - h_lo)
    tiny = width < 2.0 ** -60
    pre = jnp.where(tiny, np.float32(2.0 ** 60), np.float32(1.0))
    unpre = jnp.where(tiny, np.float32(2.0 ** -60), np.float32(1.0))
    from_int = (width * pre) * np.float32(2.0 ** -16)
    to_int = lax.bitcast_convert_type(np.int32(0x7F000000) - lax.bitcast_convert_type(from_int, jnp.int32), F32)

    def split(i, carry):
        for half in range(2):
            k0 = pl.multiple_of((2 * i + half) * KC, KC)
            w = jnp.clip(((sc_ref[pl.ds(k0, KC), :] - h_lo) * pre) * to_int, -256.0, 65536.0)
            w_hi = jnp.floor(w * (1.0 / 256.0))
            whi_ref[pl.ds(k0, KC), :] = w_hi.astype(BF16)
            wlo_ref[pl.ds(k0, KC), :] = (w - 256.0 * w_hi).astype(BF16)
        return carry
    lax.fori_loop(0, npair, split, 0)

    ident = lambda d: d
    d_mid = digit_search(whi_ref, 8, k_top, ident)
    k_low = k_top - count_ge16(whi_ref, d_mid + 1.0)
    d_mid_b = d_mid.astype(BF16)

    def keep_low(i, carry):
        for half in range(2):
            k0 = pl.multiple_of((2 * i + half) * KC, KC)
            whi_ref[pl.ds(k0, KC), :] = jnp.where(whi_ref[pl.ds(k0, KC), :] == d_mid_b,
                                                  wlo_ref[pl.ds(k0, KC), :], -jnp.ones((), BF16))
        return carry
    lax.fori_loop(0, npair, keep_low, 0)
    d_low = digit_search(whi_ref, 8, k_low, ident)
    offset = ((d_mid * 256.0 + d_low) * from_int) * unpre
    thr = jnp.where(no_thr, -jnp.inf, h_lo + offset)

    m_ref[...] = jnp.full(m_ref.shape, NEG_BIG, F32)
    acc_ref[...] = jnp.zeros(acc_ref.shape, F32)
    r_i = lax.broadcasted_iota(jnp.int32, (KC, KC), 0)
    c_i = lax.broadcasted_iota(jnp.int32, (KC, KC), 1)
    tril = jnp.where(c_i <= r_i, 1.0, 0.0).astype(BF16)
    ones_rows = jnp.ones((DSA_VPAD, KC), BF16)

    HALF = GROUP * QB // 2

    def logits(c, slot, j, half):
        k0 = pl.multiple_of(c * KC, KC)
        cols = slice(half * HALF, (half + 1) * HALF)
        st_ref[slot, j, :, cols] = _dot(k_ref[pl.ds(k0, KC), j * LANE:(j + 1) * LANE],
                                        qa_ref[j, :, cols]).astype(BF16)

    for j in range(N_KV_HEADS):
        logits(0, 0, j, 0)
        logits(0, 0, j, 1)

    def attend(c, slot, tie_seen, c_next, need):
        k0 = pl.multiple_of(c * KC, KC)
        s = sc_ref[pl.ds(k0, KC), :]
        kpos = k0 + lax.broadcasted_iota(jnp.int32, (KC, 1), 0)
        eq = s == thr
        eqf = jnp.where(eq, 1.0, 0.0)
        incl = _dot(tril, eqf.astype(BF16))
        rank = tie_seen + incl - eqf
        sel = jnp.logical_and(jnp.logical_or(s > thr, jnp.logical_and(eq, rank < need)), kpos <= qpos)
        tie_seen = tie_seen + incl[KC - 1:KC, :]
        bias = jnp.where(sel, 0.0, NEG_BIG).astype(BF16)
        for j in range(N_KV_HEADS):
            alphas = []
            for g in range(GROUP):
                sl = slice(g * QB, (g + 1) * QB)
                sg = st_ref[slot, j, :, sl] + bias
                m_old = m_ref[j, :, sl]
                mx = jnp.max(_fold_rows(sg, jnp.maximum, 16), axis=0, keepdims=True).astype(F32)
                m_new = jnp.maximum(m_old, mx)
                alphas.append(jnp.exp2(m_old - m_new))
                m_ref[j, :, sl] = m_new
                p_ref[j, :, sl] = jnp.exp2(sg - m_new.astype(BF16))
                if g % 2 == 1:
                    logits(c_next, 1 - slot, j, g // 2)
            v_aug = jnp.concatenate([vT_ref[c, j * HEAD_DIM:(j + 1) * HEAD_DIM, :], ones_rows], axis=0)
            acc_ref[j] = acc_ref[j] * jnp.concatenate(alphas, axis=1) + _dot(v_aug, p_ref[j])
        return tie_seen

    need = k_top - (n_staged - count_not(lambda s: s > thr))

    def p3(i, tie_seen):
        c = 2 * i
        tie_seen = attend(c, 0, tie_seen, c + 1, need)
        return attend(c + 1, 1, tie_seen, clamp_chunk(c + 2), need)

    lax.fori_loop(0, npair, p3, jnp.zeros((1, QB), F32))

    for j in range(N_KV_HEADS):
        o = acc_ref[j, 0:HEAD_DIM, :] / acc_ref[j, HEAD_DIM:HEAD_DIM + 1, :]
        for g in range(GROUP):
            hh = j * GROUP + g
            oT_ref[hh * HEAD_DIM:(hh + 1) * HEAD_DIM, :] = o[:, g * QB:(g + 1) * QB].astype(BF16)


def _dsa_core(qT, k, vT, iqT, ik, iwT):
    B, QW, S = qT.shape
    KC, QB = DSA_KC, DSA_QB
    k_top = min(TOPK_MAX, S // 4)
    nc = S // KC
    return pl.pallas_call(
        functools.partial(_dsa_core_kernel, k_top=k_top, n_chunks=nc),
        grid=(B, S // QB),
        in_specs=[
            pl.BlockSpec((None, IDX_HEADS * IDX_DIM, QB), lambda b, i: (b, 0, i)),
            pl.BlockSpec((None, IDX_HEADS, QB), lambda b, i: (b, 0, i)),
            pl.BlockSpec((None, QW, QB), lambda b, i: (b, 0, i)),
            pl.BlockSpec((None, S, LANE), lambda b, i: (b, 0, 0)),
            pl.BlockSpec((None, S, N_KV_HEADS * LANE), lambda b, i: (b, 0, 0)),
            pl.BlockSpec((None, nc, N_KV_HEADS * HEAD_DIM, KC), lambda b, i: (b, 0, 0, 0)),
        ],
        out_specs=pl.BlockSpec((None, QW, QB), lambda b, i: (b, 0, i)),
        out_shape=jax.ShapeDtypeStruct((B, QW, S), BF16),
        scratch_shapes=[
            pltpu.VMEM((S, QB), F32),
            pltpu.VMEM((S, QB), BF16),
            pltpu.VMEM((S, QB), BF16),
            pltpu.VMEM((S, QB), BF16),
            pltpu.VMEM((N_KV_HEADS, LANE, GROUP * QB), BF16),
            pltpu.VMEM((LANE, IDX_HEADS * QB), BF16),
            pltpu.VMEM((2, KC, IDX_HEADS * QB), F32),
            pltpu.VMEM((2, N_KV_HEADS, KC, GROUP * QB), BF16),
            pltpu.VMEM((N_KV_HEADS, KC, GROUP * QB), BF16),
            pltpu.VMEM((N_KV_HEADS, 1, GROUP * QB), F32),
            pltpu.VMEM((N_KV_HEADS, HEAD_DIM + DSA_VPAD, GROUP * QB), F32),
        ],
        compiler_params=_cparams("parallel", "arbitrary"),
    )(iqT, iwT, qT, ik, k, vT)


def _out_T_kernel(xT_ref, w_ref, h_ref, gt_ref, o_ref):
    o_ref[...] = h_ref[...] + gt_ref[...] * _dot_tn(xT_ref[...], w_ref[...])


def _dsa_out(oT, w_out, h, mod, tm=512):
    B, S, D = h.shape
    QW = oT.shape[1]
    return pl.pallas_call(
        _out_T_kernel,
        grid=(B, S // tm),
        in_specs=[
            pl.BlockSpec((None, QW, tm), lambda b, i: (b, 0, i)),
            pl.BlockSpec((QW, D), lambda b, i: (0, 0)),
            pl.BlockSpec((None, tm, D), lambda b, i: (b, i, 0)),
            pl.BlockSpec((None, None, 1, D), lambda b, i: (b, 5, 0, 0)),
        ],
        out_specs=pl.BlockSpec((None, tm, D), lambda b, i: (b, i, 0)),
        out_shape=jax.ShapeDtypeStruct((B, S, D), F32),
        compiler_params=_cparams("parallel", "parallel"),
    )(oT, w_out.astype(BF16), h, mod)


def _dsa_mixer(h, mod, w_in, q_norm, k_norm, ik_norm, w_out):
    qT, k, vT, iqT, ik, iwT = _dsa_in(h, mod, w_in, q_norm, k_norm, ik_norm)
    oT = _dsa_core(qT, k, vT, iqT, ik, iwT)
    return _dsa_out(oT, w_out, h, mod)


RWKV_HEAD = 64
GN_EPS = 64e-5
RW_PACK = 4
RW_LANES = RW_PACK * RWKV_HEAD
RW_C = 64
RW_TB = 256


def _split3(x):
    hi = x.astype(BF16)
    r1 = x - hi.astype(F32)
    mid = r1.astype(BF16)
    lo = (r1 - mid.astype(F32)).astype(BF16)
    return hi, mid, lo


def _gsum(x, g_ref, gt_ref):
    hi = x.astype(BF16)
    lo = (x - hi.astype(F32)).astype(BF16)
    s = _dot(jnp.concatenate([hi, lo], axis=1), g_ref[...])
    return _dot(jnp.concatenate(_split3(s), axis=1), gt_ref[...])


def _rwkv_in_kernel(h_ref, hp_ref, sh_ref, sc_ref, mu_ref, wrkv_ref, w0_ref, w1_ref, w2_ref,
                    a0_ref, a1_ref, a2_ref, g1_ref, g2_ref, kk_ref, ka_ref, gm_ref, gmt_ref,
                    r_ref, lw_ref, k_ref, v_ref, kkn_ref, b_ref, g_ref):
    sh, sc = sh_ref[...], sc_ref[...]
    u = _modulate(h_ref[...], sh, sc)
    tm = u.shape[0]
    prev = _modulate(hp_ref[...], sh, sc)[7:8, :]
    prev = jnp.where(pl.program_id(1) == 0, 0.0, prev)
    row = lax.broadcasted_iota(jnp.int32, (tm, 1), 0)
    xx = jnp.where(row == 0, prev, pltpu.roll(u, 1, 0)) - u
    mix = lambda i: (u + xx * mu_ref[i:i + 1, :]).astype(BF16)
    wl = w0_ref[...] + _dot(jnp.tanh(_dot(mix(3), w1_ref[...])).astype(BF16), w2_ref[...])
    a_lin = _dot(_dot(mix(4), a1_ref[...]).astype(BF16), a2_ref[...])
    g_hid = _dot(mix(5), g1_ref[...])
    k = _dot(mix(1), wrkv_ref[1])
    nwl = -wl
    w_log = -(jnp.maximum(nwl, 0.0) + jnp.log(1.0 + jnp.exp(-jnp.abs(nwl)))) - 0.5
    lw_ref[...] = -jnp.exp(w_log)
    r_ref[...] = _dot(mix(0), wrkv_ref[0])
    a = jax.nn.sigmoid(a0_ref[...] + a_lin)
    g_ref[...] = _dot(jax.nn.sigmoid(g_hid).astype(BF16), g2_ref[...])
    kk = k * kk_ref[...]
    nrm = jnp.sqrt(_gsum(kk * kk, gm_ref, gmt_ref))
    kk = kk / jnp.maximum(nrm, 1e-12)
    k_ref[...] = k * (1.0 + (a - 1.0) * ka_ref[...])
    kkn_ref[...] = kk
    b_ref[...] = kk * a
    v_ref[...] = _dot(mix(2), wrkv_ref[2])


def _head_indicator(D):
    gm = (np.arange(D)[:, None] // RWKV_HEAD == np.arange(D // RWKV_HEAD)[None, :]).astype(np.float32)
    return jnp.asarray(np.concatenate([gm, gm], axis=0), BF16), jnp.asarray(np.concatenate([gm.T] * 3, axis=0), BF16)


def _rwkv_in(h, mod, mu, w_rkv, w0, w1, w2, a0, a1, a2, g1, g2, k_k, k_a, tm=256):
    B, S, D = h.shape
    gm, gmt = _head_indicator(D)
    row = lambda a: a.astype(F32).reshape(1, D)
    full = lambda a: pl.BlockSpec(a.shape, lambda b, i: (0,) * a.ndim)
    mspec = lambda k: pl.BlockSpec((None, None, 1, D), lambda b, i: (b, k, 0, 0))
    tile = pl.BlockSpec((None, tm, D), lambda b, i: (b, i, 0))
    params = [mu.astype(F32), w_rkv.astype(BF16), row(w0), w1.astype(BF16), w2.astype(BF16),
              row(a0), a1.astype(BF16), a2.astype(BF16), g1.astype(BF16), g2.astype(BF16),
              row(k_k), row(k_a), gm, gmt]
    return pl.pallas_call(
        _rwkv_in_kernel,
        grid=(B, S // tm),
        in_specs=[tile,
                  pl.BlockSpec((None, 8, D), lambda b, i: (b, jnp.maximum(i * (tm // 8) - 1, 0), 0)),
                  mspec(3), mspec(4)] + [full(p) for p in params],
        out_specs=[tile] * 7,
        out_shape=[jax.ShapeDtypeStruct((B, S, D), F32)] * 7,
        compiler_params=_cparams("parallel", "parallel"),
    )(h, h, mod, mod, *params)


def _rwkv_scan_kernel(r_ref, lw_ref, k_ref, v_ref, kk_ref, b_ref, y_ref, zt_ref):
    C, P, L = RW_C, RW_PACK, RW_LANES
    N = P * C
    assert C == RWKV_HEAD
    n_packs = r_ref.shape[1] // L
    n_chunks = r_ref.shape[0] // C
    units = [(p, c) for p in range(n_packs) for c in range(n_chunks)]

    @pl.when(pl.program_id(1) == 0)
    def _():
        zt_ref[...] = jnp.zeros(zt_ref.shape, F32)

    lane = lax.broadcasted_iota(jnp.int32, (1, L), 1)
    hmask = [lane // C == hd for hd in range(P)]
    bf = lambda x: x.astype(BF16)
    stack = lambda x: jnp.concatenate([jnp.where(hmask[hd], x, jnp.zeros_like(x)) for hd in range(P)], axis=0)
    t_i = lax.broadcasted_iota(jnp.int32, (C, N), 0)
    i_i = lax.broadcasted_iota(jnp.int32, (C, N), 1) % C
    low_strict = t_i > i_i
    low_incl = t_i >= i_i
    eye_w = jnp.where(t_i == i_i, 1.0, 0.0)
    bd_mask = (lax.broadcasted_iota(jnp.int32, (L, L), 0) // C) == (lax.broadcasted_iota(jnp.int32, (L, L), 1) // C)
    tri = jnp.where(lax.broadcasted_iota(jnp.int32, (C, C), 0) >= lax.broadcasted_iota(jnp.int32, (C, C), 1),
                    1.0, 0.0).astype(BF16)
    tile = lambda ref, u: ref[u[1] * C:(u[1] + 1) * C, u[0] * L:(u[0] + 1) * L]

    cw = {}
    tri3 = jnp.concatenate([tri, tri, tri], axis=1)
    for u in units:
        cw[u] = _dot(tri3, jnp.concatenate(_split3(tile(lw_ref, u)), axis=0))
    al, rt, bw, kw, v, vs, aa, w_end = {}, {}, {}, {}, {}, {}, {}, {}
    for u in units:
        lw = tile(lw_ref, u)
        w_incl = jnp.exp(cw[u])
        w_excl = jnp.exp(cw[u] - lw)
        w_inv = jnp.exp(-cw[u])
        w_end[u] = w_incl[C - 1:C, :]
        v[u] = bf(tile(v_ref, u))
        vs[u] = stack(v[u])
        al[u] = bf(-tile(kk_ref, u) * w_excl)
        rt[u] = tile(r_ref, u) * w_incl
        bh = tile(b_ref, u) * w_inv
        kh = tile(k_ref, u) * w_inv
        bw[u] = bf(bh * w_end[u])
        kw[u] = bf(kh * w_end[u])
        aa[u] = _dot_nt(jnp.concatenate([al[u], bf(rt[u])], axis=0),
                        jnp.concatenate([stack(bf(bh)), stack(bf(kh))], axis=0))
    a_k, a_rb, pw, t = {}, {}, {}, {}
    for u in units:
        a_ab = jnp.where(low_strict, aa[u][:C, :N], 0.0)
        a_k[u] = bf(jnp.concatenate([jnp.where(low_strict, aa[u][:C, N:], 0.0),
                                     jnp.where(low_incl, aa[u][C:, N:], 0.0)], axis=0))
        a_rb[u] = bf(jnp.where(low_incl, aa[u][C:, :N], 0.0))
        pw[u] = a_ab
        t[u] = eye_w + a_ab
    n_fac = int(np.log2(C))
    akv, rkv = {}, {}
    for u in units:
        pwb = bf(pw[u])
        pw[u] = _dot(pwb, stack(pwb))
        both_v = _dot(a_k[u], vs[u])
        akv[u], rkv[u] = both_v[:C], both_v[C:]
    for s in range(2, n_fac + 1):
        for u in units:
            pwb = bf(pw[u])
            rhs = stack(pwb)
            if s < n_fac:
                both = _dot(jnp.concatenate([bf(t[u]), pwb], axis=0), rhs)
                t[u] = t[u] + both[:C]
                pw[u] = both[C:]
            else:
                t[u] = t[u] + _dot(bf(t[u]), rhs)
    pm, q = {}, {}
    for u in units:
        tb = bf(t[u])
        pm[u] = bf(_dot(tb, stack(al[u])))
        q[u] = bf(_dot(tb, stack(bf(akv[u]))))
    pr, y0, bk = {}, {}, {}
    for u in units:
        rp = bf(rt[u] + _dot(a_rb[u], stack(pm[u])))
        pr[u] = jnp.concatenate([pm[u], rp], axis=0)
        y0[u] = _dot(a_rb[u], stack(q[u])) + rkv[u]
        bk[u] = jnp.concatenate([bw[u], kw[u]], axis=0)
    zt = [zt_ref[p] for p in range(n_packs)]
    for c in range(n_chunks):
        for p in range(n_packs):
            u = (p, c)
            uy = _dot_nt(pr[u], bf(zt[p]))
            y_ref[c * C:(c + 1) * C, p * L:(p + 1) * L] = uy[C:] + y0[u]
            upd = _dot_tn(jnp.concatenate([bf(uy[:C] + q[u]), v[u]], axis=0), bk[u])
            zt[p] = zt[p] * w_end[u] + jnp.where(bd_mask, upd, 0.0)
    for p in range(n_packs):
        zt_ref[p] = zt[p]


def _rwkv_scan(r, lw, k, v, kk, b):
    B, S, D = r.shape
    tile = pl.BlockSpec((None, RW_TB, D), lambda bb, t: (bb, t, 0))
    return pl.pallas_call(
        _rwkv_scan_kernel,
        grid=(B, S // RW_TB),
        in_specs=[tile] * 6,
        out_specs=tile,
        out_shape=jax.ShapeDtypeStruct((B, S, D), F32),
        scratch_shapes=[pltpu.VMEM((D // RW_LANES, RW_LANES, RW_LANES), F32)],
        compiler_params=_cparams("parallel", "arbitrary"),
    )(r, lw, k, v, kk, b)


def _rwkv_out_kernel(y_ref, r_ref, k_ref, v_ref, g_ref, lnw_ref, lnb_ref, rk_ref, gm_ref, gmt_ref,
                     w_ref, h_ref, gt_ref, o_ref):
    y = y_ref[...]
    inv_n = 1.0 / RWKV_HEAD
    yc = y - _gsum(y, gm_ref, gmt_ref) * inv_n
    var = _gsum(yc * yc, gm_ref, gmt_ref) * inv_n
    yn = yc * lax.rsqrt(var + GN_EPS) * lnw_ref[...] + lnb_ref[...]
    v = v_ref[...]
    bonus = _gsum(r_ref[...] * k_ref[...] * rk_ref[...], gm_ref, gmt_ref) * v
    out = ((yn + bonus) * g_ref[...]).astype(BF16)
    o_ref[...] = h_ref[...] + gt_ref[...] * _dot(out, w_ref[...])


def _rwkv_out(y, r, k, v, g, ln_w, ln_b, r_k, w_out, h, mod, tm=256):
    B, S, D = h.shape
    gm, gmt = _head_indicator(D)
    row = lambda a: a.astype(F32).reshape(1, D)
    full = lambda a: pl.BlockSpec(a.shape, lambda b, i: (0,) * a.ndim)
    tile = pl.BlockSpec((None, tm, D), lambda b, i: (b, i, 0))
    params = [row(ln_w), row(ln_b), row(r_k), gm, gmt, w_out.astype(BF16)]
    return pl.pallas_call(
        _rwkv_out_kernel,
        grid=(B, S // tm),
        in_specs=[tile] * 5 + [full(p) for p in params] + [tile, pl.BlockSpec((None, None, 1, D), lambda b, i: (b, 5, 0, 0))],
        out_specs=tile,
        out_shape=jax.ShapeDtypeStruct((B, S, D), F32),
        compiler_params=_cparams("parallel", "parallel"),
    )(y, r, k, v, g, *params, h, mod)


def _rwkv_mixer(h, mod, mu, w_rkv, w0, w1, w2, a0, a1, a2, g1, g2, k_k, k_a, r_k, ln_w, ln_b, w_out):
    r, lw, k, v, kk, b, g = _rwkv_in(h, mod, mu, w_rkv, w0, w1, w2, a0, a1, a2, g1, g2, k_k, k_a)
    y = _rwkv_scan(r, lw, k, v, kk, b)
    return _rwkv_out(y, r, k, v, g, ln_w, ln_b, r_k, w_out, h, mod)


def kernel(x, c, ada_w, ada_b, ffn_w_gu, ffn_w_down, dsa_w_in, dsa_q_norm, dsa_k_norm, dsa_ik_norm, dsa_w_out, rwkv_mu, rwkv_w_rkv, rwkv_w0, rwkv_w1, rwkv_w2, rwkv_a0, rwkv_a1, rwkv_a2, rwkv_g1, rwkv_g2, rwkv_k_k, rwkv_k_a, rwkv_r_k, rwkv_ln_w, rwkv_ln_b, rwkv_w_out):
    B, S, D = x.shape
    depth = ada_w.shape[0]
    mods = _ada_mod(c, ada_w, ada_b)
    h = x
    for i in range(depth):
        mod = mods[i].reshape(B, N_MOD, 1, D)
        j = i // 2
        h = _ffn(h, mod, 0, ffn_w_gu[i, 0].astype(BF16), ffn_w_down[i, 0].astype(BF16))
        if i % 2 == 0:
            h = _dsa_mixer(h, mod, dsa_w_in[j], dsa_q_norm[j], dsa_k_norm[j], dsa_ik_norm[j], dsa_w_out[j])
        else:
            h = _rwkv_mixer(h, mod, rwkv_mu[j], rwkv_w_rkv[j], rwkv_w0[j], rwkv_w1[j], rwkv_w2[j], rwkv_a0[j],
                            rwkv_a1[j], rwkv_a2[j], rwkv_g1[j], rwkv_g2[j], rwkv_k_k[j], rwkv_k_a[j],
                            rwkv_r_k[j], rwkv_ln_w[j], rwkv_ln_b[j], rwkv_w_out[j])
        h = _ffn(h, mod, 6, ffn_w_gu[i, 1].astype(BF16), ffn_w_down[i, 1].astype(BF16))
    return h
```

```python
import functools

import jax
import jax.numpy as jnp
import numpy as np
from jax import lax
from jax.experimental import pallas as pl
from jax.experimental.pallas import tpu as pltpu

F32 = jnp.float32
BF16 = jnp.bfloat16

RMS_EPS = 1e-6
N_MOD = 9
VMEM_LIMIT_BYTES = 56 * 1024 * 1024

N_HEADS = 16
HEAD_DIM = 64
N_KV_HEADS = 4
GROUP = N_HEADS // N_KV_HEADS
IDX_HEADS = 8
IDX_DIM = 64
TOPK_MAX = 256
LANE = 128
LOG2E = 1.4426950408889634


def _cparams(*sem):
    return pltpu.CompilerParams(dimension_semantics=sem, vmem_limit_bytes=VMEM_LIMIT_BYTES)


def _dot(a, b):
    return jnp.dot(a, b, preferred_element_type=F32)


def _dot_nt(a, b):
    return lax.dot_general(a, b, (((1,), (1,)), ((), ())), preferred_element_type=F32)


def _dot_tn(a, b):
    return lax.dot_general(a, b, (((0,), (0,)), ((), ())), preferred_element_type=F32)


def _modulate(h, sh, sc):
    ms = jnp.mean(h * h, axis=-1, keepdims=True)
    return h * lax.rsqrt(ms + RMS_EPS) * (1.0 + sc) + sh


def _ada_kernel(c_ref, w_ref, b_ref, o_ref):
    c = c_ref[...]
    ca = (c * jax.nn.sigmoid(c)).astype(BF16)
    o_ref[...] = _dot(ca, w_ref[...].astype(BF16)) + b_ref[...]


def _ada_mod(c, ada_w, ada_b):
    L, D, N = ada_w.shape
    B = c.shape[0]
    tn = 1536
    return pl.pallas_call(
        _ada_kernel,
        grid=(L, N // tn),
        in_specs=[
            pl.BlockSpec((B, D), lambda l, j: (0, 0)),
            pl.BlockSpec((None, D, tn), lambda l, j: (l, 0, j)),
            pl.BlockSpec((None, 1, tn), lambda l, j: (l, 0, j)),
        ],
        out_specs=pl.BlockSpec((None, B, tn), lambda l, j: (l, 0, j)),
        out_shape=jax.ShapeDtypeStruct((L, B, N), F32),
        compiler_params=_cparams("parallel", "parallel"),
    )(c, ada_w, ada_b.reshape(L, 1, N))


FF_CHUNK = 256


def _ffn_kernel(h_ref, sh_ref, sc_ref, gt_ref, wgu_ref, wd_ref, o_ref, *, d_ff):
    h = h_ref[...]
    u = _modulate(h, sh_ref[...], sc_ref[...]).astype(BF16)
    acc = None
    for c in range(d_ff // FF_CHUNK):
        lo = c * FF_CHUNK
        g = _dot(u, wgu_ref[:, lo:lo + FF_CHUNK])
        up = _dot(u, wgu_ref[:, d_ff + lo:d_ff + lo + FF_CHUNK])
        a = (g * jax.nn.sigmoid(g) * up).astype(BF16)
        part = _dot(a, wd_ref[lo:lo + FF_CHUNK, :])
        acc = part if acc is None else acc + part
    o_ref[...] = h + (0.5 * gt_ref[...]) * acc


def _ffn(h, mod, k0, w_gu, w_down, tm=512):
    B, S, D = h.shape
    d_ff = w_down.shape[0]
    mspec = lambda k: pl.BlockSpec((None, None, 1, D), lambda b, i: (b, k, 0, 0))
    return pl.pallas_call(
        functools.partial(_ffn_kernel, d_ff=d_ff),
        grid=(B, S // tm),
        in_specs=[
            pl.BlockSpec((None, tm, D), lambda b, i: (b, i, 0)),
            mspec(k0), mspec(k0 + 1), mspec(k0 + 2),
            pl.BlockSpec((D, 2 * d_ff), lambda b, i: (0, 0), pipeline_mode=pl.Buffered(1)),
            pl.BlockSpec((d_ff, D), lambda b, i: (0, 0), pipeline_mode=pl.Buffered(1)),
        ],
        out_specs=pl.BlockSpec((None, tm, D), lambda b, i: (b, i, 0)),
        out_shape=jax.ShapeDtypeStruct((B, S, D), F32),
        compiler_params=_cparams("parallel", "parallel"),
    )(h, mod, mod, mod, w_gu, w_down)


def _dsa_in_kernel(h_ref, sh_ref, sc_ref, wT_ref, wn_ref, qn_ref, kn_ref, ikn_ref,
                   qT_ref, k_ref, vT_ref, iqT_ref, ik_ref, iwT_ref):
    u = _modulate(h_ref[...], sh_ref[...], sc_ref[...]).astype(BF16)
    tm = u.shape[0]
    QW, KW, IQW = N_HEADS * HEAD_DIM, N_KV_HEADS * HEAD_DIM, IDX_HEADS * IDX_DIM
    yT = _dot_nt(wT_ref[...], u)
    qT = yT[0:QW].reshape(N_HEADS, HEAD_DIM, tm)
    ms = jnp.mean(qT * qT, axis=1, keepdims=True)
    qT = qT * lax.rsqrt(ms + RMS_EPS) * qn_ref[...][None] * (HEAD_DIM ** -0.5 * LOG2E)
    qT_ref[...] = qT.reshape(QW, tm).astype(BF16)
    for cc in range(tm // DSA_KC):
        vT_ref[cc] = yT[QW:QW + KW, cc * DSA_KC:(cc + 1) * DSA_KC].astype(BF16)
    iqT_ref[...] = yT[QW + KW:QW + KW + IQW].astype(BF16)
    iwT_ref[...] = yT[QW + KW + IQW:QW + KW + IQW + IDX_HEADS] * (IDX_HEADS ** -0.5 * IDX_DIM ** -0.5)
    y = _dot(u, wn_ref[...])
    kn = kn_ref[...]
    for j in range(N_KV_HEADS):
        kj = y[:, j * LANE:(j + 1) * LANE]
        ms = jnp.sum(kj * kj, axis=-1, keepdims=True) * (1.0 / HEAD_DIM)
        k_ref[:, j * LANE:(j + 1) * LANE] = (kj * lax.rsqrt(ms + RMS_EPS) * kn).astype(BF16)
    ik = y[:, N_KV_HEADS * LANE:]
    ms = jnp.sum(ik * ik, axis=-1, keepdims=True) * (1.0 / IDX_DIM)
    ik_ref[...] = (ik * lax.rsqrt(ms + RMS_EPS) * ikn_ref[...]).astype(BF16)


DSA_KC = 256
DSA_QB = 256


def _dsa_in(h, mod, w_in, q_norm, k_norm, ik_norm, tm=2 * DSA_KC):
    B, S, D = h.shape
    QW, KW = N_HEADS * HEAD_DIM, N_KV_HEADS * HEAD_DIM
    IQW = IDX_HEADS * IDX_DIM
    o1, o2, o3, o4, o5 = QW, QW + KW, QW + 2 * KW, QW + 2 * KW + IQW, QW + 2 * KW + IQW + IDX_DIM
    wb = w_in.astype(BF16)
    wqT = wb[:, :o1].T
    wk = jnp.pad(wb[:, o1:o2].reshape(D, N_KV_HEADS, HEAD_DIM), ((0, 0), (0, 0), (0, LANE - HEAD_DIM))).reshape(D, N_KV_HEADS * LANE)
    wvT = wb[:, o2:o3].T
    wiqT = wb[:, o3:o4].T
    wik = jnp.pad(wb[:, o4:o5], ((0, 0), (0, LANE - IDX_DIM)))
    wiwT = jnp.pad(wb[:, o5:].T, ((0, 16 - IDX_HEADS), (0, 0)))
    wT = jnp.concatenate([wqT, wvT, wiqT, wiwT], axis=0)
    wn = jnp.concatenate([wk, wik], axis=1)
    qn = q_norm.astype(F32).reshape(HEAD_DIM, 1)
    kn = jnp.pad(k_norm.astype(F32), (0, LANE - HEAD_DIM)).reshape(1, LANE)
    ikn = jnp.pad(ik_norm.astype(F32), (0, LANE - IDX_DIM)).reshape(1, LANE)
    mspec = lambda k: pl.BlockSpec((None, None, 1, D), lambda b, i: (b, k, 0, 0))
    full = lambda a: pl.BlockSpec(a.shape, lambda b, i: (0,) * a.ndim)
    nc = S // tm
    cpt = tm // DSA_KC
    return pl.pallas_call(
        _dsa_in_kernel,
        grid=(B, nc),
        in_specs=[pl.BlockSpec((None, tm, D), lambda b, i: (b, i, 0)), mspec(3), mspec(4),
                  full(wT), full(wn), full(qn), full(kn), full(ikn)],
        out_specs=[
            pl.BlockSpec((None, QW, tm), lambda b, i: (b, 0, i)),
            pl.BlockSpec((None, tm, N_KV_HEADS * LANE), lambda b, i: (b, i, 0)),
            pl.BlockSpec((None, cpt, KW, DSA_KC), lambda b, i: (b, i, 0, 0)),
            pl.BlockSpec((None, IQW, tm), lambda b, i: (b, 0, i)),
            pl.BlockSpec((None, tm, LANE), lambda b, i: (b, i, 0)),
            pl.BlockSpec((None, IDX_HEADS, tm), lambda b, i: (b, 0, i)),
        ],
        out_shape=[
            jax.ShapeDtypeStruct((B, QW, S), BF16),
            jax.ShapeDtypeStruct((B, S, N_KV_HEADS * LANE), BF16),
            jax.ShapeDtypeStruct((B, nc * cpt, KW, DSA_KC), BF16),
            jax.ShapeDtypeStruct((B, IQW, S), BF16),
            jax.ShapeDtypeStruct((B, S, LANE), BF16),
            jax.ShapeDtypeStruct((B, IDX_HEADS, S), F32),
        ],
        compiler_params=_cparams("parallel", "parallel"),
    )(h, mod, mod, wT, wn, qn, kn, ikn)


NEG_BIG = -(2.0 ** 100)
DSA_VPAD = 16
N_CHAINS = 4


def _ukey_to_float(ukey):
    key = ukey ^ np.int32(-2 ** 31)
    bits = key ^ ((key >> 31) & np.int32(0x7FFFFFFF))
    return lax.bitcast_convert_type(bits, F32)


def _fold_rows(x, op, rows):
    while x.shape[0] > rows:
        half = x.shape[0] // 2
        x = op(x[:half], x[half:])
    return x


def _dsa_core_kernel(iqT_ref, iwT_ref, qT_ref, ik_ref, k_ref, vT_ref, oT_ref,
                     sc_ref, fl_ref, whi_ref, wlo_ref, qa_ref, iqa_ref, d_ref, st_ref, p_ref, m_ref, acc_ref,
                     *, k_top, n_chunks):
    KC, QB = DSA_KC, DSA_QB
    qi = pl.program_id(1)
    q0 = qi * QB
    nk = (q0 + QB + KC - 1) // KC
    qpos = q0 + lax.broadcasted_iota(jnp.int32, (1, QB), 1)
    zeros_h = jnp.zeros((LANE - HEAD_DIM, QB), BF16)

    for h in range(IDX_HEADS):
        iqa_ref[:, h * QB:(h + 1) * QB] = jnp.concatenate(
            [iqT_ref[h * IDX_DIM:(h + 1) * IDX_DIM, :], zeros_h], axis=0)
    for j in range(N_KV_HEADS):
        for g in range(GROUP):
            hh = j * GROUP + g
            qa_ref[j, :, g * QB:(g + 1) * QB] = jnp.concatenate(
                [qT_ref[hh * HEAD_DIM:(hh + 1) * HEAD_DIM, :], zeros_h], axis=0)
    iw = iwT_ref[...]

    npair = (nk + 1) // 2

    def clamp_chunk(c):
        return jnp.minimum(c, n_chunks - 1)

    def idx_dot(c, slot):
        k0 = pl.multiple_of(c * KC, KC)
        d_ref[slot] = _dot(ik_ref[pl.ds(k0, KC), :], iqa_ref[...])

    def idx_score(c, slot):
        k0 = pl.multiple_of(c * KC, KC)
        s = jnp.maximum(d_ref[slot, :, 0:QB], 0.0) * iw[0:1, :]
        for h in range(1, IDX_HEADS):
            s = s + jnp.maximum(d_ref[slot, :, h * QB:(h + 1) * QB], 0.0) * iw[h:h + 1, :]
        kpos = k0 + lax.broadcasted_iota(jnp.int32, (KC, 1), 0)
        s = jnp.where(kpos <= qpos, s, -jnp.inf)
        sc_ref[pl.ds(k0, KC), :] = s
        r = s.astype(BF16)
        rf = r.astype(F32)
        below = (rf * jnp.where(rf > 0.0, 1.0 - 1.25 * 2.0 ** -8, 1.0 + 1.25 * 2.0 ** -8)).astype(BF16)
        fl_ref[pl.ds(k0, KC), :] = jnp.where(rf > s, below, r)

    idx_dot(0, 0)

    def p1(i, carry):
        c = 2 * i
        idx_dot(c + 1, 1)
        idx_score(c, 0)
        idx_dot(clamp_chunk(c + 2), 0)
        idx_score(c + 1, 1)
        return carry

    lax.fori_loop(0, npair, p1, 0)

    def count_not(pred):
        def body(i, accs):
            accs = list(accs)
            for half in range(2):
                k0 = pl.multiple_of((2 * i + half) * KC, KC)
                ind = jnp.where(pred(sc_ref[pl.ds(k0, KC), :]), 0.0, 1.0)
                for r in range(KC // 8):
                    accs[r % N_CHAINS] = accs[r % N_CHAINS] + ind[r * 8:(r + 1) * 8]
            return tuple(accs)
        accs = lax.fori_loop(0, npair, body, tuple(jnp.zeros((8, QB), F32) for _ in range(N_CHAINS)))
        acc = (accs[0] + accs[1]) + (accs[2] + accs[3])
        return jnp.sum(acc, axis=0, keepdims=True)

    n_staged = (npair * (2 * KC)).astype(F32)

    def count_ge16(ref, t):
        tb = t.astype(BF16)
        one, zero = jnp.ones((), BF16), jnp.zeros((), BF16)

        def body(i, accs):
            accs = list(accs)
            for half in range(2):
                k0 = pl.multiple_of((2 * i + half) * KC, KC)
                ind = jnp.where(ref[pl.ds(k0, KC), :] < tb, zero, one)
                for r in range(KC // 16):
                    accs[r % N_CHAINS] = accs[r % N_CHAINS] + ind[r * 16:(r + 1) * 16]
            return tuple(accs)
        accs = lax.fori_loop(0, npair, body, tuple(jnp.zeros((16, QB), BF16) for _ in range(N_CHAINS)))
        acc = (accs[0].astype(F32) + accs[1].astype(F32)) + (accs[2].astype(F32) + accs[3].astype(F32))
        return jnp.sum(acc, axis=0, keepdims=True)

    def digit_search(ref, n_bits, k_need, to_float):
        d = jnp.zeros((1, QB), F32)
        for bit in range(n_bits - 1, -1, -1):
            cand = d + float(1 << bit)
            d = jnp.where(count_ge16(ref, to_float(cand)) >= k_need, cand, d)
        return d

    def val16(d):
        di = d.astype(jnp.int32)
        return _ukey_to_float(jnp.where(di >= 0x8000, di << 16, (di << 16) | np.int32(0xFFFF)))

    key_hi = digit_search(fl_ref, 16, k_top, val16)
    h_lo = val16(key_hi)
    h_up = val16(key_hi + 1.0)
    no_thr = h_lo == -jnp.inf
    h_up = jnp.where(h_up > h_lo, h_up, np.float32(2.0 ** -126))
    width = jnp.where(no_thr, 1.0, h_up - h_lo)
    tiny = width < 2.0 ** -60
    pre = jnp.where(tiny, np.float32(2.0 ** 60), np.float32(1.0))
    unpre = jnp.where(tiny, np.float32(2.0 ** -60), np.float32(1.0))
    from_int = (width * pre) * np.float32(2.0 ** -16)
    to_int = lax.bitcast_convert_type(np.int32(0x7F000000) - lax.bitcast_convert_type(from_int, jnp.int32), F32)

    def split(i, carry):
        for half in range(2):
            k0 = pl.multiple_of((2 * i + half) * KC, KC)
            w = jnp.clip(((sc_ref[pl.ds(k0, KC), :] - h_lo) * pre) * to_int, -256.0, 65536.0)
            w_hi = jnp.floor(w * (1.0 / 256.0))
            whi_ref[pl.ds(k0, KC), :] = w_hi.astype(BF16)
            wlo_ref[pl.ds(k0, KC), :] = (w - 256.0 * w_hi).astype(BF16)
        return carry
    lax.fori_loop(0, npair, split, 0)

    ident = lambda d: d
    d_mid = digit_search(whi_ref, 8, k_top, ident)
    k_low = k_top - count_ge16(whi_ref, d_mid + 1.0)
    d_mid_b = d_mid.astype(BF16)

    def keep_low(i, carry):
        for half in range(2):
            k0 = pl.multiple_of((2 * i + half) * KC, KC)
            whi_ref[pl.ds(k0, KC), :] = jnp.where(whi_ref[pl.ds(k0, KC), :] == d_mid_b,
                                                  wlo_ref[pl.ds(k0, KC), :], -jnp.ones((), BF16))
        return carry
    lax.fori_loop(0, npair, keep_low, 0)
    d_low = digit_search(whi_ref, 8, k_low, ident)
    offset = ((d_mid * 256.0 + d_low) * from_int) * unpre
    thr = jnp.where(no_thr, -jnp.inf, h_lo + offset)

    m_ref[...] = jnp.full(m_ref.shape, NEG_BIG, F32)
    acc_ref[...] = jnp.zeros(acc_ref.shape, F32)
    r_i = lax.broadcasted_iota(jnp.int32, (KC, KC), 0)
    c_i = lax.broadcasted_iota(jnp.int32, (KC, KC), 1)
    tril = jnp.where(c_i <= r_i, 1.0, 0.0).astype(BF16)
    ones_rows = jnp.ones((DSA_VPAD, KC), BF16)

    HALF = GROUP * QB // 2

    def logits(c, slot, j, half):
        k0 = pl.multiple_of(c * KC, KC)
        cols = slice(half * HALF, (half + 1) * HALF)
        st_ref[slot, j, :, cols] = _dot(k_ref[pl.ds(k0, KC), j * LANE:(j + 1) * LANE],
                                        qa_ref[j, :, cols]).astype(BF16)

    for j in range(N_KV_HEADS):
        logits(0, 0, j, 0)
        logits(0, 0, j, 1)

    def attend(c, slot, tie_seen, c_next, need):
        k0 = pl.multiple_of(c * KC, KC)
        s = sc_ref[pl.ds(k0, KC), :]
        kpos = k0 + lax.broadcasted_iota(jnp.int32, (KC, 1), 0)
        eq = s == thr
        eqf = jnp.where(eq, 1.0, 0.0)
        incl = _dot(tril, eqf.astype(BF16))
        rank = tie_seen + incl - eqf
        sel = jnp.logical_and(jnp.logical_or(s > thr, jnp.logical_and(eq, rank < need)), kpos <= qpos)
        tie_seen = tie_seen + incl[KC - 1:KC, :]
        bias = jnp.where(sel, 0.0, NEG_BIG).astype(BF16)
        for j in range(N_KV_HEADS):
            alphas = []
            for g in range(GROUP):
                sl = slice(g * QB, (g + 1) * QB)
                sg = st_ref[slot, j, :, sl] + bias
                m_old = m_ref[j, :, sl]
                mx = jnp.max(_fold_rows(sg, jnp.maximum, 16), axis=0, keepdims=True).astype(F32)
                m_new = jnp.maximum(m_old, mx)
                alphas.append(jnp.exp2(m_old - m_new))
                m_ref[j, :, sl] = m_new
                p_ref[j, :, sl] = jnp.exp2(sg - m_new.astype(BF16))
                if g % 2 == 1:
                    logits(c_next, 1 - slot, j, g // 2)
            v_aug = jnp.concatenate([vT_ref[c, j * HEAD_DIM:(j + 1) * HEAD_DIM, :], ones_rows], axis=0)
            acc_ref[j] = acc_ref[j] * jnp.concatenate(alphas, axis=1) + _dot(v_aug, p_ref[j])
        return tie_seen

    need = k_top - (n_staged - count_not(lambda s: s > thr))

    def p3(i, tie_seen):
        c = 2 * i
        tie_seen = attend(c, 0, tie_seen, c + 1, need)
        return attend(c + 1, 1, tie_seen, clamp_chunk(c + 2), need)

    lax.fori_loop(0, npair, p3, jnp.zeros((1, QB), F32))

    for j in range(N_KV_HEADS):
        o = acc_ref[j, 0:HEAD_DIM, :] / acc_ref[j, HEAD_DIM:HEAD_DIM + 1, :]
        for g in range(GROUP):
            hh = j * GROUP + g
            oT_ref[hh * HEAD_DIM:(hh + 1) * HEAD_DIM, :] = o[:, g * QB:(g + 1) * QB].astype(BF16)


def _dsa_core(qT, k, vT, iqT, ik, iwT):
    B, QW, S = qT.shape
    KC, QB = DSA_KC, DSA_QB
    k_top = min(TOPK_MAX, S // 4)
    nc = S // KC
    return pl.pallas_call(
        functools.partial(_dsa_core_kernel, k_top=k_top, n_chunks=nc),
        grid=(B, S // QB),
        in_specs=[
            pl.BlockSpec((None, IDX_HEADS * IDX_DIM, QB), lambda b, i: (b, 0, i)),
            pl.BlockSpec((None, IDX_HEADS, QB), lambda b, i: (b, 0, i)),
            pl.BlockSpec((None, QW, QB), lambda b, i: (b, 0, i)),
            pl.BlockSpec((None, S, LANE), lambda b, i: (b, 0, 0)),
            pl.BlockSpec((None, S, N_KV_HEADS * LANE), lambda b, i: (b, 0, 0)),
            pl.BlockSpec((None, nc, N_KV_HEADS * HEAD_DIM, KC), lambda b, i: (b, 0, 0, 0)),
        ],
        out_specs=pl.BlockSpec((None, QW, QB), lambda b, i: (b, 0, i)),
        out_shape=jax.ShapeDtypeStruct((B, QW, S), BF16),
        scratch_shapes=[
            pltpu.VMEM((S, QB), F32),
            pltpu.VMEM((S, QB), BF16),
            pltpu.VMEM((S, QB), BF16),
            pltpu.VMEM((S, QB), BF16),
            pltpu.VMEM((N_KV_HEADS, LANE, GROUP * QB), BF16),
            pltpu.VMEM((LANE, IDX_HEADS * QB), BF16),
            pltpu.VMEM((2, KC, IDX_HEADS * QB), F32),
            pltpu.VMEM((2, N_KV_HEADS, KC, GROUP * QB), BF16),
            pltpu.VMEM((N_KV_HEADS, KC, GROUP * QB), BF16),
            pltpu.VMEM((N_KV_HEADS, 1, GROUP * QB), F32),
            pltpu.VMEM((N_KV_HEADS, HEAD_DIM + DSA_VPAD, GROUP * QB), F32),
        ],
        compiler_params=_cparams("parallel", "arbitrary"),
    )(iqT, iwT, qT, ik, k, vT)


def _out_T_kernel(xT_ref, w_ref, h_ref, gt_ref, o_ref):
    o_ref[...] = h_ref[...] + gt_ref[...] * _dot_tn(xT_ref[...], w_ref[...])


def _dsa_out(oT, w_out, h, mod, tm=512):
    B, S, D = h.shape
    QW = oT.shape[1]
    return pl.pallas_call(
        _out_T_kernel,
        grid=(B, S // tm),
        in_specs=[
            pl.BlockSpec((None, QW, tm), lambda b, i: (b, 0, i)),
            pl.BlockSpec((QW, D), lambda b, i: (0, 0)),
            pl.BlockSpec((None, tm, D), lambda b, i: (b, i, 0)),
            pl.BlockSpec((None, None, 1, D), lambda b, i: (b, 5, 0, 0)),
        ],
        out_specs=pl.BlockSpec((None, tm, D), lambda b, i: (b, i, 0)),
        out_shape=jax.ShapeDtypeStruct((B, S, D), F32),
        compiler_params=_cparams("parallel", "parallel"),
    )(oT, w_out.astype(BF16), h, mod)


def _dsa_mixer(h, mod, w_in, q_norm, k_norm, ik_norm, w_out):
    qT, k, vT, iqT, ik, iwT = _dsa_in(h, mod, w_in, q_norm, k_norm, ik_norm)
    oT = _dsa_core(qT, k, vT, iqT, ik, iwT)
    return _dsa_out(oT, w_out, h, mod)


RWKV_HEAD = 64
GN_EPS = 64e-5
RW_PACK = 4
RW_LANES = RW_PACK * RWKV_HEAD
RW_C = 64
RW_TB = 512


def _split3(x):
    hi = x.astype(BF16)
    r1 = x - hi.astype(F32)
    mid = r1.astype(BF16)
    lo = (r1 - mid.astype(F32)).astype(BF16)
    return hi, mid, lo


def _gsum(x, g_ref, gt_ref):
    hi = x.astype(BF16)
    lo = (x - hi.astype(F32)).astype(BF16)
    s = _dot(jnp.concatenate([hi, lo], axis=1), g_ref[...])
    return _dot(jnp.concatenate(_split3(s), axis=1), gt_ref[...])


def _rwkv_in_kernel(h_ref, hp_ref, sh_ref, sc_ref, mu_ref, wrkv_ref, w0_ref, w1_ref, w2_ref,
                    a0_ref, a1_ref, a2_ref, g1_ref, g2_ref, kk_ref, ka_ref, gm_ref, gmt_ref,
                    r_ref, lw_ref, k_ref, v_ref, kkn_ref, b_ref, g_ref):
    sh, sc = sh_ref[...], sc_ref[...]
    u = _modulate(h_ref[...], sh, sc)
    tm = u.shape[0]
    prev = _modulate(hp_ref[...], sh, sc)[7:8, :]
    prev = jnp.where(pl.program_id(1) == 0, 0.0, prev)
    row = lax.broadcasted_iota(jnp.int32, (tm, 1), 0)
    xx = jnp.where(row == 0, prev, pltpu.roll(u, 1, 0)) - u
    mix = lambda i: (u + xx * mu_ref[i:i + 1, :]).astype(BF16)
    wl = w0_ref[...] + _dot(jnp.tanh(_dot(mix(3), w1_ref[...])).astype(BF16), w2_ref[...])
    a_lin = _dot(_dot(mix(4), a1_ref[...]).astype(BF16), a2_ref[...])
    g_hid = _dot(mix(5), g1_ref[...])
    k = _dot(mix(1), wrkv_ref[1])
    nwl = -wl
    w_log = -(jnp.maximum(nwl, 0.0) + jnp.log(1.0 + jnp.exp(-jnp.abs(nwl)))) - 0.5
    lw_ref[...] = -jnp.exp(w_log)
    r_ref[...] = _dot(mix(0), wrkv_ref[0])
    a = jax.nn.sigmoid(a0_ref[...] + a_lin)
    g_ref[...] = _dot(jax.nn.sigmoid(g_hid).astype(BF16), g2_ref[...])
    kk = k * kk_ref[...]
    nrm = jnp.sqrt(_gsum(kk * kk, gm_ref, gmt_ref))
    kk = kk / jnp.maximum(nrm, 1e-12)
    k_ref[...] = k * (1.0 + (a - 1.0) * ka_ref[...])
    kkn_ref[...] = kk
    b_ref[...] = kk * a
    v_ref[...] = _dot(mix(2), wrkv_ref[2])


def _head_indicator(D):
    gm = (np.arange(D)[:, None] // RWKV_HEAD == np.arange(D // RWKV_HEAD)[None, :]).astype(np.float32)
    return jnp.asarray(np.concatenate([gm, gm], axis=0), BF16), jnp.asarray(np.concatenate([gm.T] * 3, axis=0), BF16)


def _rwkv_in(h, mod, mu, w_rkv, w0, w1, w2, a0, a1, a2, g1, g2, k_k, k_a, tm=256):
    B, S, D = h.shape
    gm, gmt = _head_indicator(D)
    row = lambda a: a.astype(F32).reshape(1, D)
    full = lambda a: pl.BlockSpec(a.shape, lambda b, i: (0,) * a.ndim)
    mspec = lambda k: pl.BlockSpec((None, None, 1, D), lambda b, i: (b, k, 0, 0))
    tile = pl.BlockSpec((None, tm, D), lambda b, i: (b, i, 0))
    params = [mu.astype(F32), w_rkv.astype(BF16), row(w0), w1.astype(BF16), w2.astype(BF16),
              row(a0), a1.astype(BF16), a2.astype(BF16), g1.astype(BF16), g2.astype(BF16),
              row(k_k), row(k_a), gm, gmt]
    return pl.pallas_call(
        _rwkv_in_kernel,
        grid=(B, S // tm),
        in_specs=[tile,
                  pl.BlockSpec((None, 8, D), lambda b, i: (b, jnp.maximum(i * (tm // 8) - 1, 0), 0)),
                  mspec(3), mspec(4)] + [full(p) for p in params],
        out_specs=[tile] * 7,
        out_shape=[jax.ShapeDtypeStruct((B, S, D), F32)] * 7,
        compiler_params=_cparams("parallel", "parallel"),
    )(h, h, mod, mod, *params)


def _rwkv_scan_kernel(r_ref, lw_ref, k_ref, v_ref, kk_ref, b_ref, y_ref, zt_ref):
    C, P, L = RW_C, RW_PACK, RW_LANES
    N = P * C
    assert C == RWKV_HEAD
    n_packs = r_ref.shape[1] // L
    n_chunks = r_ref.shape[0] // C
    units = [(p, c) for p in range(n_packs) for c in range(n_chunks)]

    @pl.when(pl.program_id(1) == 0)
    def _():
        zt_ref[...] = jnp.zeros(zt_ref.shape, F32)

    lane = lax.broadcasted_iota(jnp.int32, (1, L), 1)
    hmask = [lane // C == hd for hd in range(P)]
    bf = lambda x: x.astype(BF16)
    stack = lambda x: jnp.concatenate([jnp.where(hmask[hd], x, jnp.zeros_like(x)) for hd in range(P)], axis=0)
    t_i = lax.broadcasted_iota(jnp.int32, (C, N), 0)
    i_i = lax.broadcasted_iota(jnp.int32, (C, N), 1) % C
    low_strict = t_i > i_i
    low_incl = t_i >= i_i
    eye_w = jnp.where(t_i == i_i, 1.0, 0.0)
    bd_mask = (lax.broadcasted_iota(jnp.int32, (L, L), 0) // C) == (lax.broadcasted_iota(jnp.int32, (L, L), 1) // C)
    tri = jnp.where(lax.broadcasted_iota(jnp.int32, (C, C), 0) >= lax.broadcasted_iota(jnp.int32, (C, C), 1),
                    1.0, 0.0).astype(BF16)
    tile = lambda ref, u: ref[u[1] * C:(u[1] + 1) * C, u[0] * L:(u[0] + 1) * L]

    cw = {}
    tri3 = jnp.concatenate([tri, tri, tri], axis=1)
    for u in units:
        cw[u] = _dot(tri3, jnp.concatenate(_split3(tile(lw_ref, u)), axis=0))
    al, rt, bw, kw, v, vs, aa, w_end = {}, {}, {}, {}, {}, {}, {}, {}
    for u in units:
        lw = tile(lw_ref, u)
        w_incl = jnp.exp(cw[u])
        w_excl = jnp.exp(cw[u] - lw)
        w_inv = jnp.exp(-cw[u])
        w_end[u] = w_incl[C - 1:C, :]
        v[u] = bf(tile(v_ref, u))
        vs[u] = stack(v[u])
        al[u] = bf(-tile(kk_ref, u) * w_excl)
        rt[u] = tile(r_ref, u) * w_incl
        bh = tile(b_ref, u) * w_inv
        kh = tile(k_ref, u) * w_inv
        bw[u] = bf(bh * w_end[u])
        kw[u] = bf(kh * w_end[u])
        aa[u] = _dot_nt(jnp.concatenate([al[u], bf(rt[u])], axis=0),
                        jnp.concatenate([stack(bf(bh)), stack(bf(kh))], axis=0))
    a_k, a_rb, pw, t = {}, {}, {}, {}
    for u in units:
        a_ab = jnp.where(low_strict, aa[u][:C, :N], 0.0)
        a_k[u] = bf(jnp.concatenate([jnp.where(low_strict, aa[u][:C, N:], 0.0),
                                     jnp.where(low_incl, aa[u][C:, N:], 0.0)], axis=0))
        a_rb[u] = bf(jnp.where(low_incl, aa[u][C:, :N], 0.0))
        pw[u] = a_ab
        t[u] = eye_w + a_ab
    n_fac = int(np.log2(C))
    akv, rkv = {}, {}
    for u in units:
        pwb = bf(pw[u])
        pw[u] = _dot(pwb, stack(pwb))
        both_v = _dot(a_k[u], vs[u])
        akv[u], rkv[u] = both_v[:C], both_v[C:]
    for s in range(2, n_fac + 1):
        for u in units:
            pwb = bf(pw[u])
            rhs = stack(pwb)
            if s < n_fac:
                both = _dot(jnp.concatenate([bf(t[u]), pwb], axis=0), rhs)
                t[u] = t[u] + both[:C]
                pw[u] = both[C:]
            else:
                t[u] = t[u] + _dot(bf(t[u]), rhs)
    pm, q = {}, {}
    for u in units:
        tb = bf(t[u])
        pm[u] = bf(_dot(tb, stack(al[u])))
        q[u] = bf(_dot(tb, stack(bf(akv[u]))))
    pr, y0, bk = {}, {}, {}
    for u in units:
        rp = bf(rt[u] + _dot(a_rb[u], stack(pm[u])))
        pr[u] = jnp.concatenate([pm[u], rp], axis=0)
        y0[u] = _dot(a_rb[u], stack(q[u])) + rkv[u]
        bk[u] = jnp.concatenate([bw[u], kw[u]], axis=0)
    zt = [zt_ref[p] for p in range(n_packs)]
    for c in range(n_chunks):
        for p in range(n_packs):
            u = (p, c)
            uy = _dot_nt(pr[u], bf(zt[p]))
            y_ref[c * C:(c + 1) * C, p * L:(p + 1) * L] = uy[C:] + y0[u]
            upd = _dot_tn(jnp.concatenate([bf(uy[:C] + q[u]), v[u]], axis=0), bk[u])
            zt[p] = zt[p] * w_end[u] + jnp.where(bd_mask, upd, 0.0)
    for p in range(n_packs):
        zt_ref[p] = zt[p]


def _rwkv_scan(r, lw, k, v, kk, b):
    B, S, D = r.shape
    tile = pl.BlockSpec((None, RW_TB, D), lambda bb, t: (bb, t, 0))
    return pl.pallas_call(
        _rwkv_scan_kernel,
        grid=(B, S // RW_TB),
        in_specs=[tile] * 6,
        out_specs=tile,
        out_shape=jax.ShapeDtypeStruct((B, S, D), F32),
        scratch_shapes=[pltpu.VMEM((D // RW_LANES, RW_LANES, RW_LANES), F32)],
        compiler_params=_cparams("parallel", "arbitrary"),
    )(r, lw, k, v, kk, b)


def _rwkv_out_kernel(y_ref, r_ref, k_ref, v_ref, g_ref, lnw_ref, lnb_ref, rk_ref, gm_ref, gmt_ref,
                     w_ref, h_ref, gt_ref, o_ref):
    y = y_ref[...]
    inv_n = 1.0 / RWKV_HEAD
    yc = y - _gsum(y, gm_ref, gmt_ref) * inv_n
    var = _gsum(yc * yc, gm_ref, gmt_ref) * inv_n
    yn = yc * lax.rsqrt(var + GN_EPS) * lnw_ref[...] + lnb_ref[...]
    v = v_ref[...]
    bonus = _gsum(r_ref[...] * k_ref[...] * rk_ref[...], gm_ref, gmt_ref) * v
    out = ((yn + bonus) * g_ref[...]).astype(BF16)
    o_ref[...] = h_ref[...] + gt_ref[...] * _dot(out, w_ref[...])


def _rwkv_out(y, r, k, v, g, ln_w, ln_b, r_k, w_out, h, mod, tm=256):
    B, S, D = h.shape
    gm, gmt = _head_indicator(D)
    row = lambda a: a.astype(F32).reshape(1, D)
    full = lambda a: pl.BlockSpec(a.shape, lambda b, i: (0,) * a.ndim)
    tile = pl.BlockSpec((None, tm, D), lambda b, i: (b, i, 0))
    params = [row(ln_w), row(ln_b), row(r_k), gm, gmt, w_out.astype(BF16)]
    return pl.pallas_call(
        _rwkv_out_kernel,
        grid=(B, S // tm),
        in_specs=[tile] * 5 + [full(p) for p in params] + [tile, pl.BlockSpec((None, None, 1, D), lambda b, i: (b, 5, 0, 0))],
        out_specs=tile,
        out_shape=jax.ShapeDtypeStruct((B, S, D), F32),
        compiler_params=_cparams("parallel", "parallel"),
    )(y, r, k, v, g, *params, h, mod)


def _rwkv_mixer(h, mod, mu, w_rkv, w0, w1, w2, a0, a1, a2, g1, g2, k_k, k_a, r_k, ln_w, ln_b, w_out):
    r, lw, k, v, kk, b, g = _rwkv_in(h, mod, mu, w_rkv, w0, w1, w2, a0, a1, a2, g1, g2, k_k, k_a)
    y = _rwkv_scan(r, lw, k, v, kk, b)
    return _rwkv_out(y, r, k, v, g, ln_w, ln_b, r_k, w_out, h, mod)


def kernel(x, c, ada_w, ada_b, ffn_w_gu, ffn_w_down, dsa_w_in, dsa_q_norm, dsa_k_norm, dsa_ik_norm, dsa_w_out, rwkv_mu, rwkv_w_rkv, rwkv_w0, rwkv_w1, rwkv_w2, rwkv_a0, rwkv_a1, rwkv_a2, rwkv_g1, rwkv_g2, rwkv_k_k, rwkv_k_a, rwkv_r_k, rwkv_ln_w, rwkv_ln_b, rwkv_w_out):
    B, S, D = x.shape
    depth = ada_w.shape[0]
    mods = _ada_mod(c, ada_w, ada_b)
    h = x
    for i in range(depth):
        mod = mods[i].reshape(B, N_MOD, 1, D)
        j = i // 2
        h = _ffn(h, mod, 0, ffn_w_gu[i, 0].astype(BF16), ffn_w_down[i, 0].astype(BF16))
        if i % 2 == 0:
            h = _dsa_mixer(h, mod, dsa_w_in[j], dsa_q_norm[j], dsa_k_norm[j], dsa_ik_norm[j], dsa_w_out[j])
        else:
            h = _rwkv_mixer(h, mod, rwkv_mu[j], rwkv_w_rkv[j], rwkv_w0[j], rwkv_w1[j], rwkv_w2[j], rwkv_a0[j],
                            rwkv_a1[j], rwkv_a2[j], rwkv_g1[j], rwkv_g2[j], rwkv_k_k[j], rwkv_k_a[j],
                            rwkv_r_k[j], rwkv_ln_w[j], rwkv_ln_b[j], rwkv_w_out[j])
        h = _ffn(h, mod, 6, ffn_w_gu[i, 1].astype(BF16), ffn_w_down[i, 1].astype(BF16))
    return h
```

```python
import functools

import jax
import jax.numpy as jnp
import numpy as np
from jax import lax
from jax.experimental import pallas as pl
from jax.experimental.pallas import tpu as pltpu

F32 = jnp.float32
BF16 = jnp.bfloat16

RMS_EPS = 1e-6
N_MOD = 9
VMEM_LIMIT_BYTES = 56 * 1024 * 1024

N_HEADS = 16
HEAD_DIM = 64
N_KV_HEADS = 4
GROUP = N_HEADS // N_KV_HEADS
IDX_HEADS = 8
IDX_DIM = 64
TOPK_MAX = 256
LANE = 128
LOG2E = 1.4426950408889634


def _cparams(*sem):
    return pltpu.CompilerParams(dimension_semantics=sem, vmem_limit_bytes=VMEM_LIMIT_BYTES)


def _dot(a, b):
    return jnp.dot(a, b, preferred_element_type=F32)


def _dot_nt(a, b):
    return lax.dot_general(a, b, (((1,), (1,)), ((), ())), preferred_element_type=F32)


def _dot_tn(a, b):
    return lax.dot_general(a, b, (((0,), (0,)), ((), ())), preferred_element_type=F32)


def _modulate(h, sh, sc):
    ms = jnp.mean(h * h, axis=-1, keepdims=True)
    return h * lax.rsqrt(ms + RMS_EPS) * (1.0 + sc) + sh


def _ada_kernel(c_ref, w_ref, b_ref, o_ref):
    c = c_ref[...]
    ca = (c * jax.nn.sigmoid(c)).astype(BF16)
    o_ref[...] = _dot(ca, w_ref[...].astype(BF16)) + b_ref[...]


def _ada_mod(c, ada_w, ada_b):
    L, D, N = ada_w.shape
    B = c.shape[0]
    tn = 1536
    return pl.pallas_call(
        _ada_kernel,
        grid=(L, N // tn),
        in_specs=[
            pl.BlockSpec((B, D), lambda l, j: (0, 0)),
            pl.BlockSpec((None, D, tn), lambda l, j: (l, 0, j)),
            pl.BlockSpec((None, 1, tn), lambda l, j: (l, 0, j)),
        ],
        out_specs=pl.BlockSpec((None, B, tn), lambda l, j: (l, 0, j)),
        out_shape=jax.ShapeDtypeStruct((L, B, N), F32),
        compiler_params=_cparams("parallel", "parallel"),
    )(c, ada_w, ada_b.reshape(L, 1, N))


FF_CHUNK = 256


def _ffn_kernel(h_ref, sh_ref, sc_ref, gt_ref, wgu_ref, wd_ref, o_ref, *, d_ff):
    h = h_ref[...]
    u = _modulate(h, sh_ref[...], sc_ref[...]).astype(BF16)
    acc = None
    for c in range(d_ff // FF_CHUNK):
        lo = c * FF_CHUNK
        g = _dot(u, wgu_ref[:, lo:lo + FF_CHUNK])
        up = _dot(u, wgu_ref[:, d_ff + lo:d_ff + lo + FF_CHUNK])
        a = (g * jax.nn.sigmoid(g) * up).astype(BF16)
        part = _dot(a, wd_ref[lo:lo + FF_CHUNK, :])
        acc = part if acc is None else acc + part
    o_ref[...] = h + (0.5 * gt_ref[...]) * acc


def _ffn(h, mod, k0, w_gu, w_down, tm=512):
    B, S, D = h.shape
    d_ff = w_down.shape[0]
    assert S % tm == 0 and d_ff % FF_CHUNK == 0
    mspec = lambda k: pl.BlockSpec((None, None, 1, D), lambda b, i: (b, k, 0, 0))
    return pl.pallas_call(
        functools.partial(_ffn_kernel, d_ff=d_ff),
        grid=(B, S // tm),
        in_specs=[
            pl.BlockSpec((None, tm, D), lambda b, i: (b, i, 0)),
            mspec(k0), mspec(k0 + 1), mspec(k0 + 2),
            pl.BlockSpec((D, 2 * d_ff), lambda b, i: (0, 0), pipeline_mode=pl.Buffered(1)),
            pl.BlockSpec((d_ff, D), lambda b, i: (0, 0), pipeline_mode=pl.Buffered(1)),
        ],
        out_specs=pl.BlockSpec((None, tm, D), lambda b, i: (b, i, 0)),
        out_shape=jax.ShapeDtypeStruct((B, S, D), F32),
        compiler_params=_cparams("parallel", "parallel"),
    )(h, mod, mod, mod, w_gu, w_down)


def _dsa_in_kernel(h_ref, sh_ref, sc_ref, wT_ref, wn_ref, qn_ref, kn_ref, ikn_ref,
                   qT_ref, k_ref, vT_ref, iqT_ref, ik_ref, iwT_ref):
    u = _modulate(h_ref[...], sh_ref[...], sc_ref[...]).astype(BF16)
    tm = u.shape[0]
    QW, KW, IQW = N_HEADS * HEAD_DIM, N_KV_HEADS * HEAD_DIM, IDX_HEADS * IDX_DIM
    yT = _dot_nt(wT_ref[...], u)
    qT = yT[0:QW].reshape(N_HEADS, HEAD_DIM, tm)
    ms = jnp.mean(qT * qT, axis=1, keepdims=True)
    qT = qT * lax.rsqrt(ms + RMS_EPS) * qn_ref[...][None] * (HEAD_DIM ** -0.5 * LOG2E)
    qT_ref[...] = qT.reshape(QW, tm).astype(BF16)
    for cc in range(tm // DSA_KC):
        vT_ref[cc] = yT[QW:QW + KW, cc * DSA_KC:(cc + 1) * DSA_KC].astype(BF16)
    iqT_ref[...] = yT[QW + KW:QW + KW + IQW].astype(BF16)
    iwT_ref[...] = yT[QW + KW + IQW:QW + KW + IQW + IDX_HEADS] * (IDX_HEADS ** -0.5 * IDX_DIM ** -0.5)
    y = _dot(u, wn_ref[...])
    kn = kn_ref[...]
    for j in range(N_KV_HEADS):
        kj = y[:, j * LANE:(j + 1) * LANE]
        ms = jnp.sum(kj * kj, axis=-1, keepdims=True) * (1.0 / HEAD_DIM)
        k_ref[:, j * LANE:(j + 1) * LANE] = (kj * lax.rsqrt(ms + RMS_EPS) * kn).astype(BF16)
    ik = y[:, N_KV_HEADS * LANE:]
    ms = jnp.sum(ik * ik, axis=-1, keepdims=True) * (1.0 / IDX_DIM)
    ik_ref[...] = (ik * lax.rsqrt(ms + RMS_EPS) * ikn_ref[...]).astype(BF16)


DSA_KC = 256
DSA_QB = 256


def _dsa_in(h, mod, w_in, q_norm, k_norm, ik_norm, tm=2 * DSA_KC):
    B, S, D = h.shape
    QW, KW = N_HEADS * HEAD_DIM, N_KV_HEADS * HEAD_DIM
    IQW = IDX_HEADS * IDX_DIM
    o1, o2, o3, o4, o5 = QW, QW + KW, QW + 2 * KW, QW + 2 * KW + IQW, QW + 2 * KW + IQW + IDX_DIM
    wb = w_in.astype(BF16)
    wqT = wb[:, :o1].T
    wk = jnp.pad(wb[:, o1:o2].reshape(D, N_KV_HEADS, HEAD_DIM), ((0, 0), (0, 0), (0, LANE - HEAD_DIM))).reshape(D, N_KV_HEADS * LANE)
    wvT = wb[:, o2:o3].T
    wiqT = wb[:, o3:o4].T
    wik = jnp.pad(wb[:, o4:o5], ((0, 0), (0, LANE - IDX_DIM)))
    wiwT = jnp.pad(wb[:, o5:].T, ((0, 16 - IDX_HEADS), (0, 0)))
    wT = jnp.concatenate([wqT, wvT, wiqT, wiwT], axis=0)
    wn = jnp.concatenate([wk, wik], axis=1)
    qn = q_norm.astype(F32).reshape(HEAD_DIM, 1)
    kn = jnp.pad(k_norm.astype(F32), (0, LANE - HEAD_DIM)).reshape(1, LANE)
    ikn = jnp.pad(ik_norm.astype(F32), (0, LANE - IDX_DIM)).reshape(1, LANE)
    mspec = lambda k: pl.BlockSpec((None, None, 1, D), lambda b, i: (b, k, 0, 0))
    full = lambda a: pl.BlockSpec(a.shape, lambda b, i: (0,) * a.ndim)
    nc = S // tm
    cpt = tm // DSA_KC
    return pl.pallas_call(
        _dsa_in_kernel,
        grid=(B, nc),
        in_specs=[pl.BlockSpec((None, tm, D), lambda b, i: (b, i, 0)), mspec(3), mspec(4),
                  full(wT), full(wn), full(qn), full(kn), full(ikn)],
        out_specs=[
            pl.BlockSpec((None, QW, tm), lambda b, i: (b, 0, i)),
            pl.BlockSpec((None, tm, N_KV_HEADS * LANE), lambda b, i: (b, i, 0)),
            pl.BlockSpec((None, cpt, KW, DSA_KC), lambda b, i: (b, i, 0, 0)),
            pl.BlockSpec((None, IQW, tm), lambda b, i: (b, 0, i)),
            pl.BlockSpec((None, tm, LANE), lambda b, i: (b, i, 0)),
            pl.BlockSpec((None, IDX_HEADS, tm), lambda b, i: (b, 0, i)),
        ],
        out_shape=[
            jax.ShapeDtypeStruct((B, QW, S), BF16),
            jax.ShapeDtypeStruct((B, S, N_KV_HEADS * LANE), BF16),
            jax.ShapeDtypeStruct((B, nc * cpt, KW, DSA_KC), BF16),
            jax.ShapeDtypeStruct((B, IQW, S), BF16),
            jax.ShapeDtypeStruct((B, S, LANE), BF16),
            jax.ShapeDtypeStruct((B, IDX_HEADS, S), F32),
        ],
        compiler_params=_cparams("parallel", "parallel"),
    )(h, mod, mod, wT, wn, qn, kn, ikn)


NEG_BIG = -(2.0 ** 100)
DSA_VPAD = 16
N_CHAINS = 4


def _ukey_to_float(ukey):
    key = ukey ^ np.int32(-2 ** 31)
    bits = key ^ ((key >> 31) & np.int32(0x7FFFFFFF))
    return lax.bitcast_convert_type(bits, F32)


def _fold_rows(x, op, rows):
    while x.shape[0] > rows:
        half = x.shape[0] // 2
        x = op(x[:half], x[half:])
    return x


def _dsa_core_kernel(iqT_ref, iwT_ref, qT_ref, ik_ref, k_ref, vT_ref, oT_ref,
                     sc_ref, fl_ref, whi_ref, wlo_ref, qa_ref, iqa_ref, d_ref, st_ref, p_ref, m_ref, acc_ref,
                     *, k_top, n_chunks):
    KC, QB = DSA_KC, DSA_QB
    qi = pl.program_id(1)
    q0 = qi * QB
    nk = (q0 + QB + KC - 1) // KC
    qpos = q0 + lax.broadcasted_iota(jnp.int32, (1, QB), 1)
    zeros_h = jnp.zeros((LANE - HEAD_DIM, QB), BF16)

    for h in range(IDX_HEADS):
        iqa_ref[:, h * QB:(h + 1) * QB] = jnp.concatenate(
            [iqT_ref[h * IDX_DIM:(h + 1) * IDX_DIM, :], zeros_h], axis=0)
    for j in range(N_KV_HEADS):
        for g in range(GROUP):
            hh = j * GROUP + g
            qa_ref[j, :, g * QB:(g + 1) * QB] = jnp.concatenate(
                [qT_ref[hh * HEAD_DIM:(hh + 1) * HEAD_DIM, :], zeros_h], axis=0)
    iw = iwT_ref[...]

    npair = (nk + 1) // 2

    def clamp_chunk(c):
        return jnp.minimum(c, n_chunks - 1)

    def idx_dot(c, slot):
        k0 = pl.multiple_of(c * KC, KC)
        d_ref[slot] = _dot(ik_ref[pl.ds(k0, KC), :], iqa_ref[...])

    def idx_score(c, slot):
        k0 = pl.multiple_of(c * KC, KC)
        s = jnp.maximum(d_ref[slot, :, 0:QB], 0.0) * iw[0:1, :]
        for h in range(1, IDX_HEADS):
            s = s + jnp.maximum(d_ref[slot, :, h * QB:(h + 1) * QB], 0.0) * iw[h:h + 1, :]
        kpos = k0 + lax.broadcasted_iota(jnp.int32, (KC, 1), 0)
        s = jnp.where(kpos <= qpos, s, -jnp.inf)
        sc_ref[pl.ds(k0, KC), :] = s
        r = s.astype(BF16)
        rf = r.astype(F32)
        below = (rf * jnp.where(rf > 0.0, 1.0 - 1.25 * 2.0 ** -8, 1.0 + 1.25 * 2.0 ** -8)).astype(BF16)
        fl_ref[pl.ds(k0, KC), :] = jnp.where(rf > s, below, r)

    idx_dot(0, 0)

    def p1(i, carry):
        c = 2 * i
        idx_dot(c + 1, 1)
        idx_score(c, 0)
        idx_dot(clamp_chunk(c + 2), 0)
        idx_score(c + 1, 1)
        return carry

    lax.fori_loop(0, npair, p1, 0)

    def count_not(pred):
        def body(i, accs):
            accs = list(accs)
            for half in range(2):
                k0 = pl.multiple_of((2 * i + half) * KC, KC)
                ind = jnp.where(pred(sc_ref[pl.ds(k0, KC), :]), 0.0, 1.0)
                for r in range(KC // 8):
                    accs[r % N_CHAINS] = accs[r % N_CHAINS] + ind[r * 8:(r + 1) * 8]
            return tuple(accs)
        accs = lax.fori_loop(0, npair, body, tuple(jnp.zeros((8, QB), F32) for _ in range(N_CHAINS)))
        acc = (accs[0] + accs[1]) + (accs[2] + accs[3])
        return jnp.sum(acc, axis=0, keepdims=True)

    n_staged = (npair * (2 * KC)).astype(F32)

    def count_ge16(ref, t):
        tb = t.astype(BF16)
        one, zero = jnp.ones((), BF16), jnp.zeros((), BF16)

        def body(i, accs):
            accs = list(accs)
            for half in range(2):
                k0 = pl.multiple_of((2 * i + half) * KC, KC)
                ind = jnp.where(ref[pl.ds(k0, KC), :] < tb, zero, one)
                for r in range(KC // 16):
                    accs[r % N_CHAINS] = accs[r % N_CHAINS] + ind[r * 16:(r + 1) * 16]
            return tuple(accs)
        accs = lax.fori_loop(0, npair, body, tuple(jnp.zeros((16, QB), BF16) for _ in range(N_CHAINS)))
        acc = (accs[0].astype(F32) + accs[1].astype(F32)) + (accs[2].astype(F32) + accs[3].astype(F32))
        return jnp.sum(acc, axis=0, keepdims=True)

    def digit_search(ref, n_bits, k_need, to_float):
        d = jnp.zeros((1, QB), F32)
        for bit in range(n_bits - 1, -1, -1):
            cand = d + float(1 << bit)
            d = jnp.where(count_ge16(ref, to_float(cand)) >= k_need, cand, d)
        return d

    def val16(d):
        di = d.astype(jnp.int32)
        return _ukey_to_float(jnp.where(di >= 0x8000, di << 16, (di << 16) | np.int32(0xFFFF)))

    key_hi = digit_search(fl_ref, 16, k_top, val16)
    h_lo = val16(key_hi)
    h_up = val16(key_hi + 1.0)
    no_thr = h_lo == -jnp.inf
    h_up = jnp.where(h_up > h_lo, h_up, np.float32(2.0 ** -126))
    width = jnp.where(no_thr, 1.0, h_up - h_lo)
    tiny = width < 2.0 ** -60
    pre = jnp.where(tiny, np.float32(2.0 ** 60), np.float32(1.0))
    unpre = jnp.where(tiny, np.float32(2.0 ** -60), np.float32(1.0))
    from_int = (width * pre) * np.float32(2.0 ** -16)
    to_int = lax.bitcast_convert_type(np.int32(0x7F000000) - lax.bitcast_convert_type(from_int, jnp.int32), F32)

    def split(i, carry):
        for half in range(2):
            k0 = pl.multiple_of((2 * i + half) * KC, KC)
            w = jnp.clip(((sc_ref[pl.ds(k0, KC), :] - h_lo) * pre) * to_int, -256.0, 65536.0)
            w_hi = jnp.floor(w * (1.0 / 256.0))
            whi_ref[pl.ds(k0, KC), :] = w_hi.astype(BF16)
            wlo_ref[pl.ds(k0, KC), :] = (w - 256.0 * w_hi).astype(BF16)
        return carry
    lax.fori_loop(0, npair, split, 0)

    ident = lambda d: d
    d_mid = digit_search(whi_ref, 8, k_top, ident)
    k_low = k_top - count_ge16(whi_ref, d_mid + 1.0)
    d_mid_b = d_mid.astype(BF16)

    def keep_low(i, carry):
        for half in range(2):
            k0 = pl.multiple_of((2 * i + half) * KC, KC)
            whi_ref[pl.ds(k0, KC), :] = jnp.where(whi_ref[pl.ds(k0, KC), :] == d_mid_b,
                                                  wlo_ref[pl.ds(k0, KC), :], -jnp.ones((), BF16))
        return carry
    lax.fori_loop(0, npair, keep_low, 0)
    d_low = digit_search(whi_ref, 8, k_low, ident)
    offset = ((d_mid * 256.0 + d_low) * from_int) * unpre
    thr = jnp.where(no_thr, -jnp.inf, h_lo + offset)

    m_ref[...] = jnp.full(m_ref.shape, NEG_BIG, F32)
    acc_ref[...] = jnp.zeros(acc_ref.shape, F32)
    r_i = lax.broadcasted_iota(jnp.int32, (KC, KC), 0)
    c_i = lax.broadcasted_iota(jnp.int32, (KC, KC), 1)
    tril = jnp.where(c_i <= r_i, 1.0, 0.0).astype(BF16)
    ones_rows = jnp.ones((DSA_VPAD, KC), BF16)

    HALF = GROUP * QB // 2

    def logits(c, slot, j, half):
        k0 = pl.multiple_of(c * KC, KC)
        cols = slice(half * HALF, (half + 1) * HALF)
        st_ref[slot, j, :, cols] = _dot(k_ref[pl.ds(k0, KC), j * LANE:(j + 1) * LANE],
                                        qa_ref[j, :, cols]).astype(BF16)

    for j in range(N_KV_HEADS):
        logits(0, 0, j, 0)
        logits(0, 0, j, 1)

    def attend(c, slot, tie_seen, c_next, need):
        k0 = pl.multiple_of(c * KC, KC)
        s = sc_ref[pl.ds(k0, KC), :]
        kpos = k0 + lax.broadcasted_iota(jnp.int32, (KC, 1), 0)
        eq = s == thr
        eqf = jnp.where(eq, 1.0, 0.0)
        incl = _dot(tril, eqf.astype(BF16))
        rank = tie_seen + incl - eqf
        sel = jnp.logical_and(jnp.logical_or(s > thr, jnp.logical_and(eq, rank < need)), kpos <= qpos)
        tie_seen = tie_seen + incl[KC - 1:KC, :]
        bias = jnp.where(sel, 0.0, NEG_BIG).astype(BF16)
        for j in range(N_KV_HEADS):
            alphas = []
            for g in range(GROUP):
                sl = slice(g * QB, (g + 1) * QB)
                sg = st_ref[slot, j, :, sl] + bias
                m_old = m_ref[j, :, sl]
                mx = jnp.max(_fold_rows(sg, jnp.maximum, 16), axis=0, keepdims=True).astype(F32)
                m_new = jnp.maximum(m_old, mx)
                alphas.append(jnp.exp2(m_old - m_new))
                m_ref[j, :, sl] = m_new
                p_ref[j, :, sl] = jnp.exp2(sg - m_new.astype(BF16))
                if g % 2 == 1:
                    logits(c_next, 1 - slot, j, g // 2)
            v_aug = jnp.concatenate([vT_ref[c, j * HEAD_DIM:(j + 1) * HEAD_DIM, :], ones_rows], axis=0)
            acc_ref[j] = acc_ref[j] * jnp.concatenate(alphas, axis=1) + _dot(v_aug, p_ref[j])
        return tie_seen

    need = k_top - (n_staged - count_not(lambda s: s > thr))

    def p3(i, tie_seen):
        c = 2 * i
        tie_seen = attend(c, 0, tie_seen, c + 1, need)
        return attend(c + 1, 1, tie_seen, clamp_chunk(c + 2), need)

    lax.fori_loop(0, npair, p3, jnp.zeros((1, QB), F32))

    for j in range(N_KV_HEADS):
        o = acc_ref[j, 0:HEAD_DIM, :] / acc_ref[j, HEAD_DIM:HEAD_DIM + 1, :]
        for g in range(GROUP):
            hh = j * GROUP + g
            oT_ref[hh * HEAD_DIM:(hh + 1) * HEAD_DIM, :] = o[:, g * QB:(g + 1) * QB].astype(BF16)


def _dsa_core(qT, k, vT, iqT, ik, iwT):
    B, QW, S = qT.shape
    KC, QB = DSA_KC, DSA_QB
    k_top = min(TOPK_MAX, S // 4)
    nc = S // KC
    assert S % KC == 0 and S % QB == 0 and nc % 2 == 0
    assert nc * (KC // 16) // N_CHAINS <= 256
    return pl.pallas_call(
        functools.partial(_dsa_core_kernel, k_top=k_top, n_chunks=nc),
        grid=(B, S // QB),
        in_specs=[
            pl.BlockSpec((None, IDX_HEADS * IDX_DIM, QB), lambda b, i: (b, 0, i)),
            pl.BlockSpec((None, IDX_HEADS, QB), lambda b, i: (b, 0, i)),
            pl.BlockSpec((None, QW, QB), lambda b, i: (b, 0, i)),
            pl.BlockSpec((None, S, LANE), lambda b, i: (b, 0, 0)),
            pl.BlockSpec((None, S, N_KV_HEADS * LANE), lambda b, i: (b, 0, 0)),
            pl.BlockSpec((None, nc, N_KV_HEADS * HEAD_DIM, KC), lambda b, i: (b, 0, 0, 0)),
        ],
        out_specs=pl.BlockSpec((None, QW, QB), lambda b, i: (b, 0, i)),
        out_shape=jax.ShapeDtypeStruct((B, QW, S), BF16),
        scratch_shapes=[
            pltpu.VMEM((S, QB), F32),
            pltpu.VMEM((S, QB), BF16),
            pltpu.VMEM((S, QB), BF16),
            pltpu.VMEM((S, QB), BF16),
            pltpu.VMEM((N_KV_HEADS, LANE, GROUP * QB), BF16),
            pltpu.VMEM((LANE, IDX_HEADS * QB), BF16),
            pltpu.VMEM((2, KC, IDX_HEADS * QB), F32),
            pltpu.VMEM((2, N_KV_HEADS, KC, GROUP * QB), BF16),
            pltpu.VMEM((N_KV_HEADS, KC, GROUP * QB), BF16),
            pltpu.VMEM((N_KV_HEADS, 1, GROUP * QB), F32),
            pltpu.VMEM((N_KV_HEADS, HEAD_DIM + DSA_VPAD, GROUP * QB), F32),
        ],
        compiler_params=_cparams("parallel", "arbitrary"),
    )(iqT, iwT, qT, ik, k, vT)


def _out_T_kernel(xT_ref, w_ref, h_ref, gt_ref, o_ref):
    o_ref[...] = h_ref[...] + gt_ref[...] * _dot_tn(xT_ref[...], w_ref[...])


def _dsa_out(oT, w_out, h, mod, tm=512):
    B, S, D = h.shape
    QW = oT.shape[1]
    return pl.pallas_call(
        _out_T_kernel,
        grid=(B, S // tm),
        in_specs=[
            pl.BlockSpec((None, QW, tm), lambda b, i: (b, 0, i)),
            pl.BlockSpec((QW, D), lambda b, i: (0, 0)),
            pl.BlockSpec((None, tm, D), lambda b, i: (b, i, 0)),
            pl.BlockSpec((None, None, 1, D), lambda b, i: (b, 5, 0, 0)),
        ],
        out_specs=pl.BlockSpec((None, tm, D), lambda b, i: (b, i, 0)),
        out_shape=jax.ShapeDtypeStruct((B, S, D), F32),
        compiler_params=_cparams("parallel", "parallel"),
    )(oT, w_out.astype(BF16), h, mod)


def _dsa_mixer(h, mod, w_in, q_norm, k_norm, ik_norm, w_out):
    qT, k, vT, iqT, ik, iwT = _dsa_in(h, mod, w_in, q_norm, k_norm, ik_norm)
    oT = _dsa_core(qT, k, vT, iqT, ik, iwT)
    return _dsa_out(oT, w_out, h, mod)


RWKV_HEAD = 64
GN_EPS = 64e-5
RW_PACK = 4
RW_LANES = RW_PACK * RWKV_HEAD
RW_C = 64
RW_TB = 256


def _split3(x):
    hi = x.astype(BF16)
    r1 = x - hi.astype(F32)
    mid = r1.astype(BF16)
    lo = (r1 - mid.astype(F32)).astype(BF16)
    return hi, mid, lo


def _gsum(x, g_ref, gt_ref):
    hi = x.astype(BF16)
    lo = (x - hi.astype(F32)).astype(BF16)
    s = _dot(jnp.concatenate([hi, lo], axis=1), g_ref[...])
    return _dot(jnp.concatenate(_split3(s), axis=1), gt_ref[...])


def _rwkv_in_kernel(h_ref, hp_ref, sh_ref, sc_ref, mu_ref, wrkv_ref, w0_ref, w1_ref, w2_ref,
                    a0_ref, a1_ref, a2_ref, g1_ref, g2_ref, kk_ref, ka_ref, gm_ref, gmt_ref,
                    r_ref, lw_ref, k_ref, v_ref, kkn_ref, b_ref, g_ref):
    sh, sc = sh_ref[...], sc_ref[...]
    u = _modulate(h_ref[...], sh, sc)
    tm = u.shape[0]
    prev = _modulate(hp_ref[...], sh, sc)[7:8, :]
    prev = jnp.where(pl.program_id(1) == 0, 0.0, prev)
    row = lax.broadcasted_iota(jnp.int32, (tm, 1), 0)
    xx = jnp.where(row == 0, prev, pltpu.roll(u, 1, 0)) - u
    mix = lambda i: (u + xx * mu_ref[i:i + 1, :]).astype(BF16)
    wl = w0_ref[...] + _dot(jnp.tanh(_dot(mix(3), w1_ref[...])).astype(BF16), w2_ref[...])
    a_lin = _dot(_dot(mix(4), a1_ref[...]).astype(BF16), a2_ref[...])
    g_hid = _dot(mix(5), g1_ref[...])
    k = _dot(mix(1), wrkv_ref[1])
    nwl = -wl
    w_log = -(jnp.maximum(nwl, 0.0) + jnp.log(1.0 + jnp.exp(-jnp.abs(nwl)))) - 0.5
    lw_ref[...] = -jnp.exp(w_log)
    r_ref[...] = _dot(mix(0), wrkv_ref[0])
    a = jax.nn.sigmoid(a0_ref[...] + a_lin)
    g_ref[...] = _dot(jax.nn.sigmoid(g_hid).astype(BF16), g2_ref[...])
    kk = k * kk_ref[...]
    nrm = jnp.sqrt(_gsum(kk * kk, gm_ref, gmt_ref))
    kk = kk / jnp.maximum(nrm, 1e-12)
    k_ref[...] = k * (1.0 + (a - 1.0) * ka_ref[...])
    kkn_ref[...] = kk
    b_ref[...] = kk * a
    v_ref[...] = _dot(mix(2), wrkv_ref[2])


def _head_indicator(D):
    gm = (np.arange(D)[:, None] // RWKV_HEAD == np.arange(D // RWKV_HEAD)[None, :]).astype(np.float32)
    return jnp.asarray(np.concatenate([gm, gm], axis=0), BF16), jnp.asarray(np.concatenate([gm.T] * 3, axis=0), BF16)


def _rwkv_in(h, mod, mu, w_rkv, w0, w1, w2, a0, a1, a2, g1, g2, k_k, k_a, tm=256):
    B, S, D = h.shape
    gm, gmt = _head_indicator(D)
    row = lambda a: a.astype(F32).reshape(1, D)
    full = lambda a: pl.BlockSpec(a.shape, lambda b, i: (0,) * a.ndim)
    mspec = lambda k: pl.BlockSpec((None, None, 1, D), lambda b, i: (b, k, 0, 0))
    tile = pl.BlockSpec((None, tm, D), lambda b, i: (b, i, 0))
    params = [mu.astype(F32), w_rkv.astype(BF16), row(w0), w1.astype(BF16), w2.astype(BF16),
              row(a0), a1.astype(BF16), a2.astype(BF16), g1.astype(BF16), g2.astype(BF16),
              row(k_k), row(k_a), gm, gmt]
    return pl.pallas_call(
        _rwkv_in_kernel,
        grid=(B, S // tm),
        in_specs=[tile,
                  pl.BlockSpec((None, 8, D), lambda b, i: (b, jnp.maximum(i * (tm // 8) - 1, 0), 0)),
                  mspec(3), mspec(4)] + [full(p) for p in params],
        out_specs=[tile] * 7,
        out_shape=[jax.ShapeDtypeStruct((B, S, D), F32)] * 7,
        compiler_params=_cparams("parallel", "parallel"),
    )(h, h, mod, mod, *params)


def _rwkv_scan_kernel(r_ref, lw_ref, k_ref, v_ref, kk_ref, b_ref, y_ref, zt_ref):
    C, P, L = RW_C, RW_PACK, RW_LANES
    N = P * C
    assert C == RWKV_HEAD
    n_packs = r_ref.shape[1] // L
    n_chunks = r_ref.shape[0] // C
    units = [(p, c) for p in range(n_packs) for c in range(n_chunks)]

    @pl.when(pl.program_id(1) == 0)
    def _():
        zt_ref[...] = jnp.zeros(zt_ref.shape, F32)

    lane = lax.broadcasted_iota(jnp.int32, (1, L), 1)
    hmask = [lane // C == hd for hd in range(P)]
    bf = lambda x: x.astype(BF16)
    stack = lambda x: jnp.concatenate([jnp.where(hmask[hd], x, jnp.zeros_like(x)) for hd in range(P)], axis=0)
    t_i = lax.broadcasted_iota(jnp.int32, (C, N), 0)
    i_i = lax.broadcasted_iota(jnp.int32, (C, N), 1) % C
    low_strict = t_i > i_i
    low_incl = t_i >= i_i
    eye_w = jnp.where(t_i == i_i, 1.0, 0.0)
    bd_mask = (lax.broadcasted_iota(jnp.int32, (L, L), 0) // C) == (lax.broadcasted_iota(jnp.int32, (L, L), 1) // C)
    tri = jnp.where(lax.broadcasted_iota(jnp.int32, (C, C), 0) >= lax.broadcasted_iota(jnp.int32, (C, C), 1),
                    1.0, 0.0).astype(BF16)
    tile = lambda ref, u: ref[u[1] * C:(u[1] + 1) * C, u[0] * L:(u[0] + 1) * L]

    cw = {}
    tri3 = jnp.concatenate([tri, tri, tri], axis=1)
    for u in units:
        cw[u] = _dot(tri3, jnp.concatenate(_split3(tile(lw_ref, u)), axis=0))
    al, rt, bw, kw, v, vs, aa, w_end = {}, {}, {}, {}, {}, {}, {}, {}
    for u in units:
        lw = tile(lw_ref, u)
        w_incl = jnp.exp(cw[u])
        w_excl = jnp.exp(cw[u] - lw)
        w_inv = jnp.exp(-cw[u])
        w_end[u] = w_incl[C - 1:C, :]
        v[u] = bf(tile(v_ref, u))
        vs[u] = stack(v[u])
        al[u] = bf(-tile(kk_ref, u) * w_excl)
        rt[u] = tile(r_ref, u) * w_incl
        bh = tile(b_ref, u) * w_inv
        kh = tile(k_ref, u) * w_inv
        bw[u] = bf(bh * w_end[u])
        kw[u] = bf(kh * w_end[u])
        aa[u] = _dot_nt(jnp.concatenate([al[u], bf(rt[u])], axis=0),
                        jnp.concatenate([stack(bf(bh)), stack(bf(kh))], axis=0))
    a_k, a_rb, pw, t = {}, {}, {}, {}
    for u in units:
        a_ab = jnp.where(low_strict, aa[u][:C, :N], 0.0)
        a_k[u] = bf(jnp.concatenate([jnp.where(low_strict, aa[u][:C, N:], 0.0),
                                     jnp.where(low_incl, aa[u][C:, N:], 0.0)], axis=0))
        a_rb[u] = bf(jnp.where(low_incl, aa[u][C:, :N], 0.0))
        pw[u] = a_ab
        t[u] = eye_w + a_ab
    n_fac = int(np.log2(C))
    akv, rkv = {}, {}
    for u in units:
        pwb = bf(pw[u])
        pw[u] = _dot(pwb, stack(pwb))
        both_v = _dot(a_k[u], vs[u])
        akv[u], rkv[u] = both_v[:C], both_v[C:]
    for s in range(2, n_fac + 1):
        for u in units:
            pwb = bf(pw[u])
            rhs = stack(pwb)
            if s < n_fac:
                both = _dot(jnp.concatenate([bf(t[u]), pwb], axis=0), rhs)
                t[u] = t[u] + both[:C]
                pw[u] = both[C:]
            else:
                t[u] = t[u] + _dot(bf(t[u]), rhs)
    pm, q = {}, {}
    for u in units:
        tb = bf(t[u])
        pm[u] = bf(_dot(tb, stack(al[u])))
        q[u] = bf(_dot(tb, stack(bf(akv[u]))))
    pr, y0, bk = {}, {}, {}
    for u in units:
        rp = bf(rt[u] + _dot(a_rb[u], stack(pm[u])))
        pr[u] = jnp.concatenate([pm[u], rp], axis=0)
        y0[u] = _dot(a_rb[u], stack(q[u])) + rkv[u]
        bk[u] = jnp.concatenate([bw[u], kw[u]], axis=0)
    zt = [zt_ref[p] for p in range(n_packs)]
    for c in range(n_chunks):
        for p in range(n_packs):
            u = (p, c)
            uy = _dot_nt(pr[u], bf(zt[p]))
            y_ref[c * C:(c + 1) * C, p * L:(p + 1) * L] = uy[C:] + y0[u]
            upd = _dot_tn(jnp.concatenate([bf(uy[:C] + q[u]), v[u]], axis=0), bk[u])
            zt[p] = zt[p] * w_end[u] + jnp.where(bd_mask, upd, 0.0)
    for p in range(n_packs):
        zt_ref[p] = zt[p]


def _rwkv_scan(r, lw, k, v, kk, b):
    B, S, D = r.shape
    assert S % RW_TB == 0 and RW_TB % RW_C == 0 and D % RW_LANES == 0
    tile = pl.BlockSpec((None, RW_TB, D), lambda bb, t: (bb, t, 0))
    return pl.pallas_call(
        _rwkv_scan_kernel,
        grid=(B, S // RW_TB),
        in_specs=[tile] * 6,
        out_specs=tile,
        out_shape=jax.ShapeDtypeStruct((B, S, D), F32),
        scratch_shapes=[pltpu.VMEM((D // RW_LANES, RW_LANES, RW_LANES), F32)],
        compiler_params=_cparams("parallel", "arbitrary"),
    )(r, lw, k, v, kk, b)


def _rwkv_out_kernel(y_ref, r_ref, k_ref, v_ref, g_ref, lnw_ref, lnb_ref, rk_ref, gm_ref, gmt_ref,
                     w_ref, h_ref, gt_ref, o_ref):
    y = y_ref[...]
    inv_n = 1.0 / RWKV_HEAD
    yc = y - _gsum(y, gm_ref, gmt_ref) * inv_n
    var = _gsum(yc * yc, gm_ref, gmt_ref) * inv_n
    yn = yc * lax.rsqrt(var + GN_EPS) * lnw_ref[...] + lnb_ref[...]
    v = v_ref[...]
    bonus = _gsum(r_ref[...] * k_ref[...] * rk_ref[...], gm_ref, gmt_ref) * v
    out = ((yn + bonus) * g_ref[...]).astype(BF16)
    o_ref[...] = h_ref[...] + gt_ref[...] * _dot(out, w_ref[...])


def _rwkv_out(y, r, k, v, g, ln_w, ln_b, r_k, w_out, h, mod, tm=256):
    B, S, D = h.shape
    gm, gmt = _head_indicator(D)
    row = lambda a: a.astype(F32).reshape(1, D)
    full = lambda a: pl.BlockSpec(a.shape, lambda b, i: (0,) * a.ndim)
    tile = pl.BlockSpec((None, tm, D), lambda b, i: (b, i, 0))
    params = [row(ln_w), row(ln_b), row(r_k), gm, gmt, w_out.astype(BF16)]
    return pl.pallas_call(
        _rwkv_out_kernel,
        grid=(B, S // tm),
        in_specs=[tile] * 5 + [full(p) for p in params] + [tile, pl.BlockSpec((None, None, 1, D), lambda b, i: (b, 5, 0, 0))],
        out_specs=tile,
        out_shape=jax.ShapeDtypeStruct((B, S, D), F32),
        compiler_params=_cparams("parallel", "parallel"),
    )(y, r, k, v, g, *params, h, mod)


def _rwkv_mixer(h, mod, mu, w_rkv, w0, w1, w2, a0, a1, a2, g1, g2, k_k, k_a, r_k, ln_w, ln_b, w_out):
    r, lw, k, v, kk, b, g = _rwkv_in(h, mod, mu, w_rkv, w0, w1, w2, a0, a1, a2, g1, g2, k_k, k_a)
    y = _rwkv_scan(r, lw, k, v, kk, b)
    return _rwkv_out(y, r, k, v, g, ln_w, ln_b, r_k, w_out, h, mod)


def kernel(x, c, ada_w, ada_b, ffn_w_gu, ffn_w_down, dsa_w_in, dsa_q_norm, dsa_k_norm, dsa_ik_norm, dsa_w_out, rwkv_mu, rwkv_w_rkv, rwkv_w0, rwkv_w1, rwkv_w2, rwkv_a0, rwkv_a1, rwkv_a2, rwkv_g1, rwkv_g2, rwkv_k_k, rwkv_k_a, rwkv_r_k, rwkv_ln_w, rwkv_ln_b, rwkv_w_out):
    B, S, D = x.shape
    depth = ada_w.shape[0]
    mods = _ada_mod(c, ada_w, ada_b)
    h = x
    for i in range(depth):
        mod = mods[i].reshape(B, N_MOD, 1, D)
        j = i // 2
        h = _ffn(h, mod, 0, ffn_w_gu[i, 0].astype(BF16), ffn_w_down[i, 0].astype(BF16))
        if i % 2 == 0:
            h = _dsa_mixer(h, mod, dsa_w_in[j], dsa_q_norm[j], dsa_k_norm[j], dsa_ik_norm[j], dsa_w_out[j])
        else:
            h = _rwkv_mixer(h, mod, rwkv_mu[j], rwkv_w_rkv[j], rwkv_w0[j], rwkv_w1[j], rwkv_w2[j], rwkv_a0[j],
                            rwkv_a1[j], rwkv_a2[j], rwkv_g1[j], rwkv_g2[j], rwkv_k_k[j], rwkv_k_a[j],
                            rwkv_r_k[j], rwkv_ln_w[j], rwkv_ln_b[j], rwkv_w_out[j])
        h = _ffn(h, mod, 6, ffn_w_gu[i, 1].astype(BF16), ffn_w_down[i, 1].astype(BF16))
    return h
```

```python
import functools

import jax
import jax.numpy as jnp
import numpy as np
from jax import lax
from jax.experimental import pallas as pl
from jax.experimental.pallas import tpu as pltpu

F32 = jnp.float32
BF16 = jnp.bfloat16

RMS_EPS = 1e-6
N_MOD = 9
VMEM_LIMIT_BYTES = 56 * 1024 * 1024

N_HEADS = 16
HEAD_DIM = 64
N_KV_HEADS = 4
GROUP = N_HEADS // N_KV_HEADS
IDX_HEADS = 8
IDX_DIM = 64
TOPK_MAX = 256
LANE = 128
LOG2E = 1.4426950408889634


def _cparams(*sem):
    return pltpu.CompilerParams(dimension_semantics=sem, vmem_limit_bytes=VMEM_LIMIT_BYTES)


def _dot(a, b):
    return jnp.dot(a, b, preferred_element_type=F32)


def _dot_nt(a, b):
    return lax.dot_general(a, b, (((1,), (1,)), ((), ())), preferred_element_type=F32)


def _dot_tn(a, b):
    return lax.dot_general(a, b, (((0,), (0,)), ((), ())), preferred_element_type=F32)


def _modulate(h, sh, sc):
    ms = jnp.mean(h * h, axis=-1, keepdims=True)
    return h * lax.rsqrt(ms + RMS_EPS) * (1.0 + sc) + sh


def _ada_kernel(c_ref, w_ref, b_ref, o_ref):
    c = c_ref[...]
    ca = (c * jax.nn.sigmoid(c)).astype(BF16)
    o_ref[...] = _dot(ca, w_ref[...].astype(BF16)) + b_ref[...]


def _ada_mod(c, ada_w, ada_b):
    L, D, N = ada_w.shape
    B = c.shape[0]
    tn = 1536
    return pl.pallas_call(
        _ada_kernel,
        grid=(L, N // tn),
        in_specs=[
            pl.BlockSpec((B, D), lambda l, j: (0, 0)),
            pl.BlockSpec((None, D, tn), lambda l, j: (l, 0, j)),
            pl.BlockSpec((None, 1, tn), lambda l, j: (l, 0, j)),
        ],
        out_specs=pl.BlockSpec((None, B, tn), lambda l, j: (l, 0, j)),
        out_shape=jax.ShapeDtypeStruct((L, B, N), F32),
        compiler_params=_cparams("parallel", "parallel"),
    )(c, ada_w, ada_b.reshape(L, 1, N))


FF_CHUNK = 256


def _ffn_kernel(h_ref, sh_ref, sc_ref, gt_ref, wgu_ref, wd_ref, o_ref, *, d_ff):
    h = h_ref[...]
    u = _modulate(h, sh_ref[...], sc_ref[...]).astype(BF16)
    acc = None
    for c in range(d_ff // FF_CHUNK):
        lo = c * FF_CHUNK
        g = _dot(u, wgu_ref[:, lo:lo + FF_CHUNK])
        up = _dot(u, wgu_ref[:, d_ff + lo:d_ff + lo + FF_CHUNK])
        a = (g * jax.nn.sigmoid(g) * up).astype(BF16)
        part = _dot(a, wd_ref[lo:lo + FF_CHUNK, :])
        acc = part if acc is None else acc + part
    o_ref[...] = h + (0.5 * gt_ref[...]) * acc


def _ffn(h, mod, k0, w_gu, w_down, tm=512):
    B, S, D = h.shape
    d_ff = w_down.shape[0]
    assert S % tm == 0 and d_ff % FF_CHUNK == 0
    mspec = lambda k: pl.BlockSpec((None, None, 1, D), lambda b, i: (b, k, 0, 0))
    return pl.pallas_call(
        functools.partial(_ffn_kernel, d_ff=d_ff),
        grid=(B, S // tm),
        in_specs=[
            pl.BlockSpec((None, tm, D), lambda b, i: (b, i, 0)),
            mspec(k0), mspec(k0 + 1), mspec(k0 + 2),
            pl.BlockSpec((D, 2 * d_ff), lambda b, i: (0, 0), pipeline_mode=pl.Buffered(1)),
            pl.BlockSpec((d_ff, D), lambda b, i: (0, 0), pipeline_mode=pl.Buffered(1)),
        ],
        out_specs=pl.BlockSpec((None, tm, D), lambda b, i: (b, i, 0)),
        out_shape=jax.ShapeDtypeStruct((B, S, D), F32),
        compiler_params=_cparams("parallel", "parallel"),
    )(h, mod, mod, mod, w_gu, w_down)


def _dsa_in_kernel(h_ref, sh_ref, sc_ref, wT_ref, wn_ref, qn_ref, kn_ref, ikn_ref,
                   qT_ref, k_ref, vT_ref, iqT_ref, ik_ref, iwT_ref):
    u = _modulate(h_ref[...], sh_ref[...], sc_ref[...]).astype(BF16)
    tm = u.shape[0]
    QW, KW, IQW = N_HEADS * HEAD_DIM, N_KV_HEADS * HEAD_DIM, IDX_HEADS * IDX_DIM
    yT = _dot_nt(wT_ref[...], u)
    qT = yT[0:QW].reshape(N_HEADS, HEAD_DIM, tm)
    ms = jnp.mean(qT * qT, axis=1, keepdims=True)
    qT = qT * lax.rsqrt(ms + RMS_EPS) * qn_ref[...][None] * (HEAD_DIM ** -0.5 * LOG2E)
    qT_ref[...] = qT.reshape(QW, tm).astype(BF16)
    for cc in range(tm // DSA_KC):
        vT_ref[cc] = yT[QW:QW + KW, cc * DSA_KC:(cc + 1) * DSA_KC].astype(BF16)
    iqT_ref[...] = yT[QW + KW:QW + KW + IQW].astype(BF16)
    iwT_ref[...] = yT[QW + KW + IQW:QW + KW + IQW + IDX_HEADS] * (IDX_HEADS ** -0.5 * IDX_DIM ** -0.5)
    y = _dot(u, wn_ref[...])
    kn = kn_ref[...]
    for j in range(N_KV_HEADS):
        kj = y[:, j * LANE:(j + 1) * LANE]
        ms = jnp.sum(kj * kj, axis=-1, keepdims=True) * (1.0 / HEAD_DIM)
        k_ref[:, j * LANE:(j + 1) * LANE] = (kj * lax.rsqrt(ms + RMS_EPS) * kn).astype(BF16)
    ik = y[:, N_KV_HEADS * LANE:]
    ms = jnp.sum(ik * ik, axis=-1, keepdims=True) * (1.0 / IDX_DIM)
    ik_ref[...] = (ik * lax.rsqrt(ms + RMS_EPS) * ikn_ref[...]).astype(BF16)


DSA_KC = 256
DSA_QB = 256


def _dsa_in(h, mod, w_in, q_norm, k_norm, ik_norm, tm=2 * DSA_KC):
    B, S, D = h.shape
    QW, KW = N_HEADS * HEAD_DIM, N_KV_HEADS * HEAD_DIM
    IQW = IDX_HEADS * IDX_DIM
    o1, o2, o3, o4, o5 = QW, QW + KW, QW + 2 * KW, QW + 2 * KW + IQW, QW + 2 * KW + IQW + IDX_DIM
    wb = w_in.astype(BF16)
    wqT = wb[:, :o1].T
    wk = jnp.pad(wb[:, o1:o2].reshape(D, N_KV_HEADS, HEAD_DIM), ((0, 0), (0, 0), (0, LANE - HEAD_DIM))).reshape(D, N_KV_HEADS * LANE)
    wvT = wb[:, o2:o3].T
    wiqT = wb[:, o3:o4].T
    wik = jnp.pad(wb[:, o4:o5], ((0, 0), (0, LANE - IDX_DIM)))
    wiwT = jnp.pad(wb[:, o5:].T, ((0, 16 - IDX_HEADS), (0, 0)))
    wT = jnp.concatenate([wqT, wvT, wiqT, wiwT], axis=0)
    wn = jnp.concatenate([wk, wik], axis=1)
    qn = q_norm.astype(F32).reshape(HEAD_DIM, 1)
    kn = jnp.pad(k_norm.astype(F32), (0, LANE - HEAD_DIM)).reshape(1, LANE)
    ikn = jnp.pad(ik_norm.astype(F32), (0, LANE - IDX_DIM)).reshape(1, LANE)
    mspec = lambda k: pl.BlockSpec((None, None, 1, D), lambda b, i: (b, k, 0, 0))
    full = lambda a: pl.BlockSpec(a.shape, lambda b, i: (0,) * a.ndim)
    nc = S // tm
    cpt = tm // DSA_KC
    return pl.pallas_call(
        _dsa_in_kernel,
        grid=(B, nc),
        in_specs=[pl.BlockSpec((None, tm, D), lambda b, i: (b, i, 0)), mspec(3), mspec(4),
                  full(wT), full(wn), full(qn), full(kn), full(ikn)],
        out_specs=[
            pl.BlockSpec((None, QW, tm), lambda b, i: (b, 0, i)),
            pl.BlockSpec((None, tm, N_KV_HEADS * LANE), lambda b, i: (b, i, 0)),
            pl.BlockSpec((None, cpt, KW, DSA_KC), lambda b, i: (b, i, 0, 0)),
            pl.BlockSpec((None, IQW, tm), lambda b, i: (b, 0, i)),
            pl.BlockSpec((None, tm, LANE), lambda b, i: (b, i, 0)),
            pl.BlockSpec((None, IDX_HEADS, tm), lambda b, i: (b, 0, i)),
        ],
        out_shape=[
            jax.ShapeDtypeStruct((B, QW, S), BF16),
            jax.ShapeDtypeStruct((B, S, N_KV_HEADS * LANE), BF16),
            jax.ShapeDtypeStruct((B, nc * cpt, KW, DSA_KC), BF16),
            jax.ShapeDtypeStruct((B, IQW, S), BF16),
            jax.ShapeDtypeStruct((B, S, LANE), BF16),
            jax.ShapeDtypeStruct((B, IDX_HEADS, S), F32),
        ],
        compiler_params=_cparams("parallel", "parallel"),
    )(h, mod, mod, wT, wn, qn, kn, ikn)


NEG_BIG = -(2.0 ** 100)
DSA_VPAD = 16
N_CHAINS = 4


def _ukey_to_float(ukey):
    key = ukey ^ np.int32(-2 ** 31)
    bits = key ^ ((key >> 31) & np.int32(0x7FFFFFFF))
    return lax.bitcast_convert_type(bits, F32)


def _fold_rows(x, op, rows):
    while x.shape[0] > rows:
        half = x.shape[0] // 2
        x = op(x[:half], x[half:])
    return x


def _dsa_core_kernel(iqT_ref, iwT_ref, qT_ref, ik_ref, k_ref, vT_ref, oT_ref,
                     sc_ref, fl_ref, whi_ref, wlo_ref, qa_ref, iqa_ref, d_ref, st_ref, p_ref, m_ref, acc_ref,
                     *, k_top, n_chunks):
    KC, QB = DSA_KC, DSA_QB
    qi = pl.program_id(1)
    q0 = qi * QB
    nk = (q0 + QB + KC - 1) // KC
    qpos = q0 + lax.broadcasted_iota(jnp.int32, (1, QB), 1)
    zeros_h = jnp.zeros((LANE - HEAD_DIM, QB), BF16)

    for h in range(IDX_HEADS):
        iqa_ref[:, h * QB:(h + 1) * QB] = jnp.concatenate(
            [iqT_ref[h * IDX_DIM:(h + 1) * IDX_DIM, :], zeros_h], axis=0)
    for j in range(N_KV_HEADS):
        for g in range(GROUP):
            hh = j * GROUP + g
            qa_ref[j, :, g * QB:(g + 1) * QB] = jnp.concatenate(
                [qT_ref[hh * HEAD_DIM:(hh + 1) * HEAD_DIM, :], zeros_h], axis=0)
    iw = iwT_ref[...]

    npair = (nk + 1) // 2

    def clamp_chunk(c):
        return jnp.minimum(c, n_chunks - 1)

    def idx_dot(c, slot):
        k0 = pl.multiple_of(c * KC, KC)
        d_ref[slot] = _dot(ik_ref[pl.ds(k0, KC), :], iqa_ref[...])

    def idx_score(c, slot):
        k0 = pl.multiple_of(c * KC, KC)
        s = jnp.maximum(d_ref[slot, :, 0:QB], 0.0) * iw[0:1, :]
        for h in range(1, IDX_HEADS):
            s = s + jnp.maximum(d_ref[slot, :, h * QB:(h + 1) * QB], 0.0) * iw[h:h + 1, :]
        kpos = k0 + lax.broadcasted_iota(jnp.int32, (KC, 1), 0)
        s = jnp.where(kpos <= qpos, s, -jnp.inf)
        sc_ref[pl.ds(k0, KC), :] = s
        r = s.astype(BF16)
        rf = r.astype(F32)
        below = (rf * jnp.where(rf > 0.0, 1.0 - 1.25 * 2.0 ** -8, 1.0 + 1.25 * 2.0 ** -8)).astype(BF16)
        fl_ref[pl.ds(k0, KC), :] = jnp.where(rf > s, below, r)

    idx_dot(0, 0)

    def p1(i, carry):
        c = 2 * i
        idx_dot(c + 1, 1)
        idx_score(c, 0)
        idx_dot(clamp_chunk(c + 2), 0)
        idx_score(c + 1, 1)
        return carry

    lax.fori_loop(0, npair, p1, 0)

    def count_not(pred):
        def body(i, accs):
            accs = list(accs)
            for half in range(2):
                k0 = pl.multiple_of((2 * i + half) * KC, KC)
                ind = jnp.where(pred(sc_ref[pl.ds(k0, KC), :]), 0.0, 1.0)
                for r in range(KC // 8):
                    accs[r % N_CHAINS] = accs[r % N_CHAINS] + ind[r * 8:(r + 1) * 8]
            return tuple(accs)
        accs = lax.fori_loop(0, npair, body, tuple(jnp.zeros((8, QB), F32) for _ in range(N_CHAINS)))
        acc = (accs[0] + accs[1]) + (accs[2] + accs[3])
        return jnp.sum(acc, axis=0, keepdims=True)

    n_staged = (npair * (2 * KC)).astype(F32)

    def count_ge16(ref, t):
        tb = t.astype(BF16)
        one, zero = jnp.ones((), BF16), jnp.zeros((), BF16)

        def body(i, accs):
            accs = list(accs)
            for half in range(2):
                k0 = pl.multiple_of((2 * i + half) * KC, KC)
                ind = jnp.where(ref[pl.ds(k0, KC), :] < tb, zero, one)
                for r in range(KC // 16):
                    accs[r % N_CHAINS] = accs[r % N_CHAINS] + ind[r * 16:(r + 1) * 16]
            return tuple(accs)
        accs = lax.fori_loop(0, npair, body, tuple(jnp.zeros((16, QB), BF16) for _ in range(N_CHAINS)))
        acc = (accs[0].astype(F32) + accs[1].astype(F32)) + (accs[2].astype(F32) + accs[3].astype(F32))
        return jnp.sum(acc, axis=0, keepdims=True)

    def digit_search(ref, n_bits, k_need, to_float):
        d = jnp.zeros((1, QB), F32)
        for bit in range(n_bits - 1, -1, -1):
            cand = d + float(1 << bit)
            d = jnp.where(count_ge16(ref, to_float(cand)) >= k_need, cand, d)
        return d

    def val16(d):
        di = d.astype(jnp.int32)
        return _ukey_to_float(jnp.where(di >= 0x8000, di << 16, (di << 16) | np.int32(0xFFFF)))

    key_hi = digit_search(fl_ref, 16, k_top, val16)
    h_lo = val16(key_hi)
    h_up = val16(key_hi + 1.0)
    no_thr = h_lo == -jnp.inf
    h_up = jnp.where(h_up > h_lo, h_up, np.float32(2.0 ** -126))
    width = jnp.where(no_thr, 1.0, h_up - h_lo)
    tiny = width < 2.0 ** -60
    pre = jnp.where(tiny, np.float32(2.0 ** 60), np.float32(1.0))
    unpre = jnp.where(tiny, np.float32(2.0 ** -60), np.float32(1.0))
    from_int = (width * pre) * np.float32(2.0 ** -16)
    to_int = lax.bitcast_convert_type(np.int32(0x7F000000) - lax.bitcast_convert_type(from_int, jnp.int32), F32)

    def split(i, carry):
        for half in range(2):
            k0 = pl.multiple_of((2 * i + half) * KC, KC)
            w = jnp.clip(((sc_ref[pl.ds(k0, KC), :] - h_lo) * pre) * to_int, -256.0, 65536.0)
            w_hi = jnp.floor(w * (1.0 / 256.0))
            whi_ref[pl.ds(k0, KC), :] = w_hi.astype(BF16)
            wlo_ref[pl.ds(k0, KC), :] = (w - 256.0 * w_hi).astype(BF16)
        return carry
    lax.fori_loop(0, npair, split, 0)

    ident = lambda d: d
    d_mid = digit_search(whi_ref, 8, k_top, ident)
    k_low = k_top - count_ge16(whi_ref, d_mid + 1.0)
    d_mid_b = d_mid.astype(BF16)

    def keep_low(i, carry):
        for half in range(2):
            k0 = pl.multiple_of((2 * i + half) * KC, KC)
            whi_ref[pl.ds(k0, KC), :] = jnp.where(whi_ref[pl.ds(k0, KC), :] == d_mid_b,
                                                  wlo_ref[pl.ds(k0, KC), :], -jnp.ones((), BF16))
        return carry
    lax.fori_loop(0, npair, keep_low, 0)
    d_low = digit_search(whi_ref, 8, k_low, ident)
    offset = ((d_mid * 256.0 + d_low) * from_int) * unpre
    thr = jnp.where(no_thr, -jnp.inf, h_lo + offset)

    m_ref[...] = jnp.full(m_ref.shape, NEG_BIG, F32)
    acc_ref[...] = jnp.zeros(acc_ref.shape, F32)
    r_i = lax.broadcasted_iota(jnp.int32, (KC, KC), 0)
    c_i = lax.broadcasted_iota(jnp.int32, (KC, KC), 1)
    tril = jnp.where(c_i <= r_i, 1.0, 0.0).astype(BF16)
    ones_rows = jnp.ones((DSA_VPAD, KC), BF16)

    HALF = GROUP * QB // 2

    def logits(c, slot, j, half):
        k0 = pl.multiple_of(c * KC, KC)
        cols = slice(half * HALF, (half + 1) * HALF)
        st_ref[slot, j, :, cols] = _dot(k_ref[pl.ds(k0, KC), j * LANE:(j + 1) * LANE],
                                        qa_ref[j, :, cols]).astype(BF16)

    for j in range(N_KV_HEADS):
        logits(0, 0, j, 0)
        logits(0, 0, j, 1)

    def attend(c, slot, tie_seen, c_next, need):
        k0 = pl.multiple_of(c * KC, KC)
        s = sc_ref[pl.ds(k0, KC), :]
        kpos = k0 + lax.broadcasted_iota(jnp.int32, (KC, 1), 0)
        eq = s == thr
        eqf = jnp.where(eq, 1.0, 0.0)
        incl = _dot(tril, eqf.astype(BF16))
        rank = tie_seen + incl - eqf
        sel = jnp.logical_and(jnp.logical_or(s > thr, jnp.logical_and(eq, rank < need)), kpos <= qpos)
        tie_seen = tie_seen + incl[KC - 1:KC, :]
        bias = jnp.where(sel, 0.0, NEG_BIG).astype(BF16)
        for j in range(N_KV_HEADS):
            alphas = []
            for g in range(GROUP):
                sl = slice(g * QB, (g + 1) * QB)
                sg = st_ref[slot, j, :, sl] + bias
                m_old = m_ref[j, :, sl]
                mx = jnp.max(_fold_rows(sg, jnp.maximum, 16), axis=0, keepdims=True).astype(F32)
                m_new = jnp.maximum(m_old, mx)
                alphas.append(jnp.exp2(m_old - m_new))
                m_ref[j, :, sl] = m_new
                p_ref[j, :, sl] = jnp.exp2(sg - m_new.astype(BF16))
                if g % 2 == 1:
                    logits(c_next, 1 - slot, j, g // 2)
            v_aug = jnp.concatenate([vT_ref[c, j * HEAD_DIM:(j + 1) * HEAD_DIM, :], ones_rows], axis=0)
            acc_ref[j] = acc_ref[j] * jnp.concatenate(alphas, axis=1) + _dot(v_aug, p_ref[j])
        return tie_seen

    need = k_top - (n_staged - count_not(lambda s: s > thr))

    def p3(i, tie_seen):
        c = 2 * i
        tie_seen = attend(c, 0, tie_seen, c + 1, need)
        return attend(c + 1, 1, tie_seen, clamp_chunk(c + 2), need)

    lax.fori_loop(0, npair, p3, jnp.zeros((1, QB), F32))

    for j in range(N_KV_HEADS):
        o = acc_ref[j, 0:HEAD_DIM, :] / acc_ref[j, HEAD_DIM:HEAD_DIM + 1, :]
        for g in range(GROUP):
            hh = j * GROUP + g
            oT_ref[hh * HEAD_DIM:(hh + 1) * HEAD_DIM, :] = o[:, g * QB:(g + 1) * QB].astype(BF16)


def _dsa_core(qT, k, vT, iqT, ik, iwT):
    B, QW, S = qT.shape
    KC, QB = DSA_KC, DSA_QB
    k_top = min(TOPK_MAX, S // 4)
    nc = S // KC
    assert S % KC == 0 and S % QB == 0 and nc % 2 == 0
    assert nc * (KC // 16) // N_CHAINS <= 256
    return pl.pallas_call(
        functools.partial(_dsa_core_kernel, k_top=k_top, n_chunks=nc),
        grid=(B, S // QB),
        in_specs=[
            pl.BlockSpec((None, IDX_HEADS * IDX_DIM, QB), lambda b, i: (b, 0, i)),
            pl.BlockSpec((None, IDX_HEADS, QB), lambda b, i: (b, 0, i)),
            pl.BlockSpec((None, QW, QB), lambda b, i: (b, 0, i)),
            pl.BlockSpec((None, S, LANE), lambda b, i: (b, 0, 0)),
            pl.BlockSpec((None, S, N_KV_HEADS * LANE), lambda b, i: (b, 0, 0)),
            pl.BlockSpec((None, nc, N_KV_HEADS * HEAD_DIM, KC), lambda b, i: (b, 0, 0, 0)),
        ],
        out_specs=pl.BlockSpec((None, QW, QB), lambda b, i: (b, 0, i)),
        out_shape=jax.ShapeDtypeStruct((B, QW, S), BF16),
        scratch_shapes=[
            pltpu.VMEM((S, QB), F32),
            pltpu.VMEM((S, QB), BF16),
            pltpu.VMEM((S, QB), BF16),
            pltpu.VMEM((S, QB), BF16),
            pltpu.VMEM((N_KV_HEADS, LANE, GROUP * QB), BF16),
            pltpu.VMEM((LANE, IDX_HEADS * QB), BF16),
            pltpu.VMEM((2, KC, IDX_HEADS * QB), F32),
            pltpu.VMEM((2, N_KV_HEADS, KC, GROUP * QB), BF16),
            pltpu.VMEM((N_KV_HEADS, KC, GROUP * QB), BF16),
            pltpu.VMEM((N_KV_HEADS, 1, GROUP * QB), F32),
            pltpu.VMEM((N_KV_HEADS, HEAD_DIM + DSA_VPAD, GROUP * QB), F32),
        ],
        compiler_params=_cparams("parallel", "arbitrary"),
    )(iqT, iwT, qT, ik, k, vT)


def _out_T_kernel(xT_ref, w_ref, h_ref, gt_ref, o_ref):
    o_ref[...] = h_ref[...] + gt_ref[...] * _dot_tn(xT_ref[...], w_ref[...])


def _dsa_out(oT, w_out, h, mod, tm=512):
    B, S, D = h.shape
    QW = oT.shape[1]
    return pl.pallas_call(
        _out_T_kernel,
        grid=(B, S // tm),
        in_specs=[
            pl.BlockSpec((None, QW, tm), lambda b, i: (b, 0, i)),
            pl.BlockSpec((QW, D), lambda b, i: (0, 0)),
            pl.BlockSpec((None, tm, D), lambda b, i: (b, i, 0)),
            pl.BlockSpec((None, None, 1, D), lambda b, i: (b, 5, 0, 0)),
        ],
        out_specs=pl.BlockSpec((None, tm, D), lambda b, i: (b, i, 0)),
        out_shape=jax.ShapeDtypeStruct((B, S, D), F32),
        compiler_params=_cparams("parallel", "parallel"),
    )(oT, w_out.astype(BF16), h, mod)


def _dsa_mixer(h, mod, w_in, q_norm, k_norm, ik_norm, w_out):
    qT, k, vT, iqT, ik, iwT = _dsa_in(h, mod, w_in, q_norm, k_norm, ik_norm)
    oT = _dsa_core(qT, k, vT, iqT, ik, iwT)
    return _dsa_out(oT, w_out, h, mod)


RWKV_HEAD = 64
GN_EPS = 64e-5
RW_PACK = 4
RW_LANES = RW_PACK * RWKV_HEAD
RW_C = 64
RW_TB = 512


def _split3(x):
    hi = x.astype(BF16)
    r1 = x - hi.astype(F32)
    mid = r1.astype(BF16)
    lo = (r1 - mid.astype(F32)).astype(BF16)
    return hi, mid, lo


def _gsum(x, g_ref, gt_ref):
    hi = x.astype(BF16)
    lo = (x - hi.astype(F32)).astype(BF16)
    s = _dot(jnp.concatenate([hi, lo], axis=1), g_ref[...])
    return _dot(jnp.concatenate(_split3(s), axis=1), gt_ref[...])


def _rwkv_in_kernel(h_ref, hp_ref, sh_ref, sc_ref, mu_ref, wrkv_ref, w0_ref, w1_ref, w2_ref,
                    a0_ref, a1_ref, a2_ref, g1_ref, g2_ref, kk_ref, ka_ref, gm_ref, gmt_ref,
                    r_ref, lw_ref, k_ref, v_ref, kkn_ref, b_ref, g_ref):
    sh, sc = sh_ref[...], sc_ref[...]
    u = _modulate(h_ref[...], sh, sc)
    tm = u.shape[0]
    prev = _modulate(hp_ref[...], sh, sc)[7:8, :]
    prev = jnp.where(pl.program_id(1) == 0, 0.0, prev)
    row = lax.broadcasted_iota(jnp.int32, (tm, 1), 0)
    xx = jnp.where(row == 0, prev, pltpu.roll(u, 1, 0)) - u
    mix = lambda i: (u + xx * mu_ref[i:i + 1, :]).astype(BF16)
    wl = w0_ref[...] + _dot(jnp.tanh(_dot(mix(3), w1_ref[...])).astype(BF16), w2_ref[...])
    a_lin = _dot(_dot(mix(4), a1_ref[...]).astype(BF16), a2_ref[...])
    g_hid = _dot(mix(5), g1_ref[...])
    k = _dot(mix(1), wrkv_ref[1])
    nwl = -wl
    w_log = -(jnp.maximum(nwl, 0.0) + jnp.log(1.0 + jnp.exp(-jnp.abs(nwl)))) - 0.5
    lw_ref[...] = -jnp.exp(w_log)
    r_ref[...] = _dot(mix(0), wrkv_ref[0])
    a = jax.nn.sigmoid(a0_ref[...] + a_lin)
    g_ref[...] = _dot(jax.nn.sigmoid(g_hid).astype(BF16), g2_ref[...])
    kk = k * kk_ref[...]
    nrm = jnp.sqrt(_gsum(kk * kk, gm_ref, gmt_ref))
    kk = kk / jnp.maximum(nrm, 1e-12)
    k_ref[...] = k * (1.0 + (a - 1.0) * ka_ref[...])
    kkn_ref[...] = kk
    b_ref[...] = kk * a
    v_ref[...] = _dot(mix(2), wrkv_ref[2])


def _head_indicator(D):
    gm = (np.arange(D)[:, None] // RWKV_HEAD == np.arange(D // RWKV_HEAD)[None, :]).astype(np.float32)
    return jnp.asarray(np.concatenate([gm, gm], axis=0), BF16), jnp.asarray(np.concatenate([gm.T] * 3, axis=0), BF16)


def _rwkv_in(h, mod, mu, w_rkv, w0, w1, w2, a0, a1, a2, g1, g2, k_k, k_a, tm=256):
    B, S, D = h.shape
    gm, gmt = _head_indicator(D)
    row = lambda a: a.astype(F32).reshape(1, D)
    full = lambda a: pl.BlockSpec(a.shape, lambda b, i: (0,) * a.ndim)
    mspec = lambda k: pl.BlockSpec((None, None, 1, D), lambda b, i: (b, k, 0, 0))
    tile = pl.BlockSpec((None, tm, D), lambda b, i: (b, i, 0))
    params = [mu.astype(F32), w_rkv.astype(BF16), row(w0), w1.astype(BF16), w2.astype(BF16),
              row(a0), a1.astype(BF16), a2.astype(BF16), g1.astype(BF16), g2.astype(BF16),
              row(k_k), row(k_a), gm, gmt]
    return pl.pallas_call(
        _rwkv_in_kernel,
        grid=(B, S // tm),
        in_specs=[tile,
                  pl.BlockSpec((None, 8, D), lambda b, i: (b, jnp.maximum(i * (tm // 8) - 1, 0), 0)),
                  mspec(3), mspec(4)] + [full(p) for p in params],
        out_specs=[tile] * 7,
        out_shape=[jax.ShapeDtypeStruct((B, S, D), F32)] * 7,
        compiler_params=_cparams("parallel", "parallel"),
    )(h, h, mod, mod, *params)


def _rwkv_scan_kernel(r_ref, lw_ref, k_ref, v_ref, kk_ref, b_ref, y_ref, zt_ref):
    C, P, L = RW_C, RW_PACK, RW_LANES
    N = P * C
    assert C == RWKV_HEAD
    n_packs = r_ref.shape[1] // L
    n_chunks = r_ref.shape[0] // C
    units = [(p, c) for p in range(n_packs) for c in range(n_chunks)]

    @pl.when(pl.program_id(1) == 0)
    def _():
        zt_ref[...] = jnp.zeros(zt_ref.shape, F32)

    lane = lax.broadcasted_iota(jnp.int32, (1, L), 1)
    hmask = [lane // C == hd for hd in range(P)]
    bf = lambda x: x.astype(BF16)
    stack = lambda x: jnp.concatenate([jnp.where(hmask[hd], x, jnp.zeros_like(x)) for hd in range(P)], axis=0)
    t_i = lax.broadcasted_iota(jnp.int32, (C, N), 0)
    i_i = lax.broadcasted_iota(jnp.int32, (C, N), 1) % C
    low_strict = t_i > i_i
    low_incl = t_i >= i_i
    eye_w = jnp.where(t_i == i_i, 1.0, 0.0)
    bd_mask = (lax.broadcasted_iota(jnp.int32, (L, L), 0) // C) == (lax.broadcasted_iota(jnp.int32, (L, L), 1) // C)
    tri = jnp.where(lax.broadcasted_iota(jnp.int32, (C, C), 0) >= lax.broadcasted_iota(jnp.int32, (C, C), 1),
                    1.0, 0.0).astype(BF16)
    tile = lambda ref, u: ref[u[1] * C:(u[1] + 1) * C, u[0] * L:(u[0] + 1) * L]

    cw = {}
    tri3 = jnp.concatenate([tri, tri, tri], axis=1)
    for u in units:
        cw[u] = _dot(tri3, jnp.concatenate(_split3(tile(lw_ref, u)), axis=0))
    al, rt, bw, kw, v, vs, aa, w_end = {}, {}, {}, {}, {}, {}, {}, {}
    for u in units:
        lw = tile(lw_ref, u)
        w_incl = jnp.exp(cw[u])
        w_excl = jnp.exp(cw[u] - lw)
        w_inv = jnp.exp(-cw[u])
        w_end[u] = w_incl[C - 1:C, :]
        v[u] = bf(tile(v_ref, u))
        vs[u] = stack(v[u])
        al[u] = bf(-tile(kk_ref, u) * w_excl)
        rt[u] = tile(r_ref, u) * w_incl
        bh = tile(b_ref, u) * w_inv
        kh = tile(k_ref, u) * w_inv
        bw[u] = bf(bh * w_end[u])
        kw[u] = bf(kh * w_end[u])
        aa[u] = _dot_nt(jnp.concatenate([al[u], bf(rt[u])], axis=0),
                        jnp.concatenate([stack(bf(bh)), stack(bf(kh))], axis=0))
    a_k, a_rb, pw, t = {}, {}, {}, {}
    for u in units:
        a_ab = jnp.where(low_strict, aa[u][:C, :N], 0.0)
        a_k[u] = bf(jnp.concatenate([jnp.where(low_strict, aa[u][:C, N:], 0.0),
                                     jnp.where(low_incl, aa[u][C:, N:], 0.0)], axis=0))
        a_rb[u] = bf(jnp.where(low_incl, aa[u][C:, :N], 0.0))
        pw[u] = a_ab
        t[u] = eye_w + a_ab
    n_fac = int(np.log2(C))
    akv, rkv = {}, {}
    for u in units:
        pwb = bf(pw[u])
        pw[u] = _dot(pwb, stack(pwb))
        both_v = _dot(a_k[u], vs[u])
        akv[u], rkv[u] = both_v[:C], both_v[C:]
    for s in range(2, n_fac + 1):
        for u in units:
            pwb = bf(pw[u])
            rhs = stack(pwb)
            if s < n_fac:
                both = _dot(jnp.concatenate([bf(t[u]), pwb], axis=0), rhs)
                t[u] = t[u] + both[:C]
                pw[u] = both[C:]
            else:
                t[u] = t[u] + _dot(bf(t[u]), rhs)
    pm, q = {}, {}
    for u in units:
        tb = bf(t[u])
        pm[u] = bf(_dot(tb, stack(al[u])))
        q[u] = bf(_dot(tb, stack(bf(akv[u]))))
    pr, y0, bk = {}, {}, {}
    for u in units:
        rp = bf(rt[u] + _dot(a_rb[u], stack(pm[u])))
        pr[u] = jnp.concatenate([pm[u], rp], axis=0)
        y0[u] = _dot(a_rb[u], stack(q[u])) + rkv[u]
        bk[u] = jnp.concatenate([bw[u], kw[u]], axis=0)
    zt = [zt_ref[p] for p in range(n_packs)]
    for c in range(n_chunks):
        for p in range(n_packs):
            u = (p, c)
            uy = _dot_nt(pr[u], bf(zt[p]))
            y_ref[c * C:(c + 1) * C, p * L:(p + 1) * L] = uy[C:] + y0[u]
            upd = _dot_tn(jnp.concatenate([bf(uy[:C] + q[u]), v[u]], axis=0), bk[u])
            zt[p] = zt[p] * w_end[u] + jnp.where(bd_mask, upd, 0.0)
    for p in range(n_packs):
        zt_ref[p] = zt[p]


def _rwkv_scan(r, lw, k, v, kk, b):
    B, S, D = r.shape
    assert S % RW_TB == 0 and RW_TB % RW_C == 0 and D % RW_LANES == 0
    tile = pl.BlockSpec((None, RW_TB, D), lambda bb, t: (bb, t, 0))
    return pl.pallas_call(
        _rwkv_scan_kernel,
        grid=(B, S // RW_TB),
        in_specs=[tile] * 6,
        out_specs=tile,
        out_shape=jax.ShapeDtypeStruct((B, S, D), F32),
        scratch_shapes=[pltpu.VMEM((D // RW_LANES, RW_LANES, RW_LANES), F32)],
        compiler_params=_cparams("parallel", "arbitrary"),
    )(r, lw, k, v, kk, b)


def _rwkv_out_kernel(y_ref, r_ref, k_ref, v_ref, g_ref, lnw_ref, lnb_ref, rk_ref, gm_ref, gmt_ref,
                     w_ref, h_ref, gt_ref, o_ref):
    y = y_ref[...]
    inv_n = 1.0 / RWKV_HEAD
    yc = y - _gsum(y, gm_ref, gmt_ref) * inv_n
    var = _gsum(yc * yc, gm_ref, gmt_ref) * inv_n
    yn = yc * lax.rsqrt(var + GN_EPS) * lnw_ref[...] + lnb_ref[...]
    v = v_ref[...]
    bonus = _gsum(r_ref[...] * k_ref[...] * rk_ref[...], gm_ref, gmt_ref) * v
    out = ((yn + bonus) * g_ref[...]).astype(BF16)
    o_ref[...] = h_ref[...] + gt_ref[...] * _dot(out, w_ref[...])


def _rwkv_out(y, r, k, v, g, ln_w, ln_b, r_k, w_out, h, mod, tm=256):
    B, S, D = h.shape
    gm, gmt = _head_indicator(D)
    row = lambda a: a.astype(F32).reshape(1, D)
    full = lambda a: pl.BlockSpec(a.shape, lambda b, i: (0,) * a.ndim)
    tile = pl.BlockSpec((None, tm, D), lambda b, i: (b, i, 0))
    params = [row(ln_w), row(ln_b), row(r_k), gm, gmt, w_out.astype(BF16)]
    return pl.pallas_call(
        _rwkv_out_kernel,
        grid=(B, S // tm),
        in_specs=[tile] * 5 + [full(p) for p in params] + [tile, pl.BlockSpec((None, None, 1, D), lambda b, i: (b, 5, 0, 0))],
        out_specs=tile,
        out_shape=jax.ShapeDtypeStruct((B, S, D), F32),
        compiler_params=_cparams("parallel", "parallel"),
    )(y, r, k, v, g, *params, h, mod)


def _rwkv_mixer(h, mod, mu, w_rkv, w0, w1, w2, a0, a1, a2, g1, g2, k_k, k_a, r_k, ln_w, ln_b, w_out):
    r, lw, k, v, kk, b, g = _rwkv_in(h, mod, mu, w_rkv, w0, w1, w2, a0, a1, a2, g1, g2, k_k, k_a)
    y = _rwkv_scan(r, lw, k, v, kk, b)
    return _rwkv_out(y, r, k, v, g, ln_w, ln_b, r_k, w_out, h, mod)


def kernel(x, c, ada_w, ada_b, ffn_w_gu, ffn_w_down, dsa_w_in, dsa_q_norm, dsa_k_norm, dsa_ik_norm, dsa_w_out, rwkv_mu, rwkv_w_rkv, rwkv_w0, rwkv_w1, rwkv_w2, rwkv_a0, rwkv_a1, rwkv_a2, rwkv_g1, rwkv_g2, rwkv_k_k, rwkv_k_a, rwkv_r_k, rwkv_ln_w, rwkv_ln_b, rwkv_w_out):
    B, S, D = x.shape
    depth = ada_w.shape[0]
    mods = _ada_mod(c, ada_w, ada_b)
    h = x
    for i in range(depth):
        mod = mods[i].reshape(B, N_MOD, 1, D)
        j = i // 2
        h = _ffn(h, mod, 0, ffn_w_gu[i, 0].astype(BF16), ffn_w_down[i, 0].astype(BF16))
        if i % 2 == 0:
            h = _dsa_mixer(h, mod, dsa_w_in[j], dsa_q_norm[j], dsa_k_norm[j], dsa_ik_norm[j], dsa_w_out[j])
        else:
            h = _rwkv_mixer(h, mod, rwkv_mu[j], rwkv_w_rkv[j], rwkv_w0[j], rwkv_w1[j], rwkv_w2[j], rwkv_a0[j],
                            rwkv_a1[j], rwkv_a2[j], rwkv_g1[j], rwkv_g2[j], rwkv_k_k[j], rwkv_k_a[j],
                            rwkv_r_k[j], rwkv_ln_w[j], rwkv_ln_b[j], rwkv_w_out[j])
        h = _ffn(h, mod, 6, ffn_w_gu[i, 1].astype(BF16), ffn_w_down[i, 1].astype(BF16))
    return h
```

```python
import functools

import jax
import jax.numpy as jnp
import numpy as np
from jax import lax
from jax.experimental import pallas as pl
from jax.experimental.pallas import tpu as pltpu

F32 = jnp.float32
BF16 = jnp.bfloat16

RMS_EPS = 1e-6
N_MOD = 9
VMEM_LIMIT_BYTES = 56 * 1024 * 1024

N_HEADS = 16
HEAD_DIM = 64
N_KV_HEADS = 4
GROUP = N_HEADS // N_KV_HEADS
IDX_HEADS = 8
IDX_DIM = 64
TOPK_MAX = 256
LANE = 128
LOG2E = 1.4426950408889634


def _cparams(*sem):
    return pltpu.CompilerParams(dimension_semantics=sem, vmem_limit_bytes=VMEM_LIMIT_BYTES)


def _dot(a, b):
    return jnp.dot(a, b, preferred_element_type=F32)


def _dot_nt(a, b):
    return lax.dot_general(a, b, (((1,), (1,)), ((), ())), preferred_element_type=F32)


def _dot_tn(a, b):
    return lax.dot_general(a, b, (((0,), (0,)), ((), ())), preferred_element_type=F32)


def _modulate(h, sh, sc):
    ms = jnp.mean(h * h, axis=-1, keepdims=True)
    return h * lax.rsqrt(ms + RMS_EPS) * (1.0 + sc) + sh


def _ada_kernel(c_ref, w_ref, b_ref, o_ref):
    c = c_ref[...]
    ca = (c * jax.nn.sigmoid(c)).astype(BF16)
    o_ref[...] = _dot(ca, w_ref[...].astype(BF16)) + b_ref[...]


def _ada_mod(c, ada_w, ada_b):
    L, D, N = ada_w.shape
    B = c.shape[0]
    tn = 1536
    return pl.pallas_call(
        _ada_kernel,
        grid=(L, N // tn),
        in_specs=[
            pl.BlockSpec((B, D), lambda l, j: (0, 0)),
            pl.BlockSpec((None, D, tn), lambda l, j: (l, 0, j)),
            pl.BlockSpec((None, 1, tn), lambda l, j: (l, 0, j)),
        ],
        out_specs=pl.BlockSpec((None, B, tn), lambda l, j: (l, 0, j)),
        out_shape=jax.ShapeDtypeStruct((L, B, N), F32),
        compiler_params=_cparams("parallel", "parallel"),
    )(c, ada_w, ada_b.reshape(L, 1, N))


FF_CHUNK = 256


def _ffn_kernel(h_ref, sh_ref, sc_ref, gt_ref, wgu_ref, wd_ref, o_ref, *, d_ff):
    h = h_ref[...]
    u = _modulate(h, sh_ref[...], sc_ref[...]).astype(BF16)
    acc = None
    for c in range(d_ff // FF_CHUNK):
        lo = c * FF_CHUNK
        g = _dot(u, wgu_ref[:, lo:lo + FF_CHUNK])
        up = _dot(u, wgu_ref[:, d_ff + lo:d_ff + lo + FF_CHUNK])
        a = (g * jax.nn.sigmoid(g) * up).astype(BF16)
        part = _dot(a, wd_ref[lo:lo + FF_CHUNK, :])
        acc = part if acc is None else acc + part
    o_ref[...] = h + (0.5 * gt_ref[...]) * acc


def _ffn(h, mod, k0, w_gu, w_down, tm=512):
    B, S, D = h.shape
    d_ff = w_down.shape[0]
    assert S % tm == 0 and d_ff % FF_CHUNK == 0
    mspec = lambda k: pl.BlockSpec((None, None, 1, D), lambda b, i: (b, k, 0, 0))
    return pl.pallas_call(
        functools.partial(_ffn_kernel, d_ff=d_ff),
        grid=(B, S // tm),
        in_specs=[
            pl.BlockSpec((None, tm, D), lambda b, i: (b, i, 0)),
            mspec(k0), mspec(k0 + 1), mspec(k0 + 2),
            pl.BlockSpec((D, 2 * d_ff), lambda b, i: (0, 0), pipeline_mode=pl.Buffered(1)),
            pl.BlockSpec((d_ff, D), lambda b, i: (0, 0), pipeline_mode=pl.Buffered(1)),
        ],
        out_specs=pl.BlockSpec((None, tm, D), lambda b, i: (b, i, 0)),
        out_shape=jax.ShapeDtypeStruct((B, S, D), F32),
        compiler_params=_cparams("parallel", "parallel"),
    )(h, mod, mod, mod, w_gu, w_down)


def _dsa_in_kernel(h_ref, sh_ref, sc_ref, wT_ref, wn_ref, qn_ref, kn_ref, ikn_ref,
                   qT_ref, k_ref, vT_ref, iqT_ref, ik_ref, iwT_ref):
    u = _modulate(h_ref[...], sh_ref[...], sc_ref[...]).astype(BF16)
    tm = u.shape[0]
    QW, KW, IQW = N_HEADS * HEAD_DIM, N_KV_HEADS * HEAD_DIM, IDX_HEADS * IDX_DIM
    yT = _dot_nt(wT_ref[...], u)
    qT = yT[0:QW].reshape(N_HEADS, HEAD_DIM, tm)
    ms = jnp.mean(qT * qT, axis=1, keepdims=True)
    qT = qT * lax.rsqrt(ms + RMS_EPS) * qn_ref[...][None] * (HEAD_DIM ** -0.5 * LOG2E)
    qT_ref[...] = qT.reshape(QW, tm).astype(BF16)
    for cc in range(tm // DSA_KC):
        vT_ref[cc] = yT[QW:QW + KW, cc * DSA_KC:(cc + 1) * DSA_KC].astype(BF16)
    iqT_ref[...] = yT[QW + KW:QW + KW + IQW].astype(BF16)
    iwT_ref[...] = yT[QW + KW + IQW:QW + KW + IQW + IDX_HEADS] * (IDX_HEADS ** -0.5 * IDX_DIM ** -0.5)
    y = _dot(u, wn_ref[...])
    kn = kn_ref[...]
    for j in range(N_KV_HEADS):
        kj = y[:, j * LANE:(j + 1) * LANE]
        ms = jnp.sum(kj * kj, axis=-1, keepdims=True) * (1.0 / HEAD_DIM)
        k_ref[:, j * LANE:(j + 1) * LANE] = (kj * lax.rsqrt(ms + RMS_EPS) * kn).astype(BF16)
    ik = y[:, N_KV_HEADS * LANE:]
    ms = jnp.sum(ik * ik, axis=-1, keepdims=True) * (1.0 / IDX_DIM)
    ik_ref[...] = (ik * lax.rsqrt(ms + RMS_EPS) * ikn_ref[...]).astype(BF16)


DSA_KC = 256
DSA_QB = 256


def _dsa_in(h, mod, w_in, q_norm, k_norm, ik_norm, tm=2 * DSA_KC):
    B, S, D = h.shape
    QW, KW = N_HEADS * HEAD_DIM, N_KV_HEADS * HEAD_DIM
    IQW = IDX_HEADS * IDX_DIM
    o1, o2, o3, o4, o5 = QW, QW + KW, QW + 2 * KW, QW + 2 * KW + IQW, QW + 2 * KW + IQW + IDX_DIM
    wb = w_in.astype(BF16)
    wqT = wb[:, :o1].T
    wk = jnp.pad(wb[:, o1:o2].reshape(D, N_KV_HEADS, HEAD_DIM), ((0, 0), (0, 0), (0, LANE - HEAD_DIM))).reshape(D, N_KV_HEADS * LANE)
    wvT = wb[:, o2:o3].T
    wiqT = wb[:, o3:o4].T
    wik = jnp.pad(wb[:, o4:o5], ((0, 0), (0, LANE - IDX_DIM)))
    wiwT = jnp.pad(wb[:, o5:].T, ((0, 16 - IDX_HEADS), (0, 0)))
    wT = jnp.concatenate([wqT, wvT, wiqT, wiwT], axis=0)
    wn = jnp.concatenate([wk, wik], axis=1)
    qn = q_norm.astype(F32).reshape(HEAD_DIM, 1)
    kn = jnp.pad(k_norm.astype(F32), (0, LANE - HEAD_DIM)).reshape(1, LANE)
    ikn = jnp.pad(ik_norm.astype(F32), (0, LANE - IDX_DIM)).reshape(1, LANE)
    mspec = lambda k: pl.BlockSpec((None, None, 1, D), lambda b, i: (b, k, 0, 0))
    full = lambda a: pl.BlockSpec(a.shape, lambda b, i: (0,) * a.ndim)
    nc = S // tm
    cpt = tm // DSA_KC
    return pl.pallas_call(
        _dsa_in_kernel,
        grid=(B, nc),
        in_specs=[pl.BlockSpec((None, tm, D), lambda b, i: (b, i, 0)), mspec(3), mspec(4),
                  full(wT), full(wn), full(qn), full(kn), full(ikn)],
        out_specs=[
            pl.BlockSpec((None, QW, tm), lambda b, i: (b, 0, i)),
            pl.BlockSpec((None, tm, N_KV_HEADS * LANE), lambda b, i: (b, i, 0)),
            pl.BlockSpec((None, cpt, KW, DSA_KC), lambda b, i: (b, i, 0, 0)),
            pl.BlockSpec((None, IQW, tm), lambda b, i: (b, 0, i)),
            pl.BlockSpec((None, tm, LANE), lambda b, i: (b, i, 0)),
            pl.BlockSpec((None, IDX_HEADS, tm), lambda b, i: (b, 0, i)),
        ],
        out_shape=[
            jax.ShapeDtypeStruct((B, QW, S), BF16),
            jax.ShapeDtypeStruct((B, S, N_KV_HEADS * LANE), BF16),
            jax.ShapeDtypeStruct((B, nc * cpt, KW, DSA_KC), BF16),
            jax.ShapeDtypeStruct((B, IQW, S), BF16),
            jax.ShapeDtypeStruct((B, S, LANE), BF16),
            jax.ShapeDtypeStruct((B, IDX_HEADS, S), F32),
        ],
        compiler_params=_cparams("parallel", "parallel"),
    )(h, mod, mod, wT, wn, qn, kn, ikn)


NEG_BIG = -(2.0 ** 100)
DSA_VPAD = 16
N_CHAINS = 4


def _ukey_to_float(ukey):
    key = ukey ^ np.int32(-2 ** 31)
    bits = key ^ ((key >> 31) & np.int32(0x7FFFFFFF))
    return lax.bitcast_convert_type(bits, F32)


def _fold_rows(x, op, rows):
    while x.shape[0] > rows:
        half = x.shape[0] // 2
        x = op(x[:half], x[half:])
    return x


def _dsa_core_kernel(iqT_ref, iwT_ref, qT_ref, ik_ref, k_ref, vT_ref, oT_ref,
                     sc_ref, fl_ref, whi_ref, wlo_ref, qa_ref, iqa_ref, d_ref, st_ref, p_ref, m_ref, acc_ref,
                     *, k_top, n_chunks):
    KC, QB = DSA_KC, DSA_QB
    qi = pl.program_id(1)
    q0 = qi * QB
    nk = (q0 + QB + KC - 1) // KC
    qpos = q0 + lax.broadcasted_iota(jnp.int32, (1, QB), 1)
    zeros_h = jnp.zeros((LANE - HEAD_DIM, QB), BF16)

    for h in range(IDX_HEADS):
        iqa_ref[:, h * QB:(h + 1) * QB] = jnp.concatenate(
            [iqT_ref[h * IDX_DIM:(h + 1) * IDX_DIM, :], zeros_h], axis=0)
    for j in range(N_KV_HEADS):
        for g in range(GROUP):
            hh = j * GROUP + g
            qa_ref[j, :, g * QB:(g + 1) * QB] = jnp.concatenate(
                [qT_ref[hh * HEAD_DIM:(hh + 1) * HEAD_DIM, :], zeros_h], axis=0)
    iw = iwT_ref[...]

    npair = (nk + 1) // 2

    def clamp_chunk(c):
        return jnp.minimum(c, n_chunks - 1)

    def idx_dot(c, slot):
        k0 = pl.multiple_of(c * KC, KC)
        d_ref[slot] = _dot(ik_ref[pl.ds(k0, KC), :], iqa_ref[...])

    def idx_score(c, slot):
        k0 = pl.multiple_of(c * KC, KC)
        s = jnp.maximum(d_ref[slot, :, 0:QB], 0.0) * iw[0:1, :]
        for h in range(1, IDX_HEADS):
            s = s + jnp.maximum(d_ref[slot, :, h * QB:(h + 1) * QB], 0.0) * iw[h:h + 1, :]
        kpos = k0 + lax.broadcasted_iota(jnp.int32, (KC, 1), 0)
        s = jnp.where(kpos <= qpos, s, -jnp.inf)
        sc_ref[pl.ds(k0, KC), :] = s
        r = s.astype(BF16)
        rf = r.astype(F32)
        below = (rf * jnp.where(rf > 0.0, 1.0 - 1.25 * 2.0 ** -8, 1.0 + 1.25 * 2.0 ** -8)).astype(BF16)
        fl_ref[pl.ds(k0, KC), :] = jnp.where(rf > s, below, r)

    idx_dot(0, 0)

    def p1(i, carry):
        c = 2 * i
        idx_dot(c + 1, 1)
        idx_score(c, 0)
        idx_dot(clamp_chunk(c + 2), 0)
        idx_score(c + 1, 1)
        return carry

    lax.fori_loop(0, npair, p1, 0)

    def count_not(pred):
        def body(i, accs):
            accs = list(accs)
            for half in range(2):
                k0 = pl.multiple_of((2 * i + half) * KC, KC)
                ind = jnp.where(pred(sc_ref[pl.ds(k0, KC), :]), 0.0, 1.0)
                for r in range(KC // 8):
                    accs[r % N_CHAINS] = accs[r % N_CHAINS] + ind[r * 8:(r + 1) * 8]
            return tuple(accs)
        accs = lax.fori_loop(0, npair, body, tuple(jnp.zeros((8, QB), F32) for _ in range(N_CHAINS)))
        acc = (accs[0] + accs[1]) + (accs[2] + accs[3])
        return jnp.sum(acc, axis=0, keepdims=True)

    n_staged = (npair * (2 * KC)).astype(F32)

    def count_ge16(ref, t):
        tb = t.astype(BF16)
        one, zero = jnp.ones((), BF16), jnp.zeros((), BF16)

        def body(i, accs):
            accs = list(accs)
            for half in range(2):
                k0 = pl.multiple_of((2 * i + half) * KC, KC)
                ind = jnp.where(ref[pl.ds(k0, KC), :] < tb, zero, one)
                for r in range(KC // 16):
                    accs[r % N_CHAINS] = accs[r % N_CHAINS] + ind[r * 16:(r + 1) * 16]
            return tuple(accs)
        accs = lax.fori_loop(0, npair, body, tuple(jnp.zeros((16, QB), BF16) for _ in range(N_CHAINS)))
        acc = (accs[0].astype(F32) + accs[1].astype(F32)) + (accs[2].astype(F32) + accs[3].astype(F32))
        return jnp.sum(acc, axis=0, keepdims=True)

    def digit_search(ref, n_bits, k_need, to_float):
        d = jnp.zeros((1, QB), F32)
        for bit in range(n_bits - 1, -1, -1):
            cand = d + float(1 << bit)
            d = jnp.where(count_ge16(ref, to_float(cand)) >= k_need, cand, d)
        return d

    def val16(d):
        di = d.astype(jnp.int32)
        return _ukey_to_float(jnp.where(di >= 0x8000, di << 16, (di << 16) | np.int32(0xFFFF)))

    key_hi = digit_search(fl_ref, 16, k_top, val16)
    h_lo = val16(key_hi)
    h_up = val16(key_hi + 1.0)
    no_thr = h_lo == -jnp.inf
    h_up = jnp.where(h_up > h_lo, h_up, np.float32(2.0 ** -126))
    width = jnp.where(no_thr, 1.0, h_up - h_lo)
    tiny = width < 2.0 ** -60
    pre = jnp.where(tiny, np.float32(2.0 ** 60), np.float32(1.0))
    unpre = jnp.where(tiny, np.float32(2.0 ** -60), np.float32(1.0))
    from_int = (width * pre) * np.float32(2.0 ** -16)
    to_int = lax.bitcast_convert_type(np.int32(0x7F000000) - lax.bitcast_convert_type(from_int, jnp.int32), F32)

    def split(i, carry):
        for half in range(2):
            k0 = pl.multiple_of((2 * i + half) * KC, KC)
            w = jnp.clip(((sc_ref[pl.ds(k0, KC), :] - h_lo) * pre) * to_int, -256.0, 65536.0)
            w_hi = jnp.floor(w * (1.0 / 256.0))
            whi_ref[pl.ds(k0, KC), :] = w_hi.astype(BF16)
            wlo_ref[pl.ds(k0, KC), :] = (w - 256.0 * w_hi).astype(BF16)
        return carry
    lax.fori_loop(0, npair, split, 0)

    ident = lambda d: d
    d_mid = digit_search(whi_ref, 8, k_top, ident)
    k_low = k_top - count_ge16(whi_ref, d_mid + 1.0)
    d_mid_b = d_mid.astype(BF16)

    def keep_low(i, carry):
        for half in range(2):
            k0 = pl.multiple_of((2 * i + half) * KC, KC)
            whi_ref[pl.ds(k0, KC), :] = jnp.where(whi_ref[pl.ds(k0, KC), :] == d_mid_b,
                                                  wlo_ref[pl.ds(k0, KC), :], -jnp.ones((), BF16))
        return carry
    lax.fori_loop(0, npair, keep_low, 0)
    d_low = digit_search(whi_ref, 8, k_low, ident)
    offset = ((d_mid * 256.0 + d_low) * from_int) * unpre
    thr = jnp.where(no_thr, -jnp.inf, h_lo + offset)

    m_ref[...] = jnp.full(m_ref.shape, NEG_BIG, F32)
    acc_ref[...] = jnp.zeros(acc_ref.shape, F32)
    r_i = lax.broadcasted_iota(jnp.int32, (KC, KC), 0)
    c_i = lax.broadcasted_iota(jnp.int32, (KC, KC), 1)
    tril = jnp.where(c_i <= r_i, 1.0, 0.0).astype(BF16)
    ones_rows = jnp.ones((DSA_VPAD, KC), BF16)

    HALF = GROUP * QB // 2

    def logits(c, slot, j, half):
        k0 = pl.multiple_of(c * KC, KC)
        cols = slice(half * HALF, (half + 1) * HALF)
        st_ref[slot, j, :, cols] = _dot(k_ref[pl.ds(k0, KC), j * LANE:(j + 1) * LANE],
                                        qa_ref[j, :, cols]).astype(BF16)

    for j in range(N_KV_HEADS):
        logits(0, 0, j, 0)
        logits(0, 0, j, 1)

    def attend(c, slot, tie_seen, c_next, need):
        k0 = pl.multiple_of(c * KC, KC)
        s = sc_ref[pl.ds(k0, KC), :]
        kpos = k0 + lax.broadcasted_iota(jnp.int32, (KC, 1), 0)
        eq = s == thr
        eqf = jnp.where(eq, 1.0, 0.0)
        incl = _dot(tril, eqf.astype(BF16))
        rank = tie_seen + incl - eqf
        sel = jnp.logical_and(jnp.logical_or(s > thr, jnp.logical_and(eq, rank < need)), kpos <= qpos)
        tie_seen = tie_seen + incl[KC - 1:KC, :]
        bias = jnp.where(sel, 0.0, NEG_BIG).astype(BF16)
        for j in range(N_KV_HEADS):
            alphas = []
            for g in range(GROUP):
                sl = slice(g * QB, (g + 1) * QB)
                sg = st_ref[slot, j, :, sl] + bias
                m_old = m_ref[j, :, sl]
                mx = jnp.max(_fold_rows(sg, jnp.maximum, 16), axis=0, keepdims=True).astype(F32)
                m_new = jnp.maximum(m_old, mx)
                alphas.append(jnp.exp2(m_old - m_new))
                m_ref[j, :, sl] = m_new
                p_ref[j, :, sl] = jnp.exp2(sg - m_new.astype(BF16))
                if g % 2 == 1:
                    logits(c_next, 1 - slot, j, g // 2)
            v_aug = jnp.concatenate([vT_ref[c, j * HEAD_DIM:(j + 1) * HEAD_DIM, :], ones_rows], axis=0)
            acc_ref[j] = acc_ref[j] * jnp.concatenate(alphas, axis=1) + _dot(v_aug, p_ref[j])
        return tie_seen

    need = k_top - (n_staged - count_not(lambda s: s > thr))

    def p3(i, tie_seen):
        c = 2 * i
        tie_seen = attend(c, 0, tie_seen, c + 1, need)
        return attend(c + 1, 1, tie_seen, clamp_chunk(c + 2), need)

    lax.fori_loop(0, npair, p3, jnp.zeros((1, QB), F32))

    for j in range(N_KV_HEADS):
        o = acc_ref[j, 0:HEAD_DIM, :] / acc_ref[j, HEAD_DIM:HEAD_DIM + 1, :]
        for g in range(GROUP):
            hh = j * GROUP + g
            oT_ref[hh * HEAD_DIM:(hh + 1) * HEAD_DIM, :] = o[:, g * QB:(g + 1) * QB].astype(BF16)


def _dsa_core(qT, k, vT, iqT, ik, iwT):
    B, QW, S = qT.shape
    KC, QB = DSA_KC, DSA_QB
    k_top = min(TOPK_MAX, S // 4)
    nc = S // KC
    assert S % KC == 0 and S % QB == 0 and nc % 2 == 0
    assert nc * (KC // 16) // N_CHAINS <= 256
    return pl.pallas_call(
        functools.partial(_dsa_core_kernel, k_top=k_top, n_chunks=nc),
        grid=(B, S // QB),
        in_specs=[
            pl.BlockSpec((None, IDX_HEADS * IDX_DIM, QB), lambda b, i: (b, 0, i)),
            pl.BlockSpec((None, IDX_HEADS, QB), lambda b, i: (b, 0, i)),
            pl.BlockSpec((None, QW, QB), lambda b, i: (b, 0, i)),
            pl.BlockSpec((None, S, LANE), lambda b, i: (b, 0, 0)),
            pl.BlockSpec((None, S, N_KV_HEADS * LANE), lambda b, i: (b, 0, 0)),
            pl.BlockSpec((None, nc, N_KV_HEADS * HEAD_DIM, KC), lambda b, i: (b, 0, 0, 0)),
        ],
        out_specs=pl.BlockSpec((None, QW, QB), lambda b, i: (b, 0, i)),
        out_shape=jax.ShapeDtypeStruct((B, QW, S), BF16),
        scratch_shapes=[
            pltpu.VMEM((S, QB), F32),
            pltpu.VMEM((S, QB), BF16),
            pltpu.VMEM((S, QB), BF16),
            pltpu.VMEM((S, QB), BF16),
            pltpu.VMEM((N_KV_HEADS, LANE, GROUP * QB), BF16),
            pltpu.VMEM((LANE, IDX_HEADS * QB), BF16),
            pltpu.VMEM((2, KC, IDX_HEADS * QB), F32),
            pltpu.VMEM((2, N_KV_HEADS, KC, GROUP * QB), BF16),
            pltpu.VMEM((N_KV_HEADS, KC, GROUP * QB), BF16),
            pltpu.VMEM((N_KV_HEADS, 1, GROUP * QB), F32),
            pltpu.VMEM((N_KV_HEADS, HEAD_DIM + DSA_VPAD, GROUP * QB), F32),
        ],
        compiler_params=_cparams("parallel", "arbitrary"),
    )(iqT, iwT, qT, ik, k, vT)


def _out_T_kernel(xT_ref, w_ref, h_ref, gt_ref, o_ref):
    o_ref[...] = h_ref[...] + gt_ref[...] * _dot_tn(xT_ref[...], w_ref[...])


def _dsa_out(oT, w_out, h, mod, tm=512):
    B, S, D = h.shape
    QW = oT.shape[1]
    return pl.pallas_call(
        _out_T_kernel,
        grid=(B, S // tm),
        in_specs=[
            pl.BlockSpec((None, QW, tm), lambda b, i: (b, 0, i)),
            pl.BlockSpec((QW, D), lambda b, i: (0, 0)),
            pl.BlockSpec((None, tm, D), lambda b, i: (b, i, 0)),
            pl.BlockSpec((None, None, 1, D), lambda b, i: (b, 5, 0, 0)),
        ],
        out_specs=pl.BlockSpec((None, tm, D), lambda b, i: (b, i, 0)),
        out_shape=jax.ShapeDtypeStruct((B, S, D), F32),
        compiler_params=_cparams("parallel", "parallel"),
    )(oT, w_out.astype(BF16), h, mod)


def _dsa_mixer(h, mod, w_in, q_norm, k_norm, ik_norm, w_out):
    qT, k, vT, iqT, ik, iwT = _dsa_in(h, mod, w_in, q_norm, k_norm, ik_norm)
    oT = _dsa_core(qT, k, vT, iqT, ik, iwT)
    return _dsa_out(oT, w_out, h, mod)


RWKV_HEAD = 64
GN_EPS = 64e-5
RW_PACK = 4
RW_LANES = RW_PACK * RWKV_HEAD
RW_C = 64
RW_TB = 256


def _split3(x):
    hi = x.astype(BF16)
    r1 = x - hi.astype(F32)
    mid = r1.astype(BF16)
    lo = (r1 - mid.astype(F32)).astype(BF16)
    return hi, mid, lo


def _gsum(x, g_ref, gt_ref):
    hi = x.astype(BF16)
    lo = (x - hi.astype(F32)).astype(BF16)
    s = _dot(jnp.concatenate([hi, lo], axis=1), g_ref[...])
    return _dot(jnp.concatenate(_split3(s), axis=1), gt_ref[...])


def _rwkv_in_kernel(h_ref, hp_ref, sh_ref, sc_ref, mu_ref, wrkv_ref, w0_ref, w1_ref, w2_ref,
                    a0_ref, a1_ref, a2_ref, g1_ref, g2_ref, kk_ref, ka_ref, gm_ref, gmt_ref,
                    r_ref, lw_ref, k_ref, v_ref, kkn_ref, b_ref, g_ref):
    sh, sc = sh_ref[...], sc_ref[...]
    u = _modulate(h_ref[...], sh, sc)
    tm = u.shape[0]
    prev = _modulate(hp_ref[...], sh, sc)[7:8, :]
    prev = jnp.where(pl.program_id(1) == 0, 0.0, prev)
    row = lax.broadcasted_iota(jnp.int32, (tm, 1), 0)
    xx = jnp.where(row == 0, prev, pltpu.roll(u, 1, 0)) - u
    mix = lambda i: (u + xx * mu_ref[i:i + 1, :]).astype(BF16)
    wl = w0_ref[...] + _dot(jnp.tanh(_dot(mix(3), w1_ref[...])).astype(BF16), w2_ref[...])
    a_lin = _dot(_dot(mix(4), a1_ref[...]).astype(BF16), a2_ref[...])
    g_hid = _dot(mix(5), g1_ref[...])
    k = _dot(mix(1), wrkv_ref[1])
    nwl = -wl
    w_log = -(jnp.maximum(nwl, 0.0) + jnp.log(1.0 + jnp.exp(-jnp.abs(nwl)))) - 0.5
    lw_ref[...] = -jnp.exp(w_log)
    r_ref[...] = _dot(mix(0), wrkv_ref[0])
    a = jax.nn.sigmoid(a0_ref[...] + a_lin)
    g_ref[...] = _dot(jax.nn.sigmoid(g_hid).astype(BF16), g2_ref[...])
    kk = k * kk_ref[...]
    nrm = jnp.sqrt(_gsum(kk * kk, gm_ref, gmt_ref))
    kk = kk / jnp.maximum(nrm, 1e-12)
    k_ref[...] = k * (1.0 + (a - 1.0) * ka_ref[...])
    kkn_ref[...] = kk
    b_ref[...] = kk * a
    v_ref[...] = _dot(mix(2), wrkv_ref[2])


def _head_indicator(D):
    gm = (np.arange(D)[:, None] // RWKV_HEAD == np.arange(D // RWKV_HEAD)[None, :]).astype(np.float32)
    return jnp.asarray(np.concatenate([gm, gm], axis=0), BF16), jnp.asarray(np.concatenate([gm.T] * 3, axis=0), BF16)


def _rwkv_in(h, mod, mu, w_rkv, w0, w1, w2, a0, a1, a2, g1, g2, k_k, k_a, tm=256):
    B, S, D = h.shape
    gm, gmt = _head_indicator(D)
    row = lambda a: a.astype(F32).reshape(1, D)
    full = lambda a: pl.BlockSpec(a.shape, lambda b, i: (0,) * a.ndim)
    mspec = lambda k: pl.BlockSpec((None, None, 1, D), lambda b, i: (b, k, 0, 0))
    tile = pl.BlockSpec((None, tm, D), lambda b, i: (b, i, 0))
    params = [mu.astype(F32), w_rkv.astype(BF16), row(w0), w1.astype(BF16), w2.astype(BF16),
              row(a0), a1.astype(BF16), a2.astype(BF16), g1.astype(BF16), g2.astype(BF16),
              row(k_k), row(k_a), gm, gmt]
    return pl.pallas_call(
        _rwkv_in_kernel,
        grid=(B, S // tm),
        in_specs=[tile,
                  pl.BlockSpec((None, 8, D), lambda b, i: (b, jnp.maximum(i * (tm // 8) - 1, 0), 0)),
                  mspec(3), mspec(4)] + [full(p) for p in params],
        out_specs=[tile] * 7,
        out_shape=[jax.ShapeDtypeStruct((B, S, D), F32)] * 7,
        compiler_params=_cparams("parallel", "parallel"),
    )(h, h, mod, mod, *params)


def _rwkv_scan_kernel(r_ref, lw_ref, k_ref, v_ref, kk_ref, b_ref, g_ref, h_ref, gt_ref,
                      lnw_ref, lnb_ref, rk_ref, gm_ref, gmt_ref, w_ref, o_ref, zt_ref, y_ref):
    C, P, L = RW_C, RW_PACK, RW_LANES
    N = P * C
    assert C == RWKV_HEAD
    n_packs = r_ref.shape[1] // L
    n_chunks = r_ref.shape[0] // C
    units = [(p, c) for p in range(n_packs) for c in range(n_chunks)]

    @pl.when(pl.program_id(1) == 0)
    def _():
        zt_ref[...] = jnp.zeros(zt_ref.shape, F32)

    lane = lax.broadcasted_iota(jnp.int32, (1, L), 1)
    hmask = [lane // C == hd for hd in range(P)]
    bf = lambda x: x.astype(BF16)
    stack = lambda x: jnp.concatenate([jnp.where(hmask[hd], x, jnp.zeros_like(x)) for hd in range(P)], axis=0)
    t_i = lax.broadcasted_iota(jnp.int32, (C, N), 0)
    i_i = lax.broadcasted_iota(jnp.int32, (C, N), 1) % C
    low_strict = t_i > i_i
    low_incl = t_i >= i_i
    eye_w = jnp.where(t_i == i_i, 1.0, 0.0)
    bd_mask = (lax.broadcasted_iota(jnp.int32, (L, L), 0) // C) == (lax.broadcasted_iota(jnp.int32, (L, L), 1) // C)
    tri = jnp.where(lax.broadcasted_iota(jnp.int32, (C, C), 0) >= lax.broadcasted_iota(jnp.int32, (C, C), 1),
                    1.0, 0.0).astype(BF16)
    tile = lambda ref, u: ref[u[1] * C:(u[1] + 1) * C, u[0] * L:(u[0] + 1) * L]

    cw = {}
    tri3 = jnp.concatenate([tri, tri, tri], axis=1)
    for u in units:
        cw[u] = _dot(tri3, jnp.concatenate(_split3(tile(lw_ref, u)), axis=0))
    al, rt, bw, kw, v, vs, aa, w_end = {}, {}, {}, {}, {}, {}, {}, {}
    for u in units:
        lw = tile(lw_ref, u)
        w_incl = jnp.exp(cw[u])
        w_excl = jnp.exp(cw[u] - lw)
        w_inv = jnp.exp(-cw[u])
        w_end[u] = w_incl[C - 1:C, :]
        v[u] = bf(tile(v_ref, u))
        vs[u] = stack(v[u])
        al[u] = bf(-tile(kk_ref, u) * w_excl)
        rt[u] = tile(r_ref, u) * w_incl
        bh = tile(b_ref, u) * w_inv
        kh = tile(k_ref, u) * w_inv
        bw[u] = bf(bh * w_end[u])
        kw[u] = bf(kh * w_end[u])
        aa[u] = _dot_nt(jnp.concatenate([al[u], bf(rt[u])], axis=0),
                        jnp.concatenate([stack(bf(bh)), stack(bf(kh))], axis=0))
    a_k, a_rb, pw, t = {}, {}, {}, {}
    for u in units:
        a_ab = jnp.where(low_strict, aa[u][:C, :N], 0.0)
        a_k[u] = bf(jnp.concatenate([jnp.where(low_strict, aa[u][:C, N:], 0.0),
                                     jnp.where(low_incl, aa[u][C:, N:], 0.0)], axis=0))
        a_rb[u] = bf(jnp.where(low_incl, aa[u][C:, :N], 0.0))
        pw[u] = a_ab
        t[u] = eye_w + a_ab
    n_fac = int(np.log2(C))
    akv, rkv = {}, {}
    for u in units:
        pwb = bf(pw[u])
        pw[u] = _dot(pwb, stack(pwb))
        both_v = _dot(a_k[u], vs[u])
        akv[u], rkv[u] = both_v[:C], both_v[C:]
    for s in range(2, n_fac + 1):
        for u in units:
            pwb = bf(pw[u])
            rhs = stack(pwb)
            if s < n_fac:
                both = _dot(jnp.concatenate([bf(t[u]), pwb], axis=0), rhs)
                t[u] = t[u] + both[:C]
                pw[u] = both[C:]
            else:
                t[u] = t[u] + _dot(bf(t[u]), rhs)
    pm, q = {}, {}
    for u in units:
        tb = bf(t[u])
        pm[u] = bf(_dot(tb, stack(al[u])))
        q[u] = bf(_dot(tb, stack(bf(akv[u]))))
    pr, y0, bk = {}, {}, {}
    for u in units:
        rp = bf(rt[u] + _dot(a_rb[u], stack(pm[u])))
        pr[u] = jnp.concatenate([pm[u], rp], axis=0)
        y0[u] = _dot(a_rb[u], stack(q[u])) + rkv[u]
        bk[u] = jnp.concatenate([bw[u], kw[u]], axis=0)
    zt = [zt_ref[p] for p in range(n_packs)]
    for c in range(n_chunks):
        for p in range(n_packs):
            u = (p, c)
            uy = _dot_nt(pr[u], bf(zt[p]))
            y_ref[c * C:(c + 1) * C, p * L:(p + 1) * L] = uy[C:] + y0[u]
            upd = _dot_tn(jnp.concatenate([bf(uy[:C] + q[u]), v[u]], axis=0), bk[u])
            zt[p] = zt[p] * w_end[u] + jnp.where(bd_mask, upd, 0.0)
    for p in range(n_packs):
        zt_ref[p] = zt[p]
    y = y_ref[...]
    inv_n = 1.0 / RWKV_HEAD
    yc = y - _gsum(y, gm_ref, gmt_ref) * inv_n
    var = _gsum(yc * yc, gm_ref, gmt_ref) * inv_n
    yn = yc * lax.rsqrt(var + GN_EPS) * lnw_ref[...] + lnb_ref[...]
    bonus = _gsum(r_ref[...] * k_ref[...] * rk_ref[...], gm_ref, gmt_ref) * v_ref[...]
    out = ((yn + bonus) * g_ref[...]).astype(BF16)
    o_ref[...] = h_ref[...] + gt_ref[...] * _dot(out, w_ref[...])


def _rwkv_scan_out(r, lw, k, v, kk, b, g, ln_w, ln_b, r_k, w_out, h, mod):
    B, S, D = r.shape
    assert S % RW_TB == 0 and RW_TB % RW_C == 0 and D % RW_LANES == 0
    gm, gmt = _head_indicator(D)
    row = lambda a: a.astype(F32).reshape(1, D)
    full = lambda a: pl.BlockSpec(a.shape, lambda bb, t: (0,) * a.ndim)
    tile = pl.BlockSpec((None, RW_TB, D), lambda bb, t: (bb, t, 0))
    params = [row(ln_w), row(ln_b), row(r_k), gm, gmt, w_out.astype(BF16)]
    return pl.pallas_call(
        _rwkv_scan_kernel,
        grid=(B, S // RW_TB),
        in_specs=[tile] * 8 + [pl.BlockSpec((None, None, 1, D), lambda bb, t: (bb, 5, 0, 0))] + [full(p) for p in params],
        out_specs=tile,
        out_shape=jax.ShapeDtypeStruct((B, S, D), F32),
        scratch_shapes=[pltpu.VMEM((D // RW_LANES, RW_LANES, RW_LANES), F32), pltpu.VMEM((RW_TB, D), F32)],
        compiler_params=_cparams("parallel", "arbitrary"),
    )(r, lw, k, v, kk, b, g, h, mod, *params)


def _rwkv_mixer(h, mod, mu, w_rkv, w0, w1, w2, a0, a1, a2, g1, g2, k_k, k_a, r_k, ln_w, ln_b, w_out):
    r, lw, k, v, kk, b, g = _rwkv_in(h, mod, mu, w_rkv, w0, w1, w2, a0, a1, a2, g1, g2, k_k, k_a)
    return _rwkv_scan_out(r, lw, k, v, kk, b, g, ln_w, ln_b, r_k, w_out, h, mod)


def kernel(x, c, ada_w, ada_b, ffn_w_gu, ffn_w_down, dsa_w_in, dsa_q_norm, dsa_k_norm, dsa_ik_norm, dsa_w_out, rwkv_mu, rwkv_w_rkv, rwkv_w0, rwkv_w1, rwkv_w2, rwkv_a0, rwkv_a1, rwkv_a2, rwkv_g1, rwkv_g2, rwkv_k_k, rwkv_k_a, rwkv_r_k, rwkv_ln_w, rwkv_ln_b, rwkv_w_out):
    B, S, D = x.shape
    depth = ada_w.shape[0]
    mods = _ada_mod(c, ada_w, ada_b)
    h = x
    for i in range(depth):
        mod = mods[i].reshape(B, N_MOD, 1, D)
        j = i // 2
        h = _ffn(h, mod, 0, ffn_w_gu[i, 0].astype(BF16), ffn_w_down[i, 0].astype(BF16))
        if i % 2 == 0:
            h = _dsa_mixer(h, mod, dsa_w_in[j], dsa_q_norm[j], dsa_k_norm[j], dsa_ik_norm[j], dsa_w_out[j])
        else:
            h = _rwkv_mixer(h, mod, rwkv_mu[j], rwkv_w_rkv[j], rwkv_w0[j], rwkv_w1[j], rwkv_w2[j], rwkv_a0[j],
                            rwkv_a1[j], rwkv_a2[j], rwkv_g1[j], rwkv_g2[j], rwkv_k_k[j], rwkv_k_a[j],
                            rwkv_r_k[j], rwkv_ln_w[j], rwkv_ln_b[j], rwkv_w_out[j])
        h = _ffn(h, mod, 6, ffn_w_gu[i, 1].astype(BF16), ffn_w_down[i, 1].astype(BF16))
    return h
```
